```python
import math
import jax, jax.numpy as jnp
from jax import lax
import numpy as np

D_MODEL = 1024
BATCH = 2
SEQ = 8192
DEPTH = 2
DEC_BATCH = 32
DEC_SEQ = 1
PAST_LEN = 8192
PAGE_SIZE = 128

HEAD_DIM = 64
N_HEADS_A = D_MODEL // (4 * HEAD_DIM)
N_HEADS_B = D_MODEL // (2 * HEAD_DIM)
N_HEADS_C = D_MODEL // HEAD_DIM
WIDTH_A = N_HEADS_A * 2 * HEAD_DIM
WIDTH_B = N_HEADS_B * HEAD_DIM
WIDTH_C = N_HEADS_C * HEAD_DIM
W_EVEN_IN = 3 * WIDTH_A + 3 * WIDTH_B + (WIDTH_A + WIDTH_B)
W_ODD_IN = 4 * WIDTH_C
ROT_DIM = HEAD_DIM // 4
ROPE_THETA = 500000.0
Q_BLOCK = 128
MOBA_BLOCK = 256
MOBA_TOPK = 3
MOBA_Q_CHUNK = 32
RMS_EPS = 1e-6
N_EVEN = (DEPTH + 1) // 2
N_ODD = DEPTH // 2

kernel_name = 'hybrid_diffattn_stickbreak_moba_step'


def _rmsnorm(x, g):
    xf = x.astype(jnp.float32)
    y = xf * lax.rsqrt(jnp.mean(xf * xf, axis=-1, keepdims=True) + RMS_EPS)
    return (y * g.astype(jnp.float32)).astype(x.dtype)


def _rope_partial(x, pos):
    half = ROT_DIM // 2
    inv = ROPE_THETA ** (-jnp.arange(0, ROT_DIM, 2, dtype=jnp.float32) / ROT_DIM)
    ang = pos.astype(jnp.float32)[:, None] * inv[None, :]
    cos = jnp.cos(ang)[:, None, :]
    sin = jnp.sin(ang)[:, None, :]
    xr = x[..., :ROT_DIM].astype(jnp.float32)
    x1, x2 = xr[..., :half], xr[..., half:]
    rot = jnp.concatenate([x1 * cos - x2 * sin, x2 * cos + x1 * sin], axis=-1)
    return jnp.concatenate([rot.astype(x.dtype), x[..., ROT_DIM:]], axis=-1)


def _qk_norm_rope(x, g, pos):
    return _rope_partial(_rmsnorm(x, g), pos)


def _split(p, widths):
    cuts = np.cumsum(widths)[:-1].tolist()
    return jnp.split(p, cuts, axis=-1)


def _gather_pages(cache, li, pt):
    g = cache[li, pt]
    return g.reshape((g.shape[0], g.shape[1] * g.shape[2]) + g.shape[3:])


def _to_maps(x):
    b, t = x.shape[:2]
    return x.reshape(b, t, N_HEADS_A, 2, HEAD_DIM).transpose(0, 2, 3, 1, 4)


def _diff_attn(q, k, v, mask, lam):
    s = jnp.einsum('bhmqd,bhmkd->bhmqk', q, k).astype(jnp.float32) * (HEAD_DIM ** -0.5)
    a = jax.nn.softmax(jnp.where(mask, s, -jnp.inf), axis=-1)
    w = a[:, :, 0] - lam * a[:, :, 1]
    return jnp.einsum('bhqk,bhkd->bhqd', w.astype(v.dtype), v)


def _stick_breaking(q, k, v, q_pos, k_pos):
    z = jnp.einsum('bhqd,bhkd->bhqk', q, k).astype(jnp.float32) * (HEAD_DIM ** -0.5)
    mask = k_pos[None, :] < q_pos[:, None]
    log_keep = jnp.where(mask, jax.nn.log_sigmoid(-z), 0.0)
    later = lax.cumsum(log_keep, axis=log_keep.ndim - 1, reverse=True) - log_keep
    w = jnp.where(mask, jnp.exp(jax.nn.log_sigmoid(z) + later), 0.0)
    return jnp.einsum('bhqk,bhkd->bhqd', w.astype(v.dtype), v)


def _diff_attn_prompt(q, k, v, lam):
    b, s = q.shape[:2]
    nblk = s // Q_BLOCK
    qm = _to_maps(q).reshape(b, N_HEADS_A, 2, nblk, Q_BLOCK, HEAD_DIM).transpose(3, 0, 1, 2, 4, 5)
    km = _to_maps(k)
    vh = v.transpose(0, 2, 1, 3)
    k_pos = jnp.arange(s)

    def block(args):
        qi, i = args
        q_pos = i * Q_BLOCK + jnp.arange(Q_BLOCK)
        return _diff_attn(qi, km, vh, k_pos[None, :] <= q_pos[:, None], lam)

    o = lax.map(block, (qm, jnp.arange(nblk)))
    return o.transpose(1, 0, 3, 2, 4).reshape(b, s, N_HEADS_A, 2 * HEAD_DIM)


def _sb_prompt(q, k, v):
    b, s = q.shape[:2]
    nblk = s // Q_BLOCK
    qb = q.transpose(0, 2, 1, 3).reshape(b, N_HEADS_B, nblk, Q_BLOCK, HEAD_DIM).transpose(2, 0, 1, 3, 4)
    kh = k.transpose(0, 2, 1, 3)
    vh = v.transpose(0, 2, 1, 3)
    k_pos = jnp.arange(s)

    def block(args):
        qi, i = args
        return _stick_breaking(qi, kh, vh, i * Q_BLOCK + jnp.arange(Q_BLOCK), k_pos)

    o = lax.map(block, (qb, jnp.arange(nblk)))
    return o.transpose(1, 0, 3, 2, 4).reshape(b, s, N_HEADS_B, HEAD_DIM)


def _moba_attend(q, kg, vg, valid, ko, vo, own_mask):
    scale = HEAD_DIM ** -0.5
    s_own = jnp.einsum('bhqd,bhkd->bhqk', q, ko).astype(jnp.float32) * scale
    s_own = jnp.where(own_mask, s_own, -jnp.inf)
    if kg is None:
        a = jax.nn.softmax(s_own, axis=-1).astype(vo.dtype)
        return jnp.einsum('bhqk,bhkd->bhqd', a, vo)
    b, h, tq, n, l, d = kg.shape
    s_sel = jnp.einsum('bhqd,bhqnkd->bhqnk', q, kg).astype(jnp.float32) * scale
    s_sel = jnp.where(valid[..., None], s_sel, -jnp.inf).reshape(b, h, tq, n * l)
    a = jax.nn.softmax(jnp.concatenate([s_sel, s_own], axis=-1), axis=-1).astype(vo.dtype)
    return (jnp.einsum('bhqk,bhqkd->bhqd', a[..., :n * l], vg.reshape(b, h, tq, n * l, d))
            + jnp.einsum('bhqk,bhkd->bhqd', a[..., n * l:], vo))


def _moba_prompt(q, k, v):
    b, s = q.shape[:2]
    nb = -(-s // MOBA_BLOCK)
    pad = nb * MOBA_BLOCK - s
    qh = q.transpose(0, 2, 1, 3)
    kh = jnp.pad(k.transpose(0, 2, 1, 3), ((0, 0), (0, 0), (0, pad), (0, 0)))
    vh = jnp.pad(v.transpose(0, 2, 1, 3), ((0, 0), (0, 0), (0, pad), (0, 0)))
    kblk = kh.reshape(b, N_HEADS_C, nb, MOBA_BLOCK, HEAD_DIM)
    vblk = vh.reshape(b, N_HEADS_C, nb, MOBA_BLOCK, HEAD_DIM)
    kmean = kblk.astype(jnp.float32).mean(axis=3)
    n_sel = min(MOBA_TOPK, nb)
    nc = s // MOBA_Q_CHUNK
    qc = qh.reshape(b, N_HEADS_C, nc, MOBA_Q_CHUNK, HEAD_DIM).transpose(2, 0, 1, 3, 4)
    bi = jnp.arange(b)[:, None, None, None]
    hi = jnp.arange(N_HEADS_C)[None, :, None, None]

    def chunk(args):
        qi, c = args
        start = c * MOBA_Q_CHUNK
        q_pos = start + jnp.arange(MOBA_Q_CHUNK)
        cur = start // MOBA_BLOCK
        g = jnp.einsum('bhqd,bhnd->bhqn', qi.astype(jnp.float32), kmean)
        g = jnp.where(jnp.arange(nb) < cur, g, -jnp.inf)
        _, idx = lax.top_k(g, n_sel)
        valid = idx < cur
        kg = kblk[bi, hi, idx]
        vg = vblk[bi, hi, idx]
        own0 = cur * MOBA_BLOCK
        ko = lax.dynamic_slice_in_dim(kh, own0, MOBA_BLOCK, axis=2)
        vo = lax.dynamic_slice_in_dim(vh, own0, MOBA_BLOCK, axis=2)
        own_mask = (own0 + jnp.arange(MOBA_BLOCK))[None, :] <= q_pos[:, None]
        return _moba_attend(qi, kg, vg, valid, ko, vo, own_mask)

    o = lax.map(chunk, (qc, jnp.arange(nc)))
    return o.transpose(1, 0, 3, 2, 4).reshape(b, s, N_HEADS_C, HEAD_DIM)


def _moba_sample(q, k_new, v_new, cache_k, cache_v, li, pt):
    db, t = q.shape[:2]
    p = cache_k.shape[2]
    past_len = pt.shape[1] * p
    n_full = past_len // MOBA_BLOCK
    own0 = n_full * MOBA_BLOCK
    q_pos = past_len + jnp.arange(t)
    k_past = _gather_pages(cache_k, li, pt)
    v_own_past = _gather_pages(cache_v, li, pt[:, own0 // p:])
    ko = jnp.concatenate([k_past[:, own0:], k_new], axis=1).transpose(0, 2, 1, 3)
    vo = jnp.concatenate([v_own_past, v_new], axis=1).transpose(0, 2, 1, 3)
    own_mask = (own0 + jnp.arange(ko.shape[2]))[None, :] <= q_pos[:, None]
    qh = q.transpose(0, 2, 1, 3)
    if n_full == 0:
        return _moba_attend(qh, None, None, None, ko, vo, own_mask).transpose(0, 2, 1, 3)
    kmean = (k_past[:, :own0].reshape(db, n_full, MOBA_BLOCK, N_HEADS_C, HEAD_DIM)
             .astype(jnp.float32).mean(axis=2).transpose(0, 2, 1, 3))
    g = jnp.einsum('bhqd,bhnd->bhqn', qh.astype(jnp.float32), kmean)
    n_sel = min(MOBA_TOPK, n_full)
    _, idx = lax.top_k(g, n_sel)
    ppb = MOBA_BLOCK // p
    bi = jnp.arange(db)[:, None, None, None, None]
    phys = pt[bi, idx[..., None] * ppb + jnp.arange(ppb)]
    hi = jnp.arange(N_HEADS_C)[None, :, None, None, None, None]
    rows = jnp.arange(p)
    kg = cache_k[li, phys[..., None], rows, hi].reshape(db, N_HEADS_C, t, n_sel, MOBA_BLOCK, HEAD_DIM)
    vg = cache_v[li, phys[..., None], rows, hi].reshape(db, N_HEADS_C, t, n_sel, MOBA_BLOCK, HEAD_DIM)
    valid = jnp.ones(idx.shape, dtype=bool)
    return _moba_attend(qh, kg, vg, valid, ko, vo, own_mask).transpose(0, 2, 1, 3)


def _even_layer(x_p, x_s, cak, cav, cbk, cbv, li, pt, g_norm, w_in, qk_g, lam_p, subln_g, w_out, layer):
    lam_init = 0.8 - 0.6 * math.exp(-0.3 * layer)
    lp = lam_p.astype(jnp.float32)
    lam = jnp.exp(jnp.sum(lp[0] * lp[1])) - jnp.exp(jnp.sum(lp[2] * lp[3])) + lam_init

    def project(x, pos):
        b, t, _ = x.shape
        h = _rmsnorm(x, g_norm) @ w_in
        qa, ka, va, qb, kb, vb, gate = _split(h, [WIDTH_A] * 3 + [WIDTH_B] * 3 + [WIDTH_A + WIDTH_B])
        qa = _qk_norm_rope(qa.reshape(b, t, 2 * N_HEADS_A, HEAD_DIM), qk_g[0], pos).reshape(b, t, N_HEADS_A, 2 * HEAD_DIM)
        ka = _qk_norm_rope(ka.reshape(b, t, 2 * N_HEADS_A, HEAD_DIM), qk_g[1], pos).reshape(b, t, N_HEADS_A, 2 * HEAD_DIM)
        va = va.reshape(b, t, N_HEADS_A, 2 * HEAD_DIM)
        qb = qb.reshape(b, t, N_HEADS_B, HEAD_DIM)
        kb = kb.reshape(b, t, N_HEADS_B, HEAD_DIM)
        vb = vb.reshape(b, t, N_HEADS_B, HEAD_DIM)
        return qa, ka, va, qb, kb, vb, gate

    def finish(x, oa, ob, gate):
        b, t, _ = x.shape
        oa = _rmsnorm(oa, subln_g) * (1.0 - lam_init)
        mix = jnp.concatenate([oa.reshape(b, t, WIDTH_A), ob.reshape(b, t, WIDTH_B)], axis=-1) * jax.nn.silu(gate)
        return x + mix @ w_out

    s = x_p.shape[1]
    qa, ka, va, qb, kb, vb, gate = project(x_p, jnp.arange(s))
    y_p = finish(x_p, _diff_attn_prompt(qa, ka, va, lam), _sb_prompt(qb, kb, vb), gate)

    t = x_s.shape[1]
    past_len = pt.shape[1] * cak.shape[2]
    q_pos = past_len + jnp.arange(t)
    k_pos = jnp.arange(past_len + t)
    qa_s, ka_s, va_s, qb_s, kb_s, vb_s, gate_s = project(x_s, q_pos)
    ka_all = jnp.concatenate([_gather_pages(cak, li, pt), ka_s], axis=1)
    va_all = jnp.concatenate([_gather_pages(cav, li, pt), va_s], axis=1)
    oa_s = _diff_attn(_to_maps(qa_s), _to_maps(ka_all), va_all.transpose(0, 2, 1, 3),
                      k_pos[None, :] <= q_pos[:, None], lam).transpose(0, 2, 1, 3)
    kb_all = jnp.concatenate([_gather_pages(cbk, li, pt), kb_s], axis=1)
    vb_all = jnp.concatenate([_gather_pages(cbv, li, pt), vb_s], axis=1)
    ob_s = _stick_breaking(qb_s.transpose(0, 2, 1, 3), kb_all.transpose(0, 2, 1, 3),
                           vb_all.transpose(0, 2, 1, 3), q_pos, k_pos).transpose(0, 2, 1, 3)
    y_s = finish(x_s, oa_s, ob_s, gate_s)
    return y_p, y_s, (ka, va, kb, vb), (ka_s, va_s, kb_s, vb_s)


def _odd_layer(x_p, x_s, cck, ccv, li, pt, g_norm, w_in, qk_g, w_out):
    def project(x, pos):
        b, t, _ = x.shape
        h = _rmsnorm(x, g_norm) @ w_in
        q, k, v, gate = _split(h, [WIDTH_C] * 4)
        q = _qk_norm_rope(q.reshape(b, t, N_HEADS_C, HEAD_DIM), qk_g[0], pos)
        k = _qk_norm_rope(k.reshape(b, t, N_HEADS_C, HEAD_DIM), qk_g[1], pos)
        v = v.reshape(b, t, N_HEADS_C, HEAD_DIM)
        return q, k, v, gate

    def finish(x, o, gate):
        b, t, _ = x.shape
        return x + (o.reshape(b, t, WIDTH_C) * jax.nn.silu(gate)) @ w_out

    s = x_p.shape[1]
    q, k, v, gate = project(x_p, jnp.arange(s))
    y_p = finish(x_p, _moba_prompt(q, k, v), gate)

    t = x_s.shape[1]
    past_len = pt.shape[1] * cck.shape[2]
    q_s, k_s, v_s, gate_s = project(x_s, past_len + jnp.arange(t))
    y_s = finish(x_s, _moba_sample(q_s, k_s, v_s, cck, ccv, li, pt), gate_s)
    return y_p, y_s, (k, v), (k_s, v_s)


def setup_inputs(seed: int = 0) -> dict:
    key = jax.random.key(seed)
    ks = jax.random.split(key, 20)
    n_pages = PAST_LEN // PAGE_SIZE
    n_used = DEC_BATCH * n_pages
    n_pool = n_used + max(1, n_used // 4)

    def nrm(k, shape, scale):
        return scale * jax.random.normal(k, shape, jnp.float32)

    page_table = jax.random.permutation(ks[8], n_pool)[:n_used].reshape(DEC_BATCH, n_pages).astype(jnp.int32)
    return {
        'x_prompt': nrm(ks[0], (BATCH, SEQ, D_MODEL), 1.0),
        'x_sample': nrm(ks[1], (DEC_BATCH, DEC_SEQ, D_MODEL), 1.0),
        'cache_a_k': nrm(ks[2], (N_EVEN, n_pool, PAGE_SIZE, N_HEADS_A, 2 * HEAD_DIM), 1.0),
        'cache_a_v': nrm(ks[3], (N_EVEN, n_pool, PAGE_SIZE, N_HEADS_A, 2 * HEAD_DIM), 1.0),
        'cache_b_k': nrm(ks[4], (N_EVEN, n_pool, PAGE_SIZE, N_HEADS_B, HEAD_DIM), 1.0),
        'cache_b_v': nrm(ks[5], (N_EVEN, n_pool, PAGE_SIZE, N_HEADS_B, HEAD_DIM), 1.0),
        'cache_c_k': nrm(ks[6], (N_ODD, n_pool, PAGE_SIZE, N_HEADS_C, HEAD_DIM), 1.0),
        'cache_c_v': nrm(ks[7], (N_ODD, n_pool, PAGE_SIZE, N_HEADS_C, HEAD_DIM), 1.0),
        'page_table': page_table,
        'norm_even': 1.0 + nrm(ks[9], (N_EVEN, D_MODEL), 0.02),
        'w_in_even': nrm(ks[10], (N_EVEN, D_MODEL, W_EVEN_IN), D_MODEL ** -0.5),
        'qk_norm_a': 1.0 + nrm(ks[11], (N_EVEN, 2, HEAD_DIM), 0.02),
        'lambda_a': nrm(ks[12], (N_EVEN, 4, HEAD_DIM), 0.1),
        'subln_a': 1.0 + nrm(ks[13], (N_EVEN, 2 * HEAD_DIM), 0.02),
        'w_out_even': nrm(ks[14], (N_EVEN, WIDTH_A + WIDTH_B, D_MODEL), (WIDTH_A + WIDTH_B) ** -0.5),
        'norm_odd': 1.0 + nrm(ks[15], (N_ODD, D_MODEL), 0.02),
        'w_in_odd': nrm(ks[16], (N_ODD, D_MODEL, W_ODD_IN), D_MODEL ** -0.5),
        'qk_norm_c': 1.0 + nrm(ks[17], (N_ODD, 2, HEAD_DIM), 0.02),
        'w_out_odd': nrm(ks[18], (N_ODD, WIDTH_C, D_MODEL), WIDTH_C ** -0.5),
    }


def reference(x_prompt, x_sample, cache_a_k, cache_a_v, cache_b_k, cache_b_v, cache_c_k, cache_c_v,
              page_table, norm_even, w_in_even, qk_norm_a, lambda_a, subln_a, w_out_even,
              norm_odd, w_in_odd, qk_norm_c, w_out_odd):
    y_p, y_s = x_prompt, x_sample
    ev_p, ev_s, od_p, od_s = [], [], [], []
    for layer in range(DEPTH):
        i = layer // 2
        if layer % 2 == 0:
            y_p, y_s, rp, rs = _even_layer(y_p, y_s, cache_a_k, cache_a_v, cache_b_k, cache_b_v, i, page_table,
                                           norm_even[i], w_in_even[i], qk_norm_a[i], lambda_a[i], subln_a[i],
                                           w_out_even[i], layer)
            ev_p.append(rp)
            ev_s.append(rs)
        else:
            y_p, y_s, rp, rs = _odd_layer(y_p, y_s, cache_c_k, cache_c_v, i, page_table,
                                          norm_odd[i], w_in_odd[i], qk_norm_c[i], w_out_odd[i])
            od_p.append(rp)
            od_s.append(rs)

    def stack(rows, j):
        return jnp.stack([r[j] for r in rows])

    return (y_p, y_s,
            stack(ev_p, 0), stack(ev_p, 1), stack(ev_p, 2), stack(ev_p, 3), stack(od_p, 0), stack(od_p, 1),
            stack(ev_s, 0), stack(ev_s, 1), stack(ev_s, 2), stack(ev_s, 3), stack(od_s, 0), stack(od_s, 1))
```

```python
import functools
import math

import jax
import jax.numpy as jnp
from jax import lax
from jax.experimental import pallas as pl
from jax.experimental.pallas import tpu as pltpu

HEAD_DIM = 64
ROT_DIM = HEAD_DIM // 4
ROPE_THETA = 500000.0
RMS_EPS = 1e-6
MOBA_BLOCK = 256
MOBA_TOPK = 3
LANES = 128
NEG = -1e30
QK_SCALE = HEAD_DIM ** -0.5
VMEM_LIMIT = 52 * 1024 * 1024

F32 = jnp.float32
BF16 = jnp.bfloat16


def _params(sem, vmem=VMEM_LIMIT):
    return pltpu.CompilerParams(dimension_semantics=sem, vmem_limit_bytes=vmem)


def _dot(a, b):
    return jnp.dot(a, b, preferred_element_type=F32)


def _dot_nt(a, b):
    return lax.dot_general(a, b, (((1,), (1,)), ((), ())), preferred_element_type=F32)


def _split_bf16(x):
    hi = x.astype(BF16)
    lo = (x - hi.astype(F32)).astype(BF16)
    return hi, lo


def _iota(shape, dim):
    return lax.broadcasted_iota(jnp.int32, shape, dim)


def _head_seg_matrix():
    return (_iota((LANES, LANES), 0) // HEAD_DIM == _iota((LANES, LANES), 1) // HEAD_DIM).astype(BF16)


def _norm_rope(x, g, c, s1, s2, seg):
    hi, lo = _split_bf16(x * x)
    ms = (_dot(hi, seg) + _dot(lo, seg)) * (1.0 / HEAD_DIM)
    xn = x * lax.rsqrt(ms + RMS_EPS) * g
    half = ROT_DIM // 2
    return xn * c + pltpu.roll(xn, LANES - half, 1) * s1 + pltpu.roll(xn, half, 1) * s2


def _proj_kernel(x_ref, gn_ref, w_ref, gq_ref, gk_ref, c_ref, s1_ref, s2_ref, *out_refs, plan):
    x = x_ref[...]
    ms = jnp.mean(x * x, axis=-1, keepdims=True)
    xn = (x * lax.rsqrt(ms + RMS_EPS) * gn_ref[...]).astype(BF16)
    seg = _head_seg_matrix()
    c, s1, s2 = c_ref[...], s1_ref[...], s2_ref[...]
    outs = list(out_refs)
    for kind, col0, width in plan:
        h = _dot(xn, w_ref[:, col0:col0 + width])
        if kind in ("q_rope_bf16", "q_rope_f32", "k_rope", "k_rope_mean"):
            g = gq_ref[...] if kind.startswith("q") else gk_ref[...]
            if kind == "q_rope_bf16":
                o_ref = outs.pop(0)
            elif kind == "q_rope_f32":
                o_ref = outs.pop(0)
            else:
                o_ref, o16_ref = outs.pop(0), outs.pop(0)
                mean_ref = outs.pop(0) if kind == "k_rope_mean" else None
            for t in range(width // LANES):
                sl = slice(t * LANES, (t + 1) * LANES)
                y = _norm_rope(h[:, sl], g, c, s1, s2, seg)
                if kind == "q_rope_bf16":
                    o_ref[:, sl] = (y * QK_SCALE).astype(BF16)
                elif kind == "q_rope_f32":
                    o_ref[:, sl] = y
                else:
                    o_ref[:, sl] = y
                    o16_ref[:, sl] = y.astype(BF16)
                    if mean_ref is not None:
                        mean_ref[:, sl] = jnp.mean(y, axis=0, keepdims=True)
        elif kind == "q_plain_bf16":
            outs.pop(0)[...] = (h * QK_SCALE).astype(BF16)
        elif kind == "kv_plain":
            outs.pop(0)[...] = h
            outs.pop(0)[...] = h.astype(BF16)
        elif kind == "gate":
            outs.pop(0)[...] = h / (1.0 + jnp.exp(-h))
        else:
            raise ValueError(kind)
    assert not outs


def _rope_tables(pos):
    half = ROT_DIM // 2
    inv = ROPE_THETA ** (-jnp.arange(0, ROT_DIM, 2, dtype=F32) / ROT_DIM)
    ang = pos.astype(F32)[:, None] * inv[None, :]
    cos, sin = jnp.cos(ang), jnp.sin(ang)
    t = pos.shape[0]
    z_half = jnp.zeros((t, half), F32)
    z_rest = jnp.zeros((t, HEAD_DIM - ROT_DIM), F32)
    c = jnp.concatenate([cos, cos, jnp.ones((t, HEAD_DIM - ROT_DIM), F32)], axis=-1)
    s1 = jnp.concatenate([-sin, z_half, z_rest], axis=-1)
    s2 = jnp.concatenate([z_half, sin, z_rest], axis=-1)
    rep = LANES // HEAD_DIM
    return tuple(jnp.tile(a, (1, rep)) for a in (c, s1, s2))


def _project(x, g_norm, w_bf16, gq, gk, tables, plan, tm, name):
    m, d = x.shape
    n_tab = tables[0].shape[0] // tm
    out_shapes, out_specs = [], []
    for kind, _, width in plan:
        row = pl.BlockSpec((tm, width), lambda i: (i, 0))
        if kind in ("q_rope_bf16", "q_plain_bf16"):
            out_shapes += [jax.ShapeDtypeStruct((m, width), BF16)]
            out_specs += [row]
        elif kind in ("q_rope_f32", "gate"):
            out_shapes += [jax.ShapeDtypeStruct((m, width), F32)]
            out_specs += [row]
        elif kind in ("k_rope", "kv_plain", "k_rope_mean"):
            out_shapes += [jax.ShapeDtypeStruct((m, width), F32), jax.ShapeDtypeStruct((m, width), BF16)]
            out_specs += [row, row]
            if kind == "k_rope_mean":
                assert tm == MOBA_BLOCK
                out_shapes += [jax.ShapeDtypeStruct((m // tm, 1, width), F32)]
                out_specs += [pl.BlockSpec((None, 1, width), lambda i: (i, 0, 0))]
    tab_spec = pl.BlockSpec((tm, LANES), lambda i: (i % n_tab, 0))
    vec_d = pl.BlockSpec((1, d), lambda i: (0, 0))
    vec_l = pl.BlockSpec((1, LANES), lambda i: (0, 0))
    rep = LANES // HEAD_DIM
    return pl.pallas_call(
        functools.partial(_proj_kernel, plan=plan),
        grid=(m // tm,),
        in_specs=[pl.BlockSpec((tm, d), lambda i: (i, 0)), vec_d,
                  pl.BlockSpec(w_bf16.shape, lambda i: (0, 0), pipeline_mode=pl.Buffered(1)),
                  vec_l, vec_l, tab_spec, tab_spec, tab_spec],
        out_specs=out_specs,
        out_shape=out_shapes,
        compiler_params=_params(("parallel",)),
        name=name,
    )(x, g_norm.reshape(1, d), w_bf16, jnp.tile(gq, rep).reshape(1, LANES), jnp.tile(gk, rep).reshape(1, LANES),
      *tables)


def _finish_kernel(x_ref, w_ref, *refs):
    mix_refs, y_ref = refs[:-1], refs[-1]
    y = x_ref[...]
    row0 = 0
    for mix_ref in mix_refs:
        rows = mix_ref.shape[1]
        y = y + _dot(mix_ref[...], w_ref[row0:row0 + rows, :])
        row0 += rows
    y_ref[...] = y


def _finish(x, mix_parts, w_bf16, tm, name):
    m, d = x.shape
    kdim = w_bf16.shape[0]
    assert sum(p.shape[1] for p in mix_parts) == kdim
    return pl.pallas_call(
        _finish_kernel,
        grid=(m // tm,),
        in_specs=[pl.BlockSpec((tm, d), lambda i: (i, 0)), pl.BlockSpec((kdim, d), lambda i: (0, 0))]
        + [pl.BlockSpec((tm, p.shape[1]), lambda i: (i, 0)) for p in mix_parts],
        out_specs=pl.BlockSpec((tm, d), lambda i: (i, 0)),
        out_shape=jax.ShapeDtypeStruct((m, d), F32),
        compiler_params=_params(("parallel",)),
        name=name,
    )(x, w_bf16, *mix_parts)


def _stack_pair(q):
    lane = _iota(q.shape, 1)
    zero = jnp.zeros_like(q)
    return jnp.concatenate([jnp.where(lane < HEAD_DIM, q, zero), jnp.where(lane >= HEAD_DIM, q, zero)], axis=0)


def _merge_pair(x, tq):
    lane = _iota((tq, LANES), 1)
    return jnp.where(lane < HEAD_DIM, x[:tq], x[tq:])


def _lambda_value(lp, lam_init):
    a = jnp.sum(lp[0:1] * lp[1:2], axis=-1, keepdims=True)
    b = jnp.sum(lp[2:3] * lp[3:4], axis=-1, keepdims=True)
    return jnp.exp(a) - jnp.exp(b) + lam_init


def _softmax_tile(qq, k, v, carry, mask):
    m, l, acc = carry
    s = _dot_nt(qq, k)
    if mask is not None:
        s = jnp.where(mask, s, NEG)
    m_new = jnp.maximum(m, jnp.max(s, axis=-1, keepdims=True))
    alpha = jnp.exp(m - m_new)
    p = jnp.exp(s - m_new)
    l = alpha * l + jnp.sum(p, axis=-1, keepdims=True)
    acc = alpha * acc + _dot(p.astype(BF16), v)
    return m_new, l, acc


def _diff_attn_kernel(lam_ref, subg_ref, q_ref, k_ref, v_ref, sg_ref, o_ref, *, tq, tk, lam_init):
    i = pl.program_id(2)
    r = 2 * tq
    qq = _stack_pair(q_ref[...])
    n_full = (i * tq) // tk

    def full_tile(j, carry):
        start = pl.multiple_of(j * tk, tk)
        return _softmax_tile(qq, k_ref[pl.ds(start, tk), :], v_ref[pl.ds(start, tk), :], carry, None)

    carry = (jnp.full((r, 1), NEG, F32), jnp.zeros((r, 1), F32), jnp.zeros((r, LANES), F32))
    carry = lax.fori_loop(0, n_full, full_tile, carry)
    start = pl.multiple_of(n_full * tk, tk)
    q_pos = i * tq + _iota((r, tk), 0) % tq
    k_pos = start + _iota((r, tk), 1)
    _, l, acc = _softmax_tile(qq, k_ref[pl.ds(start, tk), :], v_ref[pl.ds(start, tk), :], carry, k_pos <= q_pos)

    a = acc / l
    lam = _lambda_value(lam_ref[...], lam_init)
    o = a[:tq] - lam * a[tq:]
    ms = jnp.mean(o * o, axis=-1, keepdims=True)
    o = o * lax.rsqrt(ms + RMS_EPS) * subg_ref[...] * (1.0 - lam_init)
    o_ref[...] = (o * sg_ref[...]).astype(BF16)


def _sb_weights(z, tri, c, valid):
    sp = jnp.maximum(z, 0.0) + jnp.log(1.0 + jnp.exp(-jnp.abs(z)))
    log_beta = z - sp
    log_keep = -sp
    if valid is not None:
        log_keep = jnp.where(valid, log_keep, 0.0)
    hi, lo = _split_bf16(log_keep)
    rows = z.shape[0]
    both = _dot(jnp.concatenate([hi, lo], axis=0), tri)
    later = both[:rows] + both[rows:]
    w = jnp.exp(log_beta + later + c)
    if valid is not None:
        w = jnp.where(valid, w, 0.0)
    return w, c + later[:, 0:1] + log_keep[:, 0:1]


def _sb_tile(qq, k, v, tri, c, acc, valid):
    w, c = _sb_weights(_dot_nt(qq, k), tri, c, valid)
    return c, acc + _dot(w.astype(BF16), v)


def _later_matrix(tk):
    return (_iota((tk, tk), 0) > _iota((tk, tk), 1)).astype(BF16)


def _sb_kernel(q_ref, k_ref, v_ref, sg_ref, o_ref, *, tq, tk):
    i = pl.program_id(2)
    r = 2 * tq
    qq = _stack_pair(q_ref[...])
    tri = _later_matrix(tk)
    n_full = (i * tq) // tk

    start = pl.multiple_of(n_full * tk, tk)
    q_pos = i * tq + _iota((r, tk), 0) % tq
    k_pos = start + _iota((r, tk), 1)
    c, acc = _sb_tile(qq, k_ref[pl.ds(start, tk), :], v_ref[pl.ds(start, tk), :], tri,
                      jnp.zeros((r, 1), F32), jnp.zeros((r, LANES), F32), k_pos < q_pos)

    def full_tile(t, carry):
        st = pl.multiple_of((n_full - 1 - t) * tk, tk)
        return _sb_tile(qq, k_ref[pl.ds(st, tk), :], v_ref[pl.ds(st, tk), :], tri, carry[0], carry[1], None)

    c, acc = lax.fori_loop(0, n_full, full_tile, (c, acc))
    o_ref[...] = (_merge_pair(acc, tq) * sg_ref[...]).astype(BF16)


def _top_blocks(g, n_valid):
    blk = _iota(g.shape, 1)
    nb = g.shape[1]
    g = jnp.where(blk < n_valid, g, -jnp.inf)
    sel = jnp.zeros(g.shape, jnp.bool_)
    for _ in range(min(MOBA_TOPK, nb)):
        mx = jnp.max(g, axis=-1, keepdims=True)
        idx = jnp.min(jnp.where(g == mx, blk, nb), axis=-1, keepdims=True)
        pick = blk == idx
        sel = jnp.logical_or(sel, pick)
        g = jnp.where(pick, -jnp.inf, g)
    return jnp.logical_and(sel, blk < n_valid)


def _moba_kernel(q_ref, k_ref, v_ref, km_ref, sg_ref, o_ref, *, tq):
    i = pl.program_id(2)
    r = 2 * tq
    qf = _stack_pair(q_ref[...])
    qq = (qf * QK_SCALE).astype(BF16)

    km = km_ref[...]
    q_hi, q_lo = _split_bf16(qf)
    km_hi, km_lo = _split_bf16(km)
    g = _dot_nt(q_hi, km_hi) + _dot_nt(q_lo, km_hi) + _dot_nt(q_hi, km_lo)
    sel = jnp.where(_top_blocks(g, i), 1.0, 0.0)
    blk = _iota(sel.shape, 1)

    start = pl.multiple_of(i * tq, tq)
    q_pos = _iota((r, tq), 0) % tq
    k_pos = _iota((r, tq), 1)
    carry = (jnp.full((r, 1), NEG, F32), jnp.zeros((r, 1), F32), jnp.zeros((r, LANES), F32))
    carry = _softmax_tile(qq, k_ref[pl.ds(start, tq), :], v_ref[pl.ds(start, tq), :], carry, k_pos <= q_pos)

    def past_block(j, carry):
        st = pl.multiple_of(j * tq, tq)
        chosen = jnp.max(jnp.where(blk == j, sel, 0.0), axis=-1, keepdims=True) > 0.0
        return _softmax_tile(qq, k_ref[pl.ds(st, tq), :], v_ref[pl.ds(st, tq), :], carry, chosen)

    _, l, acc = lax.fori_loop(0, i, past_block, carry)
    o_ref[...] = (_merge_pair(acc / l, tq) * sg_ref[...]).astype(BF16)


def _attn_specs(tq, s, group0=0):
    q_spec = pl.BlockSpec((None, tq, LANES), lambda b, h, i: (b, i, h + group0))
    kv_spec = pl.BlockSpec((None, s, LANES), lambda b, h, i: (b, 0, h))
    return q_spec, kv_spec


def _diff_attn_prompt(q16, k16, v16, sgate, lam_p, subln_g, lam_init, tq, tk):
    b, s, w = q16.shape
    q_spec, kv_spec = _attn_specs(tq, s)
    return pl.pallas_call(
        functools.partial(_diff_attn_kernel, tq=tq, tk=tk, lam_init=lam_init),
        grid=(b, w // LANES, s // tq),
        in_specs=[pl.BlockSpec(lam_p.shape, lambda b, h, i: (0, 0)),
                  pl.BlockSpec((1, LANES), lambda b, h, i: (0, 0)),
                  q_spec, kv_spec, kv_spec, q_spec],
        out_specs=q_spec,
        out_shape=jax.ShapeDtypeStruct((b, s, w), BF16),
        compiler_params=_params(("parallel", "parallel", "arbitrary")),
        name="diff_attn_prompt",
    )(lam_p, subln_g.reshape(1, LANES), q16, k16, v16, sgate)


def _sb_prompt(q16, k16, v16, sgate, gate_group0, tq, tk):
    b, s, w = q16.shape
    q_spec, kv_spec = _attn_specs(tq, s)
    sg_spec = pl.BlockSpec((None, tq, LANES), lambda b, h, i: (b, i, h + gate_group0))
    return pl.pallas_call(
        functools.partial(_sb_kernel, tq=tq, tk=tk),
        grid=(b, w // LANES, s // tq),
        in_specs=[q_spec, kv_spec, kv_spec, sg_spec],
        out_specs=q_spec,
        out_shape=jax.ShapeDtypeStruct((b, s, w), BF16),
        compiler_params=_params(("parallel", "parallel", "arbitrary")),
        name="sb_prompt",
    )(q16, k16, v16, sgate)


def _moba_prompt(q, k16, v16, kmean, sgate):
    b, s, w = q.shape
    tq = MOBA_BLOCK
    nb = s // tq
    q_spec, kv_spec = _attn_specs(tq, s)
    return pl.pallas_call(
        functools.partial(_moba_kernel, tq=tq),
        grid=(b, w // LANES, nb),
        in_specs=[q_spec, kv_spec, kv_spec,
                  pl.BlockSpec((None, nb, LANES), lambda b, h, i: (b, 0, h)), q_spec],
        out_specs=q_spec,
        out_shape=jax.ShapeDtypeStruct((b, s, w), BF16),
        compiler_params=_params(("parallel", "parallel", "arbitrary")),
        name="moba_prompt",
    )(q, k16, v16, kmean, sgate)


def _row_heads(x_row, n_rows, lanes_per_row):
    w = x_row.shape[1]
    keep = _iota((n_rows, w), 1) // lanes_per_row == _iota((n_rows, w), 0)
    return jnp.where(keep, jnp.broadcast_to(x_row, (n_rows, w)), 0.0)


def _dup_rows(x, reps):
    n, w = x.shape
    row = _iota((n * reps, w), 0) // reps
    out = jnp.zeros((n * reps, w), x.dtype)
    for h in range(n):
        out = jnp.where(row == h, x[h:h + 1, :], out)
    return out


def _stack_rows(rows):
    n, w = len(rows), rows[0].shape[1]
    row = _iota((n, w), 0)
    out = jnp.zeros((n, w), rows[0].dtype)
    for h, r in enumerate(rows):
        out = jnp.where(row == h, r, out)
    return out


def _head_dots(qcol_ref, kt_ref, n_heads):
    rows = []
    for h in range(n_heads):
        sl = slice(h * HEAD_DIM, (h + 1) * HEAD_DIM)
        rows.append(jnp.sum(qcol_ref[sl, :] * kt_ref[sl, :], axis=0, keepdims=True))
    return _stack_rows(rows)


def _add_weighted(acc_ref, w, vt_ref, n_heads):
    for h in range(n_heads):
        sl = slice(h * HEAD_DIM, (h + 1) * HEAD_DIM)
        acc_ref[sl, :] = acc_ref[sl, :] + w[h:h + 1, :] * vt_ref[sl, :]


def _lane_sums_as_row(x):
    hi, lo = _split_bf16(x)
    ones = jnp.ones((8, LANES), BF16)
    return (_dot_nt(ones, hi) + _dot_nt(ones, lo))[0:1, :]


def _decode_even_kernel(pt_ref, lam_ref, subg_ref, qa_ref, kan_ref, van_ref, qb_ref, sg_ref, *refs,
                        pps, lam_init):
    del pt_ref
    page_refs = refs[:4 * pps]
    o_ref = refs[4 * pps]
    qa_s, m_s, l_s, acca_s, c_s, accb_s = refs[4 * pps + 1:]
    step = pl.program_id(1)
    n_ha = qa_ref.shape[0]
    n_a = 2 * n_ha
    n_b = accb_s.shape[0] // HEAD_DIM
    wa = n_ha * LANES

    @pl.when(step == 0)
    def _():
        rows = _dup_rows(qa_ref[...].astype(F32), 2)
        qa = jnp.where(_iota(rows.shape, 1) // HEAD_DIM == _iota(rows.shape, 0) % 2, rows, 0.0)
        qa_s[...] = qa
        m_s[...] = jnp.sum(qa * _dup_rows(kan_ref[...], 2), axis=-1, keepdims=True)
        l_s[...] = jnp.ones_like(l_s)
        acca_s[...] = _dup_rows(van_ref[...], 2)
        c_s[...] = jnp.zeros_like(c_s)
        accb_s[...] = jnp.zeros_like(accb_s)

    page_rows = page_refs[0].shape[0]
    own_head = _iota((n_a, page_rows), 1) % n_ha == _iota((n_a, page_rows), 0) // 2
    tri = _later_matrix(LANES)
    qa16 = qa_s[...].astype(BF16)
    for u in range(pps):
        ak, av = page_refs[4 * u][...].astype(BF16), page_refs[4 * u + 1][...].astype(BF16)
        m_s[...], l_s[...], acca_s[...] = _softmax_tile(qa16, ak, av, (m_s[...], l_s[...], acca_s[...]), own_head)
        z = _head_dots(qb_ref, page_refs[4 * u + 2], n_b)
        w, c_s[...] = _sb_weights(z, tri, c_s[...], None)
        _add_weighted(accb_s, w, page_refs[4 * u + 3], n_b)

    @pl.when(step == pl.num_programs(1) - 1)
    def _():
        lam = _lambda_value(lam_ref[...], lam_init)
        a = acca_s[...] / l_s[...]
        for h in range(n_ha):
            sl = slice(h * LANES, (h + 1) * LANES)
            x = a[2 * h:2 * h + 1, :] - lam * a[2 * h + 1:2 * h + 2, :]
            ms = jnp.mean(x * x, axis=-1, keepdims=True)
            x = x * lax.rsqrt(ms + RMS_EPS) * subg_ref[...] * (1.0 - lam_init)
            o_ref[:, sl] = (x * sg_ref[:, sl]).astype(BF16)
        o_ref[:, wa:] = (_lane_sums_as_row(accb_s[...]) * sg_ref[:, wa:]).astype(BF16)


def _page_specs(cache, li, n_pages, pps, reverse):
    r = cache.shape[2]

    def make(u):
        def index(b, s, pt):
            j = s * pps + u
            j = n_pages - 1 - j if reverse else j
            return (li, pt[b, j], 0, 0)
        return pl.BlockSpec((None, None, r, LANES), index)

    return [make(u) for u in range(pps)]


def _col_bcast(x):
    return jnp.broadcast_to(x.astype(F32)[:, :, None], x.shape + (LANES,))


def _decode_even(pt, lam_p, subln_g, lam_init, qa16, ka, va, qb16, sgate, caches, li, pps):
    db, n_pages = pt.shape
    wa, wb = qa16.shape[-1], qb16.shape[-1]
    n_ha = wa // LANES
    whole = lambda shp: pl.BlockSpec((None,) + shp, lambda b, s, pt: (b,) + (0,) * len(shp))
    specs_by_cache = [_page_specs(c, li, n_pages, pps, True) for c in caches]
    page_specs = [specs_by_cache[t][u] for u in range(pps) for t in range(4)]
    page_args = [caches[t] for u in range(pps) for t in range(4)]
    grid_spec = pltpu.PrefetchScalarGridSpec(
        num_scalar_prefetch=1,
        grid=(db, n_pages // pps),
        in_specs=[pl.BlockSpec(lam_p.shape, lambda b, s, pt: (0, 0)),
                  pl.BlockSpec((1, LANES), lambda b, s, pt: (0, 0)),
                  whole((n_ha, LANES)), whole((n_ha, LANES)), whole((n_ha, LANES)), whole((wb, LANES)),
                  whole((1, wa + wb))] + page_specs,
        out_specs=whole((1, wa + wb)),
        scratch_shapes=[pltpu.VMEM((2 * n_ha, LANES), F32),
                        pltpu.VMEM((2 * n_ha, 1), F32), pltpu.VMEM((2 * n_ha, 1), F32),
                        pltpu.VMEM((2 * n_ha, LANES), F32),
                        pltpu.VMEM((wb // HEAD_DIM, 1), F32), pltpu.VMEM((wb, LANES), F32)],
    )
    heads = lambda x: x.reshape(db, n_ha, LANES)
    return pl.pallas_call(
        functools.partial(_decode_even_kernel, pps=pps, lam_init=lam_init),
        grid_spec=grid_spec,
        out_shape=jax.ShapeDtypeStruct((db, 1, wa + wb), BF16),
        compiler_params=_params(("parallel", "arbitrary")),
        name="decode_even",
    )(pt, lam_p, subln_g.reshape(1, LANES), heads(qa16), heads(ka), heads(va), _col_bcast(qb16),
      sgate.reshape(db, 1, wa + wb), *page_args)


def _moba_scores_kernel(pt_ref, qcol_ref, qrow_ref, kn_ref, *refs, pps, n_blocks):
    del pt_ref
    page_refs = refs[:pps]
    p_ref, pnew_ref = refs[pps], refs[pps + 1]
    s_s = refs[pps + 2]
    step = pl.program_id(1)
    n_h = qcol_ref.shape[0] // HEAD_DIM
    ppb = MOBA_BLOCK // LANES

    for u in range(pps):
        s_s[step * pps + u] = _head_dots(qcol_ref, page_refs[u], n_h)

    @pl.when(step == pl.num_programs(1) - 1)
    def _():
        lane = _iota((n_h, LANES), 1)
        g = jnp.zeros((n_h, LANES), F32)
        for n in range(n_blocks):
            tot = sum(jnp.sum(s_s[n * ppb + t], axis=-1, keepdims=True) for t in range(ppb))
            g = jnp.where(lane == n, tot * (1.0 / MOBA_BLOCK), g)
        sel = _top_blocks(g, n_blocks)
        s_new = jnp.sum(_row_heads(qrow_ref[...], n_h, HEAD_DIM) * kn_ref[...], axis=-1, keepdims=True) * QK_SCALE
        m = s_new
        for j in range(n_blocks * ppb):
            n = j // ppb
            page_max = jnp.max(s_s[j], axis=-1, keepdims=True) * QK_SCALE
            m = jnp.maximum(m, jnp.where(sel[:, n:n + 1], page_max, NEG))
        e_new = jnp.exp(s_new - m)
        l = e_new
        for j in range(n_blocks * ppb):
            n = j // ppb
            e = jnp.where(sel[:, n:n + 1], jnp.exp(s_s[j] * QK_SCALE - m), 0.0)
            s_s[j] = e
            l = l + jnp.sum(e, axis=-1, keepdims=True)
        inv = 1.0 / l
        for j in range(n_blocks * ppb):
            p_ref[j] = s_s[j] * inv
        pnew_ref[...] = jnp.broadcast_to(e_new * inv, pnew_ref.shape)


def _moba_values_kernel(pt_ref, p_ref, pnew_ref, vn_ref, sg_ref, *refs, pps):
    del pt_ref
    page_refs = refs[:pps]
    o_ref = refs[pps]
    acc_s = refs[pps + 1]
    step = pl.program_id(1)
    n_h = acc_s.shape[0] // HEAD_DIM

    @pl.when(step == 0)
    def _():
        acc_s[...] = jnp.zeros_like(acc_s)

    for u in range(pps):
        _add_weighted(acc_s, p_ref[u], page_refs[u], n_h)

    @pl.when(step == pl.num_programs(1) - 1)
    def _():
        o = _lane_sums_as_row(acc_s[...]) + pnew_ref[...] * vn_ref[...]
        o_ref[...] = (o * sg_ref[...]).astype(BF16)


def _decode_odd(pt, q, k_new, v_new, sgate, cache_k, cache_v, li, pps):
    db, n_pages = pt.shape
    w = q.shape[-1]
    n_h = w // HEAD_DIM
    page = LANES
    past = n_pages * page
    assert past % MOBA_BLOCK == 0 and past >= MOBA_BLOCK and MOBA_BLOCK % page == 0
    n_blocks = past // MOBA_BLOCK
    row = lambda wd: pl.BlockSpec((None, 1, wd), lambda b, s, pt: (b, 0, 0))
    r3 = lambda x: x.reshape(db, 1, x.shape[-1])
    probs, p_new = pl.pallas_call(
        functools.partial(_moba_scores_kernel, pps=pps, n_blocks=n_blocks),
        grid_spec=pltpu.PrefetchScalarGridSpec(
            num_scalar_prefetch=1,
            grid=(db, n_pages // pps),
            in_specs=[pl.BlockSpec((None, w, LANES), lambda b, s, pt: (b, 0, 0)), row(w), row(w)]
            + _page_specs(cache_k, li, n_pages, pps, False),
            out_specs=[pl.BlockSpec((None, n_pages, n_h, page), lambda b, s, pt: (b, 0, 0, 0)),
                       pl.BlockSpec((None, n_h, LANES), lambda b, s, pt: (b, 0, 0))],
            scratch_shapes=[pltpu.VMEM((n_pages, n_h, page), F32)],
        ),
        out_shape=[jax.ShapeDtypeStruct((db, n_pages, n_h, page), F32),
                   jax.ShapeDtypeStruct((db, n_h, LANES), F32)],
        compiler_params=_params(("parallel", "arbitrary")),
        name="decode_moba_scores",
    )(pt, _col_bcast(q), r3(q), r3(k_new), *([cache_k] * pps))
    p_new_row = jnp.repeat(p_new[:, :, 0], HEAD_DIM, axis=-1).reshape(db, 1, w)
    return pl.pallas_call(
        functools.partial(_moba_values_kernel, pps=pps),
        grid_spec=pltpu.PrefetchScalarGridSpec(
            num_scalar_prefetch=1,
            grid=(db, n_pages // pps),
            in_specs=[pl.BlockSpec((None, pps, n_h, page), lambda b, s, pt: (b, s, 0, 0)),
                      row(w), row(w), row(w)] + _page_specs(cache_v, li, n_pages, pps, False),
            out_specs=row(w),
            scratch_shapes=[pltpu.VMEM((w, LANES), F32)],
        ),
        out_shape=jax.ShapeDtypeStruct((db, 1, w), BF16),
        compiler_params=_params(("parallel", "arbitrary")),
        name="decode_moba_values",
    )(pt, probs, p_new_row, r3(v_new), r3(sgate), *([cache_v] * pps))


PROMPT_TM = 256
ATTN_TQ = 256
ATTN_TK = 512
PAGES_PER_STEP = 4


def _rows_view(c):
    l, n, p, h, hd = c.shape
    assert p == LANES and hd == LANES
    return c.reshape(l, n, p * h, hd)


def _cols_view(c):
    l, n, p, h, hd = c.shape
    assert p == LANES and hd == HEAD_DIM
    return jnp.transpose(c, (0, 1, 3, 4, 2)).reshape(l, n, h * hd, p)


def _even_layer(x_p, x_s, caches, li, pt, g_norm, w_in, qk_g, lam_p, subln_g, w_out, layer):
    lam_init = 0.8 - 0.6 * math.exp(-0.3 * layer)
    b, s, d = x_p.shape
    db, t, _ = x_s.shape
    wa = w_out.shape[0] // 2
    wb = wa
    plan = (("q_rope_bf16", 0, wa), ("k_rope", wa, wa), ("kv_plain", 2 * wa, wa),
            ("q_plain_bf16", 3 * wa, wb), ("kv_plain", 3 * wa + wb, wb), ("kv_plain", 3 * wa + 2 * wb, wb),
            ("gate", 3 * wa + 3 * wb, wa + wb))
    w_in16 = w_in.astype(BF16)
    w_out16 = w_out.astype(BF16)
    past = pt.shape[1] * LANES

    tabs = _rope_tables(jnp.arange(s))
    qa16, ka, ka16, va, va16, qb16, kb, kb16, vb, vb16, sgate = _project(
        x_p.reshape(b * s, d), g_norm, w_in16, qk_g[0], qk_g[1], tabs, plan, PROMPT_TM, "proj_even_prompt")
    r3 = lambda a: a.reshape(b, s, a.shape[-1])
    sg3 = r3(sgate)
    mix_a = _diff_attn_prompt(r3(qa16), r3(ka16), r3(va16), sg3, lam_p, subln_g, lam_init, ATTN_TQ, ATTN_TK)
    mix_b = _sb_prompt(r3(qb16), r3(kb16), r3(vb16), sg3, wa // LANES, ATTN_TQ, ATTN_TK)
    y_p = _finish(x_p.reshape(b * s, d), [mix_a.reshape(b * s, wa), mix_b.reshape(b * s, wb)], w_out16, 512,
                  "finish_even_prompt").reshape(b, s, d)

    tabs_s = _rope_tables(jnp.tile(past + jnp.arange(t), db))
    qa16_s, ka_s, _, va_s, _, qb16_s, kb_s, _, vb_s, _, sgate_s = _project(
        x_s.reshape(db * t, d), g_norm, w_in16, qk_g[0], qk_g[1], tabs_s, plan, db * t, "proj_even_sample")
    assert t == 1
    mix_s = _decode_even(pt, lam_p, subln_g, lam_init, qa16_s, ka_s, va_s, qb16_s, sgate_s,
                         caches, li, PAGES_PER_STEP)
    y_s = _finish(x_s.reshape(db * t, d), [mix_s.reshape(db * t, wa + wb)], w_out16, db * t,
                  "finish_even_sample").reshape(db, t, d)

    n_a = wa // (2 * HEAD_DIM)
    n_b = wb // HEAD_DIM
    rows_p = (ka.reshape(b, s, n_a, 2 * HEAD_DIM), va.reshape(b, s, n_a, 2 * HEAD_DIM),
              kb.reshape(b, s, n_b, HEAD_DIM), vb.reshape(b, s, n_b, HEAD_DIM))
    rows_s = (ka_s.reshape(db, t, n_a, 2 * HEAD_DIM), va_s.reshape(db, t, n_a, 2 * HEAD_DIM),
              kb_s.reshape(db, t, n_b, HEAD_DIM), vb_s.reshape(db, t, n_b, HEAD_DIM))
    return y_p, y_s, rows_p, rows_s


def _odd_layer(x_p, x_s, cache_k, cache_v, li, pt, g_norm, w_in, qk_g, w_out):
    b, s, d = x_p.shape
    db, t, _ = x_s.shape
    wc = w_out.shape[0]
    w_in16 = w_in.astype(BF16)
    w_out16 = w_out.astype(BF16)
    past = pt.shape[1] * LANES
    assert s % MOBA_BLOCK == 0

    def plan(k_kind):
        return (("q_rope_f32", 0, wc), (k_kind, wc, wc), ("kv_plain", 2 * wc, wc), ("gate", 3 * wc, wc))

    tabs = _rope_tables(jnp.arange(s))
    q, k, k16, kmean, v, v16, sgate = _project(
        x_p.reshape(b * s, d), g_norm, w_in16, qk_g[0], qk_g[1], tabs, plan("k_rope_mean"), MOBA_BLOCK,
        "proj_odd_prompt")
    r3 = lambda a: a.reshape(b, s, a.shape[-1])
    mix = _moba_prompt(r3(q), r3(k16), r3(v16), kmean.reshape(b, s // MOBA_BLOCK, wc), r3(sgate))
    y_p = _finish(x_p.reshape(b * s, d), [mix.reshape(b * s, wc)], w_out16, 512,
                  "finish_odd_prompt").reshape(b, s, d)

    assert t == 1
    tabs_s = _rope_tables(jnp.tile(past + jnp.arange(t), db))
    q_s, k_s, _, v_s, _, sgate_s = _project(
        x_s.reshape(db * t, d), g_norm, w_in16, qk_g[0], qk_g[1], tabs_s, plan("k_rope"), db * t,
        "proj_odd_sample")
    mix_s = _decode_odd(pt, q_s, k_s, v_s, sgate_s, cache_k, cache_v, li, PAGES_PER_STEP)
    y_s = _finish(x_s.reshape(db * t, d), [mix_s.reshape(db * t, wc)], w_out16, db * t,
                  "finish_odd_sample").reshape(db, t, d)

    n_c = wc // HEAD_DIM
    return (y_p, y_s, (k.reshape(b, s, n_c, HEAD_DIM), v.reshape(b, s, n_c, HEAD_DIM)),
            (k_s.reshape(db, t, n_c, HEAD_DIM), v_s.reshape(db, t, n_c, HEAD_DIM)))


def kernel(x_prompt, x_sample, cache_a_k, cache_a_v, cache_b_k, cache_b_v, cache_c_k, cache_c_v, page_table,
           norm_even, w_in_even, qk_norm_a, lambda_a, subln_a, w_out_even, norm_odd, w_in_odd, qk_norm_c,
           w_out_odd):
    depth = norm_even.shape[0] + norm_odd.shape[0]
    even_caches = [_rows_view(cache_a_k), _rows_view(cache_a_v), _cols_view(cache_b_k), _cols_view(cache_b_v)]
    odd_k, odd_v = _cols_view(cache_c_k), _cols_view(cache_c_v)
    y_p, y_s = x_prompt, x_sample
    ev_p, ev_s, od_p, od_s = [], [], [], []
    for layer in range(depth):
        i = layer // 2
        if layer % 2 == 0:
            y_p, y_s, rp, rs = _even_layer(y_p, y_s, even_caches, i, page_table, norm_even[i], w_in_even[i],
                                           qk_norm_a[i], lambda_a[i], subln_a[i], w_out_even[i], layer)
            ev_p.append(rp)
            ev_s.append(rs)
        else:
            y_p, y_s, rp, rs = _odd_layer(y_p, y_s, odd_k, odd_v, i, page_table, norm_odd[i], w_in_odd[i],
                                          qk_norm_c[i], w_out_odd[i])
            od_p.append(rp)
            od_s.append(rs)

    def stack(rows, j):
        return jnp.stack([r[j] for r in rows])

    return (y_p, y_s,
            stack(ev_p, 0), stack(ev_p, 1), stack(ev_p, 2), stack(ev_p, 3), stack(od_p, 0), stack(od_p, 1),
            stack(ev_s, 0), stack(ev_s, 1), stack(ev_s, 2), stack(ev_s, 3), stack(od_s, 0), stack(od_s, 1))
```

```python
import functools
import math

import jax
import jax.numpy as jnp
from jax import lax
from jax.experimental import pallas as pl
from jax.experimental.pallas import tpu as pltpu

HEAD_DIM = 64
ROT_DIM = HEAD_DIM // 4
ROPE_THETA = 500000.0
RMS_EPS = 1e-6
MOBA_BLOCK = 256
MOBA_TOPK = 3
LANES = 128
NEG = -1e30
SB_DEAD_LOG = 104.0
QK_SCALE = HEAD_DIM ** -0.5
VMEM_LIMIT = 52 * 1024 * 1024

F32 = jnp.float32
BF16 = jnp.bfloat16


def _params(sem, vmem=VMEM_LIMIT):
    return pltpu.CompilerParams(dimension_semantics=sem, vmem_limit_bytes=vmem)


def _dot(a, b):
    return jnp.dot(a, b, preferred_element_type=F32)


def _dot_nt(a, b):
    return lax.dot_general(a, b, (((1,), (1,)), ((), ())), preferred_element_type=F32)


def _split_bf16(x):
    hi = x.astype(BF16)
    lo = (x - hi.astype(F32)).astype(BF16)
    return hi, lo


def _iota(shape, dim):
    return lax.broadcasted_iota(jnp.int32, shape, dim)


def _head_seg_matrix():
    return (_iota((LANES, LANES), 0) // HEAD_DIM == _iota((LANES, LANES), 1) // HEAD_DIM).astype(BF16)


def _norm_rope(x, g, c, s1, s2, seg):
    hi, lo = _split_bf16(x * x)
    ms = (_dot(hi, seg) + _dot(lo, seg)) * (1.0 / HEAD_DIM)
    xn = x * lax.rsqrt(ms + RMS_EPS) * g
    half = ROT_DIM // 2
    return xn * c + pltpu.roll(xn, LANES - half, 1) * s1 + pltpu.roll(xn, half, 1) * s2


def _proj_kernel(x_ref, gn_ref, w_ref, gq_ref, gk_ref, c_ref, s1_ref, s2_ref, *out_refs, plan):
    x = x_ref[...]
    ms = jnp.mean(x * x, axis=-1, keepdims=True)
    xn = (x * lax.rsqrt(ms + RMS_EPS) * gn_ref[...]).astype(BF16)
    seg = _head_seg_matrix()
    c, s1, s2 = c_ref[...], s1_ref[...], s2_ref[...]
    outs = list(out_refs)
    for kind, col0, width in plan:
        h = _dot(xn, w_ref[:, col0:col0 + width])
        if kind in ("q_rope_bf16", "q_rope_f32", "k_rope", "k_rope_mean"):
            g = gq_ref[...] if kind.startswith("q") else gk_ref[...]
            if kind == "q_rope_bf16":
                o_ref = outs.pop(0)
            elif kind == "q_rope_f32":
                o_ref = outs.pop(0)
            else:
                o_ref, o16_ref = outs.pop(0), outs.pop(0)
                mean_ref = outs.pop(0) if kind == "k_rope_mean" else None
            for t in range(width // LANES):
                sl = slice(t * LANES, (t + 1) * LANES)
                y = _norm_rope(h[:, sl], g, c, s1, s2, seg)
                if kind == "q_rope_bf16":
                    o_ref[:, sl] = (y * QK_SCALE).astype(BF16)
                elif kind == "q_rope_f32":
                    o_ref[:, sl] = y
                else:
                    o_ref[:, sl] = y
                    o16_ref[:, sl] = y.astype(BF16)
                    if mean_ref is not None:
                        mean_ref[:, sl] = jnp.mean(y, axis=0, keepdims=True)
        elif kind == "q_plain_bf16":
            outs.pop(0)[...] = (h * QK_SCALE).astype(BF16)
        elif kind == "kv_plain":
            outs.pop(0)[...] = h
            outs.pop(0)[...] = h.astype(BF16)
        elif kind == "gate":
            outs.pop(0)[...] = h / (1.0 + jnp.exp(-h))
        else:
            raise ValueError(kind)
    assert not outs


def _rope_tables(pos):
    half = ROT_DIM // 2
    inv = ROPE_THETA ** (-jnp.arange(0, ROT_DIM, 2, dtype=F32) / ROT_DIM)
    ang = pos.astype(F32)[:, None] * inv[None, :]
    cos, sin = jnp.cos(ang), jnp.sin(ang)
    t = pos.shape[0]
    z_half = jnp.zeros((t, half), F32)
    z_rest = jnp.zeros((t, HEAD_DIM - ROT_DIM), F32)
    c = jnp.concatenate([cos, cos, jnp.ones((t, HEAD_DIM - ROT_DIM), F32)], axis=-1)
    s1 = jnp.concatenate([-sin, z_half, z_rest], axis=-1)
    s2 = jnp.concatenate([z_half, sin, z_rest], axis=-1)
    rep = LANES // HEAD_DIM
    return tuple(jnp.tile(a, (1, rep)) for a in (c, s1, s2))


def _project(x, g_norm, w_bf16, gq, gk, tables, plan, tm, name):
    m, d = x.shape
    n_tab = tables[0].shape[0] // tm
    out_shapes, out_specs = [], []
    for kind, _, width in plan:
        row = pl.BlockSpec((tm, width), lambda i: (i, 0))
        if kind in ("q_rope_bf16", "q_plain_bf16"):
            out_shapes += [jax.ShapeDtypeStruct((m, width), BF16)]
            out_specs += [row]
        elif kind in ("q_rope_f32", "gate"):
            out_shapes += [jax.ShapeDtypeStruct((m, width), F32)]
            out_specs += [row]
        elif kind in ("k_rope", "kv_plain", "k_rope_mean"):
            out_shapes += [jax.ShapeDtypeStruct((m, width), F32), jax.ShapeDtypeStruct((m, width), BF16)]
            out_specs += [row, row]
            if kind == "k_rope_mean":
                assert tm == MOBA_BLOCK
                out_shapes += [jax.ShapeDtypeStruct((m // tm, 1, width), F32)]
                out_specs += [pl.BlockSpec((None, 1, width), lambda i: (i, 0, 0))]
    tab_spec = pl.BlockSpec((tm, LANES), lambda i: (i % n_tab, 0))
    vec_d = pl.BlockSpec((1, d), lambda i: (0, 0))
    vec_l = pl.BlockSpec((1, LANES), lambda i: (0, 0))
    rep = LANES // HEAD_DIM
    return pl.pallas_call(
        functools.partial(_proj_kernel, plan=plan),
        grid=(m // tm,),
        in_specs=[pl.BlockSpec((tm, d), lambda i: (i, 0)), vec_d,
                  pl.BlockSpec(w_bf16.shape, lambda i: (0, 0), pipeline_mode=pl.Buffered(1)),
                  vec_l, vec_l, tab_spec, tab_spec, tab_spec],
        out_specs=out_specs,
        out_shape=out_shapes,
        compiler_params=_params(("parallel",)),
        name=name,
    )(x, g_norm.reshape(1, d), w_bf16, jnp.tile(gq, rep).reshape(1, LANES), jnp.tile(gk, rep).reshape(1, LANES),
      *tables)


def _finish_kernel(x_ref, w_ref, *refs):
    mix_refs, y_ref = refs[:-1], refs[-1]
    y = x_ref[...]
    row0 = 0
    for mix_ref in mix_refs:
        rows = mix_ref.shape[1]
        y = y + _dot(mix_ref[...], w_ref[row0:row0 + rows, :])
        row0 += rows
    y_ref[...] = y


def _finish(x, mix_parts, w_bf16, tm, name):
    m, d = x.shape
    kdim = w_bf16.shape[0]
    assert sum(p.shape[1] for p in mix_parts) == kdim
    return pl.pallas_call(
        _finish_kernel,
        grid=(m // tm,),
        in_specs=[pl.BlockSpec((tm, d), lambda i: (i, 0)), pl.BlockSpec((kdim, d), lambda i: (0, 0))]
        + [pl.BlockSpec((tm, p.shape[1]), lambda i: (i, 0)) for p in mix_parts],
        out_specs=pl.BlockSpec((tm, d), lambda i: (i, 0)),
        out_shape=jax.ShapeDtypeStruct((m, d), F32),
        compiler_params=_params(("parallel",)),
        name=name,
    )(x, w_bf16, *mix_parts)


def _lambda_value(lp, lam_init):
    a = jnp.sum(lp[0:1] * lp[1:2], axis=-1, keepdims=True)
    b = jnp.sum(lp[2:3] * lp[3:4], axis=-1, keepdims=True)
    return jnp.exp(a) - jnp.exp(b) + lam_init


def _softmax_tile(qq, k, v, carry, mask):
    m, l, acc = carry
    s = _dot_nt(qq, k)
    if mask is not None:
        s = jnp.where(mask, s, NEG)
    m_new = jnp.maximum(m, jnp.max(s, axis=-1, keepdims=True))
    alpha = jnp.exp(m - m_new)
    p = jnp.exp(s - m_new)
    l = alpha * l + jnp.sum(p, axis=-1, keepdims=True)
    acc = alpha * acc + _dot(p.astype(BF16), v)
    return m_new, l, acc


def _dot_tn(a, b):
    return lax.dot_general(a, b, (((0,), (0,)), ((), ())), preferred_element_type=F32)


def _softmax_tile_t(qq_t, k, v, carry, mask):
    m, l, acc = carry
    s = _dot(k, qq_t)
    if isinstance(mask, (list, tuple)):
        slab = s.shape[0] // len(mask)
        s = jnp.concatenate([jnp.where(mk, s[c * slab:(c + 1) * slab], NEG) for c, mk in enumerate(mask)], axis=0)
    elif mask is not None:
        s = jnp.where(mask, s, NEG)
    m_new = jnp.maximum(m, jnp.max(s, axis=0, keepdims=True))
    alpha = jnp.exp(m - m_new)
    p = jnp.exp(s - m_new)
    l = alpha * l + jnp.sum(p, axis=0, keepdims=True)
    acc = alpha * acc + _dot_tn(v, p.astype(BF16))
    return m_new, l, acc


def _stack_pair_t(q_t):
    row = _iota(q_t.shape, 0)
    zero = jnp.zeros_like(q_t)
    return jnp.concatenate([jnp.where(row < HEAD_DIM, q_t, zero), jnp.where(row >= HEAD_DIM, q_t, zero)], axis=1)


def _diff_attn_kernel(lam_ref, subg_ref, q_ref, k_ref, v_ref, sg_ref, o_ref, *, tq, tk, lam_init):
    i = pl.program_id(2)
    r = 2 * tq
    qq_t = _stack_pair_t(jnp.transpose(q_ref[...].astype(F32))).astype(BF16)
    n_full = (i * tq) // tk

    def full_tile(j, carry):
        start = pl.multiple_of(j * tk, tk)
        return _softmax_tile_t(qq_t, k_ref[pl.ds(start, tk), :], v_ref[pl.ds(start, tk), :], carry, None)

    carry = (jnp.full((1, r), NEG, F32), jnp.zeros((1, r), F32), jnp.zeros((LANES, r), F32))
    carry = lax.fori_loop(0, n_full, full_tile, carry)
    start = pl.multiple_of(n_full * tk, tk)
    q_pos = i * tq + _iota((tk, r), 1) % tq
    k_pos = start + _iota((tk, r), 0)
    _, l, acc = _softmax_tile_t(qq_t, k_ref[pl.ds(start, tk), :], v_ref[pl.ds(start, tk), :], carry,
                                k_pos <= q_pos)

    a = acc / l
    lam = _lambda_value(lam_ref[...], lam_init)
    o = jnp.transpose(a[:, :tq] - lam * a[:, tq:])
    ms = jnp.mean(o * o, axis=-1, keepdims=True)
    o = o * lax.rsqrt(ms + RMS_EPS) * subg_ref[...] * (1.0 - lam_init)
    o_ref[...] = (o * sg_ref[...]).astype(BF16)


def _sb_weights(z, tri, c, valid, keys_axis=1):
    sp = jnp.maximum(z, 0.0) + jnp.log(1.0 + jnp.exp(-jnp.abs(z)))
    log_beta = z - sp
    log_keep = -sp
    if valid is not None:
        log_keep = jnp.where(valid, log_keep, 0.0)
    hi, lo = _split_bf16(log_keep)
    n = z.shape[1 - keys_axis]
    if keys_axis == 1:
        both = _dot(jnp.concatenate([hi, lo], axis=0), tri)
        later = both[:n] + both[n:]
        whole = later[:, 0:1] + log_keep[:, 0:1]
    else:
        both = _dot(tri, jnp.concatenate([hi, lo], axis=1))
        later = both[:, :n] + both[:, n:]
        whole = later[0:1, :] + log_keep[0:1, :]
    w = jnp.exp(log_beta + later + c)
    if valid is not None:
        w = jnp.where(valid, w, 0.0)
    return w, c + whole


def _later_matrix(tk, keys_axis=1):
    a, b = _iota((tk, tk), 0), _iota((tk, tk), 1)
    return (a > b if keys_axis == 1 else b > a).astype(BF16)


def _sb_kernel(q_ref, k_ref, v_ref, sg_ref, o_ref, *, tq, tk):
    i = pl.program_id(2)
    r = 2 * tq
    qq_t = _stack_pair_t(jnp.transpose(q_ref[...].astype(F32))).astype(BF16)
    tri = _later_matrix(tk, keys_axis=0)
    n_full = (i * tq) // tk

    def tile(st, c, acc, valid):
        w, c = _sb_weights(_dot(k_ref[pl.ds(st, tk), :], qq_t), tri, c, valid, keys_axis=0)
        return c, acc + _dot_tn(v_ref[pl.ds(st, tk), :], w.astype(BF16))

    start = pl.multiple_of(n_full * tk, tk)
    q_pos = i * tq + _iota((tk, r), 1) % tq
    k_pos = start + _iota((tk, r), 0)
    c, acc = tile(start, jnp.zeros((1, r), F32), jnp.zeros((LANES, r), F32), k_pos < q_pos)

    def live(state):
        t, c_max, _, _ = state
        return jnp.logical_and(t < n_full, c_max > -SB_DEAD_LOG)

    def full_tile(state):
        t, _, c, acc = state
        c, acc = tile(pl.multiple_of((n_full - 1 - t) * tk, tk), c, acc, None)
        return t + 1, jnp.max(c), c, acc

    _, _, _, acc = lax.while_loop(live, full_tile, (jnp.int32(0), jnp.max(c), c, acc))
    row = _iota((LANES, tq), 0)
    o = jnp.transpose(jnp.where(row < HEAD_DIM, acc[:, :tq], acc[:, tq:]))
    o_ref[...] = (o * sg_ref[...]).astype(BF16)


def _top_blocks(g, n_valid, axis=-1):
    axis = axis % g.ndim
    blk = _iota(g.shape, axis)
    nb = g.shape[axis]
    g = jnp.where(blk < n_valid, g, -jnp.inf)
    sel = jnp.zeros(g.shape, jnp.bool_)
    for _ in range(min(MOBA_TOPK, nb)):
        mx = jnp.max(g, axis=axis, keepdims=True)
        idx = jnp.min(jnp.where(g == mx, blk, nb), axis=axis, keepdims=True)
        pick = blk == idx
        sel = jnp.logical_or(sel, pick)
        g = jnp.where(pick, -jnp.inf, g)
    return jnp.logical_and(sel, blk < n_valid)


def _moba_kernel(q_ref, k_ref, v_ref, km_ref, sg_ref, o_ref, sel_s, *, tq, group):
    i = pl.program_id(2)
    r = 2 * tq
    qf_t = _stack_pair_t(jnp.transpose(q_ref[...]))
    row = _iota((LANES, tq), 0)
    q_hi, q_lo = _split_bf16(qf_t)
    qq_t = (qf_t * QK_SCALE).astype(BF16)

    km_hi, km_lo = _split_bf16(km_ref[...])
    g = _dot(km_hi, q_hi) + _dot(km_hi, q_lo) + _dot(km_lo, q_hi)
    sel_s[...] = jnp.where(_top_blocks(g, i, axis=0), 1.0, 0.0)

    start = pl.multiple_of(i * tq, tq)
    causal = _iota((tq, r), 0) <= _iota((tq, r), 1) % tq
    carry = (jnp.full((1, r), NEG, F32), jnp.zeros((1, r), F32), jnp.zeros((LANES, r), F32))
    carry = _softmax_tile_t(qq_t, k_ref[pl.ds(start, tq), :], v_ref[pl.ds(start, tq), :], carry, causal)

    def past_group(t, carry):
        st = pl.multiple_of(t * group * tq, group * tq)
        chosen = [sel_s[pl.ds(t * group + c, 1), :] > 0.0 for c in range(group)]
        return _softmax_tile_t(qq_t, k_ref[pl.ds(st, group * tq), :], v_ref[pl.ds(st, group * tq), :], carry,
                               chosen)

    _, l, acc = lax.fori_loop(0, (i + group - 1) // group, past_group, carry)
    o_t = acc / l
    o = jnp.transpose(jnp.where(row < HEAD_DIM, o_t[:, :tq], o_t[:, tq:]))
    o_ref[...] = (o * sg_ref[...]).astype(BF16)


def _attn_specs(tq, s, group0=0):
    q_spec = pl.BlockSpec((None, tq, LANES), lambda b, h, i: (b, i, h + group0))
    kv_spec = pl.BlockSpec((None, s, LANES), lambda b, h, i: (b, 0, h))
    return q_spec, kv_spec


def _diff_attn_prompt(q16, k16, v16, sgate, lam_p, subln_g, lam_init, tq, tk):
    b, s, w = q16.shape
    q_spec, kv_spec = _attn_specs(tq, s)
    return pl.pallas_call(
        functools.partial(_diff_attn_kernel, tq=tq, tk=tk, lam_init=lam_init),
        grid=(b, w // LANES, s // tq),
        in_specs=[pl.BlockSpec(lam_p.shape, lambda b, h, i: (0, 0)),
                  pl.BlockSpec((1, LANES), lambda b, h, i: (0, 0)),
                  q_spec, kv_spec, kv_spec, q_spec],
        out_specs=q_spec,
        out_shape=jax.ShapeDtypeStruct((b, s, w), BF16),
        compiler_params=_params(("parallel", "parallel", "arbitrary")),
        name="diff_attn_prompt",
    )(lam_p, subln_g.reshape(1, LANES), q16, k16, v16, sgate)


def _sb_prompt(q16, k16, v16, sgate, gate_group0, tq, tk):
    b, s, w = q16.shape
    q_spec, kv_spec = _attn_specs(tq, s)
    sg_spec = pl.BlockSpec((None, tq, LANES), lambda b, h, i: (b, i, h + gate_group0))
    return pl.pallas_call(
        functools.partial(_sb_kernel, tq=tq, tk=tk),
        grid=(b, w // LANES, s // tq),
        in_specs=[q_spec, kv_spec, kv_spec, sg_spec],
        out_specs=q_spec,
        out_shape=jax.ShapeDtypeStruct((b, s, w), BF16),
        compiler_params=_params(("parallel", "parallel", "arbitrary")),
        name="sb_prompt",
    )(q16, k16, v16, sgate)


def _moba_prompt(q, k16, v16, kmean, sgate):
    b, s, w = q.shape
    tq = MOBA_BLOCK
    nb = s // tq
    q_spec, kv_spec = _attn_specs(tq, s)
    group = math.gcd(nb, MOBA_GROUP)
    return pl.pallas_call(
        functools.partial(_moba_kernel, tq=tq, group=group),
        grid=(b, w // LANES, nb),
        in_specs=[q_spec, kv_spec, kv_spec,
                  pl.BlockSpec((None, nb, LANES), lambda b, h, i: (b, 0, h)), q_spec],
        out_specs=q_spec,
        out_shape=jax.ShapeDtypeStruct((b, s, w), BF16),
        scratch_shapes=[pltpu.VMEM((nb, 2 * tq), F32)],
        compiler_params=_params(("parallel", "parallel", "arbitrary")),
        name="moba_prompt",
    )(q, k16, v16, kmean, sgate)


def _row_heads(x_row, n_rows, lanes_per_row):
    w = x_row.shape[1]
    keep = _iota((n_rows, w), 1) // lanes_per_row == _iota((n_rows, w), 0)
    return jnp.where(keep, jnp.broadcast_to(x_row, (n_rows, w)), 0.0)


def _dup_rows(x, reps):
    n, w = x.shape
    row = _iota((n * reps, w), 0) // reps
    out = jnp.zeros((n * reps, w), x.dtype)
    for h in range(n):
        out = jnp.where(row == h, x[h:h + 1, :], out)
    return out


def _stack_rows(rows):
    n, w = len(rows), rows[0].shape[1]
    row = _iota((n, w), 0)
    out = jnp.zeros((n, w), rows[0].dtype)
    for h, r in enumerate(rows):
        out = jnp.where(row == h, r, out)
    return out


def _head_dots(qcol_ref, kt_ref, n_heads):
    rows = []
    for h in range(n_heads):
        sl = slice(h * HEAD_DIM, (h + 1) * HEAD_DIM)
        rows.append(jnp.sum(qcol_ref[sl, :] * kt_ref[sl, :], axis=0, keepdims=True))
    return _stack_rows(rows)


def _add_weighted(acc_ref, w, vt_refs, n_heads):
    for h in range(n_heads):
        sl = slice(h * HEAD_DIM, (h + 1) * HEAD_DIM)
        a = acc_ref[sl, :]
        for u, vt_ref in enumerate(vt_refs):
            a = a + w[u * n_heads + h:u * n_heads + h + 1, :] * vt_ref[sl, :]
        acc_ref[sl, :] = a


def _sb_weights_pages(zs, tri, c):
    n_h = zs[0].shape[0]
    z = jnp.concatenate(zs, axis=0)
    sp = jnp.maximum(z, 0.0) + jnp.log(1.0 + jnp.exp(-jnp.abs(z)))
    log_keep = -sp
    hi, lo = _split_bf16(log_keep)
    rows = z.shape[0]
    both = _dot(jnp.concatenate([hi, lo], axis=0), tri)
    later = both[:rows] + both[rows:]
    whole = later[:, 0:1] + log_keep[:, 0:1]
    offsets = []
    for u in range(len(zs)):
        offsets.append(c)
        c = c + whole[u * n_h:(u + 1) * n_h]
    return jnp.exp(z - sp + later + jnp.concatenate(offsets, axis=0)), c


def _lane_sums_as_row(x):
    hi, lo = _split_bf16(x)
    ones = jnp.ones((8, LANES), BF16)
    return (_dot_nt(ones, hi) + _dot_nt(ones, lo))[0:1, :]


def _decode_even_kernel(pt_ref, lam_ref, subg_ref, qa_ref, kan_ref, van_ref, qb_ref, sg_ref, *refs,
                        pps, lam_init):
    del pt_ref
    page_refs = refs[:4 * pps]
    o_ref = refs[4 * pps]
    qa_s, m_s, l_s, acca_s, c_s, accb_s = refs[4 * pps + 1:]
    step = pl.program_id(1)
    n_ha = qa_ref.shape[0]
    n_a = 2 * n_ha
    n_b = accb_s.shape[0] // HEAD_DIM
    wa = n_ha * LANES

    @pl.when(step == 0)
    def _():
        rows = _dup_rows(qa_ref[...].astype(F32), 2)
        qa = jnp.where(_iota(rows.shape, 1) // HEAD_DIM == _iota(rows.shape, 0) % 2, rows, 0.0)
        qa_s[...] = qa
        m_s[...] = jnp.sum(qa * _dup_rows(kan_ref[...], 2), axis=-1, keepdims=True)
        l_s[...] = jnp.ones_like(l_s)
        acca_s[...] = _dup_rows(van_ref[...], 2)
        c_s[...] = jnp.zeros_like(c_s)
        accb_s[...] = jnp.zeros_like(accb_s)

    rows_a = pps * page_refs[0].shape[0]
    own_head = _iota((n_a, rows_a), 1) % n_ha == _iota((n_a, rows_a), 0) // 2
    ak = jnp.concatenate([page_refs[4 * u][...].astype(BF16) for u in range(pps)], axis=0)
    av = jnp.concatenate([page_refs[4 * u + 1][...].astype(BF16) for u in range(pps)], axis=0)
    m_s[...], l_s[...], acca_s[...] = _softmax_tile(qa_s[...].astype(BF16), ak, av,
                                                    (m_s[...], l_s[...], acca_s[...]), own_head)
    zs = [_head_dots(qb_ref, page_refs[4 * u + 2], n_b) for u in range(pps)]
    w, c_s[...] = _sb_weights_pages(zs, _later_matrix(LANES), c_s[...])
    _add_weighted(accb_s, w, [page_refs[4 * u + 3] for u in range(pps)], n_b)

    @pl.when(step == pl.num_programs(1) - 1)
    def _():
        lam = _lambda_value(lam_ref[...], lam_init)
        a = acca_s[...] / l_s[...]
        for h in range(n_ha):
            sl = slice(h * LANES, (h + 1) * LANES)
            x = a[2 * h:2 * h + 1, :] - lam * a[2 * h + 1:2 * h + 2, :]
            ms = jnp.mean(x * x, axis=-1, keepdims=True)
            x = x * lax.rsqrt(ms + RMS_EPS) * subg_ref[...] * (1.0 - lam_init)
            o_ref[:, sl] = (x * sg_ref[:, sl]).astype(BF16)
        o_ref[:, wa:] = (_lane_sums_as_row(accb_s[...]) * sg_ref[:, wa:]).astype(BF16)


def _page_specs(cache, li, n_pages, pps, reverse):
    r = cache.shape[2]

    def make(u):
        def index(b, s, pt):
            j = s * pps + u
            j = n_pages - 1 - j if reverse else j
            return (li, pt[b, j], 0, 0)
        return pl.BlockSpec((None, None, r, LANES), index)

    return [make(u) for u in range(pps)]


def _col_bcast(x):
    return jnp.broadcast_to(x.astype(F32)[:, :, None], x.shape + (LANES,))


def _decode_even(pt, lam_p, subln_g, lam_init, qa16, ka, va, qb16, sgate, caches, li, pps):
    db, n_pages = pt.shape
    wa, wb = qa16.shape[-1], qb16.shape[-1]
    n_ha = wa // LANES
    whole = lambda shp: pl.BlockSpec((None,) + shp, lambda b, s, pt: (b,) + (0,) * len(shp))
    specs_by_cache = [_page_specs(c, li, n_pages, pps, True) for c in caches]
    page_specs = [specs_by_cache[t][u] for u in range(pps) for t in range(4)]
    page_args = [caches[t] for u in range(pps) for t in range(4)]
    grid_spec = pltpu.PrefetchScalarGridSpec(
        num_scalar_prefetch=1,
        grid=(db, n_pages // pps),
        in_specs=[pl.BlockSpec(lam_p.shape, lambda b, s, pt: (0, 0)),
                  pl.BlockSpec((1, LANES), lambda b, s, pt: (0, 0)),
                  whole((n_ha, LANES)), whole((n_ha, LANES)), whole((n_ha, LANES)), whole((wb, LANES)),
                  whole((1, wa + wb))] + page_specs,
        out_specs=whole((1, wa + wb)),
        scratch_shapes=[pltpu.VMEM((2 * n_ha, LANES), F32),
                        pltpu.VMEM((2 * n_ha, 1), F32), pltpu.VMEM((2 * n_ha, 1), F32),
                        pltpu.VMEM((2 * n_ha, LANES), F32),
                        pltpu.VMEM((wb // HEAD_DIM, 1), F32), pltpu.VMEM((wb, LANES), F32)],
    )
    heads = lambda x: x.reshape(db, n_ha, LANES)
    return pl.pallas_call(
        functools.partial(_decode_even_kernel, pps=pps, lam_init=lam_init),
        grid_spec=grid_spec,
        out_shape=jax.ShapeDtypeStruct((db, 1, wa + wb), BF16),
        compiler_params=_params(("parallel", "arbitrary")),
        name="decode_even",
    )(pt, lam_p, subln_g.reshape(1, LANES), heads(qa16), heads(ka), heads(va), _col_bcast(qb16),
      sgate.reshape(db, 1, wa + wb), *page_args)


def _moba_scores_kernel(pt_ref, qcol_ref, qrow_ref, kn_ref, *refs, pps, n_blocks):
    del pt_ref
    page_refs = refs[:pps]
    p_ref, pnew_ref = refs[pps], refs[pps + 1]
    s_s = refs[pps + 2]
    step = pl.program_id(1)
    n_h = qcol_ref.shape[0] // HEAD_DIM
    ppb = MOBA_BLOCK // LANES

    for u in range(pps):
        s_s[step * pps + u] = _head_dots(qcol_ref, page_refs[u], n_h)

    @pl.when(step == pl.num_programs(1) - 1)
    def _():
        lane = _iota((n_h, LANES), 1)
        g = jnp.zeros((n_h, LANES), F32)
        for n in range(n_blocks):
            tot = sum(s_s[n * ppb + t] for t in range(ppb))
            g = jnp.where(lane == n, jnp.sum(tot, axis=-1, keepdims=True) * (1.0 / MOBA_BLOCK), g)
        sel = _top_blocks(g, n_blocks)
        chosen = [sel[:, n:n + 1] for n in range(n_blocks)]
        s_new = jnp.sum(_row_heads(qrow_ref[...], n_h, HEAD_DIM) * kn_ref[...], axis=-1, keepdims=True) * QK_SCALE
        m_lanes = jnp.full((n_h, LANES), NEG, F32)
        for j in range(n_blocks * ppb):
            m_lanes = jnp.maximum(m_lanes, jnp.where(chosen[j // ppb], s_s[j] * QK_SCALE, NEG))
        m = jnp.maximum(jnp.max(m_lanes, axis=-1, keepdims=True), s_new)
        l_lanes = jnp.zeros((n_h, LANES), F32)
        for j in range(n_blocks * ppb):
            e = jnp.where(chosen[j // ppb], jnp.exp(s_s[j] * QK_SCALE - m), 0.0)
            s_s[j] = e
            l_lanes = l_lanes + e
        e_new = jnp.exp(s_new - m)
        inv = 1.0 / (jnp.sum(l_lanes, axis=-1, keepdims=True) + e_new)
        for j in range(n_blocks * ppb):
            p_ref[j] = s_s[j] * inv
        pnew_ref[...] = jnp.broadcast_to(e_new * inv, pnew_ref.shape)


def _moba_values_kernel(pt_ref, p_ref, pnew_ref, vn_ref, sg_ref, *refs, pps):
    del pt_ref
    page_refs = refs[:pps]
    o_ref = refs[pps]
    acc_s = refs[pps + 1]
    step = pl.program_id(1)
    n_h = acc_s.shape[0] // HEAD_DIM

    @pl.when(step == 0)
    def _():
        acc_s[...] = jnp.zeros_like(acc_s)

    w = jnp.concatenate([p_ref[u] for u in range(pps)], axis=0)
    _add_weighted(acc_s, w, list(page_refs), n_h)

    @pl.when(step == pl.num_programs(1) - 1)
    def _():
        o = _lane_sums_as_row(acc_s[...]) + pnew_ref[...] * vn_ref[...]
        o_ref[...] = (o * sg_ref[...]).astype(BF16)


def _decode_odd(pt, q, k_new, v_new, sgate, cache_k, cache_v, li, pps):
    db, n_pages = pt.shape
    w = q.shape[-1]
    n_h = w // HEAD_DIM
    page = LANES
    past = n_pages * page
    assert past % MOBA_BLOCK == 0 and past >= MOBA_BLOCK and MOBA_BLOCK % page == 0
    n_blocks = past // MOBA_BLOCK
    row = lambda wd: pl.BlockSpec((None, 1, wd), lambda b, s, pt: (b, 0, 0))
    r3 = lambda x: x.reshape(db, 1, x.shape[-1])
    probs, p_new = pl.pallas_call(
        functools.partial(_moba_scores_kernel, pps=pps, n_blocks=n_blocks),
        grid_spec=pltpu.PrefetchScalarGridSpec(
            num_scalar_prefetch=1,
            grid=(db, n_pages // pps),
            in_specs=[pl.BlockSpec((None, w, LANES), lambda b, s, pt: (b, 0, 0)), row(w), row(w)]
            + _page_specs(cache_k, li, n_pages, pps, False),
            out_specs=[pl.BlockSpec((None, n_pages, n_h, page), lambda b, s, pt: (b, 0, 0, 0)),
                       pl.BlockSpec((None, n_h, LANES), lambda b, s, pt: (b, 0, 0))],
            scratch_shapes=[pltpu.VMEM((n_pages, n_h, page), F32)],
        ),
        out_shape=[jax.ShapeDtypeStruct((db, n_pages, n_h, page), F32),
                   jax.ShapeDtypeStruct((db, n_h, LANES), F32)],
        compiler_params=_params(("parallel", "arbitrary")),
        name="decode_moba_scores",
    )(pt, _col_bcast(q), r3(q), r3(k_new), *([cache_k] * pps))
    p_new_row = jnp.repeat(p_new[:, :, 0], HEAD_DIM, axis=-1).reshape(db, 1, w)
    return pl.pallas_call(
        functools.partial(_moba_values_kernel, pps=pps),
        grid_spec=pltpu.PrefetchScalarGridSpec(
            num_scalar_prefetch=1,
            grid=(db, n_pages // pps),
            in_specs=[pl.BlockSpec((None, pps, n_h, page), lambda b, s, pt: (b, s, 0, 0)),
                      row(w), row(w), row(w)] + _page_specs(cache_v, li, n_pages, pps, False),
            out_specs=row(w),
            scratch_shapes=[pltpu.VMEM((w, LANES), F32)],
        ),
        out_shape=jax.ShapeDtypeStruct((db, 1, w), BF16),
        compiler_params=_params(("parallel", "arbitrary")),
        name="decode_moba_values",
    )(pt, probs, p_new_row, r3(v_new), r3(sgate), *([cache_v] * pps))


PROMPT_TM = 256
ATTN_TQ = 512
ATTN_TK = 512
MOBA_GROUP = 4
PAGES_PER_STEP = 8


def _rows_view(c):
    l, n, p, h, hd = c.shape
    assert p == LANES and hd == LANES
    return c.reshape(l, n, p * h, hd)


def _cols_view(c):
    l, n, p, h, hd = c.shape
    assert p == LANES and hd == HEAD_DIM
    return jnp.transpose(c, (0, 1, 3, 4, 2)).reshape(l, n, h * hd, p)


def _even_layer(x_p, x_s, caches, li, pt, g_norm, w_in, qk_g, lam_p, subln_g, w_out, layer):
    lam_init = 0.8 - 0.6 * math.exp(-0.3 * layer)
    b, s, d = x_p.shape
    db, t, _ = x_s.shape
    wa = w_out.shape[0] // 2
    wb = wa
    plan = (("q_rope_bf16", 0, wa), ("k_rope", wa, wa), ("kv_plain", 2 * wa, wa),
            ("q_plain_bf16", 3 * wa, wb), ("kv_plain", 3 * wa + wb, wb), ("kv_plain", 3 * wa + 2 * wb, wb),
            ("gate", 3 * wa + 3 * wb, wa + wb))
    w_in16 = w_in.astype(BF16)
    w_out16 = w_out.astype(BF16)
    past = pt.shape[1] * LANES

    tabs = _rope_tables(jnp.arange(s))
    qa16, ka, ka16, va, va16, qb16, kb, kb16, vb, vb16, sgate = _project(
        x_p.reshape(b * s, d), g_norm, w_in16, qk_g[0], qk_g[1], tabs, plan, PROMPT_TM, "proj_even_prompt")
    r3 = lambda a: a.reshape(b, s, a.shape[-1])
    sg3 = r3(sgate)
    mix_a = _diff_attn_prompt(r3(qa16), r3(ka16), r3(va16), sg3, lam_p, subln_g, lam_init, ATTN_TQ, ATTN_TK)
    mix_b = _sb_prompt(r3(qb16), r3(kb16), r3(vb16), sg3, wa // LANES, ATTN_TQ, ATTN_TK)
    y_p = _finish(x_p.reshape(b * s, d), [mix_a.reshape(b * s, wa), mix_b.reshape(b * s, wb)], w_out16, 512,
                  "finish_even_prompt").reshape(b, s, d)

    tabs_s = _rope_tables(jnp.tile(past + jnp.arange(t), db))
    qa16_s, ka_s, _, va_s, _, qb16_s, kb_s, _, vb_s, _, sgate_s = _project(
        x_s.reshape(db * t, d), g_norm, w_in16, qk_g[0], qk_g[1], tabs_s, plan, db * t, "proj_even_sample")
    assert t == 1
    mix_s = _decode_even(pt, lam_p, subln_g, lam_init, qa16_s, ka_s, va_s, qb16_s, sgate_s,
                         caches, li, PAGES_PER_STEP)
    y_s = _finish(x_s.reshape(db * t, d), [mix_s.reshape(db * t, wa + wb)], w_out16, db * t,
                  "finish_even_sample").reshape(db, t, d)

    n_a = wa // (2 * HEAD_DIM)
    n_b = wb // HEAD_DIM
    rows_p = (ka.reshape(b, s, n_a, 2 * HEAD_DIM), va.reshape(b, s, n_a, 2 * HEAD_DIM),
              kb.reshape(b, s, n_b, HEAD_DIM), vb.reshape(b, s, n_b, HEAD_DIM))
    rows_s = (ka_s.reshape(db, t, n_a, 2 * HEAD_DIM), va_s.reshape(db, t, n_a, 2 * HEAD_DIM),
              kb_s.reshape(db, t, n_b, HEAD_DIM), vb_s.reshape(db, t, n_b, HEAD_DIM))
    return y_p, y_s, rows_p, rows_s


def _odd_layer(x_p, x_s, cache_k, cache_v, li, pt, g_norm, w_in, qk_g, w_out):
    b, s, d = x_p.shape
    db, t, _ = x_s.shape
    wc = w_out.shape[0]
    w_in16 = w_in.astype(BF16)
    w_out16 = w_out.astype(BF16)
    past = pt.shape[1] * LANES
    assert s % MOBA_BLOCK == 0

    def plan(k_kind):
        return (("q_rope_f32", 0, wc), (k_kind, wc, wc), ("kv_plain", 2 * wc, wc), ("gate", 3 * wc, wc))

    tabs = _rope_tables(jnp.arange(s))
    q, k, k16, kmean, v, v16, sgate = _project(
        x_p.reshape(b * s, d), g_norm, w_in16, qk_g[0], qk_g[1], tabs, plan("k_rope_mean"), MOBA_BLOCK,
        "proj_odd_prompt")
    r3 = lambda a: a.reshape(b, s, a.shape[-1])
    mix = _moba_prompt(r3(q), r3(k16), r3(v16), kmean.reshape(b, s // MOBA_BLOCK, wc), r3(sgate))
    y_p = _finish(x_p.reshape(b * s, d), [mix.reshape(b * s, wc)], w_out16, 512,
                  "finish_odd_prompt").reshape(b, s, d)

    assert t == 1
    tabs_s = _rope_tables(jnp.tile(past + jnp.arange(t), db))
    q_s, k_s, _, v_s, _, sgate_s = _project(
        x_s.reshape(db * t, d), g_norm, w_in16, qk_g[0], qk_g[1], tabs_s, plan("k_rope"), db * t,
        "proj_odd_sample")
    mix_s = _decode_odd(pt, q_s, k_s, v_s, sgate_s, cache_k, cache_v, li, PAGES_PER_STEP)
    y_s = _finish(x_s.reshape(db * t, d), [mix_s.reshape(db * t, wc)], w_out16, db * t,
                  "finish_odd_sample").reshape(db, t, d)

    n_c = wc // HEAD_DIM
    return (y_p, y_s, (k.reshape(b, s, n_c, HEAD_DIM), v.reshape(b, s, n_c, HEAD_DIM)),
            (k_s.reshape(db, t, n_c, HEAD_DIM), v_s.reshape(db, t, n_c, HEAD_DIM)))


def kernel(x_prompt, x_sample, cache_a_k, cache_a_v, cache_b_k, cache_b_v, cache_c_k, cache_c_v, page_table,
           norm_even, w_in_even, qk_norm_a, lambda_a, subln_a, w_out_even, norm_odd, w_in_odd, qk_norm_c,
           w_out_odd):
    depth = norm_even.shape[0] + norm_odd.shape[0]
    even_caches = [_rows_view(cache_a_k), _rows_view(cache_a_v), _cols_view(cache_b_k), _cols_view(cache_b_v)]
    odd_k, odd_v = _cols_view(cache_c_k), _cols_view(cache_c_v)
    y_p, y_s = x_prompt, x_sample
    ev_p, ev_s, od_p, od_s = [], [], [], []
    for layer in range(depth):
        i = layer // 2
        if layer % 2 == 0:
            y_p, y_s, rp, rs = _even_layer(y_p, y_s, even_caches, i, page_table, norm_even[i], w_in_even[i],
                                           qk_norm_a[i], lambda_a[i], subln_a[i], w_out_even[i], layer)
            ev_p.append(rp)
            ev_s.append(rs)
        else:
            y_p, y_s, rp, rs = _odd_layer(y_p, y_s, odd_k, odd_v, i, page_table, norm_odd[i], w_in_odd[i],
                                          qk_norm_c[i], w_out_odd[i])
            od_p.append(rp)
            od_s.append(rs)

    def stack(rows, j):
        return jnp.stack([r[j] for r in rows])

    return (y_p, y_s,
            stack(ev_p, 0), stack(ev_p, 1), stack(ev_p, 2), stack(ev_p, 3), stack(od_p, 0), stack(od_p, 1),
            stack(ev_s, 0), stack(ev_s, 1), stack(ev_s, 2), stack(ev_s, 3), stack(od_s, 0), stack(od_s, 1))
```

```python
import functools
import math

import jax
import jax.numpy as jnp
from jax import lax
from jax.experimental import pallas as pl
from jax.experimental.pallas import tpu as pltpu

HEAD_DIM = 64
ROT_DIM = HEAD_DIM // 4
ROPE_THETA = 500000.0
RMS_EPS = 1e-6
MOBA_BLOCK = 256
MOBA_TOPK = 3
LANES = 128
NEG = -1e30
SB_DEAD_LOG2 = 150.0
QK_SCALE = HEAD_DIM ** -0.5
Q_SOFTMAX_SCALE = QK_SCALE * math.log2(math.e)
ATTN_SLAB = 256
VMEM_LIMIT = 52 * 1024 * 1024

F32 = jnp.float32
BF16 = jnp.bfloat16


def _params(sem, vmem=VMEM_LIMIT):
    return pltpu.CompilerParams(dimension_semantics=sem, vmem_limit_bytes=vmem)


def _dot(a, b):
    return jnp.dot(a, b, preferred_element_type=F32)


def _dot_nt(a, b):
    return lax.dot_general(a, b, (((1,), (1,)), ((), ())), preferred_element_type=F32)


def _split_bf16(x):
    hi = x.astype(BF16)
    lo = (x - hi.astype(F32)).astype(BF16)
    return hi, lo


def _iota(shape, dim):
    return lax.broadcasted_iota(jnp.int32, shape, dim)


def _head_seg_matrix():
    return (_iota((LANES, LANES), 0) // HEAD_DIM == _iota((LANES, LANES), 1) // HEAD_DIM).astype(BF16)


def _norm_rope(x, g, c, s1, s2, seg):
    hi, lo = _split_bf16(x * x)
    ms = (_dot(hi, seg) + _dot(lo, seg)) * (1.0 / HEAD_DIM)
    xn = x * lax.rsqrt(ms + RMS_EPS) * g
    half = ROT_DIM // 2
    return xn * c + pltpu.roll(xn, LANES - half, 1) * s1 + pltpu.roll(xn, half, 1) * s2


def _proj_kernel(x_ref, gn_ref, w_ref, gq_ref, gk_ref, c_ref, s1_ref, s2_ref, *out_refs, plan):
    x = x_ref[...]
    ms = jnp.mean(x * x, axis=-1, keepdims=True)
    xn = (x * lax.rsqrt(ms + RMS_EPS) * gn_ref[...]).astype(BF16)
    seg = _head_seg_matrix()
    c, s1, s2 = c_ref[...], s1_ref[...], s2_ref[...]
    outs = list(out_refs)
    tm = x.shape[0]

    def store_rows(o_ref, layout, t, n_t, y):
        if layout == "rows":
            o_ref[:, t * LANES:(t + 1) * LANES] = y
        elif layout == "cols":
            o_ref[t * LANES:(t + 1) * LANES, :] = jnp.transpose(y)
        elif layout == "heads":
            o_ref[pl.ds(t, tm, stride=n_t), :] = y
        else:
            raise ValueError(layout)

    for kind, col0, width, layout in plan:
        h = _dot(xn, w_ref[:, col0:col0 + width])
        n_t = width // LANES
        if kind in ("q_rope_bf16", "q_rope_f32", "k_rope", "k_rope_mean"):
            g = gq_ref[...] if kind.startswith("q") else gk_ref[...]
            if kind == "q_rope_bf16":
                o_ref = outs.pop(0)
            elif kind == "q_rope_f32":
                o_ref = outs.pop(0)
            else:
                o_ref, o16_ref = outs.pop(0), outs.pop(0)
                mean_ref = outs.pop(0) if kind == "k_rope_mean" else None
            for t in range(n_t):
                sl = slice(t * LANES, (t + 1) * LANES)
                y = _norm_rope(h[:, sl], g, c, s1, s2, seg)
                if kind == "q_rope_bf16":
                    o_ref[:, sl] = (y * Q_SOFTMAX_SCALE).astype(BF16)
                elif kind == "q_rope_f32":
                    o_ref[:, sl] = y
                else:
                    store_rows(o_ref, layout, t, n_t, y)
                    o16_ref[:, sl] = y.astype(BF16)
                    if mean_ref is not None:
                        mean_ref[:, sl] = jnp.mean(y, axis=0, keepdims=True)
        elif kind == "q_plain_bf16":
            outs.pop(0)[...] = (h * Q_SOFTMAX_SCALE).astype(BF16)
        elif kind == "kv_plain":
            o_ref = outs.pop(0)
            for t in range(n_t):
                store_rows(o_ref, layout, t, n_t, h[:, t * LANES:(t + 1) * LANES])
            outs.pop(0)[...] = h.astype(BF16)
        elif kind == "gate":
            outs.pop(0)[...] = h / (1.0 + jnp.exp(-h))
        else:
            raise ValueError(kind)
    assert not outs


def _rope_tables(pos):
    half = ROT_DIM // 2
    inv = ROPE_THETA ** (-jnp.arange(0, ROT_DIM, 2, dtype=F32) / ROT_DIM)
    ang = pos.astype(F32)[:, None] * inv[None, :]
    cos, sin = jnp.cos(ang), jnp.sin(ang)
    t = pos.shape[0]
    z_half = jnp.zeros((t, half), F32)
    z_rest = jnp.zeros((t, HEAD_DIM - ROT_DIM), F32)
    c = jnp.concatenate([cos, cos, jnp.ones((t, HEAD_DIM - ROT_DIM), F32)], axis=-1)
    s1 = jnp.concatenate([-sin, z_half, z_rest], axis=-1)
    s2 = jnp.concatenate([z_half, sin, z_rest], axis=-1)
    rep = LANES // HEAD_DIM
    return tuple(jnp.tile(a, (1, rep)) for a in (c, s1, s2))


def _project(x, g_norm, w_bf16, gq, gk, tables, plan, tm, seq, name):
    m, d = x.shape
    n_tab = tables[0].shape[0] // tm
    tiles_per_seq = seq // tm
    out_shapes, out_specs = [], []
    for kind, _, width, layout in plan:
        row = pl.BlockSpec((tm, width), lambda i: (i, 0))
        if kind in ("q_rope_bf16", "q_plain_bf16"):
            out_shapes += [jax.ShapeDtypeStruct((m, width), BF16)]
            out_specs += [row]
        elif kind in ("q_rope_f32", "gate"):
            out_shapes += [jax.ShapeDtypeStruct((m, width), F32)]
            out_specs += [row]
        elif kind in ("k_rope", "kv_plain", "k_rope_mean"):
            if layout == "rows":
                out_shapes += [jax.ShapeDtypeStruct((m, width), F32)]
                out_specs += [row]
            elif layout == "cols":
                out_shapes += [jax.ShapeDtypeStruct((m // seq, width, seq), F32)]
                out_specs += [pl.BlockSpec((None, width, tm),
                                           lambda i: (i // tiles_per_seq, 0, i % tiles_per_seq))]
            else:
                n_t = width // LANES
                out_shapes += [jax.ShapeDtypeStruct((m * n_t, LANES), F32)]
                out_specs += [pl.BlockSpec((tm * n_t, LANES), lambda i: (i, 0))]
            out_shapes += [jax.ShapeDtypeStruct((m, width), BF16)]
            out_specs += [row]
            if kind == "k_rope_mean":
                assert tm == MOBA_BLOCK
                out_shapes += [jax.ShapeDtypeStruct((m // tm, 1, width), F32)]
                out_specs += [pl.BlockSpec((None, 1, width), lambda i: (i, 0, 0))]
    tab_spec = pl.BlockSpec((tm, LANES), lambda i: (i % n_tab, 0))
    vec_d = pl.BlockSpec((1, d), lambda i: (0, 0))
    vec_l = pl.BlockSpec((1, LANES), lambda i: (0, 0))
    rep = LANES // HEAD_DIM
    return pl.pallas_call(
        functools.partial(_proj_kernel, plan=plan),
        grid=(m // tm,),
        in_specs=[pl.BlockSpec((tm, d), lambda i: (i, 0)), vec_d,
                  pl.BlockSpec(w_bf16.shape, lambda i: (0, 0), pipeline_mode=pl.Buffered(1)),
                  vec_l, vec_l, tab_spec, tab_spec, tab_spec],
        out_specs=out_specs,
        out_shape=out_shapes,
        compiler_params=_params(("parallel",)),
        name=name,
    )(x, g_norm.reshape(1, d), w_bf16, jnp.tile(gq, rep).reshape(1, LANES), jnp.tile(gk, rep).reshape(1, LANES),
      *tables)


def _finish_kernel(x_ref, w_ref, *refs):
    mix_refs, y_ref = refs[:-1], refs[-1]
    y = x_ref[...]
    row0 = 0
    for mix_ref in mix_refs:
        rows = mix_ref.shape[1]
        y = y + _dot(mix_ref[...], w_ref[row0:row0 + rows, :])
        row0 += rows
    y_ref[...] = y


def _finish(x, mix_parts, w_bf16, tm, name):
    m, d = x.shape
    kdim = w_bf16.shape[0]
    assert sum(p.shape[1] for p in mix_parts) == kdim
    return pl.pallas_call(
        _finish_kernel,
        grid=(m // tm,),
        in_specs=[pl.BlockSpec((tm, d), lambda i: (i, 0)), pl.BlockSpec((kdim, d), lambda i: (0, 0))]
        + [pl.BlockSpec((tm, p.shape[1]), lambda i: (i, 0)) for p in mix_parts],
        out_specs=pl.BlockSpec((tm, d), lambda i: (i, 0)),
        out_shape=jax.ShapeDtypeStruct((m, d), F32),
        compiler_params=_params(("parallel",)),
        name=name,
    )(x, w_bf16, *mix_parts)


def _lambda_value(lp, lam_init):
    a = jnp.sum(lp[0:1] * lp[1:2], axis=-1, keepdims=True)
    b = jnp.sum(lp[2:3] * lp[3:4], axis=-1, keepdims=True)
    return jnp.exp(a) - jnp.exp(b) + lam_init


def _softmax_tile(qq, k, v, carry, mask):
    m, l, acc = carry
    s = _dot_nt(qq, k)
    if mask is not None:
        s = jnp.where(mask, s, NEG)
    m_new = jnp.maximum(m, jnp.max(s, axis=-1, keepdims=True))
    alpha = jnp.exp2(m - m_new)
    p = jnp.exp2(s - m_new)
    l = alpha * l + jnp.sum(p, axis=-1, keepdims=True)
    acc = alpha * acc + _dot(p.astype(BF16), v)
    return m_new, l, acc


def _dot_tn(a, b):
    return lax.dot_general(a, b, (((0,), (0,)), ((), ())), preferred_element_type=F32)


def _softmax_tile_t(qq_t, k_ref, v_ref, start, n_keys, carry, mask):
    s, m_s = _scores_t(qq_t, k_ref, start, n_keys, mask)
    return _absorb_t(v_ref, start, n_keys, s, m_s, carry)


def _scores_t(qq_t, k_ref, start, n_keys, mask):
    s = _dot(k_ref[pl.ds(start, n_keys), :], qq_t)
    if mask is not None:
        s = jnp.concatenate([jnp.where(mk, s[c * ATTN_SLAB:(c + 1) * ATTN_SLAB], NEG)
                             for c, mk in enumerate(mask)], axis=0)
    return s, jnp.max(s, axis=0, keepdims=True)


def _absorb_t(v_ref, start, n_keys, s, m_s, carry):
    m, l, acc = carry
    m_new = jnp.maximum(m, m_s)
    alpha = jnp.exp2(m - m_new)
    p = jnp.exp2(s - m_new)
    l = alpha * l + jnp.sum(p, axis=0, keepdims=True)
    acc = alpha * acc + _dot_tn(v_ref[pl.ds(start, n_keys), :], p.astype(BF16))
    return m_new, l, acc


def _causal_masks(k_start, q_start, tq, n_keys, r, strict):
    q_pos = q_start + _iota((ATTN_SLAB, r), 1) % tq
    masks = []
    for c in range(n_keys // ATTN_SLAB):
        k_pos = k_start + c * ATTN_SLAB + _iota((ATTN_SLAB, r), 0)
        masks.append(k_pos < q_pos if strict else k_pos <= q_pos)
    return masks


def _stack_pair_t(q_t):
    row = _iota(q_t.shape, 0)
    zero = jnp.zeros_like(q_t)
    return jnp.concatenate([jnp.where(row < HEAD_DIM, q_t, zero), jnp.where(row >= HEAD_DIM, q_t, zero)], axis=1)


def _diff_attn_kernel(lam_ref, subg_ref, q_ref, k_ref, v_ref, sg_ref, o_ref, *, tq, tk, lam_init):
    i = pl.program_id(2)
    r = 2 * tq
    qq_t = _stack_pair_t(jnp.transpose(q_ref[...].astype(F32))).astype(BF16)
    n_full = (i * tq) // tk

    def full_tile(j, carry):
        return _softmax_tile_t(qq_t, k_ref, v_ref, pl.multiple_of(j * tk, tk), tk, carry, None)

    carry = (jnp.full((1, r), NEG, F32), jnp.zeros((1, r), F32), jnp.zeros((LANES, r), F32))
    carry = lax.fori_loop(0, n_full, full_tile, carry)
    start = pl.multiple_of(n_full * tk, tk)
    _, l, acc = _softmax_tile_t(qq_t, k_ref, v_ref, start, tk, carry,
                                _causal_masks(start, i * tq, tq, tk, r, strict=False))

    a = acc / l
    lam = _lambda_value(lam_ref[...], lam_init)
    o = jnp.transpose(a[:, :tq] - lam * a[:, tq:])
    ms = jnp.mean(o * o, axis=-1, keepdims=True)
    o = o * lax.rsqrt(ms + RMS_EPS) * subg_ref[...] * (1.0 - lam_init)
    o_ref[...] = (o * sg_ref[...]).astype(BF16)


def _softplus2(z):
    return jnp.maximum(z, 0.0) + jnp.log2(1.0 + jnp.exp2(-jnp.abs(z)))


def _later_matrix(tk, keys_axis=1):
    a, b = _iota((tk, tk), 0), _iota((tk, tk), 1)
    return (a > b if keys_axis == 1 else b > a).astype(BF16)


def _sb_kernel(q_ref, k_ref, v_ref, sg_ref, o_ref, *, tq, tk):
    i = pl.program_id(2)
    r = 2 * tq
    qq_t = _stack_pair_t(jnp.transpose(q_ref[...].astype(F32))).astype(BF16)
    tri = _later_matrix(ATTN_SLAB, keys_axis=0)
    n_full = (i * tq) // tk
    n_slabs = tk // ATTN_SLAB

    def tile(st, c, acc, valid):
        parts = []
        for u in range(n_slabs):
            z = _dot(k_ref[pl.ds(st + u * ATTN_SLAB, ATTN_SLAB), :], qq_t)
            sp = _softplus2(z)
            log_keep = -sp if valid is None else jnp.where(valid[u], -sp, 0.0)
            later = _dot(tri, log_keep.astype(BF16))
            parts.append((z - sp + later, later[0:1, :] + log_keep[0:1, :]))
        for u in reversed(range(n_slabs)):
            e, whole = parts[u]
            w = jnp.exp2(e + c)
            if valid is not None:
                w = jnp.where(valid[u], w, 0.0)
            acc = acc + _dot_tn(v_ref[pl.ds(st + u * ATTN_SLAB, ATTN_SLAB), :], w.astype(BF16))
            c = c + whole
        return c, acc

    start = pl.multiple_of(n_full * tk, tk)
    c, acc = tile(start, jnp.zeros((1, r), F32), jnp.zeros((LANES, r), F32),
                  _causal_masks(start, i * tq, tq, tk, r, strict=True))

    def live(state):
        t, c_max, _, _ = state
        return jnp.logical_and(t < n_full, c_max > -SB_DEAD_LOG2)

    def full_tile(state):
        t, _, c, acc = state
        c, acc = tile(pl.multiple_of((n_full - 1 - t) * tk, tk), c, acc, None)
        return t + 1, jnp.max(c), c, acc

    _, _, _, acc = lax.while_loop(live, full_tile, (jnp.int32(0), jnp.max(c), c, acc))
    row = _iota((LANES, tq), 0)
    o = jnp.transpose(jnp.where(row < HEAD_DIM, acc[:, :tq], acc[:, tq:]))
    o_ref[...] = (o * sg_ref[...]).astype(BF16)


def _top_blocks(g, n_valid, axis=-1):
    axis = axis % g.ndim
    blk = _iota(g.shape, axis)
    nb = g.shape[axis]
    g = jnp.where(blk < n_valid, g, -jnp.inf)
    sel = jnp.zeros(g.shape, jnp.bool_)
    for _ in range(min(MOBA_TOPK, nb)):
        mx = jnp.max(g, axis=axis, keepdims=True)
        idx = jnp.min(jnp.where(g == mx, blk, nb), axis=axis, keepdims=True)
        pick = blk == idx
        sel = jnp.logical_or(sel, pick)
        g = jnp.where(pick, -jnp.inf, g)
    return jnp.logical_and(sel, blk < n_valid)


def _moba_kernel(q_ref, k_ref, v_ref, km_ref, sg_ref, o_ref, sel_s, *, tq, group):
    i = pl.program_id(2)
    r = 2 * tq
    qf_t = _stack_pair_t(jnp.transpose(q_ref[...]))
    row = _iota((LANES, tq), 0)
    q_hi, q_lo = _split_bf16(qf_t)
    qq_t = (qf_t * Q_SOFTMAX_SCALE).astype(BF16)

    km_hi, km_lo = _split_bf16(km_ref[...])
    g = _dot(km_hi, q_hi) + _dot(km_hi, q_lo) + _dot(km_lo, q_hi)
    sel_s[...] = jnp.where(_top_blocks(g, i, axis=0), 1.0, 0.0)

    start = pl.multiple_of(i * tq, tq)
    carry = (jnp.full((1, r), NEG, F32), jnp.zeros((1, r), F32), jnp.zeros((LANES, r), F32))
    carry = _softmax_tile_t(qq_t, k_ref, v_ref, start, tq, carry, _causal_masks(0, 0, tq, tq, r, strict=False))

    def past_group(t, carry):
        st = pl.multiple_of(t * group * tq, group * tq)
        chosen = [sel_s[pl.ds(t * group + c, 1), :] > 0.0 for c in range(group)]
        return _softmax_tile_t(qq_t, k_ref, v_ref, st, group * tq, carry, chosen)

    _, l, acc = lax.fori_loop(0, (i + group - 1) // group, past_group, carry)
    o_t = acc / l
    o = jnp.transpose(jnp.where(row < HEAD_DIM, o_t[:, :tq], o_t[:, tq:]))
    o_ref[...] = (o * sg_ref[...]).astype(BF16)


def _attn_specs(tq, s, group0=0):
    q_spec = pl.BlockSpec((None, tq, LANES), lambda b, h, i: (b, i, h + group0))
    kv_spec = pl.BlockSpec((None, s, LANES), lambda b, h, i: (b, 0, h))
    return q_spec, kv_spec


def _diff_attn_prompt(q16, k16, v16, sgate, lam_p, subln_g, lam_init, tq, tk):
    b, s, w = q16.shape
    q_spec, kv_spec = _attn_specs(tq, s)
    return pl.pallas_call(
        functools.partial(_diff_attn_kernel, tq=tq, tk=tk, lam_init=lam_init),
        grid=(b, w // LANES, s // tq),
        in_specs=[pl.BlockSpec(lam_p.shape, lambda b, h, i: (0, 0)),
                  pl.BlockSpec((1, LANES), lambda b, h, i: (0, 0)),
                  q_spec, kv_spec, kv_spec, q_spec],
        out_specs=q_spec,
        out_shape=jax.ShapeDtypeStruct((b, s, w), BF16),
        compiler_params=_params(("parallel", "parallel", "arbitrary")),
        name="diff_attn_prompt",
    )(lam_p, subln_g.reshape(1, LANES), q16, k16, v16, sgate)


def _sb_prompt(q16, k16, v16, sgate, gate_group0, tq, tk):
    b, s, w = q16.shape
    q_spec, kv_spec = _attn_specs(tq, s)
    sg_spec = pl.BlockSpec((None, tq, LANES), lambda b, h, i: (b, i, h + gate_group0))
    return pl.pallas_call(
        functools.partial(_sb_kernel, tq=tq, tk=tk),
        grid=(b, w // LANES, s // tq),
        in_specs=[q_spec, kv_spec, kv_spec, sg_spec],
        out_specs=q_spec,
        out_shape=jax.ShapeDtypeStruct((b, s, w), BF16),
        compiler_params=_params(("parallel", "parallel", "arbitrary")),
        name="sb_prompt",
    )(q16, k16, v16, sgate)


def _moba_prompt(q, k16, v16, kmean, sgate):
    b, s, w = q.shape
    tq = MOBA_BLOCK
    nb = s // tq
    q_spec, kv_spec = _attn_specs(tq, s)
    group = math.gcd(nb, MOBA_GROUP)
    return pl.pallas_call(
        functools.partial(_moba_kernel, tq=tq, group=group),
        grid=(b, w // LANES, nb),
        in_specs=[q_spec, kv_spec, kv_spec,
                  pl.BlockSpec((None, nb, LANES), lambda b, h, i: (b, 0, h)), q_spec],
        out_specs=q_spec,
        out_shape=jax.ShapeDtypeStruct((b, s, w), BF16),
        scratch_shapes=[pltpu.VMEM((nb, 2 * tq), F32)],
        compiler_params=_params(("parallel", "parallel", "arbitrary")),
        name="moba_prompt",
    )(q, k16, v16, kmean, sgate)


def _row_heads(x_row, n_rows, lanes_per_row):
    w = x_row.shape[1]
    keep = _iota((n_rows, w), 1) // lanes_per_row == _iota((n_rows, w), 0)
    return jnp.where(keep, jnp.broadcast_to(x_row, (n_rows, w)), 0.0)


def _dup_rows(x, reps):
    n, w = x.shape
    row = _iota((n * reps, w), 0) // reps
    out = jnp.zeros((n * reps, w), x.dtype)
    for h in range(n):
        out = jnp.where(row == h, x[h:h + 1, :], out)
    return out


def _stack_rows(rows):
    n, w = len(rows), rows[0].shape[1]
    row = _iota((n, w), 0)
    out = jnp.zeros((n, w), rows[0].dtype)
    for h, r in enumerate(rows):
        out = jnp.where(row == h, r, out)
    return out


def _head_dots(qcol_ref, kt_ref, n_heads):
    rows = []
    for h in range(n_heads):
        sl = slice(h * HEAD_DIM, (h + 1) * HEAD_DIM)
        rows.append(jnp.sum(qcol_ref[sl, :] * kt_ref[sl, :], axis=0, keepdims=True))
    return _stack_rows(rows)


def _add_weighted(acc_ref, w, vt_refs, n_heads):
    for h in range(n_heads):
        sl = slice(h * HEAD_DIM, (h + 1) * HEAD_DIM)
        a = acc_ref[sl, :]
        for u, vt_ref in enumerate(vt_refs):
            a = a + w[u * n_heads + h:u * n_heads + h + 1, :] * vt_ref[sl, :]
        acc_ref[sl, :] = a


def _sb_weights_pages(zs, tri, c):
    n_h = zs[0].shape[0]
    z = jnp.concatenate(zs, axis=0)
    sp = _softplus2(z)
    log_keep = -sp
    hi, lo = _split_bf16(log_keep)
    rows = z.shape[0]
    both = _dot(jnp.concatenate([hi, lo], axis=0), tri)
    later = both[:rows] + both[rows:]
    whole = later[:, 0:1] + log_keep[:, 0:1]
    offsets = []
    for u in range(len(zs)):
        offsets.append(c)
        c = c + whole[u * n_h:(u + 1) * n_h]
    return jnp.exp2(z - sp + later + jnp.concatenate(offsets, axis=0)), c


def _lane_sums_as_row(x):
    hi, lo = _split_bf16(x)
    ones = jnp.ones((8, LANES), BF16)
    return (_dot_nt(ones, hi) + _dot_nt(ones, lo))[0:1, :]


def _decode_even_kernel(pt_ref, lam_ref, subg_ref, qa_ref, kan_ref, van_ref, qb_ref, sg_ref, *refs,
                        pps, lam_init):
    del pt_ref
    page_refs = refs[:4 * pps]
    o_ref = refs[4 * pps]
    qa_s, m_s, l_s, acca_s, c_s, accb_s = refs[4 * pps + 1:]
    step = pl.program_id(1)
    n_ha = qa_ref.shape[0]
    n_a = 2 * n_ha
    n_b = accb_s.shape[0] // HEAD_DIM
    wa = n_ha * LANES

    @pl.when(step == 0)
    def _():
        rows = _dup_rows(qa_ref[...].astype(F32), 2)
        qa = jnp.where(_iota(rows.shape, 1) // HEAD_DIM == _iota(rows.shape, 0) % 2, rows, 0.0)
        qa_s[...] = qa
        m_s[...] = jnp.sum(qa * _dup_rows(kan_ref[...], 2), axis=-1, keepdims=True)
        l_s[...] = jnp.ones_like(l_s)
        acca_s[...] = _dup_rows(van_ref[...], 2)
        c_s[...] = jnp.zeros_like(c_s)
        accb_s[...] = jnp.zeros_like(accb_s)

    rows_a = pps * page_refs[0].shape[0]
    own_head = _iota((n_a, rows_a), 1) % n_ha == _iota((n_a, rows_a), 0) // 2
    ak = jnp.concatenate([page_refs[4 * u][...].astype(BF16) for u in range(pps)], axis=0)
    av = jnp.concatenate([page_refs[4 * u + 1][...].astype(BF16) for u in range(pps)], axis=0)
    m_s[...], l_s[...], acca_s[...] = _softmax_tile(qa_s[...].astype(BF16), ak, av,
                                                    (m_s[...], l_s[...], acca_s[...]), own_head)
    zs = [_head_dots(qb_ref, page_refs[4 * u + 2], n_b) for u in range(pps)]
    w, c_s[...] = _sb_weights_pages(zs, _later_matrix(LANES), c_s[...])
    _add_weighted(accb_s, w, [page_refs[4 * u + 3] for u in range(pps)], n_b)

    @pl.when(step == pl.num_programs(1) - 1)
    def _():
        lam = _lambda_value(lam_ref[...], lam_init)
        a = acca_s[...] / l_s[...]
        for h in range(n_ha):
            sl = slice(h * LANES, (h + 1) * LANES)
            x = a[2 * h:2 * h + 1, :] - lam * a[2 * h + 1:2 * h + 2, :]
            ms = jnp.mean(x * x, axis=-1, keepdims=True)
            x = x * lax.rsqrt(ms + RMS_EPS) * subg_ref[...] * (1.0 - lam_init)
            o_ref[:, sl] = (x * sg_ref[:, sl]).astype(BF16)
        o_ref[:, wa:] = (_lane_sums_as_row(accb_s[...]) * sg_ref[:, wa:]).astype(BF16)


def _page_specs(cache, li, n_pages, pps, reverse):
    r = cache.shape[2]

    def make(u):
        def index(b, s, pt):
            j = s * pps + u
            j = n_pages - 1 - j if reverse else j
            return (li, pt[b, j], 0, 0)
        return pl.BlockSpec((None, None, r, LANES), index)

    return [make(u) for u in range(pps)]


def _col_bcast(x):
    return jnp.broadcast_to(x.astype(F32)[:, :, None], x.shape + (LANES,))


def _decode_even(pt, lam_p, subln_g, lam_init, qa16, ka, va, qb16, sgate, caches, li, pps):
    db, n_pages = pt.shape
    wa, wb = qa16.shape[-1], qb16.shape[-1]
    n_ha = wa // LANES
    whole = lambda shp: pl.BlockSpec((None,) + shp, lambda b, s, pt: (b,) + (0,) * len(shp))
    specs_by_cache = [_page_specs(c, li, n_pages, pps, True) for c in caches]
    page_specs = [specs_by_cache[t][u] for u in range(pps) for t in range(4)]
    page_args = [caches[t] for u in range(pps) for t in range(4)]
    grid_spec = pltpu.PrefetchScalarGridSpec(
        num_scalar_prefetch=1,
        grid=(db, n_pages // pps),
        in_specs=[pl.BlockSpec(lam_p.shape, lambda b, s, pt: (0, 0)),
                  pl.BlockSpec((1, LANES), lambda b, s, pt: (0, 0)),
                  whole((n_ha, LANES)), whole((n_ha, LANES)), whole((n_ha, LANES)), whole((wb, LANES)),
                  whole((1, wa + wb))] + page_specs,
        out_specs=whole((1, wa + wb)),
        scratch_shapes=[pltpu.VMEM((2 * n_ha, LANES), F32),
                        pltpu.VMEM((2 * n_ha, 1), F32), pltpu.VMEM((2 * n_ha, 1), F32),
                        pltpu.VMEM((2 * n_ha, LANES), F32),
                        pltpu.VMEM((wb // HEAD_DIM, 1), F32), pltpu.VMEM((wb, LANES), F32)],
    )
    heads = lambda x: x.reshape(db, n_ha, LANES)
    return pl.pallas_call(
        functools.partial(_decode_even_kernel, pps=pps, lam_init=lam_init),
        grid_spec=grid_spec,
        out_shape=jax.ShapeDtypeStruct((db, 1, wa + wb), BF16),
        compiler_params=_params(("parallel", "arbitrary")),
        name="decode_even",
    )(pt, lam_p, subln_g.reshape(1, LANES), heads(qa16), heads(ka), heads(va), _col_bcast(qb16),
      sgate.reshape(db, 1, wa + wb), *page_args)


def _moba_scores_kernel(pt_ref, qcol_ref, qrow_ref, kn_ref, *refs, pps, n_blocks):
    del pt_ref
    page_refs = refs[:pps]
    p_ref, pnew_ref = refs[pps], refs[pps + 1]
    s_s = refs[pps + 2]
    step = pl.program_id(1)
    n_h = qcol_ref.shape[0] // HEAD_DIM
    ppb = MOBA_BLOCK // LANES

    for u in range(pps):
        s_s[step * pps + u] = _head_dots(qcol_ref, page_refs[u], n_h)

    @pl.when(step == pl.num_programs(1) - 1)
    def _():
        lane = _iota((n_h, LANES), 1)
        g = jnp.zeros((n_h, LANES), F32)
        for n in range(n_blocks):
            tot = sum(s_s[n * ppb + t] for t in range(ppb))
            g = jnp.where(lane == n, jnp.sum(tot, axis=-1, keepdims=True) * (1.0 / MOBA_BLOCK), g)
        sel = _top_blocks(g, n_blocks)
        chosen = [sel[:, n:n + 1] for n in range(n_blocks)]
        s_new = jnp.sum(_row_heads(qrow_ref[...], n_h, HEAD_DIM) * kn_ref[...], axis=-1, keepdims=True) * QK_SCALE
        m_lanes = jnp.full((n_h, LANES), NEG, F32)
        for j in range(n_blocks * ppb):
            m_lanes = jnp.maximum(m_lanes, jnp.where(chosen[j // ppb], s_s[j] * QK_SCALE, NEG))
        m = jnp.maximum(jnp.max(m_lanes, axis=-1, keepdims=True), s_new)
        l_lanes = jnp.zeros((n_h, LANES), F32)
        for j in range(n_blocks * ppb):
            e = jnp.where(chosen[j // ppb], jnp.exp(s_s[j] * QK_SCALE - m), 0.0)
            s_s[j] = e
            l_lanes = l_lanes + e
        e_new = jnp.exp(s_new - m)
        inv = 1.0 / (jnp.sum(l_lanes, axis=-1, keepdims=True) + e_new)
        for j in range(n_blocks * ppb):
            p_ref[j] = s_s[j] * inv
        pnew_ref[...] = jnp.broadcast_to(e_new * inv, pnew_ref.shape)


def _moba_values_kernel(pt_ref, p_ref, pnew_ref, vn_ref, sg_ref, *refs, pps):
    del pt_ref
    page_refs = refs[:pps]
    o_ref = refs[pps]
    acc_s = refs[pps + 1]
    step = pl.program_id(1)
    n_h = acc_s.shape[0] // HEAD_DIM

    @pl.when(step == 0)
    def _():
        acc_s[...] = jnp.zeros_like(acc_s)

    w = jnp.concatenate([p_ref[u] for u in range(pps)], axis=0)
    _add_weighted(acc_s, w, list(page_refs), n_h)

    @pl.when(step == pl.num_programs(1) - 1)
    def _():
        o = _lane_sums_as_row(acc_s[...]) + pnew_ref[...] * vn_ref[...]
        o_ref[...] = (o * sg_ref[...]).astype(BF16)


def _decode_odd(pt, q, k_new, v_new, sgate, cache_k, cache_v, li, pps):
    db, n_pages = pt.shape
    w = q.shape[-1]
    n_h = w // HEAD_DIM
    page = LANES
    past = n_pages * page
    assert past % MOBA_BLOCK == 0 and past >= MOBA_BLOCK and MOBA_BLOCK % page == 0
    n_blocks = past // MOBA_BLOCK
    row = lambda wd: pl.BlockSpec((None, 1, wd), lambda b, s, pt: (b, 0, 0))
    r3 = lambda x: x.reshape(db, 1, x.shape[-1])
    probs, p_new = pl.pallas_call(
        functools.partial(_moba_scores_kernel, pps=pps, n_blocks=n_blocks),
        grid_spec=pltpu.PrefetchScalarGridSpec(
            num_scalar_prefetch=1,
            grid=(db, n_pages // pps),
            in_specs=[pl.BlockSpec((None, w, LANES), lambda b, s, pt: (b, 0, 0)), row(w), row(w)]
            + _page_specs(cache_k, li, n_pages, pps, False),
            out_specs=[pl.BlockSpec((None, n_pages, n_h, page), lambda b, s, pt: (b, 0, 0, 0)),
                       pl.BlockSpec((None, n_h, LANES), lambda b, s, pt: (b, 0, 0))],
            scratch_shapes=[pltpu.VMEM((n_pages, n_h, page), F32)],
        ),
        out_shape=[jax.ShapeDtypeStruct((db, n_pages, n_h, page), F32),
                   jax.ShapeDtypeStruct((db, n_h, LANES), F32)],
        compiler_params=_params(("parallel", "arbitrary")),
        name="decode_moba_scores",
    )(pt, _col_bcast(q), r3(q), r3(k_new), *([cache_k] * pps))
    p_new_row = jnp.repeat(p_new[:, :, 0], HEAD_DIM, axis=-1).reshape(db, 1, w)
    return pl.pallas_call(
        functools.partial(_moba_values_kernel, pps=pps),
        grid_spec=pltpu.PrefetchScalarGridSpec(
            num_scalar_prefetch=1,
            grid=(db, n_pages // pps),
            in_specs=[pl.BlockSpec((None, pps, n_h, page), lambda b, s, pt: (b, s, 0, 0)),
                      row(w), row(w), row(w)] + _page_specs(cache_v, li, n_pages, pps, False),
            out_specs=row(w),
            scratch_shapes=[pltpu.VMEM((w, LANES), F32)],
        ),
        out_shape=jax.ShapeDtypeStruct((db, 1, w), BF16),
        compiler_params=_params(("parallel", "arbitrary")),
        name="decode_moba_values",
    )(pt, probs, p_new_row, r3(v_new), r3(sgate), *([cache_v] * pps))


PROMPT_TM = 256
ATTN_TQ = 512
ATTN_TK = 512
MOBA_GROUP = 4
PAGES_PER_STEP = 8


def _rows_view(c):
    l, n, p, h, hd = c.shape
    assert p == LANES and hd == LANES
    return c.reshape(l, n, p * h, hd)


def _cols_view(c):
    l, n, p, h, hd = c.shape
    assert p == LANES and hd == HEAD_DIM
    return jnp.transpose(c, (0, 1, 3, 4, 2)).reshape(l, n, h * hd, p)


def _heads_from_cols(x, n_heads):
    b, w, s = x.shape
    return jnp.transpose(x.reshape(b, n_heads, w // n_heads, s), (0, 3, 1, 2))


def _even_layer(x_p, x_s, caches, li, pt, g_norm, w_in, qk_g, lam_p, subln_g, w_out, layer):
    lam_init = 0.8 - 0.6 * math.exp(-0.3 * layer)
    b, s, d = x_p.shape
    db, t, _ = x_s.shape
    wa = w_out.shape[0] // 2
    wb = wa
    def plan(wide, narrow):
        return (("q_rope_bf16", 0, wa, None), ("k_rope", wa, wa, wide), ("kv_plain", 2 * wa, wa, wide),
                ("q_plain_bf16", 3 * wa, wb, None), ("kv_plain", 3 * wa + wb, wb, narrow),
                ("kv_plain", 3 * wa + 2 * wb, wb, narrow), ("gate", 3 * wa + 3 * wb, wa + wb, None))

    w_in16 = w_in.astype(BF16)
    w_out16 = w_out.astype(BF16)
    past = pt.shape[1] * LANES
    n_a = wa // (2 * HEAD_DIM)
    n_b = wb // HEAD_DIM

    tabs = _rope_tables(jnp.arange(s))
    qa16, ka, ka16, va, va16, qb16, kb, kb16, vb, vb16, sgate = _project(
        x_p.reshape(b * s, d), g_norm, w_in16, qk_g[0], qk_g[1], tabs, plan("heads", "cols"), PROMPT_TM, s,
        "proj_even_prompt")
    r3 = lambda a: a.reshape(b, s, a.shape[-1])
    sg3 = r3(sgate)
    mix_a = _diff_attn_prompt(r3(qa16), r3(ka16), r3(va16), sg3, lam_p, subln_g, lam_init, ATTN_TQ, ATTN_TK)
    mix_b = _sb_prompt(r3(qb16), r3(kb16), r3(vb16), sg3, wa // LANES, ATTN_TQ, ATTN_TK)
    y_p = _finish(x_p.reshape(b * s, d), [mix_a.reshape(b * s, wa), mix_b.reshape(b * s, wb)], w_out16, 512,
                  "finish_even_prompt").reshape(b, s, d)

    tabs_s = _rope_tables(jnp.tile(past + jnp.arange(t), db))
    qa16_s, ka_s, _, va_s, _, qb16_s, kb_s, _, vb_s, _, sgate_s = _project(
        x_s.reshape(db * t, d), g_norm, w_in16, qk_g[0], qk_g[1], tabs_s, plan("rows", "rows"), db * t, db * t,
        "proj_even_sample")
    assert t == 1
    mix_s = _decode_even(pt, lam_p, subln_g, lam_init, qa16_s, ka_s, va_s, qb16_s, sgate_s,
                         caches, li, PAGES_PER_STEP)
    y_s = _finish(x_s.reshape(db * t, d), [mix_s.reshape(db * t, wa + wb)], w_out16, db * t,
                  "finish_even_sample").reshape(db, t, d)

    rows_p = (ka.reshape(b, s, n_a, 2 * HEAD_DIM), va.reshape(b, s, n_a, 2 * HEAD_DIM),
              _heads_from_cols(kb, n_b), _heads_from_cols(vb, n_b))
    rows_s = (ka_s.reshape(db, t, n_a, 2 * HEAD_DIM), va_s.reshape(db, t, n_a, 2 * HEAD_DIM),
              kb_s.reshape(db, t, n_b, HEAD_DIM), vb_s.reshape(db, t, n_b, HEAD_DIM))
    return y_p, y_s, rows_p, rows_s


def _odd_layer(x_p, x_s, cache_k, cache_v, li, pt, g_norm, w_in, qk_g, w_out):
    b, s, d = x_p.shape
    db, t, _ = x_s.shape
    wc = w_out.shape[0]
    w_in16 = w_in.astype(BF16)
    w_out16 = w_out.astype(BF16)
    past = pt.shape[1] * LANES
    assert s % MOBA_BLOCK == 0

    def plan(k_kind, layout):
        return (("q_rope_f32", 0, wc, None), (k_kind, wc, wc, layout), ("kv_plain", 2 * wc, wc, layout),
                ("gate", 3 * wc, wc, None))

    tabs = _rope_tables(jnp.arange(s))
    q, k, k16, kmean, v, v16, sgate = _project(
        x_p.reshape(b * s, d), g_norm, w_in16, qk_g[0], qk_g[1], tabs, plan("k_rope_mean", "cols"), MOBA_BLOCK, s,
        "proj_odd_prompt")
    r3 = lambda a: a.reshape(b, s, a.shape[-1])
    mix = _moba_prompt(r3(q), r3(k16), r3(v16), kmean.reshape(b, s // MOBA_BLOCK, wc), r3(sgate))
    y_p = _finish(x_p.reshape(b * s, d), [mix.reshape(b * s, wc)], w_out16, 512,
                  "finish_odd_prompt").reshape(b, s, d)

    assert t == 1
    tabs_s = _rope_tables(jnp.tile(past + jnp.arange(t), db))
    q_s, k_s, _, v_s, _, sgate_s = _project(
        x_s.reshape(db * t, d), g_norm, w_in16, qk_g[0], qk_g[1], tabs_s, plan("k_rope", "rows"), db * t, db * t,
        "proj_odd_sample")
    mix_s = _decode_odd(pt, q_s, k_s, v_s, sgate_s, cache_k, cache_v, li, PAGES_PER_STEP)
    y_s = _finish(x_s.reshape(db * t, d), [mix_s.reshape(db * t, wc)], w_out16, db * t,
                  "finish_odd_sample").reshape(db, t, d)

    n_c = wc // HEAD_DIM
    return (y_p, y_s, (_heads_from_cols(k, n_c), _heads_from_cols(v, n_c)),
            (k_s.reshape(db, t, n_c, HEAD_DIM), v_s.reshape(db, t, n_c, HEAD_DIM)))


def kernel(x_prompt, x_sample, cache_a_k, cache_a_v, cache_b_k, cache_b_v, cache_c_k, cache_c_v, page_table,
           norm_even, w_in_even, qk_norm_a, lambda_a, subln_a, w_out_even, norm_odd, w_in_odd, qk_norm_c,
           w_out_odd):
    depth = norm_even.shape[0] + norm_odd.shape[0]
    even_caches = [_rows_view(cache_a_k), _rows_view(cache_a_v), _cols_view(cache_b_k), _cols_view(cache_b_v)]
    odd_k, odd_v = _cols_view(cache_c_k), _cols_view(cache_c_v)
    y_p, y_s = x_prompt, x_sample
    ev_p, ev_s, od_p, od_s = [], [], [], []
    for layer in range(depth):
        i = layer // 2
        if layer % 2 == 0:
            y_p, y_s, rp, rs = _even_layer(y_p, y_s, even_caches, i, page_table, norm_even[i], w_in_even[i],
                                           qk_norm_a[i], lambda_a[i], subln_a[i], w_out_even[i], layer)
            ev_p.append(rp)
            ev_s.append(rs)
        else:
            y_p, y_s, rp, rs = _odd_layer(y_p, y_s, odd_k, odd_v, i, page_table, norm_odd[i], w_in_odd[i],
                                          qk_norm_c[i], w_out_odd[i])
            od_p.append(rp)
            od_s.append(rs)

    def stack(rows, j):
        return jnp.stack([r[j] for r in rows])

    return (y_p, y_s,
            stack(ev_p, 0), stack(ev_p, 1), stack(ev_p, 2), stack(ev_p, 3), stack(od_p, 0), stack(od_p, 1),
            stack(ev_s, 0), stack(ev_s, 1), stack(ev_s, 2), stack(ev_s, 3), stack(od_s, 0), stack(od_s, 1))
```

```python
import functools
import math

import jax
import jax.numpy as jnp
from jax import lax
from jax.experimental import pallas as pl
from jax.experimental.pallas import tpu as pltpu

HEAD_DIM = 64
ROT_DIM = HEAD_DIM // 4
ROPE_THETA = 500000.0
RMS_EPS = 1e-6
MOBA_BLOCK = 256
MOBA_TOPK = 3
LANES = 128
NEG = -1e30
SB_DEAD_LOG2 = 150.0
QK_SCALE = HEAD_DIM ** -0.5
Q_SOFTMAX_SCALE = QK_SCALE * math.log2(math.e)
ATTN_SLAB = 256
VMEM_LIMIT = 52 * 1024 * 1024

F32 = jnp.float32
BF16 = jnp.bfloat16


def _params(sem, vmem=VMEM_LIMIT):
    return pltpu.CompilerParams(dimension_semantics=sem, vmem_limit_bytes=vmem)


def _dot(a, b):
    return jnp.dot(a, b, preferred_element_type=F32)


def _dot_nt(a, b):
    return lax.dot_general(a, b, (((1,), (1,)), ((), ())), preferred_element_type=F32)


def _split_bf16(x):
    hi = x.astype(BF16)
    lo = (x - hi.astype(F32)).astype(BF16)
    return hi, lo


def _iota(shape, dim):
    return lax.broadcasted_iota(jnp.int32, shape, dim)


def _head_seg_matrix():
    return (_iota((LANES, LANES), 0) // HEAD_DIM == _iota((LANES, LANES), 1) // HEAD_DIM).astype(BF16)


def _norm_rope(x, g, c, s1, s2, seg):
    hi, lo = _split_bf16(x * x)
    ms = (_dot(hi, seg) + _dot(lo, seg)) * (1.0 / HEAD_DIM)
    xn = x * lax.rsqrt(ms + RMS_EPS) * g
    half = ROT_DIM // 2
    return xn * c + pltpu.roll(xn, LANES - half, 1) * s1 + pltpu.roll(xn, half, 1) * s2


def _proj_kernel(x_ref, gn_ref, w_ref, gq_ref, gk_ref, c_ref, s1_ref, s2_ref, *out_refs, plan):
    x = x_ref[...]
    ms = jnp.mean(x * x, axis=-1, keepdims=True)
    xn = (x * lax.rsqrt(ms + RMS_EPS) * gn_ref[...]).astype(BF16)
    seg = _head_seg_matrix()
    c, s1, s2 = c_ref[...], s1_ref[...], s2_ref[...]
    outs = list(out_refs)
    tm = x.shape[0]

    def store_rows(o_ref, layout, t, n_t, y):
        if layout == "rows":
            o_ref[:, t * LANES:(t + 1) * LANES] = y
        elif layout == "cols":
            o_ref[t * LANES:(t + 1) * LANES, :] = jnp.transpose(y)
        elif layout == "heads":
            o_ref[pl.ds(t, tm, stride=n_t), :] = y
        else:
            raise ValueError(layout)

    for kind, col0, width, layout in plan:
        h = _dot(xn, w_ref[:, col0:col0 + width])
        n_t = width // LANES
        if kind in ("q_rope_bf16", "q_rope_f32", "k_rope", "k_rope_mean"):
            g = gq_ref[...] if kind.startswith("q") else gk_ref[...]
            if kind == "q_rope_bf16":
                o_ref = outs.pop(0)
            elif kind == "q_rope_f32":
                o_ref = outs.pop(0)
            else:
                o_ref, o16_ref = outs.pop(0), outs.pop(0)
                mean_ref = outs.pop(0) if kind == "k_rope_mean" else None
            for t in range(n_t):
                sl = slice(t * LANES, (t + 1) * LANES)
                y = _norm_rope(h[:, sl], g, c, s1, s2, seg)
                if kind == "q_rope_bf16":
                    o_ref[:, sl] = (y * Q_SOFTMAX_SCALE).astype(BF16)
                elif kind == "q_rope_f32":
                    o_ref[:, sl] = y
                else:
                    store_rows(o_ref, layout, t, n_t, y)
                    o16_ref[:, sl] = y.astype(BF16)
                    if mean_ref is not None:
                        mean_ref[:, sl] = jnp.mean(y, axis=0, keepdims=True)
        elif kind == "q_plain_bf16":
            outs.pop(0)[...] = (h * Q_SOFTMAX_SCALE).astype(BF16)
        elif kind == "kv_plain":
            o_ref = outs.pop(0)
            for t in range(n_t):
                store_rows(o_ref, layout, t, n_t, h[:, t * LANES:(t + 1) * LANES])
            outs.pop(0)[...] = h.astype(BF16)
        elif kind == "gate":
            outs.pop(0)[...] = h / (1.0 + jnp.exp(-h))
        else:
            raise ValueError(kind)
    assert not outs


def _rope_tables(pos):
    half = ROT_DIM // 2
    inv = ROPE_THETA ** (-jnp.arange(0, ROT_DIM, 2, dtype=F32) / ROT_DIM)
    ang = pos.astype(F32)[:, None] * inv[None, :]
    cos, sin = jnp.cos(ang), jnp.sin(ang)
    t = pos.shape[0]
    z_half = jnp.zeros((t, half), F32)
    z_rest = jnp.zeros((t, HEAD_DIM - ROT_DIM), F32)
    c = jnp.concatenate([cos, cos, jnp.ones((t, HEAD_DIM - ROT_DIM), F32)], axis=-1)
    s1 = jnp.concatenate([-sin, z_half, z_rest], axis=-1)
    s2 = jnp.concatenate([z_half, sin, z_rest], axis=-1)
    rep = LANES // HEAD_DIM
    return tuple(jnp.tile(a, (1, rep)) for a in (c, s1, s2))


def _project(x, g_norm, w_bf16, gq, gk, tables, plan, tm, seq, name):
    m, d = x.shape
    n_tab = tables[0].shape[0] // tm
    tiles_per_seq = seq // tm
    out_shapes, out_specs = [], []
    for kind, _, width, layout in plan:
        row = pl.BlockSpec((tm, width), lambda i: (i, 0))
        if kind in ("q_rope_bf16", "q_plain_bf16"):
            out_shapes += [jax.ShapeDtypeStruct((m, width), BF16)]
            out_specs += [row]
        elif kind in ("q_rope_f32", "gate"):
            out_shapes += [jax.ShapeDtypeStruct((m, width), F32)]
            out_specs += [row]
        elif kind in ("k_rope", "kv_plain", "k_rope_mean"):
            if layout == "rows":
                out_shapes += [jax.ShapeDtypeStruct((m, width), F32)]
                out_specs += [row]
            elif layout == "cols":
                out_shapes += [jax.ShapeDtypeStruct((m // seq, width, seq), F32)]
                out_specs += [pl.BlockSpec((None, width, tm),
                                           lambda i: (i // tiles_per_seq, 0, i % tiles_per_seq))]
            else:
                n_t = width // LANES
                out_shapes += [jax.ShapeDtypeStruct((m * n_t, LANES), F32)]
                out_specs += [pl.BlockSpec((tm * n_t, LANES), lambda i: (i, 0))]
            out_shapes += [jax.ShapeDtypeStruct((m, width), BF16)]
            out_specs += [row]
            if kind == "k_rope_mean":
                assert tm == MOBA_BLOCK
                out_shapes += [jax.ShapeDtypeStruct((m // tm, 1, width), F32)]
                out_specs += [pl.BlockSpec((None, 1, width), lambda i: (i, 0, 0))]
    tab_spec = pl.BlockSpec((tm, LANES), lambda i: (i % n_tab, 0))
    vec_d = pl.BlockSpec((1, d), lambda i: (0, 0))
    vec_l = pl.BlockSpec((1, LANES), lambda i: (0, 0))
    rep = LANES // HEAD_DIM
    return pl.pallas_call(
        functools.partial(_proj_kernel, plan=plan),
        grid=(m // tm,),
        in_specs=[pl.BlockSpec((tm, d), lambda i: (i, 0)), vec_d,
                  pl.BlockSpec(w_bf16.shape, lambda i: (0, 0), pipeline_mode=pl.Buffered(1)),
                  vec_l, vec_l, tab_spec, tab_spec, tab_spec],
        out_specs=out_specs,
        out_shape=out_shapes,
        compiler_params=_params(("parallel",)),
        name=name,
    )(x, g_norm.reshape(1, d), w_bf16, jnp.tile(gq, rep).reshape(1, LANES), jnp.tile(gk, rep).reshape(1, LANES),
      *tables)


def _finish_kernel(x_ref, w_ref, *refs):
    mix_refs, y_ref = refs[:-1], refs[-1]
    y = x_ref[...]
    row0 = 0
    for mix_ref in mix_refs:
        rows = mix_ref.shape[1]
        y = y + _dot(mix_ref[...], w_ref[row0:row0 + rows, :])
        row0 += rows
    y_ref[...] = y


def _finish(x, mix_parts, w_bf16, tm, name):
    m, d = x.shape
    kdim = w_bf16.shape[0]
    assert sum(p.shape[1] for p in mix_parts) == kdim
    return pl.pallas_call(
        _finish_kernel,
        grid=(m // tm,),
        in_specs=[pl.BlockSpec((tm, d), lambda i: (i, 0)), pl.BlockSpec((kdim, d), lambda i: (0, 0))]
        + [pl.BlockSpec((tm, p.shape[1]), lambda i: (i, 0)) for p in mix_parts],
        out_specs=pl.BlockSpec((tm, d), lambda i: (i, 0)),
        out_shape=jax.ShapeDtypeStruct((m, d), F32),
        compiler_params=_params(("parallel",)),
        name=name,
    )(x, w_bf16, *mix_parts)


def _lambda_value(lp, lam_init):
    a = jnp.sum(lp[0:1] * lp[1:2], axis=-1, keepdims=True)
    b = jnp.sum(lp[2:3] * lp[3:4], axis=-1, keepdims=True)
    return jnp.exp(a) - jnp.exp(b) + lam_init


def _softmax_tile(qq, k, v, carry, mask):
    m, l, acc = carry
    s = _dot_nt(qq, k)
    if mask is not None:
        s = jnp.where(mask, s, NEG)
    m_new = jnp.maximum(m, jnp.max(s, axis=-1, keepdims=True))
    alpha = jnp.exp2(m - m_new)
    p = jnp.exp2(s - m_new)
    l = alpha * l + jnp.sum(p, axis=-1, keepdims=True)
    acc = alpha * acc + _dot(p.astype(BF16), v)
    return m_new, l, acc


def _dot_tn(a, b):
    return lax.dot_general(a, b, (((0,), (0,)), ((), ())), preferred_element_type=F32)


def _softmax_tile_t(qq_t, k_ref, v_ref, start, n_keys, carry, mask):
    s, m_s = _scores_t(qq_t, k_ref, start, n_keys, mask)
    return _absorb_t(v_ref, start, n_keys, s, m_s, carry)


def _scores_t(qq_t, k_ref, start, n_keys, mask):
    s = _dot(k_ref[pl.ds(start, n_keys), :], qq_t)
    if mask is not None:
        s = jnp.concatenate([jnp.where(mk, s[c * ATTN_SLAB:(c + 1) * ATTN_SLAB], NEG)
                             for c, mk in enumerate(mask)], axis=0)
    return s, jnp.max(s, axis=0, keepdims=True)


def _absorb_t(v_ref, start, n_keys, s, m_s, carry):
    m, l, acc = carry
    m_new = jnp.maximum(m, m_s)
    alpha = jnp.exp2(m - m_new)
    p = jnp.exp2(s - m_new)
    l = alpha * l + jnp.sum(p, axis=0, keepdims=True)
    acc = alpha * acc + _dot_tn(v_ref[pl.ds(start, n_keys), :], p.astype(BF16))
    return m_new, l, acc


def _pipelined_pairs(n_pairs, last_tile, n_keys, qq_t, k_ref, v_ref, mask_of, s_refs, carry):
    s_a, s_b = s_refs

    def scores(t, s_ref):
        s, m_s = _scores_t(qq_t, k_ref, pl.multiple_of(t * n_keys, n_keys), n_keys, mask_of(t))
        s_ref[...] = s
        return m_s

    def absorb(t, s_ref, m_s, carry):
        return _absorb_t(v_ref, pl.multiple_of(t * n_keys, n_keys), n_keys, s_ref[...], m_s, carry)

    def trip(u, state):
        m_a, carry = state
        m_b = scores(2 * u + 1, s_b)
        carry = absorb(2 * u, s_a, m_a, carry)
        m_a = scores(jnp.minimum(2 * u + 2, last_tile), s_a)
        carry = absorb(2 * u + 1, s_b, m_b, carry)
        return m_a, carry

    return lax.fori_loop(0, n_pairs, trip, (scores(n_pairs * 0, s_a), carry))[1]


def _causal_masks(k_start, q_start, tq, n_keys, r, strict):
    q_pos = q_start + _iota((ATTN_SLAB, r), 1) % tq
    masks = []
    for c in range(n_keys // ATTN_SLAB):
        k_pos = k_start + c * ATTN_SLAB + _iota((ATTN_SLAB, r), 0)
        masks.append(k_pos < q_pos if strict else k_pos <= q_pos)
    return masks


def _stack_pair_t(q_t):
    row = _iota(q_t.shape, 0)
    zero = jnp.zeros_like(q_t)
    return jnp.concatenate([jnp.where(row < HEAD_DIM, q_t, zero), jnp.where(row >= HEAD_DIM, q_t, zero)], axis=1)


def _diff_attn_kernel(lam_ref, subg_ref, q_ref, k_ref, v_ref, sg_ref, o_ref, s_a, s_b, *, tq, tk, lam_init):
    i = pl.program_id(2)
    r = 2 * tq
    qq_t = _stack_pair_t(jnp.transpose(q_ref[...].astype(F32))).astype(BF16)
    n_full = (i * tq) // tk
    half = tk // 2

    carry = (jnp.full((1, r), NEG, F32), jnp.zeros((1, r), F32), jnp.zeros((LANES, r), F32))
    start = pl.multiple_of(n_full * tk, tk)
    carry = _softmax_tile_t(qq_t, k_ref, v_ref, start, tk, carry,
                            _causal_masks(start, i * tq, tq, tk, r, strict=False))
    _, l, acc = _pipelined_pairs(n_full, k_ref.shape[0] // half - 1, half, qq_t, k_ref, v_ref,
                                 lambda t: None, (s_a, s_b), carry)

    a = acc / l
    lam = _lambda_value(lam_ref[...], lam_init)
    o = jnp.transpose(a[:, :tq] - lam * a[:, tq:])
    ms = jnp.mean(o * o, axis=-1, keepdims=True)
    o = o * lax.rsqrt(ms + RMS_EPS) * subg_ref[...] * (1.0 - lam_init)
    o_ref[...] = (o * sg_ref[...]).astype(BF16)


def _softplus2(z):
    return jnp.maximum(z, 0.0) + jnp.log2(1.0 + jnp.exp2(-jnp.abs(z)))


def _later_matrix(tk, keys_axis=1):
    a, b = _iota((tk, tk), 0), _iota((tk, tk), 1)
    return (a > b if keys_axis == 1 else b > a).astype(BF16)


def _sb_kernel(q_ref, k_ref, v_ref, sg_ref, o_ref, *, tq, tk):
    i = pl.program_id(2)
    r = 2 * tq
    qq_t = _stack_pair_t(jnp.transpose(q_ref[...].astype(F32))).astype(BF16)
    tri = _later_matrix(ATTN_SLAB, keys_axis=0)
    n_full = (i * tq) // tk
    n_slabs = tk // ATTN_SLAB

    def tile(st, c, acc, valid):
        parts = []
        for u in range(n_slabs):
            z = _dot(k_ref[pl.ds(st + u * ATTN_SLAB, ATTN_SLAB), :], qq_t)
            sp = _softplus2(z)
            log_keep = -sp if valid is None else jnp.where(valid[u], -sp, 0.0)
            later = _dot(tri, log_keep.astype(BF16))
            parts.append((z - sp + later, later[0:1, :] + log_keep[0:1, :]))
        for u in reversed(range(n_slabs)):
            e, whole = parts[u]
            w = jnp.exp2(e + c)
            if valid is not None:
                w = jnp.where(valid[u], w, 0.0)
            acc = acc + _dot_tn(v_ref[pl.ds(st + u * ATTN_SLAB, ATTN_SLAB), :], w.astype(BF16))
            c = c + whole
        return c, acc

    start = pl.multiple_of(n_full * tk, tk)
    c, acc = tile(start, jnp.zeros((1, r), F32), jnp.zeros((LANES, r), F32),
                  _causal_masks(start, i * tq, tq, tk, r, strict=True))

    def live(state):
        t, c_max, _, _ = state
        return jnp.logical_and(t < n_full, c_max > -SB_DEAD_LOG2)

    def full_tile(state):
        t, _, c, acc = state
        c, acc = tile(pl.multiple_of((n_full - 1 - t) * tk, tk), c, acc, None)
        return t + 1, jnp.max(c), c, acc

    _, _, _, acc = lax.while_loop(live, full_tile, (jnp.int32(0), jnp.max(c), c, acc))
    row = _iota((LANES, tq), 0)
    o = jnp.transpose(jnp.where(row < HEAD_DIM, acc[:, :tq], acc[:, tq:]))
    o_ref[...] = (o * sg_ref[...]).astype(BF16)


def _top_blocks(g, n_valid, axis=-1):
    axis = axis % g.ndim
    blk = _iota(g.shape, axis)
    nb = g.shape[axis]
    g = jnp.where(blk < n_valid, g, -jnp.inf)
    sel = jnp.zeros(g.shape, jnp.bool_)
    for _ in range(min(MOBA_TOPK, nb)):
        mx = jnp.max(g, axis=axis, keepdims=True)
        idx = jnp.min(jnp.where(g == mx, blk, nb), axis=axis, keepdims=True)
        pick = blk == idx
        sel = jnp.logical_or(sel, pick)
        g = jnp.where(pick, -jnp.inf, g)
    return jnp.logical_and(sel, blk < n_valid)


def _moba_kernel(q_ref, k_ref, v_ref, km_ref, sg_ref, o_ref, sel_s, s_a, s_b, *, tq, group):
    i = pl.program_id(2)
    r = 2 * tq
    qf_t = _stack_pair_t(jnp.transpose(q_ref[...]))
    row = _iota((LANES, tq), 0)
    q_hi, q_lo = _split_bf16(qf_t)
    qq_t = (qf_t * Q_SOFTMAX_SCALE).astype(BF16)

    km_hi, km_lo = _split_bf16(km_ref[...])
    g = _dot(km_hi, q_hi) + _dot(km_hi, q_lo) + _dot(km_lo, q_hi)
    sel_s[...] = jnp.where(_top_blocks(g, i, axis=0), 1.0, 0.0)

    start = pl.multiple_of(i * tq, tq)
    carry = (jnp.full((1, r), NEG, F32), jnp.zeros((1, r), F32), jnp.zeros((LANES, r), F32))
    carry = _softmax_tile_t(qq_t, k_ref, v_ref, start, tq, carry, _causal_masks(0, 0, tq, tq, r, strict=False))

    per_tile = group // 2

    def chosen(t):
        return [sel_s[pl.ds(t * per_tile + c, 1), :] > 0.0 for c in range(per_tile)]

    _, l, acc = _pipelined_pairs((i + group - 1) // group, sel_s.shape[0] // per_tile - 1, per_tile * tq,
                                 qq_t, k_ref, v_ref, chosen, (s_a, s_b), carry)
    o_t = acc / l
    o = jnp.transpose(jnp.where(row < HEAD_DIM, o_t[:, :tq], o_t[:, tq:]))
    o_ref[...] = (o * sg_ref[...]).astype(BF16)


def _attn_specs(tq, s, group0=0):
    q_spec = pl.BlockSpec((None, tq, LANES), lambda b, h, i: (b, i, h + group0))
    kv_spec = pl.BlockSpec((None, s, LANES), lambda b, h, i: (b, 0, h))
    return q_spec, kv_spec


def _diff_attn_prompt(q16, k16, v16, sgate, lam_p, subln_g, lam_init, tq, tk):
    b, s, w = q16.shape
    q_spec, kv_spec = _attn_specs(tq, s)
    return pl.pallas_call(
        functools.partial(_diff_attn_kernel, tq=tq, tk=tk, lam_init=lam_init),
        grid=(b, w // LANES, s // tq),
        in_specs=[pl.BlockSpec(lam_p.shape, lambda b, h, i: (0, 0)),
                  pl.BlockSpec((1, LANES), lambda b, h, i: (0, 0)),
                  q_spec, kv_spec, kv_spec, q_spec],
        out_specs=q_spec,
        out_shape=jax.ShapeDtypeStruct((b, s, w), BF16),
        scratch_shapes=[pltpu.VMEM((tk // 2, 2 * tq), F32)] * 2,
        compiler_params=_params(("parallel", "parallel", "arbitrary")),
        name="diff_attn_prompt",
    )(lam_p, subln_g.reshape(1, LANES), q16, k16, v16, sgate)


def _sb_prompt(q16, k16, v16, sgate, gate_group0, tq, tk):
    b, s, w = q16.shape
    q_spec, kv_spec = _attn_specs(tq, s)
    sg_spec = pl.BlockSpec((None, tq, LANES), lambda b, h, i: (b, i, h + gate_group0))
    return pl.pallas_call(
        functools.partial(_sb_kernel, tq=tq, tk=tk),
        grid=(b, w // LANES, s // tq),
        in_specs=[q_spec, kv_spec, kv_spec, sg_spec],
        out_specs=q_spec,
        out_shape=jax.ShapeDtypeStruct((b, s, w), BF16),
        compiler_params=_params(("parallel", "parallel", "arbitrary")),
        name="sb_prompt",
    )(q16, k16, v16, sgate)


def _moba_prompt(q, k16, v16, kmean, sgate):
    b, s, w = q.shape
    tq = MOBA_BLOCK
    nb = s // tq
    q_spec, kv_spec = _attn_specs(tq, s)
    group = math.gcd(nb, MOBA_GROUP)
    return pl.pallas_call(
        functools.partial(_moba_kernel, tq=tq, group=group),
        grid=(b, w // LANES, nb),
        in_specs=[q_spec, kv_spec, kv_spec,
                  pl.BlockSpec((None, nb, LANES), lambda b, h, i: (b, 0, h)), q_spec],
        out_specs=q_spec,
        out_shape=jax.ShapeDtypeStruct((b, s, w), BF16),
        scratch_shapes=[pltpu.VMEM((nb, 2 * tq), F32)] + [pltpu.VMEM((group // 2 * tq, 2 * tq), F32)] * 2,
        compiler_params=_params(("parallel", "parallel", "arbitrary")),
        name="moba_prompt",
    )(q, k16, v16, kmean, sgate)


def _row_heads(x_row, n_rows, lanes_per_row):
    w = x_row.shape[1]
    keep = _iota((n_rows, w), 1) // lanes_per_row == _iota((n_rows, w), 0)
    return jnp.where(keep, jnp.broadcast_to(x_row, (n_rows, w)), 0.0)


def _dup_rows(x, reps):
    n, w = x.shape
    row = _iota((n * reps, w), 0) // reps
    out = jnp.zeros((n * reps, w), x.dtype)
    for h in range(n):
        out = jnp.where(row == h, x[h:h + 1, :], out)
    return out


def _stack_rows(rows):
    n, w = len(rows), rows[0].shape[1]
    row = _iota((n, w), 0)
    out = jnp.zeros((n, w), rows[0].dtype)
    for h, r in enumerate(rows):
        out = jnp.where(row == h, r, out)
    return out


def _head_dots(qcol_ref, kt_ref, n_heads):
    rows = []
    for h in range(n_heads):
        sl = slice(h * HEAD_DIM, (h + 1) * HEAD_DIM)
        rows.append(jnp.sum(qcol_ref[sl, :] * kt_ref[sl, :], axis=0, keepdims=True))
    return _stack_rows(rows)


def _add_weighted(acc_ref, w, vt_refs, n_heads):
    for h in range(n_heads):
        sl = slice(h * HEAD_DIM, (h + 1) * HEAD_DIM)
        a = acc_ref[sl, :]
        for u, vt_ref in enumerate(vt_refs):
            a = a + w[u * n_heads + h:u * n_heads + h + 1, :] * vt_ref[sl, :]
        acc_ref[sl, :] = a


def _sb_weights_pages(zs, tri, c):
    n_h = zs[0].shape[0]
    z = jnp.concatenate(zs, axis=0)
    sp = _softplus2(z)
    log_keep = -sp
    hi, lo = _split_bf16(log_keep)
    rows = z.shape[0]
    both = _dot(jnp.concatenate([hi, lo], axis=0), tri)
    later = both[:rows] + both[rows:]
    whole = later[:, 0:1] + log_keep[:, 0:1]
    offsets = []
    for u in range(len(zs)):
        offsets.append(c)
        c = c + whole[u * n_h:(u + 1) * n_h]
    return jnp.exp2(z - sp + later + jnp.concatenate(offsets, axis=0)), c


def _lane_sums_as_row(x):
    hi, lo = _split_bf16(x)
    ones = jnp.ones((8, LANES), BF16)
    return (_dot_nt(ones, hi) + _dot_nt(ones, lo))[0:1, :]


def _decode_even_kernel(pt_ref, lam_ref, subg_ref, qa_ref, kan_ref, van_ref, qb_ref, sg_ref, *refs,
                        pps, lam_init):
    del pt_ref
    page_refs = refs[:4 * pps]
    o_ref = refs[4 * pps]
    qa_s, m_s, l_s, acca_s, c_s, accb_s = refs[4 * pps + 1:]
    step = pl.program_id(1)
    n_ha = qa_ref.shape[0]
    n_a = 2 * n_ha
    n_b = accb_s.shape[0] // HEAD_DIM
    wa = n_ha * LANES

    @pl.when(step == 0)
    def _():
        rows = _dup_rows(qa_ref[...].astype(F32), 2)
        qa = jnp.where(_iota(rows.shape, 1) // HEAD_DIM == _iota(rows.shape, 0) % 2, rows, 0.0)
        qa_s[...] = qa
        m_s[...] = jnp.sum(qa * _dup_rows(kan_ref[...], 2), axis=-1, keepdims=True)
        l_s[...] = jnp.ones_like(l_s)
        acca_s[...] = _dup_rows(van_ref[...], 2)
        c_s[...] = jnp.zeros_like(c_s)
        accb_s[...] = jnp.zeros_like(accb_s)

    rows_a = pps * page_refs[0].shape[0]
    own_head = _iota((n_a, rows_a), 1) % n_ha == _iota((n_a, rows_a), 0) // 2
    ak = jnp.concatenate([page_refs[4 * u][...].astype(BF16) for u in range(pps)], axis=0)
    av = jnp.concatenate([page_refs[4 * u + 1][...].astype(BF16) for u in range(pps)], axis=0)
    m_s[...], l_s[...], acca_s[...] = _softmax_tile(qa_s[...].astype(BF16), ak, av,
                                                    (m_s[...], l_s[...], acca_s[...]), own_head)
    zs = [_head_dots(qb_ref, page_refs[4 * u + 2], n_b) for u in range(pps)]
    w, c_s[...] = _sb_weights_pages(zs, _later_matrix(LANES), c_s[...])
    _add_weighted(accb_s, w, [page_refs[4 * u + 3] for u in range(pps)], n_b)

    @pl.when(step == pl.num_programs(1) - 1)
    def _():
        lam = _lambda_value(lam_ref[...], lam_init)
        a = acca_s[...] / l_s[...]
        for h in range(n_ha):
            sl = slice(h * LANES, (h + 1) * LANES)
            x = a[2 * h:2 * h + 1, :] - lam * a[2 * h + 1:2 * h + 2, :]
            ms = jnp.mean(x * x, axis=-1, keepdims=True)
            x = x * lax.rsqrt(ms + RMS_EPS) * subg_ref[...] * (1.0 - lam_init)
            o_ref[:, sl] = (x * sg_ref[:, sl]).astype(BF16)
        o_ref[:, wa:] = (_lane_sums_as_row(accb_s[...]) * sg_ref[:, wa:]).astype(BF16)


def _page_specs(cache, li, n_pages, pps, reverse):
    r = cache.shape[2]

    def make(u):
        def index(b, s, pt):
            j = s * pps + u
            j = n_pages - 1 - j if reverse else j
            return (li, pt[b, j], 0, 0)
        return pl.BlockSpec((None, None, r, LANES), index)

    return [make(u) for u in range(pps)]


def _col_bcast(x):
    return jnp.broadcast_to(x.astype(F32)[:, :, None], x.shape + (LANES,))


def _decode_even(pt, lam_p, subln_g, lam_init, qa16, ka, va, qb16, sgate, caches, li, pps):
    db, n_pages = pt.shape
    wa, wb = qa16.shape[-1], qb16.shape[-1]
    n_ha = wa // LANES
    whole = lambda shp: pl.BlockSpec((None,) + shp, lambda b, s, pt: (b,) + (0,) * len(shp))
    specs_by_cache = [_page_specs(c, li, n_pages, pps, True) for c in caches]
    page_specs = [specs_by_cache[t][u] for u in range(pps) for t in range(4)]
    page_args = [caches[t] for u in range(pps) for t in range(4)]
    grid_spec = pltpu.PrefetchScalarGridSpec(
        num_scalar_prefetch=1,
        grid=(db, n_pages // pps),
        in_specs=[pl.BlockSpec(lam_p.shape, lambda b, s, pt: (0, 0)),
                  pl.BlockSpec((1, LANES), lambda b, s, pt: (0, 0)),
                  whole((n_ha, LANES)), whole((n_ha, LANES)), whole((n_ha, LANES)), whole((wb, LANES)),
                  whole((1, wa + wb))] + page_specs,
        out_specs=whole((1, wa + wb)),
        scratch_shapes=[pltpu.VMEM((2 * n_ha, LANES), F32),
                        pltpu.VMEM((2 * n_ha, 1), F32), pltpu.VMEM((2 * n_ha, 1), F32),
                        pltpu.VMEM((2 * n_ha, LANES), F32),
                        pltpu.VMEM((wb // HEAD_DIM, 1), F32), pltpu.VMEM((wb, LANES), F32)],
    )
    heads = lambda x: x.reshape(db, n_ha, LANES)
    return pl.pallas_call(
        functools.partial(_decode_even_kernel, pps=pps, lam_init=lam_init),
        grid_spec=grid_spec,
        out_shape=jax.ShapeDtypeStruct((db, 1, wa + wb), BF16),
        compiler_params=_params(("parallel", "arbitrary")),
        name="decode_even",
    )(pt, lam_p, subln_g.reshape(1, LANES), heads(qa16), heads(ka), heads(va), _col_bcast(qb16),
      sgate.reshape(db, 1, wa + wb), *page_args)


def _moba_scores_kernel(pt_ref, qcol_ref, qrow_ref, kn_ref, *refs, pps, n_blocks):
    del pt_ref
    page_refs = refs[:pps]
    p_ref, pnew_ref = refs[pps], refs[pps + 1]
    s_s = refs[pps + 2]
    step = pl.program_id(1)
    n_h = qcol_ref.shape[0] // HEAD_DIM
    ppb = MOBA_BLOCK // LANES

    for u in range(pps):
        s_s[step * pps + u] = _head_dots(qcol_ref, page_refs[u], n_h)

    @pl.when(step == pl.num_programs(1) - 1)
    def _():
        lane = _iota((n_h, LANES), 1)
        g = jnp.zeros((n_h, LANES), F32)
        for n in range(n_blocks):
            tot = sum(s_s[n * ppb + t] for t in range(ppb))
            g = jnp.where(lane == n, jnp.sum(tot, axis=-1, keepdims=True) * (1.0 / MOBA_BLOCK), g)
        sel = _top_blocks(g, n_blocks)
        chosen = [sel[:, n:n + 1] for n in range(n_blocks)]
        s_new = jnp.sum(_row_heads(qrow_ref[...], n_h, HEAD_DIM) * kn_ref[...], axis=-1, keepdims=True) * QK_SCALE
        m_lanes = jnp.full((n_h, LANES), NEG, F32)
        for j in range(n_blocks * ppb):
            m_lanes = jnp.maximum(m_lanes, jnp.where(chosen[j // ppb], s_s[j] * QK_SCALE, NEG))
        m = jnp.maximum(jnp.max(m_lanes, axis=-1, keepdims=True), s_new)
        l_lanes = jnp.zeros((n_h, LANES), F32)
        for j in range(n_blocks * ppb):
            e = jnp.where(chosen[j // ppb], jnp.exp(s_s[j] * QK_SCALE - m), 0.0)
            s_s[j] = e
            l_lanes = l_lanes + e
        e_new = jnp.exp(s_new - m)
        inv = 1.0 / (jnp.sum(l_lanes, axis=-1, keepdims=True) + e_new)
        for j in range(n_blocks * ppb):
            p_ref[j] = s_s[j] * inv
        pnew_ref[...] = jnp.broadcast_to(e_new * inv, pnew_ref.shape)


def _moba_values_kernel(pt_ref, p_ref, pnew_ref, vn_ref, sg_ref, *refs, pps):
    del pt_ref
    page_refs = refs[:pps]
    o_ref = refs[pps]
    acc_s = refs[pps + 1]
    step = pl.program_id(1)
    n_h = acc_s.shape[0] // HEAD_DIM

    @pl.when(step == 0)
    def _():
        acc_s[...] = jnp.zeros_like(acc_s)

    w = jnp.concatenate([p_ref[u] for u in range(pps)], axis=0)
    _add_weighted(acc_s, w, list(page_refs), n_h)

    @pl.when(step == pl.num_programs(1) - 1)
    def _():
        o = _lane_sums_as_row(acc_s[...]) + pnew_ref[...] * vn_ref[...]
        o_ref[...] = (o * sg_ref[...]).astype(BF16)


def _decode_odd(pt, q, k_new, v_new, sgate, cache_k, cache_v, li, pps):
    db, n_pages = pt.shape
    w = q.shape[-1]
    n_h = w // HEAD_DIM
    page = LANES
    past = n_pages * page
    assert past % MOBA_BLOCK == 0 and past >= MOBA_BLOCK and MOBA_BLOCK % page == 0
    n_blocks = past // MOBA_BLOCK
    row = lambda wd: pl.BlockSpec((None, 1, wd), lambda b, s, pt: (b, 0, 0))
    r3 = lambda x: x.reshape(db, 1, x.shape[-1])
    probs, p_new = pl.pallas_call(
        functools.partial(_moba_scores_kernel, pps=pps, n_blocks=n_blocks),
        grid_spec=pltpu.PrefetchScalarGridSpec(
            num_scalar_prefetch=1,
            grid=(db, n_pages // pps),
            in_specs=[pl.BlockSpec((None, w, LANES), lambda b, s, pt: (b, 0, 0)), row(w), row(w)]
            + _page_specs(cache_k, li, n_pages, pps, False),
            out_specs=[pl.BlockSpec((None, n_pages, n_h, page), lambda b, s, pt: (b, 0, 0, 0)),
                       pl.BlockSpec((None, n_h, LANES), lambda b, s, pt: (b, 0, 0))],
            scratch_shapes=[pltpu.VMEM((n_pages, n_h, page), F32)],
        ),
        out_shape=[jax.ShapeDtypeStruct((db, n_pages, n_h, page), F32),
                   jax.ShapeDtypeStruct((db, n_h, LANES), F32)],
        compiler_params=_params(("parallel", "arbitrary")),
        name="decode_moba_scores",
    )(pt, _col_bcast(q), r3(q), r3(k_new), *([cache_k] * pps))
    p_new_row = jnp.repeat(p_new[:, :, 0], HEAD_DIM, axis=-1).reshape(db, 1, w)
    return pl.pallas_call(
        functools.partial(_moba_values_kernel, pps=pps),
        grid_spec=pltpu.PrefetchScalarGridSpec(
            num_scalar_prefetch=1,
            grid=(db, n_pages // pps),
            in_specs=[pl.BlockSpec((None, pps, n_h, page), lambda b, s, pt: (b, s, 0, 0)),
                      row(w), row(w), row(w)] + _page_specs(cache_v, li, n_pages, pps, False),
            out_specs=row(w),
            scratch_shapes=[pltpu.VMEM((w, LANES), F32)],
        ),
        out_shape=jax.ShapeDtypeStruct((db, 1, w), BF16),
        compiler_params=_params(("parallel", "arbitrary")),
        name="decode_moba_values",
    )(pt, probs, p_new_row, r3(v_new), r3(sgate), *([cache_v] * pps))


PROMPT_TM = 256
ATTN_TQ = 512
ATTN_TK = 512
MOBA_GROUP = 4
PAGES_PER_STEP = 8


def _rows_view(c):
    l, n, p, h, hd = c.shape
    assert p == LANES and hd == LANES
    return c.reshape(l, n, p * h, hd)


def _cols_view(c):
    l, n, p, h, hd = c.shape
    assert p == LANES and hd == HEAD_DIM
    return jnp.transpose(c, (0, 1, 3, 4, 2)).reshape(l, n, h * hd, p)


def _heads_from_cols(x, n_heads):
    b, w, s = x.shape
    return jnp.transpose(x.reshape(b, n_heads, w // n_heads, s), (0, 3, 1, 2))


def _even_layer(x_p, x_s, caches, li, pt, g_norm, w_in, qk_g, lam_p, subln_g, w_out, layer):
    lam_init = 0.8 - 0.6 * math.exp(-0.3 * layer)
    b, s, d = x_p.shape
    db, t, _ = x_s.shape
    wa = w_out.shape[0] // 2
    wb = wa
    def plan(wide, narrow):
        return (("q_rope_bf16", 0, wa, None), ("k_rope", wa, wa, wide), ("kv_plain", 2 * wa, wa, wide),
                ("q_plain_bf16", 3 * wa, wb, None), ("kv_plain", 3 * wa + wb, wb, narrow),
                ("kv_plain", 3 * wa + 2 * wb, wb, narrow), ("gate", 3 * wa + 3 * wb, wa + wb, None))

    w_in16 = w_in.astype(BF16)
    w_out16 = w_out.astype(BF16)
    past = pt.shape[1] * LANES
    n_a = wa // (2 * HEAD_DIM)
    n_b = wb // HEAD_DIM

    tabs = _rope_tables(jnp.arange(s))
    qa16, ka, ka16, va, va16, qb16, kb, kb16, vb, vb16, sgate = _project(
        x_p.reshape(b * s, d), g_norm, w_in16, qk_g[0], qk_g[1], tabs, plan("heads", "cols"), PROMPT_TM, s,
        "proj_even_prompt")
    r3 = lambda a: a.reshape(b, s, a.shape[-1])
    sg3 = r3(sgate)
    mix_a = _diff_attn_prompt(r3(qa16), r3(ka16), r3(va16), sg3, lam_p, subln_g, lam_init, ATTN_TQ, ATTN_TK)
    mix_b = _sb_prompt(r3(qb16), r3(kb16), r3(vb16), sg3, wa // LANES, ATTN_TQ, ATTN_TK)
    y_p = _finish(x_p.reshape(b * s, d), [mix_a.reshape(b * s, wa), mix_b.reshape(b * s, wb)], w_out16, 512,
                  "finish_even_prompt").reshape(b, s, d)

    tabs_s = _rope_tables(jnp.tile(past + jnp.arange(t), db))
    qa16_s, ka_s, _, va_s, _, qb16_s, kb_s, _, vb_s, _, sgate_s = _project(
        x_s.reshape(db * t, d), g_norm, w_in16, qk_g[0], qk_g[1], tabs_s, plan("rows", "rows"), db * t, db * t,
        "proj_even_sample")
    assert t == 1
    mix_s = _decode_even(pt, lam_p, subln_g, lam_init, qa16_s, ka_s, va_s, qb16_s, sgate_s,
                         caches, li, PAGES_PER_STEP)
    y_s = _finish(x_s.reshape(db * t, d), [mix_s.reshape(db * t, wa + wb)], w_out16, db * t,
                  "finish_even_sample").reshape(db, t, d)

    rows_p = (ka.reshape(b, s, n_a, 2 * HEAD_DIM), va.reshape(b, s, n_a, 2 * HEAD_DIM),
              _heads_from_cols(kb, n_b), _heads_from_cols(vb, n_b))
    rows_s = (ka_s.reshape(db, t, n_a, 2 * HEAD_DIM), va_s.reshape(db, t, n_a, 2 * HEAD_DIM),
              kb_s.reshape(db, t, n_b, HEAD_DIM), vb_s.reshape(db, t, n_b, HEAD_DIM))
    return y_p, y_s, rows_p, rows_s


def _odd_layer(x_p, x_s, cache_k, cache_v, li, pt, g_norm, w_in, qk_g, w_out):
    b, s, d = x_p.shape
    db, t, _ = x_s.shape
    wc = w_out.shape[0]
    w_in16 = w_in.astype(BF16)
    w_out16 = w_out.astype(BF16)
    past = pt.shape[1] * LANES
    assert s % MOBA_BLOCK == 0

    def plan(k_kind, layout):
        return (("q_rope_f32", 0, wc, None), (k_kind, wc, wc, layout), ("kv_plain", 2 * wc, wc, layout),
                ("gate", 3 * wc, wc, None))

    tabs = _rope_tables(jnp.arange(s))
    q, k, k16, kmean, v, v16, sgate = _project(
        x_p.reshape(b * s, d), g_norm, w_in16, qk_g[0], qk_g[1], tabs, plan("k_rope_mean", "cols"), MOBA_BLOCK, s,
        "proj_odd_prompt")
    r3 = lambda a: a.reshape(b, s, a.shape[-1])
    mix = _moba_prompt(r3(q), r3(k16), r3(v16), kmean.reshape(b, s // MOBA_BLOCK, wc), r3(sgate))
    y_p = _finish(x_p.reshape(b * s, d), [mix.reshape(b * s, wc)], w_out16, 512,
                  "finish_odd_prompt").reshape(b, s, d)

    assert t == 1
    tabs_s = _rope_tables(jnp.tile(past + jnp.arange(t), db))
    q_s, k_s, _, v_s, _, sgate_s = _project(
        x_s.reshape(db * t, d), g_norm, w_in16, qk_g[0], qk_g[1], tabs_s, plan("k_rope", "rows"), db * t, db * t,
        "proj_odd_sample")
    mix_s = _decode_odd(pt, q_s, k_s, v_s, sgate_s, cache_k, cache_v, li, PAGES_PER_STEP)
    y_s = _finish(x_s.reshape(db * t, d), [mix_s.reshape(db * t, wc)], w_out16, db * t,
                  "finish_odd_sample").reshape(db, t, d)

    n_c = wc // HEAD_DIM
    return (y_p, y_s, (_heads_from_cols(k, n_c), _heads_from_cols(v, n_c)),
            (k_s.reshape(db, t, n_c, HEAD_DIM), v_s.reshape(db, t, n_c, HEAD_DIM)))


def kernel(x_prompt, x_sample, cache_a_k, cache_a_v, cache_b_k, cache_b_v, cache_c_k, cache_c_v, page_table,
           norm_even, w_in_even, qk_norm_a, lambda_a, subln_a, w_out_even, norm_odd, w_in_odd, qk_norm_c,
           w_out_odd):
    depth = norm_even.shape[0] + norm_odd.shape[0]
    even_caches = [_rows_view(cache_a_k), _rows_view(cache_a_v), _cols_view(cache_b_k), _cols_view(cache_b_v)]
    odd_k, odd_v = _cols_view(cache_c_k), _cols_view(cache_c_v)
    y_p, y_s = x_prompt, x_sample
    ev_p, ev_s, od_p, od_s = [], [], [], []
    for layer in range(depth):
        i = layer // 2
        if layer % 2 == 0:
            y_p, y_s, rp, rs = _even_layer(y_p, y_s, even_caches, i, page_table, norm_even[i], w_in_even[i],
                                           qk_norm_a[i], lambda_a[i], subln_a[i], w_out_even[i], layer)
            ev_p.append(rp)
            ev_s.append(rs)
        else:
            y_p, y_s, rp, rs = _odd_layer(y_p, y_s, odd_k, odd_v, i, page_table, norm_odd[i], w_in_odd[i],
                                          qk_norm_c[i], w_out_odd[i])
            od_p.append(rp)
            od_s.append(rs)

    def stack(rows, j):
        return jnp.stack([r[j] for r in rows])

    return (y_p, y_s,
            stack(ev_p, 0), stack(ev_p, 1), stack(ev_p, 2), stack(ev_p, 3), stack(od_p, 0), stack(od_p, 1),
            stack(ev_s, 0), stack(ev_s, 1), stack(ev_s, 2), stack(ev_s, 3), stack(od_s, 0), stack(od_s, 1))
```

```python
import functools
import math

import jax
import jax.numpy as jnp
from jax import lax
from jax.experimental import pallas as pl
from jax.experimental.pallas import tpu as pltpu

HEAD_DIM = 64
ROT_DIM = HEAD_DIM // 4
ROPE_THETA = 500000.0
RMS_EPS = 1e-6
MOBA_BLOCK = 256
MOBA_TOPK = 3
LANES = 128
NEG = -1e30
SB_DEAD_LOG2 = 150.0
QK_SCALE = HEAD_DIM ** -0.5
Q_SOFTMAX_SCALE = QK_SCALE * math.log2(math.e)
ATTN_SLAB = 256
VMEM_LIMIT = 52 * 1024 * 1024

F32 = jnp.float32
BF16 = jnp.bfloat16


def _params(sem, vmem=VMEM_LIMIT):
    return pltpu.CompilerParams(dimension_semantics=sem, vmem_limit_bytes=vmem)


def _dot(a, b):
    return jnp.dot(a, b, preferred_element_type=F32)


def _dot_nt(a, b):
    return lax.dot_general(a, b, (((1,), (1,)), ((), ())), preferred_element_type=F32)


def _split_bf16(x):
    hi = x.astype(BF16)
    lo = (x - hi.astype(F32)).astype(BF16)
    return hi, lo


def _iota(shape, dim):
    return lax.broadcasted_iota(jnp.int32, shape, dim)


def _head_seg_matrix():
    return (_iota((LANES, LANES), 0) // HEAD_DIM == _iota((LANES, LANES), 1) // HEAD_DIM).astype(BF16)


def _norm_rope(x, g, c, s1, s2, seg):
    hi, lo = _split_bf16(x * x)
    ms = (_dot(hi, seg) + _dot(lo, seg)) * (1.0 / HEAD_DIM)
    xn = x * lax.rsqrt(ms + RMS_EPS) * g
    half = ROT_DIM // 2
    return xn * c + pltpu.roll(xn, LANES - half, 1) * s1 + pltpu.roll(xn, half, 1) * s2


def _proj_kernel(x_ref, gn_ref, w_ref, gq_ref, gk_ref, c_ref, s1_ref, s2_ref, *out_refs, plan):
    x = x_ref[...]
    ms = jnp.mean(x * x, axis=-1, keepdims=True)
    xn = (x * lax.rsqrt(ms + RMS_EPS) * gn_ref[...]).astype(BF16)
    seg = _head_seg_matrix()
    c, s1, s2 = c_ref[...], s1_ref[...], s2_ref[...]
    outs = list(out_refs)
    tm = x.shape[0]

    def store_rows(o_ref, layout, t, n_t, y):
        if layout == "rows":
            o_ref[:, t * LANES:(t + 1) * LANES] = y
        elif layout == "cols":
            o_ref[t * LANES:(t + 1) * LANES, :] = jnp.transpose(y)
        elif layout == "heads":
            o_ref[pl.ds(t, tm, stride=n_t), :] = y
        else:
            raise ValueError(layout)

    for kind, col0, width, layout in plan:
        h = _dot(xn, w_ref[:, col0:col0 + width])
        n_t = width // LANES
        if kind in ("q_rope_bf16", "q_rope_f32", "k_rope", "k_rope_mean"):
            g = gq_ref[...] if kind.startswith("q") else gk_ref[...]
            if kind == "q_rope_bf16":
                o_ref = outs.pop(0)
            elif kind == "q_rope_f32":
                o_ref = outs.pop(0)
            else:
                o_ref, o16_ref = outs.pop(0), outs.pop(0)
                mean_ref = outs.pop(0) if kind == "k_rope_mean" else None
            for t in range(n_t):
                sl = slice(t * LANES, (t + 1) * LANES)
                y = _norm_rope(h[:, sl], g, c, s1, s2, seg)
                if kind == "q_rope_bf16":
                    o_ref[:, sl] = (y * Q_SOFTMAX_SCALE).astype(BF16)
                elif kind == "q_rope_f32":
                    o_ref[:, sl] = y
                else:
                    store_rows(o_ref, layout, t, n_t, y)
                    o16_ref[:, sl] = y.astype(BF16)
                    if mean_ref is not None:
                        mean_ref[:, sl] = jnp.mean(y, axis=0, keepdims=True)
        elif kind == "q_plain_bf16":
            outs.pop(0)[...] = (h * Q_SOFTMAX_SCALE).astype(BF16)
        elif kind == "kv_plain":
            o_ref = outs.pop(0)
            for t in range(n_t):
                store_rows(o_ref, layout, t, n_t, h[:, t * LANES:(t + 1) * LANES])
            outs.pop(0)[...] = h.astype(BF16)
        elif kind == "gate":
            outs.pop(0)[...] = h / (1.0 + jnp.exp(-h))
        else:
            raise ValueError(kind)
    assert not outs


def _rope_tables(pos):
    half = ROT_DIM // 2
    inv = ROPE_THETA ** (-jnp.arange(0, ROT_DIM, 2, dtype=F32) / ROT_DIM)
    ang = pos.astype(F32)[:, None] * inv[None, :]
    cos, sin = jnp.cos(ang), jnp.sin(ang)
    t = pos.shape[0]
    z_half = jnp.zeros((t, half), F32)
    z_rest = jnp.zeros((t, HEAD_DIM - ROT_DIM), F32)
    c = jnp.concatenate([cos, cos, jnp.ones((t, HEAD_DIM - ROT_DIM), F32)], axis=-1)
    s1 = jnp.concatenate([-sin, z_half, z_rest], axis=-1)
    s2 = jnp.concatenate([z_half, sin, z_rest], axis=-1)
    rep = LANES // HEAD_DIM
    return tuple(jnp.tile(a, (1, rep)) for a in (c, s1, s2))


def _project(x, g_norm, w_bf16, gq, gk, tables, plan, tm, seq, name):
    m, d = x.shape
    n_tab = tables[0].shape[0] // tm
    tiles_per_seq = seq // tm
    out_shapes, out_specs = [], []
    for kind, _, width, layout in plan:
        row = pl.BlockSpec((tm, width), lambda i: (i, 0))
        if kind in ("q_rope_bf16", "q_plain_bf16"):
            out_shapes += [jax.ShapeDtypeStruct((m, width), BF16)]
            out_specs += [row]
        elif kind in ("q_rope_f32", "gate"):
            out_shapes += [jax.ShapeDtypeStruct((m, width), F32)]
            out_specs += [row]
        elif kind in ("k_rope", "kv_plain", "k_rope_mean"):
            if layout == "rows":
                out_shapes += [jax.ShapeDtypeStruct((m, width), F32)]
                out_specs += [row]
            elif layout == "cols":
                out_shapes += [jax.ShapeDtypeStruct((m // seq, width, seq), F32)]
                out_specs += [pl.BlockSpec((None, width, tm),
                                           lambda i: (i // tiles_per_seq, 0, i % tiles_per_seq))]
            else:
                n_t = width // LANES
                out_shapes += [jax.ShapeDtypeStruct((m * n_t, LANES), F32)]
                out_specs += [pl.BlockSpec((tm * n_t, LANES), lambda i: (i, 0))]
            out_shapes += [jax.ShapeDtypeStruct((m, width), BF16)]
            out_specs += [row]
            if kind == "k_rope_mean":
                assert tm == MOBA_BLOCK
                out_shapes += [jax.ShapeDtypeStruct((m // tm, 1, width), F32)]
                out_specs += [pl.BlockSpec((None, 1, width), lambda i: (i, 0, 0))]
    tab_spec = pl.BlockSpec((tm, LANES), lambda i: (i % n_tab, 0))
    vec_d = pl.BlockSpec((1, d), lambda i: (0, 0))
    vec_l = pl.BlockSpec((1, LANES), lambda i: (0, 0))
    rep = LANES // HEAD_DIM
    return pl.pallas_call(
        functools.partial(_proj_kernel, plan=plan),
        grid=(m // tm,),
        in_specs=[pl.BlockSpec((tm, d), lambda i: (i, 0)), vec_d,
                  pl.BlockSpec(w_bf16.shape, lambda i: (0, 0), pipeline_mode=pl.Buffered(1)),
                  vec_l, vec_l, tab_spec, tab_spec, tab_spec],
        out_specs=out_specs,
        out_shape=out_shapes,
        compiler_params=_params(("parallel",)),
        name=name,
    )(x, g_norm.reshape(1, d), w_bf16, jnp.tile(gq, rep).reshape(1, LANES), jnp.tile(gk, rep).reshape(1, LANES),
      *tables)


def _finish_kernel(x_ref, w_ref, *refs):
    mix_refs, y_ref = refs[:-1], refs[-1]
    y = x_ref[...]
    row0 = 0
    for mix_ref in mix_refs:
        rows = mix_ref.shape[1]
        y = y + _dot(mix_ref[...], w_ref[row0:row0 + rows, :])
        row0 += rows
    y_ref[...] = y


def _finish(x, mix_parts, w_bf16, tm, name):
    m, d = x.shape
    kdim = w_bf16.shape[0]
    assert sum(p.shape[1] for p in mix_parts) == kdim
    return pl.pallas_call(
        _finish_kernel,
        grid=(m // tm,),
        in_specs=[pl.BlockSpec((tm, d), lambda i: (i, 0)), pl.BlockSpec((kdim, d), lambda i: (0, 0))]
        + [pl.BlockSpec((tm, p.shape[1]), lambda i: (i, 0)) for p in mix_parts],
        out_specs=pl.BlockSpec((tm, d), lambda i: (i, 0)),
        out_shape=jax.ShapeDtypeStruct((m, d), F32),
        compiler_params=_params(("parallel",)),
        name=name,
    )(x, w_bf16, *mix_parts)


def _lambda_value(lp, lam_init):
    a = jnp.sum(lp[0:1] * lp[1:2], axis=-1, keepdims=True)
    b = jnp.sum(lp[2:3] * lp[3:4], axis=-1, keepdims=True)
    return jnp.exp(a) - jnp.exp(b) + lam_init


def _softmax_tile(qq, k, v, carry, mask):
    m, l, acc = carry
    s = _dot_nt(qq, k)
    if mask is not None:
        s = jnp.where(mask, s, NEG)
    m_new = jnp.maximum(m, jnp.max(s, axis=-1, keepdims=True))
    alpha = jnp.exp2(m - m_new)
    p = jnp.exp2(s - m_new)
    l = alpha * l + jnp.sum(p, axis=-1, keepdims=True)
    acc = alpha * acc + _dot(p.astype(BF16), v)
    return m_new, l, acc


def _dot_tn(a, b):
    return lax.dot_general(a, b, (((0,), (0,)), ((), ())), preferred_element_type=F32)


def _softmax_tile_t(qq_t, k_ref, v_ref, start, n_keys, carry, mask):
    s, m_s = _scores_t(qq_t, k_ref, start, n_keys, mask)
    return _absorb_t(v_ref, start, n_keys, s, m_s, carry)


def _scores_t(qq_t, k_ref, start, n_keys, mask):
    s = _dot(k_ref[pl.ds(start, n_keys), :], qq_t)
    if mask is not None:
        s = jnp.concatenate([jnp.where(mk, s[c * ATTN_SLAB:(c + 1) * ATTN_SLAB], NEG)
                             for c, mk in enumerate(mask)], axis=0)
    return s, jnp.max(s, axis=0, keepdims=True)


def _absorb_t(v_ref, start, n_keys, s, m_s, carry):
    m, l, acc = carry
    m_new = jnp.maximum(m, m_s)
    alpha = jnp.exp2(m - m_new)
    p = jnp.exp2(s - m_new)
    l = alpha * l + jnp.sum(p, axis=0, keepdims=True)
    acc = alpha * acc + _dot_tn(v_ref[pl.ds(start, n_keys), :], p.astype(BF16))
    return m_new, l, acc


def _pipelined_pairs(n_pairs, last_tile, n_keys, qq_t, k_ref, v_ref, mask_of, s_refs, carry):
    s_a, s_b = s_refs

    def scores(t, s_ref):
        s, m_s = _scores_t(qq_t, k_ref, pl.multiple_of(t * n_keys, n_keys), n_keys, mask_of(t))
        s_ref[...] = s
        return m_s

    def absorb(t, s_ref, m_s, carry):
        return _absorb_t(v_ref, pl.multiple_of(t * n_keys, n_keys), n_keys, s_ref[...], m_s, carry)

    def trip(u, state):
        m_a, carry = state
        m_b = scores(2 * u + 1, s_b)
        carry = absorb(2 * u, s_a, m_a, carry)
        m_a = scores(jnp.minimum(2 * u + 2, last_tile), s_a)
        carry = absorb(2 * u + 1, s_b, m_b, carry)
        return m_a, carry

    return lax.fori_loop(0, n_pairs, trip, (scores(n_pairs * 0, s_a), carry))[1]


def _causal_masks(k_start, q_start, tq, n_keys, r, strict):
    q_pos = q_start + _iota((ATTN_SLAB, r), 1) % tq
    masks = []
    for c in range(n_keys // ATTN_SLAB):
        k_pos = k_start + c * ATTN_SLAB + _iota((ATTN_SLAB, r), 0)
        masks.append(k_pos < q_pos if strict else k_pos <= q_pos)
    return masks


def _stack_pair_t(q_t):
    row = _iota(q_t.shape, 0)
    zero = jnp.zeros_like(q_t)
    return jnp.concatenate([jnp.where(row < HEAD_DIM, q_t, zero), jnp.where(row >= HEAD_DIM, q_t, zero)], axis=1)


def _diff_attn_kernel(lam_ref, subg_ref, q_ref, k_ref, v_ref, sg_ref, o_ref, s_a, s_b, *, tq, tk, lam_init):
    i = pl.program_id(2)
    r = 2 * tq
    qq_t = _stack_pair_t(jnp.transpose(q_ref[...].astype(F32))).astype(BF16)
    n_full = (i * tq) // tk
    half = tk // 2

    carry = (jnp.full((1, r), NEG, F32), jnp.zeros((1, r), F32), jnp.zeros((LANES, r), F32))
    start = pl.multiple_of(n_full * tk, tk)
    carry = _softmax_tile_t(qq_t, k_ref, v_ref, start, tk, carry,
                            _causal_masks(start, i * tq, tq, tk, r, strict=False))
    _, l, acc = _pipelined_pairs(n_full, k_ref.shape[0] // half - 1, half, qq_t, k_ref, v_ref,
                                 lambda t: None, (s_a, s_b), carry)

    a = acc / l
    lam = _lambda_value(lam_ref[...], lam_init)
    o = jnp.transpose(a[:, :tq] - lam * a[:, tq:])
    ms = jnp.mean(o * o, axis=-1, keepdims=True)
    o = o * lax.rsqrt(ms + RMS_EPS) * subg_ref[...] * (1.0 - lam_init)
    o_ref[...] = (o * sg_ref[...]).astype(BF16)


def _softplus2(z):
    return jnp.maximum(z, 0.0) + jnp.log2(1.0 + jnp.exp2(-jnp.abs(z)))


def _later_matrix(tk, keys_axis=1):
    a, b = _iota((tk, tk), 0), _iota((tk, tk), 1)
    return (a > b if keys_axis == 1 else b > a).astype(BF16)


def _sb_kernel(q_ref, k_ref, v_ref, sg_ref, o_ref, *, tq, tk):
    i = pl.program_id(2)
    r = 2 * tq
    qq_t = _stack_pair_t(jnp.transpose(q_ref[...].astype(F32))).astype(BF16)
    tri = _later_matrix(ATTN_SLAB, keys_axis=0)
    n_full = (i * tq) // tk
    n_slabs = tk // ATTN_SLAB

    def tile(st, c, acc, valid):
        parts = []
        for u in range(n_slabs):
            z = _dot(k_ref[pl.ds(st + u * ATTN_SLAB, ATTN_SLAB), :], qq_t)
            sp = _softplus2(z)
            log_keep = -sp if valid is None else jnp.where(valid[u], -sp, 0.0)
            later = _dot(tri, log_keep.astype(BF16))
            parts.append((z - sp + later, later[0:1, :] + log_keep[0:1, :]))
        for u in reversed(range(n_slabs)):
            e, whole = parts[u]
            w = jnp.exp2(e + c)
            if valid is not None:
                w = jnp.where(valid[u], w, 0.0)
            acc = acc + _dot_tn(v_ref[pl.ds(st + u * ATTN_SLAB, ATTN_SLAB), :], w.astype(BF16))
            c = c + whole
        return c, acc

    start = pl.multiple_of(n_full * tk, tk)
    c, acc = tile(start, jnp.zeros((1, r), F32), jnp.zeros((LANES, r), F32),
                  _causal_masks(start, i * tq, tq, tk, r, strict=True))

    def live(state):
        t, c_max, _, _ = state
        return jnp.logical_and(t < n_full, c_max > -SB_DEAD_LOG2)

    def full_tile(state):
        t, _, c, acc = state
        c, acc = tile(pl.multiple_of((n_full - 1 - t) * tk, tk), c, acc, None)
        return t + 1, jnp.max(c), c, acc

    _, _, _, acc = lax.while_loop(live, full_tile, (jnp.int32(0), jnp.max(c), c, acc))
    row = _iota((LANES, tq), 0)
    o = jnp.transpose(jnp.where(row < HEAD_DIM, acc[:, :tq], acc[:, tq:]))
    o_ref[...] = (o * sg_ref[...]).astype(BF16)


def _top_blocks(g, n_valid, axis=-1, picks=None):
    axis = axis % g.ndim
    blk = _iota(g.shape, axis)
    nb = g.shape[axis]
    g = jnp.where(blk < n_valid, g, -jnp.inf)
    sel = jnp.zeros(g.shape, jnp.bool_)
    for _ in range(min(MOBA_TOPK, nb)):
        mx = jnp.max(g, axis=axis, keepdims=True)
        idx = jnp.min(jnp.where(g == mx, blk, nb), axis=axis, keepdims=True)
        pick = blk == idx
        sel = jnp.logical_or(sel, pick)
        g = jnp.where(pick, -jnp.inf, g)
        if picks is not None:
            picks.append(idx)
    return jnp.logical_and(sel, blk < n_valid)


def _moba_kernel(q_ref, k_ref, v_ref, km_ref, sg_ref, o_ref, sel_s, s_a, s_b, *, tq, group):
    i = pl.program_id(2)
    r = 2 * tq
    qf_t = _stack_pair_t(jnp.transpose(q_ref[...]))
    row = _iota((LANES, tq), 0)
    q_hi, q_lo = _split_bf16(qf_t)
    qq_t = (qf_t * Q_SOFTMAX_SCALE).astype(BF16)

    km_hi, km_lo = _split_bf16(km_ref[...])
    g = _dot(km_hi, q_hi) + _dot(km_hi, q_lo) + _dot(km_lo, q_hi)
    sel_s[...] = jnp.where(_top_blocks(g, i, axis=0), 1.0, 0.0)

    start = pl.multiple_of(i * tq, tq)
    carry = (jnp.full((1, r), NEG, F32), jnp.zeros((1, r), F32), jnp.zeros((LANES, r), F32))
    carry = _softmax_tile_t(qq_t, k_ref, v_ref, start, tq, carry, _causal_masks(0, 0, tq, tq, r, strict=False))

    per_tile = group // 2

    def chosen(t):
        return [sel_s[pl.ds(t * per_tile + c, 1), :] > 0.0 for c in range(per_tile)]

    _, l, acc = _pipelined_pairs((i + group - 1) // group, sel_s.shape[0] // per_tile - 1, per_tile * tq,
                                 qq_t, k_ref, v_ref, chosen, (s_a, s_b), carry)
    o_t = acc / l
    o = jnp.transpose(jnp.where(row < HEAD_DIM, o_t[:, :tq], o_t[:, tq:]))
    o_ref[...] = (o * sg_ref[...]).astype(BF16)


def _attn_specs(tq, s, group0=0):
    q_spec = pl.BlockSpec((None, tq, LANES), lambda b, h, i: (b, i, h + group0))
    kv_spec = pl.BlockSpec((None, s, LANES), lambda b, h, i: (b, 0, h))
    return q_spec, kv_spec


def _diff_attn_prompt(q16, k16, v16, sgate, lam_p, subln_g, lam_init, tq, tk):
    b, s, w = q16.shape
    q_spec, kv_spec = _attn_specs(tq, s)
    return pl.pallas_call(
        functools.partial(_diff_attn_kernel, tq=tq, tk=tk, lam_init=lam_init),
        grid=(b, w // LANES, s // tq),
        in_specs=[pl.BlockSpec(lam_p.shape, lambda b, h, i: (0, 0)),
                  pl.BlockSpec((1, LANES), lambda b, h, i: (0, 0)),
                  q_spec, kv_spec, kv_spec, q_spec],
        out_specs=q_spec,
        out_shape=jax.ShapeDtypeStruct((b, s, w), BF16),
        scratch_shapes=[pltpu.VMEM((tk // 2, 2 * tq), F32)] * 2,
        compiler_params=_params(("parallel", "parallel", "arbitrary")),
        name="diff_attn_prompt",
    )(lam_p, subln_g.reshape(1, LANES), q16, k16, v16, sgate)


def _sb_prompt(q16, k16, v16, sgate, gate_group0, tq, tk):
    b, s, w = q16.shape
    q_spec, kv_spec = _attn_specs(tq, s)
    sg_spec = pl.BlockSpec((None, tq, LANES), lambda b, h, i: (b, i, h + gate_group0))
    return pl.pallas_call(
        functools.partial(_sb_kernel, tq=tq, tk=tk),
        grid=(b, w // LANES, s // tq),
        in_specs=[q_spec, kv_spec, kv_spec, sg_spec],
        out_specs=q_spec,
        out_shape=jax.ShapeDtypeStruct((b, s, w), BF16),
        compiler_params=_params(("parallel", "parallel", "arbitrary")),
        name="sb_prompt",
    )(q16, k16, v16, sgate)


def _moba_prompt(q, k16, v16, kmean, sgate):
    b, s, w = q.shape
    tq = MOBA_BLOCK
    nb = s // tq
    q_spec, kv_spec = _attn_specs(tq, s)
    group = math.gcd(nb, MOBA_GROUP)
    return pl.pallas_call(
        functools.partial(_moba_kernel, tq=tq, group=group),
        grid=(b, w // LANES, nb),
        in_specs=[q_spec, kv_spec, kv_spec,
                  pl.BlockSpec((None, nb, LANES), lambda b, h, i: (b, 0, h)), q_spec],
        out_specs=q_spec,
        out_shape=jax.ShapeDtypeStruct((b, s, w), BF16),
        scratch_shapes=[pltpu.VMEM((nb, 2 * tq), F32)] + [pltpu.VMEM((group // 2 * tq, 2 * tq), F32)] * 2,
        compiler_params=_params(("parallel", "parallel", "arbitrary")),
        name="moba_prompt",
    )(q, k16, v16, kmean, sgate)


def _row_heads(x_row, n_rows, lanes_per_row):
    w = x_row.shape[1]
    keep = _iota((n_rows, w), 1) // lanes_per_row == _iota((n_rows, w), 0)
    return jnp.where(keep, jnp.broadcast_to(x_row, (n_rows, w)), 0.0)


def _dup_rows(x, reps):
    n, w = x.shape
    row = _iota((n * reps, w), 0) // reps
    out = jnp.zeros((n * reps, w), x.dtype)
    for h in range(n):
        out = jnp.where(row == h, x[h:h + 1, :], out)
    return out


def _stack_rows(rows):
    n, w = len(rows), rows[0].shape[1]
    row = _iota((n, w), 0)
    out = jnp.zeros((n, w), rows[0].dtype)
    for h, r in enumerate(rows):
        out = jnp.where(row == h, r, out)
    return out


def _head_dots(qcol_ref, kt_ref, n_heads):
    rows = []
    for h in range(n_heads):
        sl = slice(h * HEAD_DIM, (h + 1) * HEAD_DIM)
        rows.append(jnp.sum(qcol_ref[sl, :] * kt_ref[sl, :], axis=0, keepdims=True))
    return _stack_rows(rows)


def _add_weighted(acc_ref, w, vt_refs, n_heads):
    for h in range(n_heads):
        sl = slice(h * HEAD_DIM, (h + 1) * HEAD_DIM)
        a = acc_ref[sl, :]
        for u, vt_ref in enumerate(vt_refs):
            a = a + w[u * n_heads + h:u * n_heads + h + 1, :] * vt_ref[sl, :]
        acc_ref[sl, :] = a


def _sb_weights_pages(zs, tri, c):
    n_h = zs[0].shape[0]
    z = jnp.concatenate(zs, axis=0)
    sp = _softplus2(z)
    log_keep = -sp
    hi, lo = _split_bf16(log_keep)
    rows = z.shape[0]
    both = _dot(jnp.concatenate([hi, lo], axis=0), tri)
    later = both[:rows] + both[rows:]
    whole = later[:, 0:1] + log_keep[:, 0:1]
    offsets = []
    for u in range(len(zs)):
        offsets.append(c)
        c = c + whole[u * n_h:(u + 1) * n_h]
    return jnp.exp2(z - sp + later + jnp.concatenate(offsets, axis=0)), c


def _lane_sums_as_row(x):
    hi, lo = _split_bf16(x)
    ones = jnp.ones((8, LANES), BF16)
    return (_dot_nt(ones, hi) + _dot_nt(ones, lo))[0:1, :]


def _decode_even_kernel(pt_ref, lam_ref, subg_ref, qa_ref, kan_ref, van_ref, qb_ref, sg_ref, *refs,
                        pps, lam_init):
    del pt_ref
    page_refs = refs[:4 * pps]
    o_ref = refs[4 * pps]
    qa_s, m_s, l_s, acca_s, c_s, accb_s = refs[4 * pps + 1:]
    step = pl.program_id(1)
    n_ha = qa_ref.shape[0]
    n_a = 2 * n_ha
    n_b = accb_s.shape[0] // HEAD_DIM
    wa = n_ha * LANES

    @pl.when(step == 0)
    def _():
        rows = _dup_rows(qa_ref[...].astype(F32), 2)
        qa = jnp.where(_iota(rows.shape, 1) // HEAD_DIM == _iota(rows.shape, 0) % 2, rows, 0.0)
        qa_s[...] = qa
        m_s[...] = jnp.sum(qa * _dup_rows(kan_ref[...], 2), axis=-1, keepdims=True)
        l_s[...] = jnp.ones_like(l_s)
        acca_s[...] = _dup_rows(van_ref[...], 2)
        c_s[...] = jnp.zeros_like(c_s)
        accb_s[...] = jnp.zeros_like(accb_s)

    rows_a = pps * page_refs[0].shape[0]
    own_head = _iota((n_a, rows_a), 1) % n_ha == _iota((n_a, rows_a), 0) // 2
    ak = jnp.concatenate([page_refs[4 * u][...].astype(BF16) for u in range(pps)], axis=0)
    av = jnp.concatenate([page_refs[4 * u + 1][...].astype(BF16) for u in range(pps)], axis=0)
    m_s[...], l_s[...], acca_s[...] = _softmax_tile(qa_s[...].astype(BF16), ak, av,
                                                    (m_s[...], l_s[...], acca_s[...]), own_head)
    zs = [_head_dots(qb_ref, page_refs[4 * u + 2], n_b) for u in range(pps)]
    w, c_s[...] = _sb_weights_pages(zs, _later_matrix(LANES), c_s[...])
    _add_weighted(accb_s, w, [page_refs[4 * u + 3] for u in range(pps)], n_b)

    @pl.when(step == pl.num_programs(1) - 1)
    def _():
        lam = _lambda_value(lam_ref[...], lam_init)
        a = acca_s[...] / l_s[...]
        for h in range(n_ha):
            sl = slice(h * LANES, (h + 1) * LANES)
            x = a[2 * h:2 * h + 1, :] - lam * a[2 * h + 1:2 * h + 2, :]
            ms = jnp.mean(x * x, axis=-1, keepdims=True)
            x = x * lax.rsqrt(ms + RMS_EPS) * subg_ref[...] * (1.0 - lam_init)
            o_ref[:, sl] = (x * sg_ref[:, sl]).astype(BF16)
        o_ref[:, wa:] = (_lane_sums_as_row(accb_s[...]) * sg_ref[:, wa:]).astype(BF16)


def _page_specs(cache, li, n_pages, pps, reverse):
    r = cache.shape[2]

    def make(u):
        def index(b, s, pt):
            j = s * pps + u
            j = n_pages - 1 - j if reverse else j
            return (li, pt[b, j], 0, 0)
        return pl.BlockSpec((None, None, r, LANES), index)

    return [make(u) for u in range(pps)]


def _col_bcast(x):
    return jnp.broadcast_to(x.astype(F32)[:, :, None], x.shape + (LANES,))


def _decode_even(pt, lam_p, subln_g, lam_init, qa16, ka, va, qb16, sgate, caches, li, pps):
    db, n_pages = pt.shape
    wa, wb = qa16.shape[-1], qb16.shape[-1]
    n_ha = wa // LANES
    whole = lambda shp: pl.BlockSpec((None,) + shp, lambda b, s, pt: (b,) + (0,) * len(shp))
    specs_by_cache = [_page_specs(c, li, n_pages, pps, True) for c in caches]
    page_specs = [specs_by_cache[t][u] for u in range(pps) for t in range(4)]
    page_args = [caches[t] for u in range(pps) for t in range(4)]
    grid_spec = pltpu.PrefetchScalarGridSpec(
        num_scalar_prefetch=1,
        grid=(db, n_pages // pps),
        in_specs=[pl.BlockSpec(lam_p.shape, lambda b, s, pt: (0, 0)),
                  pl.BlockSpec((1, LANES), lambda b, s, pt: (0, 0)),
                  whole((n_ha, LANES)), whole((n_ha, LANES)), whole((n_ha, LANES)), whole((wb, LANES)),
                  whole((1, wa + wb))] + page_specs,
        out_specs=whole((1, wa + wb)),
        scratch_shapes=[pltpu.VMEM((2 * n_ha, LANES), F32),
                        pltpu.VMEM((2 * n_ha, 1), F32), pltpu.VMEM((2 * n_ha, 1), F32),
                        pltpu.VMEM((2 * n_ha, LANES), F32),
                        pltpu.VMEM((wb // HEAD_DIM, 1), F32), pltpu.VMEM((wb, LANES), F32)],
    )
    heads = lambda x: x.reshape(db, n_ha, LANES)
    return pl.pallas_call(
        functools.partial(_decode_even_kernel, pps=pps, lam_init=lam_init),
        grid_spec=grid_spec,
        out_shape=jax.ShapeDtypeStruct((db, 1, wa + wb), BF16),
        compiler_params=_params(("parallel", "arbitrary")),
        name="decode_even",
    )(pt, lam_p, subln_g.reshape(1, LANES), heads(qa16), heads(ka), heads(va), _col_bcast(qb16),
      sgate.reshape(db, 1, wa + wb), *page_args)


def _moba_scores_kernel(pt_ref, qcol_ref, qrow_ref, kn_ref, *refs, pps, n_blocks):
    del pt_ref
    page_refs = refs[:pps]
    p_ref, pnew_ref, ids_ref = refs[pps:pps + 3]
    s_s = refs[pps + 3]
    step = pl.program_id(1)
    n_h = qcol_ref.shape[0] // HEAD_DIM
    ppb = MOBA_BLOCK // LANES

    for u in range(pps):
        s_s[step * pps + u] = _head_dots(qcol_ref, page_refs[u], n_h)

    @pl.when(step == pl.num_programs(1) - 1)
    def _():
        lane = _iota((n_h, LANES), 1)
        g = jnp.zeros((n_h, LANES), F32)
        for n in range(n_blocks):
            tot = sum(s_s[n * ppb + t] for t in range(ppb))
            g = jnp.where(lane == n, jnp.sum(tot, axis=-1, keepdims=True) * (1.0 / MOBA_BLOCK), g)
        picks = []
        sel = _top_blocks(g, n_blocks, picks=picks)
        ids = jnp.zeros((n_h, LANES), jnp.int32)
        for r, idx in enumerate(picks):
            ids = jnp.where(lane == r, idx, ids)
        ids_ref[...] = ids
        chosen = [sel[:, n:n + 1] for n in range(n_blocks)]
        s_new = jnp.sum(_row_heads(qrow_ref[...], n_h, HEAD_DIM) * kn_ref[...], axis=-1, keepdims=True) * QK_SCALE
        m_lanes = jnp.full((n_h, LANES), NEG, F32)
        for j in range(n_blocks * ppb):
            m_lanes = jnp.maximum(m_lanes, jnp.where(chosen[j // ppb], s_s[j] * QK_SCALE, NEG))
        m = jnp.maximum(jnp.max(m_lanes, axis=-1, keepdims=True), s_new)
        l_lanes = jnp.zeros((n_h, LANES), F32)
        for j in range(n_blocks * ppb):
            e = jnp.where(chosen[j // ppb], jnp.exp(s_s[j] * QK_SCALE - m), 0.0)
            s_s[j] = e
            l_lanes = l_lanes + e
        e_new = jnp.exp(s_new - m)
        inv = 1.0 / (jnp.sum(l_lanes, axis=-1, keepdims=True) + e_new)
        for j in range(n_blocks * ppb):
            p_ref[j] = s_s[j] * inv
        pnew_ref[...] = jnp.broadcast_to(e_new * inv, pnew_ref.shape)


def _moba_values_kernel(pt_ref, ids_ref, p_ref, pnew_ref, vn_ref, sg_ref, *refs, n_sel, ppb):
    del pt_ref
    slices, o_ref = refs[:-1], refs[-1]
    b, pair = pl.program_id(0), pl.program_id(1)
    heads_per_step = LANES // HEAD_DIM
    accs = []
    for c in range(heads_per_step):
        h = pair * heads_per_step + c
        acc = jnp.zeros((HEAD_DIM, LANES), F32)
        for r in range(n_sel):
            blk = ids_ref[b, h * n_sel + r]
            for u in range(ppb):
                w = p_ref[blk * ppb + u, pl.ds(h, 1), :]
                acc = acc + w * slices[(c * n_sel + r) * ppb + u][...]
        accs.append(acc)
    o = _lane_sums_as_row(jnp.concatenate(accs, axis=0)) + pnew_ref[...] * vn_ref[...]
    o_ref[...] = (o * sg_ref[...]).astype(BF16)


def _decode_odd(pt, q, k_new, v_new, sgate, cache_k, cache_v, li, pps):
    db, n_pages = pt.shape
    w = q.shape[-1]
    n_h = w // HEAD_DIM
    page = LANES
    past = n_pages * page
    assert past % MOBA_BLOCK == 0 and past >= MOBA_BLOCK and MOBA_BLOCK % page == 0
    n_blocks = past // MOBA_BLOCK
    row = lambda wd: pl.BlockSpec((None, 1, wd), lambda b, s, pt: (b, 0, 0))
    r3 = lambda x: x.reshape(db, 1, x.shape[-1])
    assert n_blocks >= MOBA_TOPK
    per_head = pl.BlockSpec((None, n_h, LANES), lambda b, s, pt: (b, 0, 0))
    probs, p_new, ids = pl.pallas_call(
        functools.partial(_moba_scores_kernel, pps=pps, n_blocks=n_blocks),
        grid_spec=pltpu.PrefetchScalarGridSpec(
            num_scalar_prefetch=1,
            grid=(db, n_pages // pps),
            in_specs=[pl.BlockSpec((None, w, LANES), lambda b, s, pt: (b, 0, 0)), row(w), row(w)]
            + _page_specs(cache_k, li, n_pages, pps, False),
            out_specs=[pl.BlockSpec((None, n_pages, n_h, page), lambda b, s, pt: (b, 0, 0, 0)),
                       per_head, per_head],
            scratch_shapes=[pltpu.VMEM((n_pages, n_h, page), F32)],
        ),
        out_shape=[jax.ShapeDtypeStruct((db, n_pages, n_h, page), F32),
                   jax.ShapeDtypeStruct((db, n_h, LANES), F32),
                   jax.ShapeDtypeStruct((db, n_h, LANES), jnp.int32)],
        compiler_params=_params(("parallel", "arbitrary")),
        name="decode_moba_scores",
    )(pt, _col_bcast(q), r3(q), r3(k_new), *([cache_k] * pps))
    p_new_row = jnp.repeat(p_new[:, :, 0], HEAD_DIM, axis=-1).reshape(db, 1, w)

    ppb = MOBA_BLOCK // page
    heads_per_step = LANES // HEAD_DIM
    block_ids = ids[:, :, :MOBA_TOPK].reshape(db, n_h * MOBA_TOPK)

    def slice_spec(c, r, u):
        def index(b, pair, pt, bid):
            h = pair * heads_per_step + c
            return (li, pt[b, bid[b, h * MOBA_TOPK + r] * ppb + u], h, 0)
        return pl.BlockSpec((None, None, HEAD_DIM, LANES), index)

    slice_specs = [slice_spec(c, r, u) for c in range(heads_per_step) for r in range(MOBA_TOPK)
                   for u in range(ppb)]
    lanes = lambda: pl.BlockSpec((None, 1, LANES), lambda b, pair, pt, bid: (b, 0, pair))
    return pl.pallas_call(
        functools.partial(_moba_values_kernel, n_sel=MOBA_TOPK, ppb=ppb),
        grid_spec=pltpu.PrefetchScalarGridSpec(
            num_scalar_prefetch=2,
            grid=(db, n_h // heads_per_step),
            in_specs=[pl.BlockSpec((None, n_pages, n_h, page), lambda b, pair, pt, bid: (b, 0, 0, 0)),
                      lanes(), lanes(), lanes()] + slice_specs,
            out_specs=lanes(),
        ),
        out_shape=jax.ShapeDtypeStruct((db, 1, w), BF16),
        compiler_params=_params(("parallel", "arbitrary")),
        name="decode_moba_values",
    )(pt, block_ids, probs, p_new_row, r3(v_new), r3(sgate), *([cache_v] * len(slice_specs)))


PROMPT_TM = 256
ATTN_TQ = 512
ATTN_TK = 512
SB_TILE = 256
MOBA_GROUP = 4
PAGES_PER_STEP = 8


def _rows_view(c):
    l, n, p, h, hd = c.shape
    assert p == LANES and hd == LANES
    return c.reshape(l, n, p * h, hd)


def _cols_view(c):
    l, n, p, h, hd = c.shape
    assert p == LANES and hd == HEAD_DIM
    return jnp.transpose(c, (0, 1, 3, 4, 2)).reshape(l, n, h * hd, p)


def _heads_from_cols(x, n_heads):
    b, w, s = x.shape
    return jnp.transpose(x.reshape(b, n_heads, w // n_heads, s), (0, 3, 1, 2))


def _even_layer(x_p, x_s, caches, li, pt, g_norm, w_in, qk_g, lam_p, subln_g, w_out, layer):
    lam_init = 0.8 - 0.6 * math.exp(-0.3 * layer)
    b, s, d = x_p.shape
    db, t, _ = x_s.shape
    wa = w_out.shape[0] // 2
    wb = wa
    def plan(wide, narrow):
        return (("q_rope_bf16", 0, wa, None), ("k_rope", wa, wa, wide), ("kv_plain", 2 * wa, wa, wide),
                ("q_plain_bf16", 3 * wa, wb, None), ("kv_plain", 3 * wa + wb, wb, narrow),
                ("kv_plain", 3 * wa + 2 * wb, wb, narrow), ("gate", 3 * wa + 3 * wb, wa + wb, None))

    w_in16 = w_in.astype(BF16)
    w_out16 = w_out.astype(BF16)
    past = pt.shape[1] * LANES
    n_a = wa // (2 * HEAD_DIM)
    n_b = wb // HEAD_DIM

    tabs = _rope_tables(jnp.arange(s))
    qa16, ka, ka16, va, va16, qb16, kb, kb16, vb, vb16, sgate = _project(
        x_p.reshape(b * s, d), g_norm, w_in16, qk_g[0], qk_g[1], tabs, plan("heads", "cols"), PROMPT_TM, s,
        "proj_even_prompt")
    r3 = lambda a: a.reshape(b, s, a.shape[-1])
    sg3 = r3(sgate)
    mix_a = _diff_attn_prompt(r3(qa16), r3(ka16), r3(va16), sg3, lam_p, subln_g, lam_init, ATTN_TQ, ATTN_TK)
    mix_b = _sb_prompt(r3(qb16), r3(kb16), r3(vb16), sg3, wa // LANES, SB_TILE, SB_TILE)
    y_p = _finish(x_p.reshape(b * s, d), [mix_a.reshape(b * s, wa), mix_b.reshape(b * s, wb)], w_out16, 512,
                  "finish_even_prompt").reshape(b, s, d)

    tabs_s = _rope_tables(jnp.tile(past + jnp.arange(t), db))
    qa16_s, ka_s, _, va_s, _, qb16_s, kb_s, _, vb_s, _, sgate_s = _project(
        x_s.reshape(db * t, d), g_norm, w_in16, qk_g[0], qk_g[1], tabs_s, plan("rows", "rows"), db * t, db * t,
        "proj_even_sample")
    assert t == 1
    mix_s = _decode_even(pt, lam_p, subln_g, lam_init, qa16_s, ka_s, va_s, qb16_s, sgate_s,
                         caches, li, PAGES_PER_STEP)
    y_s = _finish(x_s.reshape(db * t, d), [mix_s.reshape(db * t, wa + wb)], w_out16, db * t,
                  "finish_even_sample").reshape(db, t, d)

    rows_p = (ka.reshape(b, s, n_a, 2 * HEAD_DIM), va.reshape(b, s, n_a, 2 * HEAD_DIM),
              _heads_from_cols(kb, n_b), _heads_from_cols(vb, n_b))
    rows_s = (ka_s.reshape(db, t, n_a, 2 * HEAD_DIM), va_s.reshape(db, t, n_a, 2 * HEAD_DIM),
              kb_s.reshape(db, t, n_b, HEAD_DIM), vb_s.reshape(db, t, n_b, HEAD_DIM))
    return y_p, y_s, rows_p, rows_s


def _odd_layer(x_p, x_s, cache_k, cache_v, li, pt, g_norm, w_in, qk_g, w_out):
    b, s, d = x_p.shape
    db, t, _ = x_s.shape
    wc = w_out.shape[0]
    w_in16 = w_in.astype(BF16)
    w_out16 = w_out.astype(BF16)
    past = pt.shape[1] * LANES
    assert s % MOBA_BLOCK == 0

    def plan(k_kind, layout):
        return (("q_rope_f32", 0, wc, None), (k_kind, wc, wc, layout), ("kv_plain", 2 * wc, wc, layout),
                ("gate", 3 * wc, wc, None))

    tabs = _rope_tables(jnp.arange(s))
    q, k, k16, kmean, v, v16, sgate = _project(
        x_p.reshape(b * s, d), g_norm, w_in16, qk_g[0], qk_g[1], tabs, plan("k_rope_mean", "cols"), MOBA_BLOCK, s,
        "proj_odd_prompt")
    r3 = lambda a: a.reshape(b, s, a.shape[-1])
    mix = _moba_prompt(r3(q), r3(k16), r3(v16), kmean.reshape(b, s // MOBA_BLOCK, wc), r3(sgate))
    y_p = _finish(x_p.reshape(b * s, d), [mix.reshape(b * s, wc)], w_out16, 512,
                  "finish_odd_prompt").reshape(b, s, d)

    assert t == 1
    tabs_s = _rope_tables(jnp.tile(past + jnp.arange(t), db))
    q_s, k_s, _, v_s, _, sgate_s = _project(
        x_s.reshape(db * t, d), g_norm, w_in16, qk_g[0], qk_g[1], tabs_s, plan("k_rope", "rows"), db * t, db * t,
        "proj_odd_sample")
    mix_s = _decode_odd(pt, q_s, k_s, v_s, sgate_s, cache_k, cache_v, li, PAGES_PER_STEP)
    y_s = _finish(x_s.reshape(db * t, d), [mix_s.reshape(db * t, wc)], w_out16, db * t,
                  "finish_odd_sample").reshape(db, t, d)

    n_c = wc // HEAD_DIM
    return (y_p, y_s, (_heads_from_cols(k, n_c), _heads_from_cols(v, n_c)),
            (k_s.reshape(db, t, n_c, HEAD_DIM), v_s.reshape(db, t, n_c, HEAD_DIM)))


def kernel(x_prompt, x_sample, cache_a_k, cache_a_v, cache_b_k, cache_b_v, cache_c_k, cache_c_v, page_table,
           norm_even, w_in_even, qk_norm_a, lambda_a, subln_a, w_out_even, norm_odd, w_in_odd, qk_norm_c,
           w_out_odd):
    depth = norm_even.shape[0] + norm_odd.shape[0]
    even_caches = [_rows_view(cache_a_k), _rows_view(cache_a_v), _cols_view(cache_b_k), _cols_view(cache_b_v)]
    odd_k, odd_v = _cols_view(cache_c_k), _cols_view(cache_c_v)
    y_p, y_s = x_prompt, x_sample
    ev_p, ev_s, od_p, od_s = [], [], [], []
    for layer in range(depth):
        i = layer // 2
        if layer % 2 == 0:
            y_p, y_s, rp, rs = _even_layer(y_p, y_s, even_caches, i, page_table, norm_even[i], w_in_even[i],
                                           qk_norm_a[i], lambda_a[i], subln_a[i], w_out_even[i], layer)
            ev_p.append(rp)
            ev_s.append(rs)
        else:
            y_p, y_s, rp, rs = _odd_layer(y_p, y_s, odd_k, odd_v, i, page_table, norm_odd[i], w_in_odd[i],
                                          qk_norm_c[i], w_out_odd[i])
            od_p.append(rp)
            od_s.append(rs)

    def stack(rows, j):
        return jnp.stack([r[j] for r in rows])

    return (y_p, y_s,
            stack(ev_p, 0), stack(ev_p, 1), stack(ev_p, 2), stack(ev_p, 3), stack(od_p, 0), stack(od_p, 1),
            stack(ev_s, 0), stack(ev_s, 1), stack(ev_s, 2), stack(ev_s, 3), stack(od_s, 0), stack(od_s, 1))
```

```python
import functools
import math

import jax
import jax.numpy as jnp
from jax import lax
from jax.experimental import pallas as pl
from jax.experimental.pallas import tpu as pltpu

HEAD_DIM = 64
ROT_DIM = HEAD_DIM // 4
ROPE_THETA = 500000.0
RMS_EPS = 1e-6
MOBA_BLOCK = 256
MOBA_TOPK = 3
LANES = 128
NEG = -1e30
SB_DEAD_LOG2 = 150.0
QK_SCALE = HEAD_DIM ** -0.5
Q_SOFTMAX_SCALE = QK_SCALE * math.log2(math.e)
ATTN_SLAB = 256
VMEM_LIMIT = 52 * 1024 * 1024

F32 = jnp.float32
BF16 = jnp.bfloat16


def _params(sem, vmem=VMEM_LIMIT):
    return pltpu.CompilerParams(dimension_semantics=sem, vmem_limit_bytes=vmem)


def _dot(a, b):
    return jnp.dot(a, b, preferred_element_type=F32)


def _dot_nt(a, b):
    return lax.dot_general(a, b, (((1,), (1,)), ((), ())), preferred_element_type=F32)


def _split_bf16(x):
    hi = x.astype(BF16)
    lo = (x - hi.astype(F32)).astype(BF16)
    return hi, lo


def _iota(shape, dim):
    return lax.broadcasted_iota(jnp.int32, shape, dim)


def _head_seg_matrix():
    return (_iota((LANES, LANES), 0) // HEAD_DIM == _iota((LANES, LANES), 1) // HEAD_DIM).astype(BF16)


def _norm_rope(x, g, c, s1, s2, seg):
    hi, lo = _split_bf16(x * x)
    ms = (_dot(hi, seg) + _dot(lo, seg)) * (1.0 / HEAD_DIM)
    xn = x * lax.rsqrt(ms + RMS_EPS) * g
    half = ROT_DIM // 2
    return xn * c + pltpu.roll(xn, LANES - half, 1) * s1 + pltpu.roll(xn, half, 1) * s2


def _proj_kernel(x_ref, gn_ref, w_ref, gq_ref, gk_ref, c_ref, s1_ref, s2_ref, *out_refs, plan):
    x = x_ref[...]
    ms = jnp.mean(x * x, axis=-1, keepdims=True)
    xn = (x * lax.rsqrt(ms + RMS_EPS) * gn_ref[...]).astype(BF16)
    seg = _head_seg_matrix()
    c, s1, s2 = c_ref[...], s1_ref[...], s2_ref[...]
    outs = list(out_refs)
    tm = x.shape[0]

    def store_rows(o_ref, layout, t, n_t, y):
        if layout == "rows":
            o_ref[:, t * LANES:(t + 1) * LANES] = y
        elif layout == "cols":
            o_ref[t * LANES:(t + 1) * LANES, :] = jnp.transpose(y)
        elif layout == "heads":
            o_ref[pl.ds(t, tm, stride=n_t), :] = y
        else:
            raise ValueError(layout)

    for kind, col0, width, layout in plan:
        h = _dot(xn, w_ref[:, col0:col0 + width])
        n_t = width // LANES
        if kind in ("q_rope_bf16", "q_rope_f32", "k_rope", "k_rope_mean"):
            g = gq_ref[...] if kind.startswith("q") else gk_ref[...]
            if kind == "q_rope_bf16":
                o_ref = outs.pop(0)
            elif kind == "q_rope_f32":
                o_ref = outs.pop(0)
            else:
                o_ref, o16_ref = outs.pop(0), outs.pop(0)
                mean_ref = outs.pop(0) if kind == "k_rope_mean" else None
            for t in range(n_t):
                sl = slice(t * LANES, (t + 1) * LANES)
                y = _norm_rope(h[:, sl], g, c, s1, s2, seg)
                if kind == "q_rope_bf16":
                    o_ref[:, sl] = (y * Q_SOFTMAX_SCALE).astype(BF16)
                elif kind == "q_rope_f32":
                    o_ref[:, sl] = y
                else:
                    store_rows(o_ref, layout, t, n_t, y)
                    o16_ref[:, sl] = y.astype(BF16)
                    if mean_ref is not None:
                        mean_ref[:, sl] = jnp.mean(y, axis=0, keepdims=True)
        elif kind == "q_plain_bf16":
            outs.pop(0)[...] = (h * Q_SOFTMAX_SCALE).astype(BF16)
        elif kind == "kv_plain":
            o_ref = outs.pop(0)
            for t in range(n_t):
                store_rows(o_ref, layout, t, n_t, h[:, t * LANES:(t + 1) * LANES])
            outs.pop(0)[...] = h.astype(BF16)
        elif kind == "gate":
            outs.pop(0)[...] = h / (1.0 + jnp.exp(-h))
        else:
            raise ValueError(kind)
    assert not outs


def _rope_tables(pos):
    half = ROT_DIM // 2
    inv = ROPE_THETA ** (-jnp.arange(0, ROT_DIM, 2, dtype=F32) / ROT_DIM)
    ang = pos.astype(F32)[:, None] * inv[None, :]
    cos, sin = jnp.cos(ang), jnp.sin(ang)
    t = pos.shape[0]
    z_half = jnp.zeros((t, half), F32)
    z_rest = jnp.zeros((t, HEAD_DIM - ROT_DIM), F32)
    c = jnp.concatenate([cos, cos, jnp.ones((t, HEAD_DIM - ROT_DIM), F32)], axis=-1)
    s1 = jnp.concatenate([-sin, z_half, z_rest], axis=-1)
    s2 = jnp.concatenate([z_half, sin, z_rest], axis=-1)
    rep = LANES // HEAD_DIM
    return tuple(jnp.tile(a, (1, rep)) for a in (c, s1, s2))


def _project(x, g_norm, w_bf16, gq, gk, tables, plan, tm, seq, name):
    m, d = x.shape
    n_tab = tables[0].shape[0] // tm
    tiles_per_seq = seq // tm
    out_shapes, out_specs = [], []
    for kind, _, width, layout in plan:
        row = pl.BlockSpec((tm, width), lambda i: (i, 0))
        if kind in ("q_rope_bf16", "q_plain_bf16"):
            out_shapes += [jax.ShapeDtypeStruct((m, width), BF16)]
            out_specs += [row]
        elif kind in ("q_rope_f32", "gate"):
            out_shapes += [jax.ShapeDtypeStruct((m, width), F32)]
            out_specs += [row]
        elif kind in ("k_rope", "kv_plain", "k_rope_mean"):
            if layout == "rows":
                out_shapes += [jax.ShapeDtypeStruct((m, width), F32)]
                out_specs += [row]
            elif layout == "cols":
                out_shapes += [jax.ShapeDtypeStruct((m // seq, width, seq), F32)]
                out_specs += [pl.BlockSpec((None, width, tm),
                                           lambda i: (i // tiles_per_seq, 0, i % tiles_per_seq))]
            else:
                n_t = width // LANES
                out_shapes += [jax.ShapeDtypeStruct((m * n_t, LANES), F32)]
                out_specs += [pl.BlockSpec((tm * n_t, LANES), lambda i: (i, 0))]
            out_shapes += [jax.ShapeDtypeStruct((m, width), BF16)]
            out_specs += [row]
            if kind == "k_rope_mean":
                assert tm == MOBA_BLOCK
                out_shapes += [jax.ShapeDtypeStruct((m // tm, 1, width), F32)]
                out_specs += [pl.BlockSpec((None, 1, width), lambda i: (i, 0, 0))]
    tab_spec = pl.BlockSpec((tm, LANES), lambda i: (i % n_tab, 0))
    vec_d = pl.BlockSpec((1, d), lambda i: (0, 0))
    vec_l = pl.BlockSpec((1, LANES), lambda i: (0, 0))
    rep = LANES // HEAD_DIM
    return pl.pallas_call(
        functools.partial(_proj_kernel, plan=plan),
        grid=(m // tm,),
        in_specs=[pl.BlockSpec((tm, d), lambda i: (i, 0)), vec_d,
                  pl.BlockSpec(w_bf16.shape, lambda i: (0, 0), pipeline_mode=pl.Buffered(1)),
                  vec_l, vec_l, tab_spec, tab_spec, tab_spec],
        out_specs=out_specs,
        out_shape=out_shapes,
        compiler_params=_params(("parallel",)),
        name=name,
    )(x, g_norm.reshape(1, d), w_bf16, jnp.tile(gq, rep).reshape(1, LANES), jnp.tile(gk, rep).reshape(1, LANES),
      *tables)


def _finish_kernel(x_ref, w_ref, *refs):
    mix_refs, y_ref = refs[:-1], refs[-1]
    y = x_ref[...]
    row0 = 0
    for mix_ref in mix_refs:
        rows = mix_ref.shape[1]
        y = y + _dot(mix_ref[...], w_ref[row0:row0 + rows, :])
        row0 += rows
    y_ref[...] = y


def _finish(x, mix_parts, w_bf16, tm, name):
    m, d = x.shape
    kdim = w_bf16.shape[0]
    assert sum(p.shape[1] for p in mix_parts) == kdim
    return pl.pallas_call(
        _finish_kernel,
        grid=(m // tm,),
        in_specs=[pl.BlockSpec((tm, d), lambda i: (i, 0)), pl.BlockSpec((kdim, d), lambda i: (0, 0))]
        + [pl.BlockSpec((tm, p.shape[1]), lambda i: (i, 0)) for p in mix_parts],
        out_specs=pl.BlockSpec((tm, d), lambda i: (i, 0)),
        out_shape=jax.ShapeDtypeStruct((m, d), F32),
        compiler_params=_params(("parallel",)),
        name=name,
    )(x, w_bf16, *mix_parts)


def _lambda_value(lp, lam_init):
    a = jnp.sum(lp[0:1] * lp[1:2], axis=-1, keepdims=True)
    b = jnp.sum(lp[2:3] * lp[3:4], axis=-1, keepdims=True)
    return jnp.exp(a) - jnp.exp(b) + lam_init


def _softmax_tile(qq, k, v, carry, mask):
    m, l, acc = carry
    s = _dot_nt(qq, k)
    if mask is not None:
        s = jnp.where(mask, s, NEG)
    m_new = jnp.maximum(m, jnp.max(s, axis=-1, keepdims=True))
    alpha = jnp.exp2(m - m_new)
    p = jnp.exp2(s - m_new)
    l = alpha * l + jnp.sum(p, axis=-1, keepdims=True)
    acc = alpha * acc + _dot(p.astype(BF16), v)
    return m_new, l, acc


def _dot_tn(a, b):
    return lax.dot_general(a, b, (((0,), (0,)), ((), ())), preferred_element_type=F32)


def _softmax_tile_t(qq_t, k_ref, v_ref, start, n_keys, carry, mask):
    s, m_s, bias = _scores_t(qq_t, k_ref, start, n_keys, mask)
    return _absorb_t(v_ref, start, n_keys, s, m_s, bias, carry)


def _scores_t(qq_t, k_ref, start, n_keys, mask):
    s = _dot(k_ref[pl.ds(start, n_keys), :], qq_t)
    slabs = [s[c * ATTN_SLAB:(c + 1) * ATTN_SLAB] for c in range(n_keys // ATTN_SLAB)]
    if mask is None:
        return s, jnp.max(s, axis=0, keepdims=True), None
    if mask[0].dtype == jnp.bool_:
        s = jnp.concatenate([jnp.where(mk, x, NEG) for x, mk in zip(slabs, mask)], axis=0)
        return s, jnp.max(s, axis=0, keepdims=True), None
    m_s = None
    for x, row in zip(slabs, mask):
        m_c = jnp.max(x, axis=0, keepdims=True) + row
        m_s = m_c if m_s is None else jnp.maximum(m_s, m_c)
    return s, m_s, jnp.concatenate(mask, axis=0)


def _absorb_t(v_ref, start, n_keys, s, m_s, bias, carry):
    m, l, acc = carry
    m_new = jnp.maximum(m, m_s)
    alpha = jnp.exp2(m - m_new)
    if bias is None:
        p = jnp.exp2(s - m_new)
    else:
        shift = bias - m_new
        p = jnp.concatenate([jnp.exp2(s[c * ATTN_SLAB:(c + 1) * ATTN_SLAB] + shift[c:c + 1])
                             for c in range(n_keys // ATTN_SLAB)], axis=0)
    l = alpha * l + jnp.sum(p, axis=0, keepdims=True)
    acc = alpha * acc + _dot_tn(v_ref[pl.ds(start, n_keys), :], p.astype(BF16))
    return m_new, l, acc


def _pipelined_pairs(n_pairs, last_tile, n_keys, qq_t, k_ref, v_ref, mask_of, s_refs, carry):
    s_a, s_b = s_refs

    def scores(t, s_ref):
        s, m_s, bias = _scores_t(qq_t, k_ref, pl.multiple_of(t * n_keys, n_keys), n_keys, mask_of(t))
        s_ref[...] = s
        return m_s, bias

    def absorb(t, s_ref, stats, carry):
        return _absorb_t(v_ref, pl.multiple_of(t * n_keys, n_keys), n_keys, s_ref[...], *stats, carry)

    def trip(u, state):
        m_a, carry = state
        m_b = scores(2 * u + 1, s_b)
        carry = absorb(2 * u, s_a, m_a, carry)
        m_a = scores(jnp.minimum(2 * u + 2, last_tile), s_a)
        carry = absorb(2 * u + 1, s_b, m_b, carry)
        return m_a, carry

    return lax.fori_loop(0, n_pairs, trip, (scores(n_pairs * 0, s_a), carry))[1]


def _causal_masks(k_start, q_start, tq, n_keys, r, strict):
    q_pos = q_start + _iota((ATTN_SLAB, r), 1) % tq
    masks = []
    for c in range(n_keys // ATTN_SLAB):
        k_pos = k_start + c * ATTN_SLAB + _iota((ATTN_SLAB, r), 0)
        masks.append(k_pos < q_pos if strict else k_pos <= q_pos)
    return masks


def _stack_pair_t(q_t):
    row = _iota(q_t.shape, 0)
    zero = jnp.zeros_like(q_t)
    return jnp.concatenate([jnp.where(row < HEAD_DIM, q_t, zero), jnp.where(row >= HEAD_DIM, q_t, zero)], axis=1)


def _diff_attn_kernel(lam_ref, subg_ref, q_ref, k_ref, v_ref, sg_ref, o_ref, s_a, s_b, *, tq, tk, lam_init):
    i = pl.program_id(2)
    r = 2 * tq
    qq_t = _stack_pair_t(jnp.transpose(q_ref[...].astype(F32))).astype(BF16)
    n_full = (i * tq) // tk
    half = tk // 2

    carry = (jnp.full((1, r), NEG, F32), jnp.zeros((1, r), F32), jnp.zeros((LANES, r), F32))
    start = pl.multiple_of(n_full * tk, tk)
    carry = _softmax_tile_t(qq_t, k_ref, v_ref, start, tk, carry,
                            _causal_masks(start, i * tq, tq, tk, r, strict=False))
    _, l, acc = _pipelined_pairs(n_full, k_ref.shape[0] // half - 1, half, qq_t, k_ref, v_ref,
                                 lambda t: None, (s_a, s_b), carry)

    a = acc / l
    lam = _lambda_value(lam_ref[...], lam_init)
    o = jnp.transpose(a[:, :tq] - lam * a[:, tq:])
    ms = jnp.mean(o * o, axis=-1, keepdims=True)
    o = o * lax.rsqrt(ms + RMS_EPS) * subg_ref[...] * (1.0 - lam_init)
    o_ref[...] = (o * sg_ref[...]).astype(BF16)


def _softplus2(z):
    return jnp.maximum(z, 0.0) + jnp.log2(1.0 + jnp.exp2(-jnp.abs(z)))


def _later_matrix(tk, keys_axis=1):
    a, b = _iota((tk, tk), 0), _iota((tk, tk), 1)
    return (a > b if keys_axis == 1 else b > a).astype(BF16)


def _sb_kernel(q_ref, k_ref, v_ref, sg_ref, o_ref, *, tq, tk):
    i = pl.program_id(2)
    r = 2 * tq
    qq_t = _stack_pair_t(jnp.transpose(q_ref[...].astype(F32))).astype(BF16)
    tri = _later_matrix(ATTN_SLAB, keys_axis=0)
    n_full = (i * tq) // tk
    n_slabs = tk // ATTN_SLAB

    def tile(st, c, acc, valid):
        parts = []
        for u in range(n_slabs):
            z = _dot(k_ref[pl.ds(st + u * ATTN_SLAB, ATTN_SLAB), :], qq_t)
            sp = _softplus2(z)
            log_keep = -sp if valid is None else jnp.where(valid[u], -sp, 0.0)
            later = _dot(tri, log_keep.astype(BF16))
            parts.append((z - sp + later, later[0:1, :] + log_keep[0:1, :]))
        for u in reversed(range(n_slabs)):
            e, whole = parts[u]
            w = jnp.exp2(e + c)
            if valid is not None:
                w = jnp.where(valid[u], w, 0.0)
            acc = acc + _dot_tn(v_ref[pl.ds(st + u * ATTN_SLAB, ATTN_SLAB), :], w.astype(BF16))
            c = c + whole
        return c, acc

    start = pl.multiple_of(n_full * tk, tk)
    c, acc = tile(start, jnp.zeros((1, r), F32), jnp.zeros((LANES, r), F32),
                  _causal_masks(start, i * tq, tq, tk, r, strict=True))

    def live(state):
        t, c_max, _, _ = state
        return jnp.logical_and(t < n_full, c_max > -SB_DEAD_LOG2)

    def full_tile(state):
        t, _, c, acc = state
        c, acc = tile(pl.multiple_of((n_full - 1 - t) * tk, tk), c, acc, None)
        return t + 1, jnp.max(c), c, acc

    _, _, _, acc = lax.while_loop(live, full_tile, (jnp.int32(0), jnp.max(c), c, acc))
    row = _iota((LANES, tq), 0)
    o = jnp.transpose(jnp.where(row < HEAD_DIM, acc[:, :tq], acc[:, tq:]))
    o_ref[...] = (o * sg_ref[...]).astype(BF16)


def _top_blocks(g, n_valid, axis=-1, picks=None):
    axis = axis % g.ndim
    blk = _iota(g.shape, axis)
    nb = g.shape[axis]
    g = jnp.where(blk < n_valid, g, -jnp.inf)
    sel = jnp.zeros(g.shape, jnp.bool_)
    for _ in range(min(MOBA_TOPK, nb)):
        mx = jnp.max(g, axis=axis, keepdims=True)
        idx = jnp.min(jnp.where(g == mx, blk, nb), axis=axis, keepdims=True)
        pick = blk == idx
        sel = jnp.logical_or(sel, pick)
        g = jnp.where(pick, -jnp.inf, g)
        if picks is not None:
            picks.append(idx)
    return jnp.logical_and(sel, blk < n_valid)


def _moba_kernel(q_ref, k_ref, v_ref, km_ref, sg_ref, o_ref, sel_s, s_a, s_b, *, tq, group):
    i = pl.program_id(2)
    r = 2 * tq
    blk = MOBA_BLOCK
    tile_blocks = tq // blk
    first = i * tile_blocks
    qf_t = _stack_pair_t(jnp.transpose(q_ref[...]))
    row = _iota((LANES, tq), 0)
    q_hi, q_lo = _split_bf16(qf_t)
    qq_t = (qf_t * Q_SOFTMAX_SCALE).astype(BF16)

    km_hi, km_lo = _split_bf16(km_ref[...])
    g = _dot(km_hi, q_hi) + _dot(km_hi, q_lo) + _dot(km_lo, q_hi)
    col = _iota((1, r), 1) % tq
    own = col // blk
    sel = _top_blocks(g, first + own, axis=0)
    block_id = _iota(sel.shape, 0)
    sel_s[...] = jnp.where(jnp.logical_and(sel, block_id < first), 0.0, NEG)

    start = pl.multiple_of(first * blk, tq)
    causal = _iota((blk, r), 0) <= col % blk
    masks = []
    for a in range(tile_blocks):
        picked = jnp.max(jnp.where(jnp.logical_and(sel, block_id == first + a), 1.0, 0.0), axis=0,
                         keepdims=True) > 0.0
        masks.append(jnp.logical_or(jnp.logical_and(own == a, causal), jnp.logical_and(own > a, picked)))
    carry = (jnp.full((1, r), NEG, F32), jnp.zeros((1, r), F32), jnp.zeros((LANES, r), F32))
    carry = _softmax_tile_t(qq_t, k_ref, v_ref, start, tq, carry, masks)

    per_tile = group // 2

    def chosen(t):
        return [sel_s[pl.ds(t * per_tile + c, 1), :] for c in range(per_tile)]

    _, l, acc = _pipelined_pairs((first + group - 1) // group, sel_s.shape[0] // per_tile - 1, per_tile * blk,
                                 qq_t, k_ref, v_ref, chosen, (s_a, s_b), carry)
    o_t = acc / l
    o = jnp.transpose(jnp.where(row < HEAD_DIM, o_t[:, :tq], o_t[:, tq:]))
    o_ref[...] = (o * sg_ref[...]).astype(BF16)


def _attn_specs(tq, s, group0=0):
    q_spec = pl.BlockSpec((None, tq, LANES), lambda b, h, i: (b, i, h + group0))
    kv_spec = pl.BlockSpec((None, s, LANES), lambda b, h, i: (b, 0, h))
    return q_spec, kv_spec


def _diff_attn_prompt(q16, k16, v16, sgate, lam_p, subln_g, lam_init, tq, tk):
    b, s, w = q16.shape
    q_spec, kv_spec = _attn_specs(tq, s)
    return pl.pallas_call(
        functools.partial(_diff_attn_kernel, tq=tq, tk=tk, lam_init=lam_init),
        grid=(b, w // LANES, s // tq),
        in_specs=[pl.BlockSpec(lam_p.shape, lambda b, h, i: (0, 0)),
                  pl.BlockSpec((1, LANES), lambda b, h, i: (0, 0)),
                  q_spec, kv_spec, kv_spec, q_spec],
        out_specs=q_spec,
        out_shape=jax.ShapeDtypeStruct((b, s, w), BF16),
        scratch_shapes=[pltpu.VMEM((tk // 2, 2 * tq), F32)] * 2,
        compiler_params=_params(("parallel", "parallel", "arbitrary")),
        name="diff_attn_prompt",
    )(lam_p, subln_g.reshape(1, LANES), q16, k16, v16, sgate)


def _sb_prompt(q16, k16, v16, sgate, gate_group0, tq, tk):
    b, s, w = q16.shape
    q_spec, kv_spec = _attn_specs(tq, s)
    sg_spec = pl.BlockSpec((None, tq, LANES), lambda b, h, i: (b, i, h + gate_group0))
    return pl.pallas_call(
        functools.partial(_sb_kernel, tq=tq, tk=tk),
        grid=(b, w // LANES, s // tq),
        in_specs=[q_spec, kv_spec, kv_spec, sg_spec],
        out_specs=q_spec,
        out_shape=jax.ShapeDtypeStruct((b, s, w), BF16),
        compiler_params=_params(("parallel", "parallel", "arbitrary")),
        name="sb_prompt",
    )(q16, k16, v16, sgate)


def _moba_prompt(q, k16, v16, kmean, sgate):
    b, s, w = q.shape
    assert MOBA_BLOCK == ATTN_SLAB
    nb = s // MOBA_BLOCK
    tq = MOBA_BLOCK * math.gcd(nb, MOBA_TILE_BLOCKS)
    q_spec, kv_spec = _attn_specs(tq, s)
    group = math.gcd(nb, MOBA_GROUP)
    return pl.pallas_call(
        functools.partial(_moba_kernel, tq=tq, group=group),
        grid=(b, w // LANES, s // tq),
        in_specs=[q_spec, kv_spec, kv_spec,
                  pl.BlockSpec((None, nb, LANES), lambda b, h, i: (b, 0, h)), q_spec],
        out_specs=q_spec,
        out_shape=jax.ShapeDtypeStruct((b, s, w), BF16),
        scratch_shapes=[pltpu.VMEM((nb, 2 * tq), F32)]
        + [pltpu.VMEM((group // 2 * MOBA_BLOCK, 2 * tq), F32)] * 2,
        compiler_params=_params(("parallel", "parallel", "arbitrary")),
        name="moba_prompt",
    )(q, k16, v16, kmean, sgate)


def _row_heads(x_row, n_rows, lanes_per_row):
    w = x_row.shape[1]
    keep = _iota((n_rows, w), 1) // lanes_per_row == _iota((n_rows, w), 0)
    return jnp.where(keep, jnp.broadcast_to(x_row, (n_rows, w)), 0.0)


def _dup_rows(x, reps):
    n, w = x.shape
    row = _iota((n * reps, w), 0) // reps
    out = jnp.zeros((n * reps, w), x.dtype)
    for h in range(n):
        out = jnp.where(row == h, x[h:h + 1, :], out)
    return out


def _stack_rows(rows):
    n, w = len(rows), rows[0].shape[1]
    row = _iota((n, w), 0)
    out = jnp.zeros((n, w), rows[0].dtype)
    for h, r in enumerate(rows):
        out = jnp.where(row == h, r, out)
    return out


def _head_dots(qcol_ref, kt_ref, n_heads):
    rows = []
    for h in range(n_heads):
        sl = slice(h * HEAD_DIM, (h + 1) * HEAD_DIM)
        rows.append(jnp.sum(qcol_ref[sl, :] * kt_ref[sl, :], axis=0, keepdims=True))
    return _stack_rows(rows)


def _add_weighted(acc_ref, w, vt_refs, n_heads):
    for h in range(n_heads):
        sl = slice(h * HEAD_DIM, (h + 1) * HEAD_DIM)
        a = acc_ref[sl, :]
        for u, vt_ref in enumerate(vt_refs):
            a = a + w[u * n_heads + h:u * n_heads + h + 1, :] * vt_ref[sl, :]
        acc_ref[sl, :] = a


def _sb_weights_pages(zs, tri, c):
    n_h = zs[0].shape[0]
    z = jnp.concatenate(zs, axis=0)
    sp = _softplus2(z)
    log_keep = -sp
    hi, lo = _split_bf16(log_keep)
    rows = z.shape[0]
    both = _dot(jnp.concatenate([hi, lo], axis=0), tri)
    later = both[:rows] + both[rows:]
    whole = later[:, 0:1] + log_keep[:, 0:1]
    offsets = []
    for u in range(len(zs)):
        offsets.append(c)
        c = c + whole[u * n_h:(u + 1) * n_h]
    return jnp.exp2(z - sp + later + jnp.concatenate(offsets, axis=0)), c


def _lane_sums_as_row(x):
    hi, lo = _split_bf16(x)
    ones = jnp.ones((8, LANES), BF16)
    return (_dot_nt(ones, hi) + _dot_nt(ones, lo))[0:1, :]


def _decode_even_kernel(pt_ref, lam_ref, subg_ref, qa_ref, kan_ref, van_ref, qb_ref, sg_ref, *refs,
                        pps, lam_init):
    del pt_ref
    page_refs = refs[:4 * pps]
    o_ref = refs[4 * pps]
    qa_s, m_s, l_s, acca_s, c_s, accb_s = refs[4 * pps + 1:]
    step = pl.program_id(1)
    n_ha = qa_ref.shape[0]
    n_a = 2 * n_ha
    n_b = accb_s.shape[0] // HEAD_DIM
    wa = n_ha * LANES

    @pl.when(step == 0)
    def _():
        rows = _dup_rows(qa_ref[...].astype(F32), 2)
        qa = jnp.where(_iota(rows.shape, 1) // HEAD_DIM == _iota(rows.shape, 0) % 2, rows, 0.0)
        qa_s[...] = qa
        m_s[...] = jnp.sum(qa * _dup_rows(kan_ref[...], 2), axis=-1, keepdims=True)
        l_s[...] = jnp.ones_like(l_s)
        acca_s[...] = _dup_rows(van_ref[...], 2)
        c_s[...] = jnp.zeros_like(c_s)
        accb_s[...] = jnp.zeros_like(accb_s)

    rows_a = pps * page_refs[0].shape[0]
    own_head = _iota((n_a, rows_a), 1) % n_ha == _iota((n_a, rows_a), 0) // 2
    ak = jnp.concatenate([page_refs[4 * u][...].astype(BF16) for u in range(pps)], axis=0)
    av = jnp.concatenate([page_refs[4 * u + 1][...].astype(BF16) for u in range(pps)], axis=0)
    m_s[...], l_s[...], acca_s[...] = _softmax_tile(qa_s[...].astype(BF16), ak, av,
                                                    (m_s[...], l_s[...], acca_s[...]), own_head)
    zs = [_head_dots(qb_ref, page_refs[4 * u + 2], n_b) for u in range(pps)]
    w, c_s[...] = _sb_weights_pages(zs, _later_matrix(LANES), c_s[...])
    _add_weighted(accb_s, w, [page_refs[4 * u + 3] for u in range(pps)], n_b)

    @pl.when(step == pl.num_programs(1) - 1)
    def _():
        lam = _lambda_value(lam_ref[...], lam_init)
        a = acca_s[...] / l_s[...]
        for h in range(n_ha):
            sl = slice(h * LANES, (h + 1) * LANES)
            x = a[2 * h:2 * h + 1, :] - lam * a[2 * h + 1:2 * h + 2, :]
            ms = jnp.mean(x * x, axis=-1, keepdims=True)
            x = x * lax.rsqrt(ms + RMS_EPS) * subg_ref[...] * (1.0 - lam_init)
            o_ref[:, sl] = (x * sg_ref[:, sl]).astype(BF16)
        o_ref[:, wa:] = (_lane_sums_as_row(accb_s[...]) * sg_ref[:, wa:]).astype(BF16)


def _page_specs(cache, li, n_pages, pps, reverse):
    r = cache.shape[2]

    def make(u):
        def index(b, s, pt):
            j = s * pps + u
            j = n_pages - 1 - j if reverse else j
            return (li, pt[b, j], 0, 0)
        return pl.BlockSpec((None, None, r, LANES), index)

    return [make(u) for u in range(pps)]


def _col_bcast(x):
    return jnp.broadcast_to(x.astype(F32)[:, :, None], x.shape + (LANES,))


def _decode_even(pt, lam_p, subln_g, lam_init, qa16, ka, va, qb16, sgate, caches, li, pps):
    db, n_pages = pt.shape
    wa, wb = qa16.shape[-1], qb16.shape[-1]
    n_ha = wa // LANES
    whole = lambda shp: pl.BlockSpec((None,) + shp, lambda b, s, pt: (b,) + (0,) * len(shp))
    specs_by_cache = [_page_specs(c, li, n_pages, pps, True) for c in caches]
    page_specs = [specs_by_cache[t][u] for u in range(pps) for t in range(4)]
    page_args = [caches[t] for u in range(pps) for t in range(4)]
    grid_spec = pltpu.PrefetchScalarGridSpec(
        num_scalar_prefetch=1,
        grid=(db, n_pages // pps),
        in_specs=[pl.BlockSpec(lam_p.shape, lambda b, s, pt: (0, 0)),
                  pl.BlockSpec((1, LANES), lambda b, s, pt: (0, 0)),
                  whole((n_ha, LANES)), whole((n_ha, LANES)), whole((n_ha, LANES)), whole((wb, LANES)),
                  whole((1, wa + wb))] + page_specs,
        out_specs=whole((1, wa + wb)),
        scratch_shapes=[pltpu.VMEM((2 * n_ha, LANES), F32),
                        pltpu.VMEM((2 * n_ha, 1), F32), pltpu.VMEM((2 * n_ha, 1), F32),
                        pltpu.VMEM((2 * n_ha, LANES), F32),
                        pltpu.VMEM((wb // HEAD_DIM, 1), F32), pltpu.VMEM((wb, LANES), F32)],
    )
    heads = lambda x: x.reshape(db, n_ha, LANES)
    return pl.pallas_call(
        functools.partial(_decode_even_kernel, pps=pps, lam_init=lam_init),
        grid_spec=grid_spec,
        out_shape=jax.ShapeDtypeStruct((db, 1, wa + wb), BF16),
        compiler_params=_params(("parallel", "arbitrary")),
        name="decode_even",
    )(pt, lam_p, subln_g.reshape(1, LANES), heads(qa16), heads(ka), heads(va), _col_bcast(qb16),
      sgate.reshape(db, 1, wa + wb), *page_args)


def _moba_scores_kernel(pt_ref, qcol_ref, qrow_ref, kn_ref, *refs, pps, n_blocks):
    del pt_ref
    page_refs = refs[:pps]
    p_ref, pnew_ref, ids_ref = refs[pps:pps + 3]
    s_s = refs[pps + 3]
    step = pl.program_id(1)
    n_h = qcol_ref.shape[0] // HEAD_DIM
    ppb = MOBA_BLOCK // LANES

    for u in range(pps):
        s_s[step * pps + u] = _head_dots(qcol_ref, page_refs[u], n_h)

    @pl.when(step == pl.num_programs(1) - 1)
    def _():
        lane = _iota((n_h, LANES), 1)
        g = jnp.zeros((n_h, LANES), F32)
        for n in range(n_blocks):
            tot = sum(s_s[n * ppb + t] for t in range(ppb))
            g = jnp.where(lane == n, jnp.sum(tot, axis=-1, keepdims=True) * (1.0 / MOBA_BLOCK), g)
        picks = []
        sel = _top_blocks(g, n_blocks, picks=picks)
        ids = jnp.zeros((n_h, LANES), jnp.int32)
        for r, idx in enumerate(picks):
            ids = jnp.where(lane == r, idx, ids)
        ids_ref[...] = ids
        chosen = [sel[:, n:n + 1] for n in range(n_blocks)]
        s_new = jnp.sum(_row_heads(qrow_ref[...], n_h, HEAD_DIM) * kn_ref[...], axis=-1, keepdims=True) * QK_SCALE
        m_lanes = jnp.full((n_h, LANES), NEG, F32)
        for j in range(n_blocks * ppb):
            m_lanes = jnp.maximum(m_lanes, jnp.where(chosen[j // ppb], s_s[j] * QK_SCALE, NEG))
        m = jnp.maximum(jnp.max(m_lanes, axis=-1, keepdims=True), s_new)
        l_lanes = jnp.zeros((n_h, LANES), F32)
        for j in range(n_blocks * ppb):
            e = jnp.where(chosen[j // ppb], jnp.exp(s_s[j] * QK_SCALE - m), 0.0)
            s_s[j] = e
            l_lanes = l_lanes + e
        e_new = jnp.exp(s_new - m)
        inv = 1.0 / (jnp.sum(l_lanes, axis=-1, keepdims=True) + e_new)
        for j in range(n_blocks * ppb):
            p_ref[j] = s_s[j] * inv
        pnew_ref[...] = jnp.broadcast_to(e_new * inv, pnew_ref.shape)


def _moba_values_kernel(pt_ref, ids_ref, p_ref, pnew_ref, vn_ref, sg_ref, *refs, n_sel, ppb):
    del pt_ref
    slices, o_ref = refs[:-1], refs[-1]
    b, pair = pl.program_id(0), pl.program_id(1)
    heads_per_step = LANES // HEAD_DIM
    accs = []
    for c in range(heads_per_step):
        h = pair * heads_per_step + c
        acc = jnp.zeros((HEAD_DIM, LANES), F32)
        for r in range(n_sel):
            blk = ids_ref[b, h * n_sel + r]
            for u in range(ppb):
                w = p_ref[blk * ppb + u, pl.ds(h, 1), :]
                acc = acc + w * slices[(c * n_sel + r) * ppb + u][...]
        accs.append(acc)
    o = _lane_sums_as_row(jnp.concatenate(accs, axis=0)) + pnew_ref[...] * vn_ref[...]
    o_ref[...] = (o * sg_ref[...]).astype(BF16)


def _decode_odd(pt, q, k_new, v_new, sgate, cache_k, cache_v, li, pps):
    db, n_pages = pt.shape
    w = q.shape[-1]
    n_h = w // HEAD_DIM
    page = LANES
    past = n_pages * page
    assert past % MOBA_BLOCK == 0 and past >= MOBA_BLOCK and MOBA_BLOCK % page == 0
    n_blocks = past // MOBA_BLOCK
    row = lambda wd: pl.BlockSpec((None, 1, wd), lambda b, s, pt: (b, 0, 0))
    r3 = lambda x: x.reshape(db, 1, x.shape[-1])
    assert n_blocks >= MOBA_TOPK
    per_head = pl.BlockSpec((None, n_h, LANES), lambda b, s, pt: (b, 0, 0))
    probs, p_new, ids = pl.pallas_call(
        functools.partial(_moba_scores_kernel, pps=pps, n_blocks=n_blocks),
        grid_spec=pltpu.PrefetchScalarGridSpec(
            num_scalar_prefetch=1,
            grid=(db, n_pages // pps),
            in_specs=[pl.BlockSpec((None, w, LANES), lambda b, s, pt: (b, 0, 0)), row(w), row(w)]
            + _page_specs(cache_k, li, n_pages, pps, False),
            out_specs=[pl.BlockSpec((None, n_pages, n_h, page), lambda b, s, pt: (b, 0, 0, 0)),
                       per_head, per_head],
            scratch_shapes=[pltpu.VMEM((n_pages, n_h, page), F32)],
        ),
        out_shape=[jax.ShapeDtypeStruct((db, n_pages, n_h, page), F32),
                   jax.ShapeDtypeStruct((db, n_h, LANES), F32),
                   jax.ShapeDtypeStruct((db, n_h, LANES), jnp.int32)],
        compiler_params=_params(("parallel", "arbitrary")),
        name="decode_moba_scores",
    )(pt, _col_bcast(q), r3(q), r3(k_new), *([cache_k] * pps))
    p_new_row = jnp.repeat(p_new[:, :, 0], HEAD_DIM, axis=-1).reshape(db, 1, w)

    ppb = MOBA_BLOCK // page
    heads_per_step = LANES // HEAD_DIM
    block_ids = ids[:, :, :MOBA_TOPK].reshape(db, n_h * MOBA_TOPK)

    def slice_spec(c, r, u):
        def index(b, pair, pt, bid):
            h = pair * heads_per_step + c
            return (li, pt[b, bid[b, h * MOBA_TOPK + r] * ppb + u], h, 0)
        return pl.BlockSpec((None, None, HEAD_DIM, LANES), index)

    slice_specs = [slice_spec(c, r, u) for c in range(heads_per_step) for r in range(MOBA_TOPK)
                   for u in range(ppb)]
    lanes = lambda: pl.BlockSpec((None, 1, LANES), lambda b, pair, pt, bid: (b, 0, pair))
    return pl.pallas_call(
        functools.partial(_moba_values_kernel, n_sel=MOBA_TOPK, ppb=ppb),
        grid_spec=pltpu.PrefetchScalarGridSpec(
            num_scalar_prefetch=2,
            grid=(db, n_h // heads_per_step),
            in_specs=[pl.BlockSpec((None, n_pages, n_h, page), lambda b, pair, pt, bid: (b, 0, 0, 0)),
                      lanes(), lanes(), lanes()] + slice_specs,
            out_specs=lanes(),
        ),
        out_shape=jax.ShapeDtypeStruct((db, 1, w), BF16),
        compiler_params=_params(("parallel", "arbitrary")),
        name="decode_moba_values",
    )(pt, block_ids, probs, p_new_row, r3(v_new), r3(sgate), *([cache_v] * len(slice_specs)))


PROMPT_TM = 256
ATTN_TQ = 512
ATTN_TK = 512
SB_TILE = 256
MOBA_GROUP = 4
MOBA_TILE_BLOCKS = 2
PAGES_PER_STEP = 16


def _rows_view(c):
    l, n, p, h, hd = c.shape
    assert p == LANES and hd == LANES
    return c.reshape(l, n, p * h, hd)


def _cols_view(c):
    l, n, p, h, hd = c.shape
    assert p == LANES and hd == HEAD_DIM
    return jnp.transpose(c, (0, 1, 3, 4, 2)).reshape(l, n, h * hd, p)


def _heads_from_cols(x, n_heads):
    b, w, s = x.shape
    return jnp.transpose(x.reshape(b, n_heads, w // n_heads, s), (0, 3, 1, 2))


def _even_layer(x_p, x_s, caches, li, pt, g_norm, w_in, qk_g, lam_p, subln_g, w_out, layer):
    lam_init = 0.8 - 0.6 * math.exp(-0.3 * layer)
    b, s, d = x_p.shape
    db, t, _ = x_s.shape
    wa = w_out.shape[0] // 2
    wb = wa
    def plan(wide, narrow):
        return (("q_rope_bf16", 0, wa, None), ("k_rope", wa, wa, wide), ("kv_plain", 2 * wa, wa, wide),
                ("q_plain_bf16", 3 * wa, wb, None), ("kv_plain", 3 * wa + wb, wb, narrow),
                ("kv_plain", 3 * wa + 2 * wb, wb, narrow), ("gate", 3 * wa + 3 * wb, wa + wb, None))

    w_in16 = w_in.astype(BF16)
    w_out16 = w_out.astype(BF16)
    past = pt.shape[1] * LANES
    n_a = wa // (2 * HEAD_DIM)
    n_b = wb // HEAD_DIM

    tabs = _rope_tables(jnp.arange(s))
    qa16, ka, ka16, va, va16, qb16, kb, kb16, vb, vb16, sgate = _project(
        x_p.reshape(b * s, d), g_norm, w_in16, qk_g[0], qk_g[1], tabs, plan("heads", "cols"), PROMPT_TM, s,
        "proj_even_prompt")
    r3 = lambda a: a.reshape(b, s, a.shape[-1])
    sg3 = r3(sgate)
    mix_a = _diff_attn_prompt(r3(qa16), r3(ka16), r3(va16), sg3, lam_p, subln_g, lam_init, ATTN_TQ, ATTN_TK)
    mix_b = _sb_prompt(r3(qb16), r3(kb16), r3(vb16), sg3, wa // LANES, SB_TILE, SB_TILE)
    y_p = _finish(x_p.reshape(b * s, d), [mix_a.reshape(b * s, wa), mix_b.reshape(b * s, wb)], w_out16, 512,
                  "finish_even_prompt").reshape(b, s, d)

    tabs_s = _rope_tables(jnp.tile(past + jnp.arange(t), db))
    qa16_s, ka_s, _, va_s, _, qb16_s, kb_s, _, vb_s, _, sgate_s = _project(
        x_s.reshape(db * t, d), g_norm, w_in16, qk_g[0], qk_g[1], tabs_s, plan("rows", "rows"), db * t, db * t,
        "proj_even_sample")
    assert t == 1
    mix_s = _decode_even(pt, lam_p, subln_g, lam_init, qa16_s, ka_s, va_s, qb16_s, sgate_s,
                         caches, li, PAGES_PER_STEP)
    y_s = _finish(x_s.reshape(db * t, d), [mix_s.reshape(db * t, wa + wb)], w_out16, db * t,
                  "finish_even_sample").reshape(db, t, d)

    rows_p = (ka.reshape(b, s, n_a, 2 * HEAD_DIM), va.reshape(b, s, n_a, 2 * HEAD_DIM),
              _heads_from_cols(kb, n_b), _heads_from_cols(vb, n_b))
    rows_s = (ka_s.reshape(db, t, n_a, 2 * HEAD_DIM), va_s.reshape(db, t, n_a, 2 * HEAD_DIM),
              kb_s.reshape(db, t, n_b, HEAD_DIM), vb_s.reshape(db, t, n_b, HEAD_DIM))
    return y_p, y_s, rows_p, rows_s


def _odd_layer(x_p, x_s, cache_k, cache_v, li, pt, g_norm, w_in, qk_g, w_out):
    b, s, d = x_p.shape
    db, t, _ = x_s.shape
    wc = w_out.shape[0]
    w_in16 = w_in.astype(BF16)
    w_out16 = w_out.astype(BF16)
    past = pt.shape[1] * LANES
    assert s % MOBA_BLOCK == 0

    def plan(k_kind, layout):
        return (("q_rope_f32", 0, wc, None), (k_kind, wc, wc, layout), ("kv_plain", 2 * wc, wc, layout),
                ("gate", 3 * wc, wc, None))

    tabs = _rope_tables(jnp.arange(s))
    q, k, k16, kmean, v, v16, sgate = _project(
        x_p.reshape(b * s, d), g_norm, w_in16, qk_g[0], qk_g[1], tabs, plan("k_rope_mean", "cols"), MOBA_BLOCK, s,
        "proj_odd_prompt")
    r3 = lambda a: a.reshape(b, s, a.shape[-1])
    mix = _moba_prompt(r3(q), r3(k16), r3(v16), kmean.reshape(b, s // MOBA_BLOCK, wc), r3(sgate))
    y_p = _finish(x_p.reshape(b * s, d), [mix.reshape(b * s, wc)], w_out16, 512,
                  "finish_odd_prompt").reshape(b, s, d)

    assert t == 1
    tabs_s = _rope_tables(jnp.tile(past + jnp.arange(t), db))
    q_s, k_s, _, v_s, _, sgate_s = _project(
        x_s.reshape(db * t, d), g_norm, w_in16, qk_g[0], qk_g[1], tabs_s, plan("k_rope", "rows"), db * t, db * t,
        "proj_odd_sample")
    mix_s = _decode_odd(pt, q_s, k_s, v_s, sgate_s, cache_k, cache_v, li, PAGES_PER_STEP)
    y_s = _finish(x_s.reshape(db * t, d), [mix_s.reshape(db * t, wc)], w_out16, db * t,
                  "finish_odd_sample").reshape(db, t, d)

    n_c = wc // HEAD_DIM
    return (y_p, y_s, (_heads_from_cols(k, n_c), _heads_from_cols(v, n_c)),
            (k_s.reshape(db, t, n_c, HEAD_DIM), v_s.reshape(db, t, n_c, HEAD_DIM)))


def kernel(x_prompt, x_sample, cache_a_k, cache_a_v, cache_b_k, cache_b_v, cache_c_k, cache_c_v, page_table,
           norm_even, w_in_even, qk_norm_a, lambda_a, subln_a, w_out_even, norm_odd, w_in_odd, qk_norm_c,
           w_out_odd):
    depth = norm_even.shape[0] + norm_odd.shape[0]
    even_caches = [_rows_view(cache_a_k), _rows_view(cache_a_v), _cols_view(cache_b_k), _cols_view(cache_b_v)]
    odd_k, odd_v = _cols_view(cache_c_k), _cols_view(cache_c_v)
    y_p, y_s = x_prompt, x_sample
    ev_p, ev_s, od_p, od_s = [], [], [], []
    for layer in range(depth):
        i = layer // 2
        if layer % 2 == 0:
            y_p, y_s, rp, rs = _even_layer(y_p, y_s, even_caches, i, page_table, norm_even[i], w_in_even[i],
                                           qk_norm_a[i], lambda_a[i], subln_a[i], w_out_even[i], layer)
            ev_p.append(rp)
            ev_s.append(rs)
        else:
            y_p, y_s, rp, rs = _odd_layer(y_p, y_s, odd_k, odd_v, i, page_table, norm_odd[i], w_in_odd[i],
                                          qk_norm_c[i], w_out_odd[i])
            od_p.append(rp)
            od_s.append(rs)

    def stack(rows, j):
        return jnp.stack([r[j] for r in rows])

    return (y_p, y_s,
            stack(ev_p, 0), stack(ev_p, 1), stack(ev_p, 2), stack(ev_p, 3), stack(od_p, 0), stack(od_p, 1),
            stack(ev_s, 0), stack(ev_s, 1), stack(ev_s, 2), stack(ev_s, 3), stack(od_s, 0), stack(od_s, 1))
```

```python
import functools
import math

import jax
import jax.numpy as jnp
from jax import lax
from jax.experimental import pallas as pl
from jax.experimental.pallas import tpu as pltpu

HEAD_DIM = 64
ROT_DIM = HEAD_DIM // 4
ROPE_THETA = 500000.0
RMS_EPS = 1e-6
MOBA_BLOCK = 256
MOBA_TOPK = 3
LANES = 128
NEG = -1e30
SB_DEAD_LOG2 = 150.0
QK_SCALE = HEAD_DIM ** -0.5
Q_SOFTMAX_SCALE = QK_SCALE * math.log2(math.e)
ATTN_SLAB = 256
VMEM_LIMIT = 52 * 1024 * 1024

F32 = jnp.float32
BF16 = jnp.bfloat16


def _params(sem, vmem=VMEM_LIMIT):
    return pltpu.CompilerParams(dimension_semantics=sem, vmem_limit_bytes=vmem)


def _dot(a, b):
    return jnp.dot(a, b, preferred_element_type=F32)


def _dot_nt(a, b):
    return lax.dot_general(a, b, (((1,), (1,)), ((), ())), preferred_element_type=F32)


def _split_bf16(x):
    hi = x.astype(BF16)
    lo = (x - hi.astype(F32)).astype(BF16)
    return hi, lo


def _iota(shape, dim):
    return lax.broadcasted_iota(jnp.int32, shape, dim)


def _head_seg_matrix():
    return (_iota((LANES, LANES), 0) // HEAD_DIM == _iota((LANES, LANES), 1) // HEAD_DIM).astype(BF16)


def _norm_rope(x, g, c, s1, s2, seg):
    hi, lo = _split_bf16(x * x)
    ms = (_dot(hi, seg) + _dot(lo, seg)) * (1.0 / HEAD_DIM)
    xn = x * lax.rsqrt(ms + RMS_EPS) * g
    half = ROT_DIM // 2
    return xn * c + pltpu.roll(xn, LANES - half, 1) * s1 + pltpu.roll(xn, half, 1) * s2


def _proj_kernel(x_ref, gn_ref, w_ref, gq_ref, gk_ref, c_ref, s1_ref, s2_ref, *out_refs, plan):
    x = x_ref[...]
    ms = jnp.mean(x * x, axis=-1, keepdims=True)
    xn = (x * lax.rsqrt(ms + RMS_EPS) * gn_ref[...]).astype(BF16)
    seg = _head_seg_matrix()
    c, s1, s2 = c_ref[...], s1_ref[...], s2_ref[...]
    outs = list(out_refs)
    tm = x.shape[0]

    def store_rows(o_ref, layout, t, n_t, y):
        if layout == "rows":
            o_ref[:, t * LANES:(t + 1) * LANES] = y
        elif layout == "cols":
            o_ref[t * LANES:(t + 1) * LANES, :] = jnp.transpose(y)
        elif layout == "heads":
            o_ref[pl.ds(t, tm, stride=n_t), :] = y
        else:
            raise ValueError(layout)

    for kind, col0, width, layout in plan:
        h = _dot(xn, w_ref[:, col0:col0 + width])
        n_t = width // LANES
        if kind in ("q_rope_bf16", "q_rope_f32", "k_rope", "k_rope_mean"):
            g = gq_ref[...] if kind.startswith("q") else gk_ref[...]
            if kind == "q_rope_bf16":
                o_ref = outs.pop(0)
            elif kind == "q_rope_f32":
                o_ref = outs.pop(0)
            else:
                o_ref, o16_ref = outs.pop(0), outs.pop(0)
                mean_ref = outs.pop(0) if kind == "k_rope_mean" else None
            for t in range(n_t):
                sl = slice(t * LANES, (t + 1) * LANES)
                y = _norm_rope(h[:, sl], g, c, s1, s2, seg)
                if kind == "q_rope_bf16":
                    o_ref[:, sl] = (y * Q_SOFTMAX_SCALE).astype(BF16)
                elif kind == "q_rope_f32":
                    o_ref[:, sl] = y
                else:
                    store_rows(o_ref, layout, t, n_t, y)
                    o16_ref[:, sl] = y.astype(BF16)
                    if mean_ref is not None:
                        mean_ref[:, sl] = jnp.mean(y, axis=0, keepdims=True)
        elif kind == "q_plain_bf16":
            outs.pop(0)[...] = (h * Q_SOFTMAX_SCALE).astype(BF16)
        elif kind == "kv_plain":
            o_ref = outs.pop(0)
            for t in range(n_t):
                store_rows(o_ref, layout, t, n_t, h[:, t * LANES:(t + 1) * LANES])
            outs.pop(0)[...] = h.astype(BF16)
        elif kind == "gate":
            outs.pop(0)[...] = h / (1.0 + jnp.exp(-h))
        else:
            raise ValueError(kind)
    assert not outs


def _rope_tables(pos):
    half = ROT_DIM // 2
    inv = ROPE_THETA ** (-jnp.arange(0, ROT_DIM, 2, dtype=F32) / ROT_DIM)
    ang = pos.astype(F32)[:, None] * inv[None, :]
    cos, sin = jnp.cos(ang), jnp.sin(ang)
    t = pos.shape[0]
    z_half = jnp.zeros((t, half), F32)
    z_rest = jnp.zeros((t, HEAD_DIM - ROT_DIM), F32)
    c = jnp.concatenate([cos, cos, jnp.ones((t, HEAD_DIM - ROT_DIM), F32)], axis=-1)
    s1 = jnp.concatenate([-sin, z_half, z_rest], axis=-1)
    s2 = jnp.concatenate([z_half, sin, z_rest], axis=-1)
    rep = LANES // HEAD_DIM
    return tuple(jnp.tile(a, (1, rep)) for a in (c, s1, s2))


def _project(x, g_norm, w_bf16, gq, gk, tables, plan, tm, seq, name):
    m, d = x.shape
    n_tab = tables[0].shape[0] // tm
    tiles_per_seq = seq // tm
    out_shapes, out_specs = [], []
    for kind, _, width, layout in plan:
        row = pl.BlockSpec((tm, width), lambda i: (i, 0))
        if kind in ("q_rope_bf16", "q_plain_bf16"):
            out_shapes += [jax.ShapeDtypeStruct((m, width), BF16)]
            out_specs += [row]
        elif kind in ("q_rope_f32", "gate"):
            out_shapes += [jax.ShapeDtypeStruct((m, width), F32)]
            out_specs += [row]
        elif kind in ("k_rope", "kv_plain", "k_rope_mean"):
            if layout == "rows":
                out_shapes += [jax.ShapeDtypeStruct((m, width), F32)]
                out_specs += [row]
            elif layout == "cols":
                out_shapes += [jax.ShapeDtypeStruct((m // seq, width, seq), F32)]
                out_specs += [pl.BlockSpec((None, width, tm),
                                           lambda i: (i // tiles_per_seq, 0, i % tiles_per_seq))]
            else:
                n_t = width // LANES
                out_shapes += [jax.ShapeDtypeStruct((m * n_t, LANES), F32)]
                out_specs += [pl.BlockSpec((tm * n_t, LANES), lambda i: (i, 0))]
            out_shapes += [jax.ShapeDtypeStruct((m, width), BF16)]
            out_specs += [row]
            if kind == "k_rope_mean":
                assert tm == MOBA_BLOCK
                out_shapes += [jax.ShapeDtypeStruct((m // tm, 1, width), F32)]
                out_specs += [pl.BlockSpec((None, 1, width), lambda i: (i, 0, 0))]
    tab_spec = pl.BlockSpec((tm, LANES), lambda i: (i % n_tab, 0))
    vec_d = pl.BlockSpec((1, d), lambda i: (0, 0))
    vec_l = pl.BlockSpec((1, LANES), lambda i: (0, 0))
    rep = LANES // HEAD_DIM
    return pl.pallas_call(
        functools.partial(_proj_kernel, plan=plan),
        grid=(m // tm,),
        in_specs=[pl.BlockSpec((tm, d), lambda i: (i, 0)), vec_d,
                  pl.BlockSpec(w_bf16.shape, lambda i: (0, 0), pipeline_mode=pl.Buffered(1)),
                  vec_l, vec_l, tab_spec, tab_spec, tab_spec],
        out_specs=out_specs,
        out_shape=out_shapes,
        compiler_params=_params(("parallel",)),
        name=name,
    )(x, g_norm.reshape(1, d), w_bf16, jnp.tile(gq, rep).reshape(1, LANES), jnp.tile(gk, rep).reshape(1, LANES),
      *tables)


def _finish_kernel(x_ref, w_ref, *refs):
    mix_refs, y_ref = refs[:-1], refs[-1]
    y = x_ref[...]
    row0 = 0
    for mix_ref in mix_refs:
        rows = mix_ref.shape[1]
        y = y + _dot(mix_ref[...], w_ref[row0:row0 + rows, :])
        row0 += rows
    y_ref[...] = y


def _finish(x, mix_parts, w_bf16, tm, name):
    m, d = x.shape
    kdim = w_bf16.shape[0]
    assert sum(p.shape[1] for p in mix_parts) == kdim
    return pl.pallas_call(
        _finish_kernel,
        grid=(m // tm,),
        in_specs=[pl.BlockSpec((tm, d), lambda i: (i, 0)), pl.BlockSpec((kdim, d), lambda i: (0, 0))]
        + [pl.BlockSpec((tm, p.shape[1]), lambda i: (i, 0)) for p in mix_parts],
        out_specs=pl.BlockSpec((tm, d), lambda i: (i, 0)),
        out_shape=jax.ShapeDtypeStruct((m, d), F32),
        compiler_params=_params(("parallel",)),
        name=name,
    )(x, w_bf16, *mix_parts)


def _lambda_value(lp, lam_init):
    a = jnp.sum(lp[0:1] * lp[1:2], axis=-1, keepdims=True)
    b = jnp.sum(lp[2:3] * lp[3:4], axis=-1, keepdims=True)
    return jnp.exp(a) - jnp.exp(b) + lam_init


def _softmax_tile(qq, k, v, carry, mask):
    m, l, acc = carry
    s = _dot_nt(qq, k)
    if mask is not None:
        s = jnp.where(mask, s, NEG)
    m_new = jnp.maximum(m, jnp.max(s, axis=-1, keepdims=True))
    alpha = jnp.exp2(m - m_new)
    p = jnp.exp2(s - m_new)
    l = alpha * l + jnp.sum(p, axis=-1, keepdims=True)
    acc = alpha * acc + _dot(p.astype(BF16), v)
    return m_new, l, acc


def _dot_tn(a, b):
    return lax.dot_general(a, b, (((0,), (0,)), ((), ())), preferred_element_type=F32)


def _softmax_tile_t(qq_t, k_ref, v_ref, start, n_keys, carry, mask):
    s, m_s, bias = _scores_t(qq_t, k_ref, start, n_keys, mask)
    return _absorb_t(v_ref, start, n_keys, s, m_s, bias, carry)


def _scores_t(qq_t, k_ref, start, n_keys, mask):
    s = _dot(k_ref[pl.ds(start, n_keys), :], qq_t)
    slabs = [s[c * ATTN_SLAB:(c + 1) * ATTN_SLAB] for c in range(n_keys // ATTN_SLAB)]
    if mask is None:
        return s, jnp.max(s, axis=0, keepdims=True), None
    if mask[0].dtype == jnp.bool_:
        s = jnp.concatenate([jnp.where(mk, x, NEG) for x, mk in zip(slabs, mask)], axis=0)
        return s, jnp.max(s, axis=0, keepdims=True), None
    m_s = None
    for x, row in zip(slabs, mask):
        m_c = jnp.max(x, axis=0, keepdims=True) + row
        m_s = m_c if m_s is None else jnp.maximum(m_s, m_c)
    return s, m_s, jnp.concatenate(mask, axis=0)


def _absorb_t(v_ref, start, n_keys, s, m_s, bias, carry):
    m, l, acc = carry
    m_new = jnp.maximum(m, m_s)
    alpha = jnp.exp2(m - m_new)
    if bias is None:
        p = jnp.exp2(s - m_new)
    else:
        shift = bias - m_new
        p = jnp.concatenate([jnp.exp2(s[c * ATTN_SLAB:(c + 1) * ATTN_SLAB] + shift[c:c + 1])
                             for c in range(n_keys // ATTN_SLAB)], axis=0)
    l = alpha * l + jnp.sum(p, axis=0, keepdims=True)
    acc = alpha * acc + _dot_tn(v_ref[pl.ds(start, n_keys), :], p.astype(BF16))
    return m_new, l, acc


def _pipelined_pairs(n_pairs, last_tile, n_keys, qq_t, k_ref, v_ref, mask_of, s_refs, carry):
    s_a, s_b = s_refs

    def scores(t, s_ref):
        s, m_s, bias = _scores_t(qq_t, k_ref, pl.multiple_of(t * n_keys, n_keys), n_keys, mask_of(t))
        s_ref[...] = s
        return m_s, bias

    def absorb(t, s_ref, stats, carry):
        return _absorb_t(v_ref, pl.multiple_of(t * n_keys, n_keys), n_keys, s_ref[...], *stats, carry)

    def trip(u, state):
        m_a, carry = state
        m_b = scores(2 * u + 1, s_b)
        carry = absorb(2 * u, s_a, m_a, carry)
        m_a = scores(jnp.minimum(2 * u + 2, last_tile), s_a)
        carry = absorb(2 * u + 1, s_b, m_b, carry)
        return m_a, carry

    return lax.fori_loop(0, n_pairs, trip, (scores(n_pairs * 0, s_a), carry))[1]


def _causal_masks(k_start, q_start, tq, n_keys, r, strict):
    q_pos = q_start + _iota((ATTN_SLAB, r), 1) % tq
    masks = []
    for c in range(n_keys // ATTN_SLAB):
        k_pos = k_start + c * ATTN_SLAB + _iota((ATTN_SLAB, r), 0)
        masks.append(k_pos < q_pos if strict else k_pos <= q_pos)
    return masks


def _stack_pair_t(q_t):
    row = _iota(q_t.shape, 0)
    zero = jnp.zeros_like(q_t)
    return jnp.concatenate([jnp.where(row < HEAD_DIM, q_t, zero), jnp.where(row >= HEAD_DIM, q_t, zero)], axis=1)


def _diff_attn_kernel(lam_ref, subg_ref, q_ref, k_ref, v_ref, sg_ref, o_ref, s_a, s_b, *, tq, tk, lam_init):
    i = pl.program_id(2)
    r = 2 * tq
    qq_t = _stack_pair_t(jnp.transpose(q_ref[...].astype(F32))).astype(BF16)
    n_full = (i * tq) // tk
    half = tk // 2

    carry = (jnp.full((1, r), NEG, F32), jnp.zeros((1, r), F32), jnp.zeros((LANES, r), F32))
    start = pl.multiple_of(n_full * tk, tk)
    carry = _softmax_tile_t(qq_t, k_ref, v_ref, start, tk, carry,
                            _causal_masks(start, i * tq, tq, tk, r, strict=False))
    _, l, acc = _pipelined_pairs(n_full, k_ref.shape[0] // half - 1, half, qq_t, k_ref, v_ref,
                                 lambda t: None, (s_a, s_b), carry)

    a = acc / l
    lam = _lambda_value(lam_ref[...], lam_init)
    o = jnp.transpose(a[:, :tq] - lam * a[:, tq:])
    ms = jnp.mean(o * o, axis=-1, keepdims=True)
    o = o * lax.rsqrt(ms + RMS_EPS) * subg_ref[...] * (1.0 - lam_init)
    o_ref[...] = (o * sg_ref[...]).astype(BF16)


def _softplus2(z):
    return jnp.maximum(z, 0.0) + jnp.log2(1.0 + jnp.exp2(-jnp.abs(z)))


def _later_matrix(tk, keys_axis=1):
    a, b = _iota((tk, tk), 0), _iota((tk, tk), 1)
    return (a > b if keys_axis == 1 else b > a).astype(BF16)


def _sb_kernel(q_ref, k_ref, v_ref, sg_ref, o_ref, *, tq, tk):
    i = pl.program_id(2)
    r = 2 * tq
    qq_t = _stack_pair_t(jnp.transpose(q_ref[...].astype(F32))).astype(BF16)
    tri = _later_matrix(ATTN_SLAB, keys_axis=0)
    n_full = (i * tq) // tk
    n_slabs = tk // ATTN_SLAB

    def tile(st, c, acc, valid):
        parts = []
        for u in range(n_slabs):
            z = _dot(k_ref[pl.ds(st + u * ATTN_SLAB, ATTN_SLAB), :], qq_t)
            sp = _softplus2(z)
            log_keep = -sp if valid is None else jnp.where(valid[u], -sp, 0.0)
            later = _dot(tri, log_keep.astype(BF16))
            parts.append((z - sp + later, later[0:1, :] + log_keep[0:1, :]))
        for u in reversed(range(n_slabs)):
            e, whole = parts[u]
            w = jnp.exp2(e + c)
            if valid is not None:
                w = jnp.where(valid[u], w, 0.0)
            acc = acc + _dot_tn(v_ref[pl.ds(st + u * ATTN_SLAB, ATTN_SLAB), :], w.astype(BF16))
            c = c + whole
        return c, acc

    start = pl.multiple_of(n_full * tk, tk)
    c, acc = tile(start, jnp.zeros((1, r), F32), jnp.zeros((LANES, r), F32),
                  _causal_masks(start, i * tq, tq, tk, r, strict=True))

    def live(state):
        t, c_max, _, _ = state
        return jnp.logical_and(t < n_full, c_max > -SB_DEAD_LOG2)

    def full_tile(state):
        t, _, c, acc = state
        c, acc = tile(pl.multiple_of((n_full - 1 - t) * tk, tk), c, acc, None)
        return t + 1, jnp.max(c), c, acc

    _, _, _, acc = lax.while_loop(live, full_tile, (jnp.int32(0), jnp.max(c), c, acc))
    row = _iota((LANES, tq), 0)
    o = jnp.transpose(jnp.where(row < HEAD_DIM, acc[:, :tq], acc[:, tq:]))
    o_ref[...] = (o * sg_ref[...]).astype(BF16)


def _top_blocks(g, n_valid, axis=-1, picks=None):
    axis = axis % g.ndim
    blk = _iota(g.shape, axis)
    nb = g.shape[axis]
    g = jnp.where(blk < n_valid, g, -jnp.inf)
    sel = jnp.zeros(g.shape, jnp.bool_)
    for _ in range(min(MOBA_TOPK, nb)):
        mx = jnp.max(g, axis=axis, keepdims=True)
        idx = jnp.min(jnp.where(g == mx, blk, nb), axis=axis, keepdims=True)
        pick = blk == idx
        sel = jnp.logical_or(sel, pick)
        g = jnp.where(pick, -jnp.inf, g)
        if picks is not None:
            picks.append(idx)
    return jnp.logical_and(sel, blk < n_valid)


def _moba_kernel(q_ref, k_ref, v_ref, km_ref, sg_ref, o_ref, sel_s, s_a, s_b, *, tq, group):
    i = pl.program_id(2)
    r = 2 * tq
    blk = MOBA_BLOCK
    tile_blocks = tq // blk
    first = i * tile_blocks
    qf_t = _stack_pair_t(jnp.transpose(q_ref[...]))
    row = _iota((LANES, tq), 0)
    q_hi, q_lo = _split_bf16(qf_t)
    qq_t = (qf_t * Q_SOFTMAX_SCALE).astype(BF16)

    km_hi, km_lo = _split_bf16(km_ref[...])
    g = _dot(km_hi, q_hi) + _dot(km_hi, q_lo) + _dot(km_lo, q_hi)
    col = _iota((1, r), 1) % tq
    own = col // blk
    sel = _top_blocks(g, first + own, axis=0)
    block_id = _iota(sel.shape, 0)
    sel_s[...] = jnp.where(jnp.logical_and(sel, block_id < first), 0.0, NEG)

    start = pl.multiple_of(first * blk, tq)
    causal = _iota((blk, r), 0) <= col % blk
    masks = []
    for a in range(tile_blocks):
        picked = jnp.max(jnp.where(jnp.logical_and(sel, block_id == first + a), 1.0, 0.0), axis=0,
                         keepdims=True) > 0.0
        masks.append(jnp.logical_or(jnp.logical_and(own == a, causal), jnp.logical_and(own > a, picked)))
    carry = (jnp.full((1, r), NEG, F32), jnp.zeros((1, r), F32), jnp.zeros((LANES, r), F32))
    carry = _softmax_tile_t(qq_t, k_ref, v_ref, start, tq, carry, masks)

    per_tile = group // 2

    def chosen(t):
        return [sel_s[pl.ds(t * per_tile + c, 1), :] for c in range(per_tile)]

    _, l, acc = _pipelined_pairs((first + group - 1) // group, sel_s.shape[0] // per_tile - 1, per_tile * blk,
                                 qq_t, k_ref, v_ref, chosen, (s_a, s_b), carry)
    o_t = acc / l
    o = jnp.transpose(jnp.where(row < HEAD_DIM, o_t[:, :tq], o_t[:, tq:]))
    o_ref[...] = (o * sg_ref[...]).astype(BF16)


def _attn_specs(tq, s, group0=0):
    q_spec = pl.BlockSpec((None, tq, LANES), lambda b, h, i: (b, i, h + group0))
    kv_spec = pl.BlockSpec((None, s, LANES), lambda b, h, i: (b, 0, h))
    return q_spec, kv_spec


def _diff_attn_prompt(q16, k16, v16, sgate, lam_p, subln_g, lam_init, tq, tk):
    b, s, w = q16.shape
    q_spec, kv_spec = _attn_specs(tq, s)
    return pl.pallas_call(
        functools.partial(_diff_attn_kernel, tq=tq, tk=tk, lam_init=lam_init),
        grid=(b, w // LANES, s // tq),
        in_specs=[pl.BlockSpec(lam_p.shape, lambda b, h, i: (0, 0)),
                  pl.BlockSpec((1, LANES), lambda b, h, i: (0, 0)),
                  q_spec, kv_spec, kv_spec, q_spec],
        out_specs=q_spec,
        out_shape=jax.ShapeDtypeStruct((b, s, w), BF16),
        scratch_shapes=[pltpu.VMEM((tk // 2, 2 * tq), F32)] * 2,
        compiler_params=_params(("parallel", "parallel", "arbitrary")),
        name="diff_attn_prompt",
    )(lam_p, subln_g.reshape(1, LANES), q16, k16, v16, sgate)


def _sb_prompt(q16, k16, v16, sgate, gate_group0, tq, tk):
    b, s, w = q16.shape
    q_spec, kv_spec = _attn_specs(tq, s)
    sg_spec = pl.BlockSpec((None, tq, LANES), lambda b, h, i: (b, i, h + gate_group0))
    return pl.pallas_call(
        functools.partial(_sb_kernel, tq=tq, tk=tk),
        grid=(b, w // LANES, s // tq),
        in_specs=[q_spec, kv_spec, kv_spec, sg_spec],
        out_specs=q_spec,
        out_shape=jax.ShapeDtypeStruct((b, s, w), BF16),
        compiler_params=_params(("parallel", "parallel", "arbitrary")),
        name="sb_prompt",
    )(q16, k16, v16, sgate)


def _moba_prompt(q, k16, v16, kmean, sgate):
    b, s, w = q.shape
    assert MOBA_BLOCK == ATTN_SLAB
    nb = s // MOBA_BLOCK
    tq = MOBA_BLOCK * math.gcd(nb, MOBA_TILE_BLOCKS)
    q_spec, kv_spec = _attn_specs(tq, s)
    group = math.gcd(nb, MOBA_GROUP)
    return pl.pallas_call(
        functools.partial(_moba_kernel, tq=tq, group=group),
        grid=(b, w // LANES, s // tq),
        in_specs=[q_spec, kv_spec, kv_spec,
                  pl.BlockSpec((None, nb, LANES), lambda b, h, i: (b, 0, h)), q_spec],
        out_specs=q_spec,
        out_shape=jax.ShapeDtypeStruct((b, s, w), BF16),
        scratch_shapes=[pltpu.VMEM((nb, 2 * tq), F32)]
        + [pltpu.VMEM((group // 2 * MOBA_BLOCK, 2 * tq), F32)] * 2,
        compiler_params=_params(("parallel", "parallel", "arbitrary")),
        name="moba_prompt",
    )(q, k16, v16, kmean, sgate)


def _row_heads(x_row, n_rows, lanes_per_row):
    w = x_row.shape[1]
    keep = _iota((n_rows, w), 1) // lanes_per_row == _iota((n_rows, w), 0)
    return jnp.where(keep, jnp.broadcast_to(x_row, (n_rows, w)), 0.0)


def _dup_rows(x, reps):
    n, w = x.shape
    row = _iota((n * reps, w), 0) // reps
    out = jnp.zeros((n * reps, w), x.dtype)
    for h in range(n):
        out = jnp.where(row == h, x[h:h + 1, :], out)
    return out


def _stack_rows(rows):
    n, w = len(rows), rows[0].shape[1]
    row = _iota((n, w), 0)
    out = jnp.zeros((n, w), rows[0].dtype)
    for h, r in enumerate(rows):
        out = jnp.where(row == h, r, out)
    return out


def _head_dots(qcol_ref, kt_ref, n_heads):
    rows = []
    for h in range(n_heads):
        sl = slice(h * HEAD_DIM, (h + 1) * HEAD_DIM)
        rows.append(jnp.sum(qcol_ref[sl, :] * kt_ref[sl, :], axis=0, keepdims=True))
    return _stack_rows(rows)


def _add_weighted(acc_ref, w, vt_refs, n_heads):
    for h in range(n_heads):
        sl = slice(h * HEAD_DIM, (h + 1) * HEAD_DIM)
        a = acc_ref[sl, :]
        for u, vt_ref in enumerate(vt_refs):
            a = a + w[u * n_heads + h:u * n_heads + h + 1, :] * vt_ref[sl, :]
        acc_ref[sl, :] = a


def _sb_weights_pages(zs, tri, c):
    n_h = zs[0].shape[0]
    z = jnp.concatenate(zs, axis=0)
    sp = _softplus2(z)
    log_keep = -sp
    hi, lo = _split_bf16(log_keep)
    rows = z.shape[0]
    both = _dot(jnp.concatenate([hi, lo], axis=0), tri)
    later = both[:rows] + both[rows:]
    whole = later[:, 0:1] + log_keep[:, 0:1]
    offsets = []
    for u in range(len(zs)):
        offsets.append(c)
        c = c + whole[u * n_h:(u + 1) * n_h]
    return jnp.exp2(z - sp + later + jnp.concatenate(offsets, axis=0)), c


def _lane_sums_as_row(x):
    hi, lo = _split_bf16(x)
    ones = jnp.ones((8, LANES), BF16)
    return (_dot_nt(ones, hi) + _dot_nt(ones, lo))[0:1, :]


def _decode_even_kernel(pt_ref, lam_ref, subg_ref, qa_ref, kan_ref, van_ref, qb_ref, sg_ref, *refs,
                        pps, lam_init):
    del pt_ref
    page_refs = refs[:4 * pps]
    o_ref = refs[4 * pps]
    qa_s, m_s, l_s, acca_s, c_s, accb_s = refs[4 * pps + 1:]
    step = pl.program_id(1)
    n_ha = qa_ref.shape[0]
    n_a = 2 * n_ha
    n_b = accb_s.shape[0] // HEAD_DIM
    wa = n_ha * LANES

    @pl.when(step == 0)
    def _():
        rows = _dup_rows(qa_ref[...].astype(F32), 2)
        qa = jnp.where(_iota(rows.shape, 1) // HEAD_DIM == _iota(rows.shape, 0) % 2, rows, 0.0)
        qa_s[...] = qa
        m_s[...] = jnp.sum(qa * _dup_rows(kan_ref[...], 2), axis=-1, keepdims=True)
        l_s[...] = jnp.ones_like(l_s)
        acca_s[...] = _dup_rows(van_ref[...], 2)
        c_s[...] = jnp.zeros_like(c_s)
        accb_s[...] = jnp.zeros_like(accb_s)

    rows_a = pps * page_refs[0].shape[0]
    own_head = _iota((n_a, rows_a), 1) % n_ha == _iota((n_a, rows_a), 0) // 2
    ak = jnp.concatenate([page_refs[4 * u][...].astype(BF16) for u in range(pps)], axis=0)
    av = jnp.concatenate([page_refs[4 * u + 1][...].astype(BF16) for u in range(pps)], axis=0)
    m_s[...], l_s[...], acca_s[...] = _softmax_tile(qa_s[...].astype(BF16), ak, av,
                                                    (m_s[...], l_s[...], acca_s[...]), own_head)
    zs = [_head_dots(qb_ref, page_refs[4 * u + 2], n_b) for u in range(pps)]
    w, c_s[...] = _sb_weights_pages(zs, _later_matrix(LANES), c_s[...])
    _add_weighted(accb_s, w, [page_refs[4 * u + 3] for u in range(pps)], n_b)

    @pl.when(step == pl.num_programs(1) - 1)
    def _():
        lam = _lambda_value(lam_ref[...], lam_init)
        a = acca_s[...] / l_s[...]
        for h in range(n_ha):
            sl = slice(h * LANES, (h + 1) * LANES)
            x = a[2 * h:2 * h + 1, :] - lam * a[2 * h + 1:2 * h + 2, :]
            ms = jnp.mean(x * x, axis=-1, keepdims=True)
            x = x * lax.rsqrt(ms + RMS_EPS) * subg_ref[...] * (1.0 - lam_init)
            o_ref[:, sl] = (x * sg_ref[:, sl]).astype(BF16)
        o_ref[:, wa:] = (_lane_sums_as_row(accb_s[...]) * sg_ref[:, wa:]).astype(BF16)


def _page_specs(cache, li, n_pages, pps, reverse):
    r = cache.shape[2]

    def make(u):
        def index(b, s, pt):
            j = s * pps + u
            j = n_pages - 1 - j if reverse else j
            return (li, pt[b, j], 0, 0)
        return pl.BlockSpec((None, None, r, LANES), index)

    return [make(u) for u in range(pps)]


def _col_bcast(x):
    return jnp.broadcast_to(x.astype(F32)[:, :, None], x.shape + (LANES,))


def _decode_even(pt, lam_p, subln_g, lam_init, qa16, ka, va, qb16, sgate, caches, li, pps):
    db, n_pages = pt.shape
    wa, wb = qa16.shape[-1], qb16.shape[-1]
    n_ha = wa // LANES
    whole = lambda shp: pl.BlockSpec((None,) + shp, lambda b, s, pt: (b,) + (0,) * len(shp))
    specs_by_cache = [_page_specs(c, li, n_pages, pps, True) for c in caches]
    page_specs = [specs_by_cache[t][u] for u in range(pps) for t in range(4)]
    page_args = [caches[t] for u in range(pps) for t in range(4)]
    grid_spec = pltpu.PrefetchScalarGridSpec(
        num_scalar_prefetch=1,
        grid=(db, n_pages // pps),
        in_specs=[pl.BlockSpec(lam_p.shape, lambda b, s, pt: (0, 0)),
                  pl.BlockSpec((1, LANES), lambda b, s, pt: (0, 0)),
                  whole((n_ha, LANES)), whole((n_ha, LANES)), whole((n_ha, LANES)), whole((wb, LANES)),
                  whole((1, wa + wb))] + page_specs,
        out_specs=whole((1, wa + wb)),
        scratch_shapes=[pltpu.VMEM((2 * n_ha, LANES), F32),
                        pltpu.VMEM((2 * n_ha, 1), F32), pltpu.VMEM((2 * n_ha, 1), F32),
                        pltpu.VMEM((2 * n_ha, LANES), F32),
                        pltpu.VMEM((wb // HEAD_DIM, 1), F32), pltpu.VMEM((wb, LANES), F32)],
    )
    heads = lambda x: x.reshape(db, n_ha, LANES)
    return pl.pallas_call(
        functools.partial(_decode_even_kernel, pps=pps, lam_init=lam_init),
        grid_spec=grid_spec,
        out_shape=jax.ShapeDtypeStruct((db, 1, wa + wb), BF16),
        compiler_params=_params(("parallel", "arbitrary")),
        name="decode_even",
    )(pt, lam_p, subln_g.reshape(1, LANES), heads(qa16), heads(ka), heads(va), _col_bcast(qb16),
      sgate.reshape(db, 1, wa + wb), *page_args)


def _moba_scores_kernel(pt_ref, qcol_ref, qrow_ref, kn_ref, *refs, pps, n_blocks):
    del pt_ref
    page_refs = refs[:pps]
    p_ref, pnew_ref, ids_ref = refs[pps:pps + 3]
    s_s = refs[pps + 3]
    step = pl.program_id(1)
    n_h = qcol_ref.shape[0] // HEAD_DIM
    ppb = MOBA_BLOCK // LANES

    for u in range(pps):
        s_s[step * pps + u] = _head_dots(qcol_ref, page_refs[u], n_h)

    @pl.when(step == pl.num_programs(1) - 1)
    def _():
        lane = _iota((n_h, LANES), 1)
        g = jnp.zeros((n_h, LANES), F32)
        for n in range(n_blocks):
            tot = sum(s_s[n * ppb + t] for t in range(ppb))
            g = jnp.where(lane == n, jnp.sum(tot, axis=-1, keepdims=True) * (1.0 / MOBA_BLOCK), g)
        picks = []
        sel = _top_blocks(g, n_blocks, picks=picks)
        ids = jnp.zeros((n_h, LANES), jnp.int32)
        for r, idx in enumerate(picks):
            ids = jnp.where(lane == r, idx, ids)
        ids_ref[...] = ids
        chosen = [sel[:, n:n + 1] for n in range(n_blocks)]
        s_new = jnp.sum(_row_heads(qrow_ref[...], n_h, HEAD_DIM) * kn_ref[...], axis=-1, keepdims=True) * QK_SCALE
        m_lanes = jnp.full((n_h, LANES), NEG, F32)
        for j in range(n_blocks * ppb):
            m_lanes = jnp.maximum(m_lanes, jnp.where(chosen[j // ppb], s_s[j] * QK_SCALE, NEG))
        m = jnp.maximum(jnp.max(m_lanes, axis=-1, keepdims=True), s_new)
        l_lanes = jnp.zeros((n_h, LANES), F32)
        for j in range(n_blocks * ppb):
            e = jnp.where(chosen[j // ppb], jnp.exp(s_s[j] * QK_SCALE - m), 0.0)
            s_s[j] = e
            l_lanes = l_lanes + e
        e_new = jnp.exp(s_new - m)
        inv = 1.0 / (jnp.sum(l_lanes, axis=-1, keepdims=True) + e_new)
        for j in range(n_blocks * ppb):
            p_ref[j] = s_s[j] * inv
        pnew_ref[...] = jnp.broadcast_to(e_new * inv, pnew_ref.shape)


def _moba_values_kernel(pt_ref, ids_ref, p_ref, pnew_ref, vn_ref, sg_ref, *refs, n_sel, ppb, heads_per_step):
    del pt_ref
    slices, o_ref = refs[:-1], refs[-1]
    b, pair = pl.program_id(0), pl.program_id(1)
    accs = []
    for c in range(heads_per_step):
        h = pair * heads_per_step + c
        acc = jnp.zeros((HEAD_DIM, LANES), F32)
        for r in range(n_sel):
            blk = ids_ref[b, h * n_sel + r]
            for u in range(ppb):
                w = p_ref[blk * ppb + u, pl.ds(h, 1), :]
                acc = acc + w * slices[(c * n_sel + r) * ppb + u][...]
        accs.append(acc)
    o = _lane_sums_as_row(jnp.concatenate(accs, axis=0)) + pnew_ref[...] * vn_ref[...]
    o_ref[...] = (o * sg_ref[...]).astype(BF16)


def _decode_odd(pt, q, k_new, v_new, sgate, cache_k, cache_v, li, pps):
    db, n_pages = pt.shape
    w = q.shape[-1]
    n_h = w // HEAD_DIM
    page = LANES
    past = n_pages * page
    assert past % MOBA_BLOCK == 0 and past >= MOBA_BLOCK and MOBA_BLOCK % page == 0
    n_blocks = past // MOBA_BLOCK
    row = lambda wd: pl.BlockSpec((None, 1, wd), lambda b, s, pt: (b, 0, 0))
    r3 = lambda x: x.reshape(db, 1, x.shape[-1])
    assert n_blocks >= MOBA_TOPK
    per_head = pl.BlockSpec((None, n_h, LANES), lambda b, s, pt: (b, 0, 0))
    probs, p_new, ids = pl.pallas_call(
        functools.partial(_moba_scores_kernel, pps=pps, n_blocks=n_blocks),
        grid_spec=pltpu.PrefetchScalarGridSpec(
            num_scalar_prefetch=1,
            grid=(db, n_pages // pps),
            in_specs=[pl.BlockSpec((None, w, LANES), lambda b, s, pt: (b, 0, 0)), row(w), row(w)]
            + _page_specs(cache_k, li, n_pages, pps, False),
            out_specs=[pl.BlockSpec((None, n_pages, n_h, page), lambda b, s, pt: (b, 0, 0, 0)),
                       per_head, per_head],
            scratch_shapes=[pltpu.VMEM((n_pages, n_h, page), F32)],
        ),
        out_shape=[jax.ShapeDtypeStruct((db, n_pages, n_h, page), F32),
                   jax.ShapeDtypeStruct((db, n_h, LANES), F32),
                   jax.ShapeDtypeStruct((db, n_h, LANES), jnp.int32)],
        compiler_params=_params(("parallel", "arbitrary")),
        name="decode_moba_scores",
    )(pt, _col_bcast(q), r3(q), r3(k_new), *([cache_k] * pps))
    p_new_row = jnp.repeat(p_new[:, :, 0], HEAD_DIM, axis=-1).reshape(db, 1, w)

    ppb = MOBA_BLOCK // page
    heads_per_step = math.gcd(n_h, VALUE_HEADS_PER_STEP)
    step_lanes = heads_per_step * HEAD_DIM
    assert step_lanes % LANES == 0
    block_ids = ids[:, :, :MOBA_TOPK].reshape(db, n_h * MOBA_TOPK)

    def slice_spec(c, r, u):
        def index(b, pair, pt, bid):
            h = pair * heads_per_step + c
            return (li, pt[b, bid[b, h * MOBA_TOPK + r] * ppb + u], h, 0)
        return pl.BlockSpec((None, None, HEAD_DIM, LANES), index)

    slice_specs = [slice_spec(c, r, u) for c in range(heads_per_step) for r in range(MOBA_TOPK)
                   for u in range(ppb)]
    lanes = lambda: pl.BlockSpec((None, 1, step_lanes), lambda b, pair, pt, bid: (b, 0, pair))
    return pl.pallas_call(
        functools.partial(_moba_values_kernel, n_sel=MOBA_TOPK, ppb=ppb, heads_per_step=heads_per_step),
        grid_spec=pltpu.PrefetchScalarGridSpec(
            num_scalar_prefetch=2,
            grid=(db, n_h // heads_per_step),
            in_specs=[pl.BlockSpec((None, n_pages, n_h, page), lambda b, pair, pt, bid: (b, 0, 0, 0)),
                      lanes(), lanes(), lanes()] + slice_specs,
            out_specs=lanes(),
        ),
        out_shape=jax.ShapeDtypeStruct((db, 1, w), BF16),
        compiler_params=_params(("parallel", "arbitrary")),
        name="decode_moba_values",
    )(pt, block_ids, probs, p_new_row, r3(v_new), r3(sgate), *([cache_v] * len(slice_specs)))


PROMPT_TM = 512
ATTN_TQ = 512
ATTN_TK = 512
SB_TILE = 256
MOBA_GROUP = 4
MOBA_TILE_BLOCKS = 2
PAGES_PER_STEP = 16
VALUE_HEADS_PER_STEP = 4


def _rows_view(c):
    l, n, p, h, hd = c.shape
    assert p == LANES and hd == LANES
    return c.reshape(l, n, p * h, hd)


def _cols_view(c):
    l, n, p, h, hd = c.shape
    assert p == LANES and hd == HEAD_DIM
    return jnp.transpose(c, (0, 1, 3, 4, 2)).reshape(l, n, h * hd, p)


def _heads_from_cols(x, n_heads):
    b, w, s = x.shape
    return jnp.transpose(x.reshape(b, n_heads, w // n_heads, s), (0, 3, 1, 2))


def _even_layer(x_p, x_s, caches, li, pt, g_norm, w_in, qk_g, lam_p, subln_g, w_out, layer):
    lam_init = 0.8 - 0.6 * math.exp(-0.3 * layer)
    b, s, d = x_p.shape
    db, t, _ = x_s.shape
    wa = w_out.shape[0] // 2
    wb = wa
    def plan(wide, narrow):
        return (("q_rope_bf16", 0, wa, None), ("k_rope", wa, wa, wide), ("kv_plain", 2 * wa, wa, wide),
                ("q_plain_bf16", 3 * wa, wb, None), ("kv_plain", 3 * wa + wb, wb, narrow),
                ("kv_plain", 3 * wa + 2 * wb, wb, narrow), ("gate", 3 * wa + 3 * wb, wa + wb, None))

    w_in16 = w_in.astype(BF16)
    w_out16 = w_out.astype(BF16)
    past = pt.shape[1] * LANES
    n_a = wa // (2 * HEAD_DIM)
    n_b = wb // HEAD_DIM

    tabs = _rope_tables(jnp.arange(s))
    qa16, ka, ka16, va, va16, qb16, kb, kb16, vb, vb16, sgate = _project(
        x_p.reshape(b * s, d), g_norm, w_in16, qk_g[0], qk_g[1], tabs, plan("heads", "cols"), PROMPT_TM, s,
        "proj_even_prompt")
    r3 = lambda a: a.reshape(b, s, a.shape[-1])
    sg3 = r3(sgate)
    mix_a = _diff_attn_prompt(r3(qa16), r3(ka16), r3(va16), sg3, lam_p, subln_g, lam_init, ATTN_TQ, ATTN_TK)
    mix_b = _sb_prompt(r3(qb16), r3(kb16), r3(vb16), sg3, wa // LANES, SB_TILE, SB_TILE)
    y_p = _finish(x_p.reshape(b * s, d), [mix_a.reshape(b * s, wa), mix_b.reshape(b * s, wb)], w_out16, 512,
                  "finish_even_prompt").reshape(b, s, d)

    tabs_s = _rope_tables(jnp.tile(past + jnp.arange(t), db))
    qa16_s, ka_s, _, va_s, _, qb16_s, kb_s, _, vb_s, _, sgate_s = _project(
        x_s.reshape(db * t, d), g_norm, w_in16, qk_g[0], qk_g[1], tabs_s, plan("rows", "rows"), db * t, db * t,
        "proj_even_sample")
    assert t == 1
    mix_s = _decode_even(pt, lam_p, subln_g, lam_init, qa16_s, ka_s, va_s, qb16_s, sgate_s,
                         caches, li, PAGES_PER_STEP)
    y_s = _finish(x_s.reshape(db * t, d), [mix_s.reshape(db * t, wa + wb)], w_out16, db * t,
                  "finish_even_sample").reshape(db, t, d)

    rows_p = (ka.reshape(b, s, n_a, 2 * HEAD_DIM), va.reshape(b, s, n_a, 2 * HEAD_DIM),
              _heads_from_cols(kb, n_b), _heads_from_cols(vb, n_b))
    rows_s = (ka_s.reshape(db, t, n_a, 2 * HEAD_DIM), va_s.reshape(db, t, n_a, 2 * HEAD_DIM),
              kb_s.reshape(db, t, n_b, HEAD_DIM), vb_s.reshape(db, t, n_b, HEAD_DIM))
    return y_p, y_s, rows_p, rows_s


def _odd_layer(x_p, x_s, cache_k, cache_v, li, pt, g_norm, w_in, qk_g, w_out):
    b, s, d = x_p.shape
    db, t, _ = x_s.shape
    wc = w_out.shape[0]
    w_in16 = w_in.astype(BF16)
    w_out16 = w_out.astype(BF16)
    past = pt.shape[1] * LANES
    assert s % MOBA_BLOCK == 0

    def plan(k_kind, layout):
        return (("q_rope_f32", 0, wc, None), (k_kind, wc, wc, layout), ("kv_plain", 2 * wc, wc, layout),
                ("gate", 3 * wc, wc, None))

    tabs = _rope_tables(jnp.arange(s))
    q, k, k16, kmean, v, v16, sgate = _project(
        x_p.reshape(b * s, d), g_norm, w_in16, qk_g[0], qk_g[1], tabs, plan("k_rope_mean", "cols"), MOBA_BLOCK, s,
        "proj_odd_prompt")
    r3 = lambda a: a.reshape(b, s, a.shape[-1])
    mix = _moba_prompt(r3(q), r3(k16), r3(v16), kmean.reshape(b, s // MOBA_BLOCK, wc), r3(sgate))
    y_p = _finish(x_p.reshape(b * s, d), [mix.reshape(b * s, wc)], w_out16, 512,
                  "finish_odd_prompt").reshape(b, s, d)

    assert t == 1
    tabs_s = _rope_tables(jnp.tile(past + jnp.arange(t), db))
    q_s, k_s, _, v_s, _, sgate_s = _project(
        x_s.reshape(db * t, d), g_norm, w_in16, qk_g[0], qk_g[1], tabs_s, plan("k_rope", "rows"), db * t, db * t,
        "proj_odd_sample")
    mix_s = _decode_odd(pt, q_s, k_s, v_s, sgate_s, cache_k, cache_v, li, PAGES_PER_STEP)
    y_s = _finish(x_s.reshape(db * t, d), [mix_s.reshape(db * t, wc)], w_out16, db * t,
                  "finish_odd_sample").reshape(db, t, d)

    n_c = wc // HEAD_DIM
    return (y_p, y_s, (_heads_from_cols(k, n_c), _heads_from_cols(v, n_c)),
            (k_s.reshape(db, t, n_c, HEAD_DIM), v_s.reshape(db, t, n_c, HEAD_DIM)))


def kernel(x_prompt, x_sample, cache_a_k, cache_a_v, cache_b_k, cache_b_v, cache_c_k, cache_c_v, page_table,
           norm_even, w_in_even, qk_norm_a, lambda_a, subln_a, w_out_even, norm_odd, w_in_odd, qk_norm_c,
           w_out_odd):
    depth = norm_even.shape[0] + norm_odd.shape[0]
    even_caches = [_rows_view(cache_a_k), _rows_view(cache_a_v), _cols_view(cache_b_k), _cols_view(cache_b_v)]
    odd_k, odd_v = _cols_view(cache_c_k), _cols_view(cache_c_v)
    y_p, y_s = x_prompt, x_sample
    ev_p, ev_s, od_p, od_s = [], [], [], []
    for layer in range(depth):
        i = layer // 2
        if layer % 2 == 0:
            y_p, y_s, rp, rs = _even_layer(y_p, y_s, even_caches, i, page_table, norm_even[i], w_in_even[i],
                                           qk_norm_a[i], lambda_a[i], subln_a[i], w_out_even[i], layer)
            ev_p.append(rp)
            ev_s.append(rs)
        else:
            y_p, y_s, rp, rs = _odd_layer(y_p, y_s, odd_k, odd_v, i, page_table, norm_odd[i], w_in_odd[i],
                                          qk_norm_c[i], w_out_odd[i])
            od_p.append(rp)
            od_s.append(rs)

    def stack(rows, j):
        return jnp.stack([r[j] for r in rows])

    return (y_p, y_s,
            stack(ev_p, 0), stack(ev_p, 1), stack(ev_p, 2), stack(ev_p, 3), stack(od_p, 0), stack(od_p, 1),
            stack(ev_s, 0), stack(ev_s, 1), stack(ev_s, 2), stack(ev_s, 3), stack(od_s, 0), stack(od_s, 1))
```

```python
import functools
import math

import jax
import jax.numpy as jnp
from jax import lax
from jax.experimental import pallas as pl
from jax.experimental.pallas import tpu as pltpu

HEAD_DIM = 64
ROT_DIM = HEAD_DIM // 4
ROPE_THETA = 500000.0
RMS_EPS = 1e-6
MOBA_BLOCK = 256
MOBA_TOPK = 3
LANES = 128
NEG = -1e30
SB_DEAD_LOG2 = 150.0
QK_SCALE = HEAD_DIM ** -0.5
Q_SOFTMAX_SCALE = QK_SCALE * math.log2(math.e)
ATTN_SLAB = 256
VMEM_LIMIT = 52 * 1024 * 1024

F32 = jnp.float32
BF16 = jnp.bfloat16


def _params(sem, vmem=VMEM_LIMIT):
    return pltpu.CompilerParams(dimension_semantics=sem, vmem_limit_bytes=vmem)


def _dot(a, b):
    return jnp.dot(a, b, preferred_element_type=F32)


def _dot_nt(a, b):
    return lax.dot_general(a, b, (((1,), (1,)), ((), ())), preferred_element_type=F32)


def _split_bf16(x):
    hi = x.astype(BF16)
    lo = (x - hi.astype(F32)).astype(BF16)
    return hi, lo


def _iota(shape, dim):
    return lax.broadcasted_iota(jnp.int32, shape, dim)


def _head_seg_matrix():
    return (_iota((LANES, LANES), 0) // HEAD_DIM == _iota((LANES, LANES), 1) // HEAD_DIM).astype(BF16)


def _norm_rope(x, g, c, s1, s2, seg):
    hi, lo = _split_bf16(x * x)
    ms = (_dot(hi, seg) + _dot(lo, seg)) * (1.0 / HEAD_DIM)
    xn = x * lax.rsqrt(ms + RMS_EPS) * g
    half = ROT_DIM // 2
    return xn * c + pltpu.roll(xn, LANES - half, 1) * s1 + pltpu.roll(xn, half, 1) * s2


def _proj_kernel(x_ref, gn_ref, w_ref, gq_ref, gk_ref, c_ref, s1_ref, s2_ref, *out_refs, plan):
    x = x_ref[...]
    ms = jnp.mean(x * x, axis=-1, keepdims=True)
    xn = (x * lax.rsqrt(ms + RMS_EPS) * gn_ref[...]).astype(BF16)
    seg = _head_seg_matrix()
    c, s1, s2 = c_ref[...], s1_ref[...], s2_ref[...]
    outs = list(out_refs)
    tm = x.shape[0]

    def store_rows(o_ref, layout, t, n_t, y):
        if layout == "rows":
            o_ref[:, t * LANES:(t + 1) * LANES] = y
        elif layout == "cols":
            o_ref[t * LANES:(t + 1) * LANES, :] = jnp.transpose(y)
        elif layout == "heads":
            o_ref[pl.ds(t, tm, stride=n_t), :] = y
        else:
            raise ValueError(layout)

    for kind, col0, width, layout in plan:
        h = _dot(xn, w_ref[:, col0:col0 + width])
        n_t = width // LANES
        if kind in ("q_rope_bf16", "q_rope_f32", "k_rope", "k_rope_mean"):
            g = gq_ref[...] if kind.startswith("q") else gk_ref[...]
            if kind == "q_rope_bf16":
                o_ref = outs.pop(0)
            elif kind == "q_rope_f32":
                o_ref = outs.pop(0)
            else:
                o_ref, o16_ref = outs.pop(0), outs.pop(0)
                mean_ref = outs.pop(0) if kind == "k_rope_mean" else None
            for t in range(n_t):
                sl = slice(t * LANES, (t + 1) * LANES)
                y = _norm_rope(h[:, sl], g, c, s1, s2, seg)
                if kind == "q_rope_bf16":
                    o_ref[:, sl] = (y * Q_SOFTMAX_SCALE).astype(BF16)
                elif kind == "q_rope_f32":
                    o_ref[:, sl] = y
                else:
                    store_rows(o_ref, layout, t, n_t, y)
                    o16_ref[:, sl] = y.astype(BF16)
                    if mean_ref is not None:
                        for j in range(tm // MOBA_BLOCK):
                            mean_ref[j, :, sl] = jnp.mean(y[j * MOBA_BLOCK:(j + 1) * MOBA_BLOCK], axis=0,
                                                          keepdims=True)
        elif kind == "q_plain_bf16":
            outs.pop(0)[...] = (h * Q_SOFTMAX_SCALE).astype(BF16)
        elif kind == "kv_plain":
            o_ref = outs.pop(0)
            for t in range(n_t):
                store_rows(o_ref, layout, t, n_t, h[:, t * LANES:(t + 1) * LANES])
            outs.pop(0)[...] = h.astype(BF16)
        elif kind == "gate":
            outs.pop(0)[...] = h / (1.0 + jnp.exp(-h))
        else:
            raise ValueError(kind)
    assert not outs


def _rope_tables(pos):
    half = ROT_DIM // 2
    inv = ROPE_THETA ** (-jnp.arange(0, ROT_DIM, 2, dtype=F32) / ROT_DIM)
    ang = pos.astype(F32)[:, None] * inv[None, :]
    cos, sin = jnp.cos(ang), jnp.sin(ang)
    t = pos.shape[0]
    z_half = jnp.zeros((t, half), F32)
    z_rest = jnp.zeros((t, HEAD_DIM - ROT_DIM), F32)
    c = jnp.concatenate([cos, cos, jnp.ones((t, HEAD_DIM - ROT_DIM), F32)], axis=-1)
    s1 = jnp.concatenate([-sin, z_half, z_rest], axis=-1)
    s2 = jnp.concatenate([z_half, sin, z_rest], axis=-1)
    rep = LANES // HEAD_DIM
    return tuple(jnp.tile(a, (1, rep)) for a in (c, s1, s2))


def _project(x, g_norm, w_bf16, gq, gk, tables, plan, tm, seq, name):
    m, d = x.shape
    n_tab = tables[0].shape[0] // tm
    tiles_per_seq = seq // tm
    out_shapes, out_specs = [], []
    for kind, _, width, layout in plan:
        row = pl.BlockSpec((tm, width), lambda i: (i, 0))
        if kind in ("q_rope_bf16", "q_plain_bf16"):
            out_shapes += [jax.ShapeDtypeStruct((m, width), BF16)]
            out_specs += [row]
        elif kind in ("q_rope_f32", "gate"):
            out_shapes += [jax.ShapeDtypeStruct((m, width), F32)]
            out_specs += [row]
        elif kind in ("k_rope", "kv_plain", "k_rope_mean"):
            if layout == "rows":
                out_shapes += [jax.ShapeDtypeStruct((m, width), F32)]
                out_specs += [row]
            elif layout == "cols":
                out_shapes += [jax.ShapeDtypeStruct((m // seq, width, seq), F32)]
                out_specs += [pl.BlockSpec((None, width, tm),
                                           lambda i: (i // tiles_per_seq, 0, i % tiles_per_seq))]
            else:
                n_t = width // LANES
                out_shapes += [jax.ShapeDtypeStruct((m * n_t, LANES), F32)]
                out_specs += [pl.BlockSpec((tm * n_t, LANES), lambda i: (i, 0))]
            out_shapes += [jax.ShapeDtypeStruct((m, width), BF16)]
            out_specs += [row]
            if kind == "k_rope_mean":
                assert tm % MOBA_BLOCK == 0
                out_shapes += [jax.ShapeDtypeStruct((m // MOBA_BLOCK, 1, width), F32)]
                out_specs += [pl.BlockSpec((tm // MOBA_BLOCK, 1, width), lambda i: (i, 0, 0))]
    tab_spec = pl.BlockSpec((tm, LANES), lambda i: (i % n_tab, 0))
    vec_d = pl.BlockSpec((1, d), lambda i: (0, 0))
    vec_l = pl.BlockSpec((1, LANES), lambda i: (0, 0))
    rep = LANES // HEAD_DIM
    return pl.pallas_call(
        functools.partial(_proj_kernel, plan=plan),
        grid=(m // tm,),
        in_specs=[pl.BlockSpec((tm, d), lambda i: (i, 0)), vec_d,
                  pl.BlockSpec(w_bf16.shape, lambda i: (0, 0), pipeline_mode=pl.Buffered(1)),
                  vec_l, vec_l, tab_spec, tab_spec, tab_spec],
        out_specs=out_specs,
        out_shape=out_shapes,
        compiler_params=_params(("parallel",)),
        name=name,
    )(x, g_norm.reshape(1, d), w_bf16, jnp.tile(gq, rep).reshape(1, LANES), jnp.tile(gk, rep).reshape(1, LANES),
      *tables)


def _finish_kernel(x_ref, w_ref, *refs):
    mix_refs, y_ref = refs[:-1], refs[-1]
    y = x_ref[...]
    row0 = 0
    for mix_ref in mix_refs:
        rows = mix_ref.shape[1]
        y = y + _dot(mix_ref[...], w_ref[row0:row0 + rows, :])
        row0 += rows
    y_ref[...] = y


def _finish(x, mix_parts, w_bf16, tm, name):
    m, d = x.shape
    kdim = w_bf16.shape[0]
    assert sum(p.shape[1] for p in mix_parts) == kdim
    return pl.pallas_call(
        _finish_kernel,
        grid=(m // tm,),
        in_specs=[pl.BlockSpec((tm, d), lambda i: (i, 0)), pl.BlockSpec((kdim, d), lambda i: (0, 0))]
        + [pl.BlockSpec((tm, p.shape[1]), lambda i: (i, 0)) for p in mix_parts],
        out_specs=pl.BlockSpec((tm, d), lambda i: (i, 0)),
        out_shape=jax.ShapeDtypeStruct((m, d), F32),
        compiler_params=_params(("parallel",)),
        name=name,
    )(x, w_bf16, *mix_parts)


def _lambda_value(lp, lam_init):
    a = jnp.sum(lp[0:1] * lp[1:2], axis=-1, keepdims=True)
    b = jnp.sum(lp[2:3] * lp[3:4], axis=-1, keepdims=True)
    return jnp.exp(a) - jnp.exp(b) + lam_init


def _softmax_tile(qq, k, v, carry, mask):
    m, l, acc = carry
    s = _dot_nt(qq, k)
    if mask is not None:
        s = jnp.where(mask, s, NEG)
    m_new = jnp.maximum(m, jnp.max(s, axis=-1, keepdims=True))
    alpha = jnp.exp2(m - m_new)
    p = jnp.exp2(s - m_new)
    l = alpha * l + jnp.sum(p, axis=-1, keepdims=True)
    acc = alpha * acc + _dot(p.astype(BF16), v)
    return m_new, l, acc


def _dot_tn(a, b):
    return lax.dot_general(a, b, (((0,), (0,)), ((), ())), preferred_element_type=F32)


def _softmax_tile_t(qq_t, k_ref, v_ref, start, n_keys, carry, mask):
    s, m_s, bias = _scores_t(qq_t, k_ref, start, n_keys, mask)
    return _absorb_t(v_ref, start, n_keys, s, m_s, bias, carry)


def _scores_t(qq_t, k_ref, start, n_keys, mask):
    s = _dot(k_ref[pl.ds(start, n_keys), :], qq_t)
    slabs = [s[c * ATTN_SLAB:(c + 1) * ATTN_SLAB] for c in range(n_keys // ATTN_SLAB)]
    if mask is None:
        return s, jnp.max(s, axis=0, keepdims=True), None
    if mask[0].dtype == jnp.bool_:
        s = jnp.concatenate([jnp.where(mk, x, NEG) for x, mk in zip(slabs, mask)], axis=0)
        return s, jnp.max(s, axis=0, keepdims=True), None
    m_s = None
    for x, row in zip(slabs, mask):
        m_c = jnp.max(x, axis=0, keepdims=True) + row
        m_s = m_c if m_s is None else jnp.maximum(m_s, m_c)
    return s, m_s, jnp.concatenate(mask, axis=0)


def _absorb_t(v_ref, start, n_keys, s, m_s, bias, carry):
    m, l, acc = carry
    m_new = jnp.maximum(m, m_s)
    alpha = jnp.exp2(m - m_new)
    if bias is None:
        p = jnp.exp2(s - m_new)
    else:
        shift = bias - m_new
        p = jnp.concatenate([jnp.exp2(s[c * ATTN_SLAB:(c + 1) * ATTN_SLAB] + shift[c:c + 1])
                             for c in range(n_keys // ATTN_SLAB)], axis=0)
    l = alpha * l + jnp.sum(p, axis=0, keepdims=True)
    v, p16 = v_ref[pl.ds(start, n_keys), :], p.astype(BF16)
    pv = _dot_tn(v, p16) if acc.shape[0] == LANES else _pair_values_t(v, p16)
    return m_new, l, alpha * acc + pv


def _pair_values_t(v, w16):
    half = w16.shape[1] // 2
    return jnp.concatenate([_dot_tn(v[:, :HEAD_DIM], w16[:, :half]), _dot_tn(v[:, HEAD_DIM:], w16[:, half:])],
                           axis=1)


def _merge_pair_t(x, tq):
    return jnp.transpose(jnp.concatenate([x[:, :tq], x[:, tq:]], axis=0))


def _pipelined_pairs(n_pairs, last_tile, n_keys, qq_t, k_ref, v_ref, mask_of, s_refs, carry):
    s_a, s_b = s_refs

    def scores(t, s_ref):
        s, m_s, bias = _scores_t(qq_t, k_ref, pl.multiple_of(t * n_keys, n_keys), n_keys, mask_of(t))
        s_ref[...] = s
        return m_s, bias

    def absorb(t, s_ref, stats, carry):
        return _absorb_t(v_ref, pl.multiple_of(t * n_keys, n_keys), n_keys, s_ref[...], *stats, carry)

    def trip(u, state):
        m_a, carry = state
        m_b = scores(2 * u + 1, s_b)
        carry = absorb(2 * u, s_a, m_a, carry)
        m_a = scores(jnp.minimum(2 * u + 2, last_tile), s_a)
        carry = absorb(2 * u + 1, s_b, m_b, carry)
        return m_a, carry

    return lax.fori_loop(0, n_pairs, trip, (scores(n_pairs * 0, s_a), carry))[1]


def _causal_masks(k_start, q_start, tq, n_keys, r, strict):
    q_pos = q_start + _iota((ATTN_SLAB, r), 1) % tq
    masks = []
    for c in range(n_keys // ATTN_SLAB):
        k_pos = k_start + c * ATTN_SLAB + _iota((ATTN_SLAB, r), 0)
        masks.append(k_pos < q_pos if strict else k_pos <= q_pos)
    return masks


def _stack_pair_t(q_t):
    row = _iota(q_t.shape, 0)
    zero = jnp.zeros_like(q_t)
    return jnp.concatenate([jnp.where(row < HEAD_DIM, q_t, zero), jnp.where(row >= HEAD_DIM, q_t, zero)], axis=1)


def _diff_attn_kernel(lam_ref, subg_ref, q_ref, k_ref, v_ref, sg_ref, o_ref, s_a, s_b, *, tq, tk, lam_init):
    i = pl.program_id(2)
    r = 2 * tq
    qq_t = _stack_pair_t(jnp.transpose(q_ref[...].astype(F32))).astype(BF16)
    n_full = (i * tq) // tk
    half = tk // 2

    carry = (jnp.full((1, r), NEG, F32), jnp.zeros((1, r), F32), jnp.zeros((LANES, r), F32))
    start = pl.multiple_of(n_full * tk, tk)
    carry = _softmax_tile_t(qq_t, k_ref, v_ref, start, tk, carry,
                            _causal_masks(start, i * tq, tq, tk, r, strict=False))
    _, l, acc = _pipelined_pairs(n_full, k_ref.shape[0] // half - 1, half, qq_t, k_ref, v_ref,
                                 lambda t: None, (s_a, s_b), carry)

    a = acc / l
    lam = _lambda_value(lam_ref[...], lam_init)
    o = jnp.transpose(a[:, :tq] - lam * a[:, tq:])
    ms = jnp.mean(o * o, axis=-1, keepdims=True)
    o = o * lax.rsqrt(ms + RMS_EPS) * subg_ref[...] * (1.0 - lam_init)
    o_ref[...] = (o * sg_ref[...]).astype(BF16)


def _softplus2(z):
    return jnp.maximum(z, 0.0) + jnp.log2(1.0 + jnp.exp2(-jnp.abs(z)))


def _later_matrix(tk, keys_axis=1):
    a, b = _iota((tk, tk), 0), _iota((tk, tk), 1)
    return (a > b if keys_axis == 1 else b > a).astype(BF16)


def _sb_kernel(q_ref, k_ref, v_ref, sg_ref, o_ref, *, tq, tk):
    i = pl.program_id(2)
    r = 2 * tq
    qq_t = _stack_pair_t(jnp.transpose(q_ref[...].astype(F32))).astype(BF16)
    tri = _later_matrix(ATTN_SLAB, keys_axis=0)
    n_full = (i * tq) // tk
    n_slabs = tk // ATTN_SLAB

    def tile(st, c, acc, valid):
        parts = []
        for u in range(n_slabs):
            z = _dot(k_ref[pl.ds(st + u * ATTN_SLAB, ATTN_SLAB), :], qq_t)
            sp = _softplus2(z)
            log_keep = -sp if valid is None else jnp.where(valid[u], -sp, 0.0)
            later = _dot(tri, log_keep.astype(BF16))
            parts.append((z - sp + later, later[0:1, :] + log_keep[0:1, :]))
        for u in reversed(range(n_slabs)):
            e, whole = parts[u]
            w = jnp.exp2(e + c)
            if valid is not None:
                w = jnp.where(valid[u], w, 0.0)
            acc = acc + _pair_values_t(v_ref[pl.ds(st + u * ATTN_SLAB, ATTN_SLAB), :], w.astype(BF16))
            c = c + whole
        return c, acc

    start = pl.multiple_of(n_full * tk, tk)
    c, acc = tile(start, jnp.zeros((1, r), F32), jnp.zeros((HEAD_DIM, r), F32),
                  _causal_masks(start, i * tq, tq, tk, r, strict=True))

    def live(state):
        t, c_max, _, _ = state
        return jnp.logical_and(t < n_full, c_max > -SB_DEAD_LOG2)

    def full_tile(state):
        t, _, c, acc = state
        c, acc = tile(pl.multiple_of((n_full - 1 - t) * tk, tk), c, acc, None)
        return t + 1, jnp.max(c), c, acc

    _, _, _, acc = lax.while_loop(live, full_tile, (jnp.int32(0), jnp.max(c), c, acc))
    o_ref[...] = (_merge_pair_t(acc, tq) * sg_ref[...]).astype(BF16)


def _top_blocks(g, n_valid, axis=-1, picks=None):
    axis = axis % g.ndim
    blk = _iota(g.shape, axis)
    nb = g.shape[axis]
    g = jnp.where(blk < n_valid, g, -jnp.inf)
    sel = jnp.zeros(g.shape, jnp.bool_)
    for _ in range(min(MOBA_TOPK, nb)):
        mx = jnp.max(g, axis=axis, keepdims=True)
        idx = jnp.min(jnp.where(g == mx, blk, nb), axis=axis, keepdims=True)
        pick = blk == idx
        sel = jnp.logical_or(sel, pick)
        g = jnp.where(pick, -jnp.inf, g)
        if picks is not None:
            picks.append(idx)
    return jnp.logical_and(sel, blk < n_valid)


def _moba_kernel(q_ref, k_ref, v_ref, km_ref, sg_ref, o_ref, sel_s, s_a, s_b, *, tq, group):
    i = pl.program_id(2)
    r = 2 * tq
    blk = MOBA_BLOCK
    tile_blocks = tq // blk
    first = i * tile_blocks
    qf_t = _stack_pair_t(jnp.transpose(q_ref[...]))
    q_hi, q_lo = _split_bf16(qf_t)
    qq_t = (qf_t * Q_SOFTMAX_SCALE).astype(BF16)

    km_hi, km_lo = _split_bf16(km_ref[...])
    g = _dot(km_hi, q_hi) + _dot(km_hi, q_lo) + _dot(km_lo, q_hi)
    col = _iota((1, r), 1) % tq
    own = col // blk
    sel = _top_blocks(g, first + own, axis=0)
    block_id = _iota(sel.shape, 0)
    sel_s[...] = jnp.where(jnp.logical_and(sel, block_id < first), 0.0, NEG)

    start = pl.multiple_of(first * blk, tq)
    causal = _iota((blk, r), 0) <= col % blk
    masks = []
    for a in range(tile_blocks):
        picked = jnp.max(jnp.where(jnp.logical_and(sel, block_id == first + a), 1.0, 0.0), axis=0,
                         keepdims=True) > 0.0
        masks.append(jnp.logical_or(jnp.logical_and(own == a, causal), jnp.logical_and(own > a, picked)))
    carry = (jnp.full((1, r), NEG, F32), jnp.zeros((1, r), F32), jnp.zeros((HEAD_DIM, r), F32))
    carry = _softmax_tile_t(qq_t, k_ref, v_ref, start, tq, carry, masks)

    per_tile = group // 2

    def chosen(t):
        return [sel_s[pl.ds(t * per_tile + c, 1), :] for c in range(per_tile)]

    _, l, acc = _pipelined_pairs((first + group - 1) // group, sel_s.shape[0] // per_tile - 1, per_tile * blk,
                                 qq_t, k_ref, v_ref, chosen, (s_a, s_b), carry)
    o_ref[...] = (_merge_pair_t(acc / l, tq) * sg_ref[...]).astype(BF16)


def _attn_specs(tq, s, group0=0):
    q_spec = pl.BlockSpec((None, tq, LANES), lambda b, h, i: (b, i, h + group0))
    kv_spec = pl.BlockSpec((None, s, LANES), lambda b, h, i: (b, 0, h))
    return q_spec, kv_spec


def _diff_attn_prompt(q16, k16, v16, sgate, lam_p, subln_g, lam_init, tq, tk):
    b, s, w = q16.shape
    q_spec, kv_spec = _attn_specs(tq, s)
    return pl.pallas_call(
        functools.partial(_diff_attn_kernel, tq=tq, tk=tk, lam_init=lam_init),
        grid=(b, w // LANES, s // tq),
        in_specs=[pl.BlockSpec(lam_p.shape, lambda b, h, i: (0, 0)),
                  pl.BlockSpec((1, LANES), lambda b, h, i: (0, 0)),
                  q_spec, kv_spec, kv_spec, q_spec],
        out_specs=q_spec,
        out_shape=jax.ShapeDtypeStruct((b, s, w), BF16),
        scratch_shapes=[pltpu.VMEM((tk // 2, 2 * tq), F32)] * 2,
        compiler_params=_params(("parallel", "parallel", "arbitrary")),
        name="diff_attn_prompt",
    )(lam_p, subln_g.reshape(1, LANES), q16, k16, v16, sgate)


def _sb_prompt(q16, k16, v16, sgate, gate_group0, tq, tk):
    b, s, w = q16.shape
    q_spec, kv_spec = _attn_specs(tq, s)
    sg_spec = pl.BlockSpec((None, tq, LANES), lambda b, h, i: (b, i, h + gate_group0))
    return pl.pallas_call(
        functools.partial(_sb_kernel, tq=tq, tk=tk),
        grid=(b, w // LANES, s // tq),
        in_specs=[q_spec, kv_spec, kv_spec, sg_spec],
        out_specs=q_spec,
        out_shape=jax.ShapeDtypeStruct((b, s, w), BF16),
        compiler_params=_params(("parallel", "parallel", "arbitrary")),
        name="sb_prompt",
    )(q16, k16, v16, sgate)


def _moba_prompt(q, k16, v16, kmean, sgate):
    b, s, w = q.shape
    assert MOBA_BLOCK == ATTN_SLAB
    nb = s // MOBA_BLOCK
    tq = MOBA_BLOCK * math.gcd(nb, MOBA_TILE_BLOCKS)
    q_spec, kv_spec = _attn_specs(tq, s)
    group = math.gcd(nb, MOBA_GROUP)
    return pl.pallas_call(
        functools.partial(_moba_kernel, tq=tq, group=group),
        grid=(b, w // LANES, s // tq),
        in_specs=[q_spec, kv_spec, kv_spec,
                  pl.BlockSpec((None, nb, LANES), lambda b, h, i: (b, 0, h)), q_spec],
        out_specs=q_spec,
        out_shape=jax.ShapeDtypeStruct((b, s, w), BF16),
        scratch_shapes=[pltpu.VMEM((nb, 2 * tq), F32)]
        + [pltpu.VMEM((group // 2 * MOBA_BLOCK, 2 * tq), F32)] * 2,
        compiler_params=_params(("parallel", "parallel", "arbitrary")),
        name="moba_prompt",
    )(q, k16, v16, kmean, sgate)


def _row_heads(x_row, n_rows, lanes_per_row):
    w = x_row.shape[1]
    keep = _iota((n_rows, w), 1) // lanes_per_row == _iota((n_rows, w), 0)
    return jnp.where(keep, jnp.broadcast_to(x_row, (n_rows, w)), 0.0)


def _dup_rows(x, reps):
    n, w = x.shape
    row = _iota((n * reps, w), 0) // reps
    out = jnp.zeros((n * reps, w), x.dtype)
    for h in range(n):
        out = jnp.where(row == h, x[h:h + 1, :], out)
    return out


def _stack_rows(rows):
    n, w = len(rows), rows[0].shape[1]
    row = _iota((n, w), 0)
    out = jnp.zeros((n, w), rows[0].dtype)
    for h, r in enumerate(rows):
        out = jnp.where(row == h, r, out)
    return out


def _head_dots(qcol_ref, kt_ref, n_heads):
    rows = []
    for h in range(n_heads):
        sl = slice(h * HEAD_DIM, (h + 1) * HEAD_DIM)
        rows.append(jnp.sum(qcol_ref[sl, :] * kt_ref[sl, :], axis=0, keepdims=True))
    return _stack_rows(rows)


def _add_weighted(acc_ref, w, vt_refs, n_heads):
    for h in range(n_heads):
        sl = slice(h * HEAD_DIM, (h + 1) * HEAD_DIM)
        a = acc_ref[sl, :]
        for u, vt_ref in enumerate(vt_refs):
            a = a + w[u * n_heads + h:u * n_heads + h + 1, :] * vt_ref[sl, :]
        acc_ref[sl, :] = a


def _sb_weights_pages(zs, tri, c):
    n_h = zs[0].shape[0]
    z = jnp.concatenate(zs, axis=0)
    sp = _softplus2(z)
    log_keep = -sp
    hi, lo = _split_bf16(log_keep)
    rows = z.shape[0]
    both = _dot(jnp.concatenate([hi, lo], axis=0), tri)
    later = both[:rows] + both[rows:]
    whole = later[:, 0:1] + log_keep[:, 0:1]
    offsets = []
    for u in range(len(zs)):
        offsets.append(c)
        c = c + whole[u * n_h:(u + 1) * n_h]
    return jnp.exp2(z - sp + later + jnp.concatenate(offsets, axis=0)), c


def _lane_sums_as_row(x):
    hi, lo = _split_bf16(x)
    ones = jnp.ones((8, LANES), BF16)
    return (_dot_nt(ones, hi) + _dot_nt(ones, lo))[0:1, :]


def _decode_even_kernel(pt_ref, lam_ref, subg_ref, qa_ref, kan_ref, van_ref, qb_ref, sg_ref, *refs,
                        pps, lam_init):
    del pt_ref
    page_refs = refs[:4 * pps]
    o_ref = refs[4 * pps]
    qa_s, m_s, l_s, acca_s, c_s, accb_s = refs[4 * pps + 1:]
    step = pl.program_id(1)
    n_ha = qa_ref.shape[0]
    n_a = 2 * n_ha
    n_b = accb_s.shape[0] // HEAD_DIM
    wa = n_ha * LANES

    @pl.when(step == 0)
    def _():
        rows = _dup_rows(qa_ref[...].astype(F32), 2)
        qa = jnp.where(_iota(rows.shape, 1) // HEAD_DIM == _iota(rows.shape, 0) % 2, rows, 0.0)
        qa_s[...] = qa
        m_s[...] = jnp.sum(qa * _dup_rows(kan_ref[...], 2), axis=-1, keepdims=True)
        l_s[...] = jnp.ones_like(l_s)
        acca_s[...] = _dup_rows(van_ref[...], 2)
        c_s[...] = jnp.zeros_like(c_s)
        accb_s[...] = jnp.zeros_like(accb_s)

    rows_a = pps * page_refs[0].shape[0]
    own_head = _iota((n_a, rows_a), 1) % n_ha == _iota((n_a, rows_a), 0) // 2
    ak = jnp.concatenate([page_refs[4 * u][...].astype(BF16) for u in range(pps)], axis=0)
    av = jnp.concatenate([page_refs[4 * u + 1][...].astype(BF16) for u in range(pps)], axis=0)
    m_s[...], l_s[...], acca_s[...] = _softmax_tile(qa_s[...].astype(BF16), ak, av,
                                                    (m_s[...], l_s[...], acca_s[...]), own_head)
    zs = [_head_dots(qb_ref, page_refs[4 * u + 2], n_b) for u in range(pps)]
    w, c_s[...] = _sb_weights_pages(zs, _later_matrix(LANES), c_s[...])
    _add_weighted(accb_s, w, [page_refs[4 * u + 3] for u in range(pps)], n_b)

    @pl.when(step == pl.num_programs(1) - 1)
    def _():
        lam = _lambda_value(lam_ref[...], lam_init)
        a = acca_s[...] / l_s[...]
        for h in range(n_ha):
            sl = slice(h * LANES, (h + 1) * LANES)
            x = a[2 * h:2 * h + 1, :] - lam * a[2 * h + 1:2 * h + 2, :]
            ms = jnp.mean(x * x, axis=-1, keepdims=True)
            x = x * lax.rsqrt(ms + RMS_EPS) * subg_ref[...] * (1.0 - lam_init)
            o_ref[:, sl] = (x * sg_ref[:, sl]).astype(BF16)
        o_ref[:, wa:] = (_lane_sums_as_row(accb_s[...]) * sg_ref[:, wa:]).astype(BF16)


def _page_specs(cache, li, n_pages, pps, reverse):
    r = cache.shape[2]

    def make(u):
        def index(b, s, pt):
            j = s * pps + u
            j = n_pages - 1 - j if reverse else j
            return (li, pt[b, j], 0, 0)
        return pl.BlockSpec((None, None, r, LANES), index)

    return [make(u) for u in range(pps)]


def _col_bcast(x):
    return jnp.broadcast_to(x.astype(F32)[:, :, None], x.shape + (LANES,))


def _decode_even(pt, lam_p, subln_g, lam_init, qa16, ka, va, qb16, sgate, caches, li, pps):
    db, n_pages = pt.shape
    wa, wb = qa16.shape[-1], qb16.shape[-1]
    n_ha = wa // LANES
    whole = lambda shp: pl.BlockSpec((None,) + shp, lambda b, s, pt: (b,) + (0,) * len(shp))
    specs_by_cache = [_page_specs(c, li, n_pages, pps, True) for c in caches]
    page_specs = [specs_by_cache[t][u] for u in range(pps) for t in range(4)]
    page_args = [caches[t] for u in range(pps) for t in range(4)]
    grid_spec = pltpu.PrefetchScalarGridSpec(
        num_scalar_prefetch=1,
        grid=(db, n_pages // pps),
        in_specs=[pl.BlockSpec(lam_p.shape, lambda b, s, pt: (0, 0)),
                  pl.BlockSpec((1, LANES), lambda b, s, pt: (0, 0)),
                  whole((n_ha, LANES)), whole((n_ha, LANES)), whole((n_ha, LANES)), whole((wb, LANES)),
                  whole((1, wa + wb))] + page_specs,
        out_specs=whole((1, wa + wb)),
        scratch_shapes=[pltpu.VMEM((2 * n_ha, LANES), F32),
                        pltpu.VMEM((2 * n_ha, 1), F32), pltpu.VMEM((2 * n_ha, 1), F32),
                        pltpu.VMEM((2 * n_ha, LANES), F32),
                        pltpu.VMEM((wb // HEAD_DIM, 1), F32), pltpu.VMEM((wb, LANES), F32)],
    )
    heads = lambda x: x.reshape(db, n_ha, LANES)
    return pl.pallas_call(
        functools.partial(_decode_even_kernel, pps=pps, lam_init=lam_init),
        grid_spec=grid_spec,
        out_shape=jax.ShapeDtypeStruct((db, 1, wa + wb), BF16),
        compiler_params=_params(("parallel", "arbitrary")),
        name="decode_even",
    )(pt, lam_p, subln_g.reshape(1, LANES), heads(qa16), heads(ka), heads(va), _col_bcast(qb16),
      sgate.reshape(db, 1, wa + wb), *page_args)


def _moba_scores_kernel(pt_ref, qcol_ref, qrow_ref, kn_ref, *refs, pps, n_blocks):
    del pt_ref
    page_refs = refs[:pps]
    p_ref, pnew_ref, ids_ref = refs[pps:pps + 3]
    s_s = refs[pps + 3]
    step = pl.program_id(1)
    n_h = qcol_ref.shape[0] // HEAD_DIM
    ppb = MOBA_BLOCK // LANES

    for u in range(pps):
        s_s[step * pps + u] = _head_dots(qcol_ref, page_refs[u], n_h)

    @pl.when(step == pl.num_programs(1) - 1)
    def _():
        lane = _iota((n_h, LANES), 1)
        g = jnp.zeros((n_h, LANES), F32)
        for n in range(n_blocks):
            tot = sum(s_s[n * ppb + t] for t in range(ppb))
            g = jnp.where(lane == n, jnp.sum(tot, axis=-1, keepdims=True) * (1.0 / MOBA_BLOCK), g)
        picks = []
        sel = _top_blocks(g, n_blocks, picks=picks)
        ids = jnp.zeros((n_h, LANES), jnp.int32)
        for r, idx in enumerate(picks):
            ids = jnp.where(lane == r, idx, ids)
        ids_ref[...] = ids
        chosen = [sel[:, n:n + 1] for n in range(n_blocks)]
        s_new = jnp.sum(_row_heads(qrow_ref[...], n_h, HEAD_DIM) * kn_ref[...], axis=-1, keepdims=True) * QK_SCALE
        m_lanes = jnp.full((n_h, LANES), NEG, F32)
        for j in range(n_blocks * ppb):
            m_lanes = jnp.maximum(m_lanes, jnp.where(chosen[j // ppb], s_s[j] * QK_SCALE, NEG))
        m = jnp.maximum(jnp.max(m_lanes, axis=-1, keepdims=True), s_new)
        l_lanes = jnp.zeros((n_h, LANES), F32)
        for j in range(n_blocks * ppb):
            e = jnp.where(chosen[j // ppb], jnp.exp(s_s[j] * QK_SCALE - m), 0.0)
            s_s[j] = e
            l_lanes = l_lanes + e
        e_new = jnp.exp(s_new - m)
        inv = 1.0 / (jnp.sum(l_lanes, axis=-1, keepdims=True) + e_new)
        for j in range(n_blocks * ppb):
            p_ref[j] = s_s[j] * inv
        pnew_ref[...] = jnp.broadcast_to(e_new * inv, pnew_ref.shape)


def _moba_values_kernel(pt_ref, ids_ref, p_ref, pnew_ref, vn_ref, sg_ref, *refs, n_sel, ppb, heads_per_step):
    del pt_ref
    slices, o_ref = refs[:-1], refs[-1]
    b, pair = pl.program_id(0), pl.program_id(1)
    accs = []
    for c in range(heads_per_step):
        h = pair * heads_per_step + c
        acc = jnp.zeros((HEAD_DIM, LANES), F32)
        for r in range(n_sel):
            blk = ids_ref[b, h * n_sel + r]
            for u in range(ppb):
                w = p_ref[blk * ppb + u, pl.ds(h, 1), :]
                acc = acc + w * slices[(c * n_sel + r) * ppb + u][...]
        accs.append(acc)
    o = _lane_sums_as_row(jnp.concatenate(accs, axis=0)) + pnew_ref[...] * vn_ref[...]
    o_ref[...] = (o * sg_ref[...]).astype(BF16)


def _decode_odd(pt, q, k_new, v_new, sgate, cache_k, cache_v, li, pps):
    db, n_pages = pt.shape
    w = q.shape[-1]
    n_h = w // HEAD_DIM
    page = LANES
    past = n_pages * page
    assert past % MOBA_BLOCK == 0 and past >= MOBA_BLOCK and MOBA_BLOCK % page == 0
    n_blocks = past // MOBA_BLOCK
    row = lambda wd: pl.BlockSpec((None, 1, wd), lambda b, s, pt: (b, 0, 0))
    r3 = lambda x: x.reshape(db, 1, x.shape[-1])
    assert n_blocks >= MOBA_TOPK
    per_head = pl.BlockSpec((None, n_h, LANES), lambda b, s, pt: (b, 0, 0))
    probs, p_new, ids = pl.pallas_call(
        functools.partial(_moba_scores_kernel, pps=pps, n_blocks=n_blocks),
        grid_spec=pltpu.PrefetchScalarGridSpec(
            num_scalar_prefetch=1,
            grid=(db, n_pages // pps),
            in_specs=[pl.BlockSpec((None, w, LANES), lambda b, s, pt: (b, 0, 0)), row(w), row(w)]
            + _page_specs(cache_k, li, n_pages, pps, False),
            out_specs=[pl.BlockSpec((None, n_pages, n_h, page), lambda b, s, pt: (b, 0, 0, 0)),
                       per_head, per_head],
            scratch_shapes=[pltpu.VMEM((n_pages, n_h, page), F32)],
        ),
        out_shape=[jax.ShapeDtypeStruct((db, n_pages, n_h, page), F32),
                   jax.ShapeDtypeStruct((db, n_h, LANES), F32),
                   jax.ShapeDtypeStruct((db, n_h, LANES), jnp.int32)],
        compiler_params=_params(("parallel", "arbitrary")),
        name="decode_moba_scores",
    )(pt, _col_bcast(q), r3(q), r3(k_new), *([cache_k] * pps))
    p_new_row = jnp.repeat(p_new[:, :, 0], HEAD_DIM, axis=-1).reshape(db, 1, w)

    ppb = MOBA_BLOCK // page
    heads_per_step = math.gcd(n_h, VALUE_HEADS_PER_STEP)
    step_lanes = heads_per_step * HEAD_DIM
    assert step_lanes % LANES == 0
    block_ids = ids[:, :, :MOBA_TOPK].reshape(db, n_h * MOBA_TOPK)

    def slice_spec(c, r, u):
        def index(b, pair, pt, bid):
            h = pair * heads_per_step + c
            return (li, pt[b, bid[b, h * MOBA_TOPK + r] * ppb + u], h, 0)
        return pl.BlockSpec((None, None, HEAD_DIM, LANES), index)

    slice_specs = [slice_spec(c, r, u) for c in range(heads_per_step) for r in range(MOBA_TOPK)
                   for u in range(ppb)]
    lanes = lambda: pl.BlockSpec((None, 1, step_lanes), lambda b, pair, pt, bid: (b, 0, pair))
    return pl.pallas_call(
        functools.partial(_moba_values_kernel, n_sel=MOBA_TOPK, ppb=ppb, heads_per_step=heads_per_step),
        grid_spec=pltpu.PrefetchScalarGridSpec(
            num_scalar_prefetch=2,
            grid=(db, n_h // heads_per_step),
            in_specs=[pl.BlockSpec((None, n_pages, n_h, page), lambda b, pair, pt, bid: (b, 0, 0, 0)),
                      lanes(), lanes(), lanes()] + slice_specs,
            out_specs=lanes(),
        ),
        out_shape=jax.ShapeDtypeStruct((db, 1, w), BF16),
        compiler_params=_params(("parallel", "arbitrary")),
        name="decode_moba_values",
    )(pt, block_ids, probs, p_new_row, r3(v_new), r3(sgate), *([cache_v] * len(slice_specs)))


PROMPT_TM = 512
ATTN_TQ = 512
ATTN_TK = 512
SB_TILE = 256
MOBA_GROUP = 4
MOBA_TILE_BLOCKS = 2
PAGES_PER_STEP = 16
VALUE_HEADS_PER_STEP = 4


def _rows_view(c):
    l, n, p, h, hd = c.shape
    assert p == LANES and hd == LANES
    return c.reshape(l, n, p * h, hd)


def _cols_view(c):
    l, n, p, h, hd = c.shape
    assert p == LANES and hd == HEAD_DIM
    return jnp.transpose(c, (0, 1, 3, 4, 2)).reshape(l, n, h * hd, p)


def _heads_from_cols(x, n_heads):
    b, w, s = x.shape
    return jnp.transpose(x.reshape(b, n_heads, w // n_heads, s), (0, 3, 1, 2))


def _even_layer(x_p, x_s, caches, li, pt, g_norm, w_in, qk_g, lam_p, subln_g, w_out, layer):
    lam_init = 0.8 - 0.6 * math.exp(-0.3 * layer)
    b, s, d = x_p.shape
    db, t, _ = x_s.shape
    wa = w_out.shape[0] // 2
    wb = wa
    def plan(wide, narrow):
        return (("q_rope_bf16", 0, wa, None), ("k_rope", wa, wa, wide), ("kv_plain", 2 * wa, wa, wide),
                ("q_plain_bf16", 3 * wa, wb, None), ("kv_plain", 3 * wa + wb, wb, narrow),
                ("kv_plain", 3 * wa + 2 * wb, wb, narrow), ("gate", 3 * wa + 3 * wb, wa + wb, None))

    w_in16 = w_in.astype(BF16)
    w_out16 = w_out.astype(BF16)
    past = pt.shape[1] * LANES
    n_a = wa // (2 * HEAD_DIM)
    n_b = wb // HEAD_DIM

    tabs = _rope_tables(jnp.arange(s))
    qa16, ka, ka16, va, va16, qb16, kb, kb16, vb, vb16, sgate = _project(
        x_p.reshape(b * s, d), g_norm, w_in16, qk_g[0], qk_g[1], tabs, plan("heads", "cols"), PROMPT_TM, s,
        "proj_even_prompt")
    r3 = lambda a: a.reshape(b, s, a.shape[-1])
    sg3 = r3(sgate)
    mix_a = _diff_attn_prompt(r3(qa16), r3(ka16), r3(va16), sg3, lam_p, subln_g, lam_init, ATTN_TQ, ATTN_TK)
    mix_b = _sb_prompt(r3(qb16), r3(kb16), r3(vb16), sg3, wa // LANES, SB_TILE, SB_TILE)
    y_p = _finish(x_p.reshape(b * s, d), [mix_a.reshape(b * s, wa), mix_b.reshape(b * s, wb)], w_out16, 512,
                  "finish_even_prompt").reshape(b, s, d)

    tabs_s = _rope_tables(jnp.tile(past + jnp.arange(t), db))
    qa16_s, ka_s, _, va_s, _, qb16_s, kb_s, _, vb_s, _, sgate_s = _project(
        x_s.reshape(db * t, d), g_norm, w_in16, qk_g[0], qk_g[1], tabs_s, plan("rows", "rows"), db * t, db * t,
        "proj_even_sample")
    assert t == 1
    mix_s = _decode_even(pt, lam_p, subln_g, lam_init, qa16_s, ka_s, va_s, qb16_s, sgate_s,
                         caches, li, math.gcd(pt.shape[1], PAGES_PER_STEP))
    y_s = _finish(x_s.reshape(db * t, d), [mix_s.reshape(db * t, wa + wb)], w_out16, db * t,
                  "finish_even_sample").reshape(db, t, d)

    rows_p = (ka.reshape(b, s, n_a, 2 * HEAD_DIM), va.reshape(b, s, n_a, 2 * HEAD_DIM),
              _heads_from_cols(kb, n_b), _heads_from_cols(vb, n_b))
    rows_s = (ka_s.reshape(db, t, n_a, 2 * HEAD_DIM), va_s.reshape(db, t, n_a, 2 * HEAD_DIM),
              kb_s.reshape(db, t, n_b, HEAD_DIM), vb_s.reshape(db, t, n_b, HEAD_DIM))
    return y_p, y_s, rows_p, rows_s


def _odd_layer(x_p, x_s, cache_k, cache_v, li, pt, g_norm, w_in, qk_g, w_out):
    b, s, d = x_p.shape
    db, t, _ = x_s.shape
    wc = w_out.shape[0]
    w_in16 = w_in.astype(BF16)
    w_out16 = w_out.astype(BF16)
    past = pt.shape[1] * LANES
    assert s % MOBA_BLOCK == 0

    def plan(k_kind, layout):
        return (("q_rope_f32", 0, wc, None), (k_kind, wc, wc, layout), ("kv_plain", 2 * wc, wc, layout),
                ("gate", 3 * wc, wc, None))

    tabs = _rope_tables(jnp.arange(s))
    q, k, k16, kmean, v, v16, sgate = _project(
        x_p.reshape(b * s, d), g_norm, w_in16, qk_g[0], qk_g[1], tabs, plan("k_rope_mean", "cols"),
        MOBA_BLOCK * math.gcd(s // MOBA_BLOCK, PROMPT_TM // MOBA_BLOCK), s, "proj_odd_prompt")
    r3 = lambda a: a.reshape(b, s, a.shape[-1])
    mix = _moba_prompt(r3(q), r3(k16), r3(v16), kmean.reshape(b, s // MOBA_BLOCK, wc), r3(sgate))
    y_p = _finish(x_p.reshape(b * s, d), [mix.reshape(b * s, wc)], w_out16, 512,
                  "finish_odd_prompt").reshape(b, s, d)

    assert t == 1
    tabs_s = _rope_tables(jnp.tile(past + jnp.arange(t), db))
    q_s, k_s, _, v_s, _, sgate_s = _project(
        x_s.reshape(db * t, d), g_norm, w_in16, qk_g[0], qk_g[1], tabs_s, plan("k_rope", "rows"), db * t, db * t,
        "proj_odd_sample")
    mix_s = _decode_odd(pt, q_s, k_s, v_s, sgate_s, cache_k, cache_v, li, math.gcd(pt.shape[1], PAGES_PER_STEP))
    y_s = _finish(x_s.reshape(db * t, d), [mix_s.reshape(db * t, wc)], w_out16, db * t,
                  "finish_odd_sample").reshape(db, t, d)

    n_c = wc // HEAD_DIM
    return (y_p, y_s, (_heads_from_cols(k, n_c), _heads_from_cols(v, n_c)),
            (k_s.reshape(db, t, n_c, HEAD_DIM), v_s.reshape(db, t, n_c, HEAD_DIM)))


def kernel(x_prompt, x_sample, cache_a_k, cache_a_v, cache_b_k, cache_b_v, cache_c_k, cache_c_v, page_table,
           norm_even, w_in_even, qk_norm_a, lambda_a, subln_a, w_out_even, norm_odd, w_in_odd, qk_norm_c,
           w_out_odd):
    depth = norm_even.shape[0] + norm_odd.shape[0]
    even_caches = [_rows_view(cache_a_k), _rows_view(cache_a_v), _cols_view(cache_b_k), _cols_view(cache_b_v)]
    odd_k, odd_v = _cols_view(cache_c_k), _cols_view(cache_c_v)
    y_p, y_s = x_prompt, x_sample
    ev_p, ev_s, od_p, od_s = [], [], [], []
    for layer in range(depth):
        i = layer // 2
        if layer % 2 == 0:
            y_p, y_s, rp, rs = _even_layer(y_p, y_s, even_caches, i, page_table, norm_even[i], w_in_even[i],
                                           qk_norm_a[i], lambda_a[i], subln_a[i], w_out_even[i], layer)
            ev_p.append(rp)
            ev_s.append(rs)
        else:
            y_p, y_s, rp, rs = _odd_layer(y_p, y_s, odd_k, odd_v, i, page_table, norm_odd[i], w_in_odd[i],
                                          qk_norm_c[i], w_out_odd[i])
            od_p.append(rp)
            od_s.append(rs)

    def stack(rows, j):
        return jnp.stack([r[j] for r in rows])

    return (y_p, y_s,
            stack(ev_p, 0), stack(ev_p, 1), stack(ev_p, 2), stack(ev_p, 3), stack(od_p, 0), stack(od_p, 1),
            stack(ev_s, 0), stack(ev_s, 1), stack(ev_s, 2), stack(ev_s, 3), stack(od_s, 0), stack(od_s, 1))
```

```python
import functools
import math

import jax
import jax.numpy as jnp
from jax import lax
from jax.experimental import pallas as pl
from jax.experimental.pallas import tpu as pltpu

HEAD_DIM = 64
ROT_DIM = HEAD_DIM // 4
ROPE_THETA = 500000.0
RMS_EPS = 1e-6
MOBA_BLOCK = 256
MOBA_TOPK = 3
LANES = 128
SUBLANES = 8
NEG = -1e30
SB_DEAD_LOG2 = 150.0
QK_SCALE = HEAD_DIM ** -0.5
Q_SOFTMAX_SCALE = QK_SCALE * math.log2(math.e)
ATTN_SLAB = 256
VMEM_LIMIT = 52 * 1024 * 1024

F32 = jnp.float32
BF16 = jnp.bfloat16


def _params(sem, vmem=VMEM_LIMIT):
    return pltpu.CompilerParams(dimension_semantics=sem, vmem_limit_bytes=vmem)


def _dot(a, b):
    return jnp.dot(a, b, preferred_element_type=F32)


def _dot_nt(a, b):
    return lax.dot_general(a, b, (((1,), (1,)), ((), ())), preferred_element_type=F32)


def _split_bf16(x):
    hi = x.astype(BF16)
    lo = (x - hi.astype(F32)).astype(BF16)
    return hi, lo


def _iota(shape, dim):
    return lax.broadcasted_iota(jnp.int32, shape, dim)


def _head_seg_matrix():
    return (_iota((LANES, LANES), 0) // HEAD_DIM == _iota((LANES, LANES), 1) // HEAD_DIM).astype(BF16)


def _norm_rope(x, g, c, s1, s2, seg):
    hi, lo = _split_bf16(x * x)
    ms = (_dot(hi, seg) + _dot(lo, seg)) * (1.0 / HEAD_DIM)
    xn = x * lax.rsqrt(ms + RMS_EPS) * g
    half = ROT_DIM // 2
    return xn * c + pltpu.roll(xn, LANES - half, 1) * s1 + pltpu.roll(xn, half, 1) * s2


def _proj_kernel(x_ref, gn_ref, w_ref, gq_ref, gk_ref, c_ref, s1_ref, s2_ref, *out_refs, plan):
    x = x_ref[...]
    ms = jnp.mean(x * x, axis=-1, keepdims=True)
    xn = (x * lax.rsqrt(ms + RMS_EPS) * gn_ref[...]).astype(BF16)
    seg = _head_seg_matrix()
    c, s1, s2 = c_ref[...], s1_ref[...], s2_ref[...]
    outs = list(out_refs)
    tm = x.shape[0]

    def store_rows(o_ref, layout, t, n_t, y):
        if layout == "rows":
            o_ref[:, t * LANES:(t + 1) * LANES] = y
        elif layout == "cols":
            o_ref[t * LANES:(t + 1) * LANES, :] = jnp.transpose(y)
        elif layout == "heads":
            o_ref[pl.ds(t, tm, stride=n_t), :] = y
        else:
            raise ValueError(layout)

    for kind, col0, width, layout in plan:
        h = _dot(xn, w_ref[:, col0:col0 + width])
        n_t = width // LANES
        if kind in ("q_rope_bf16", "q_rope_f32", "k_rope", "k_rope_mean"):
            g = gq_ref[...] if kind.startswith("q") else gk_ref[...]
            if kind == "q_rope_bf16":
                o_ref = outs.pop(0)
            elif kind == "q_rope_f32":
                o_ref = outs.pop(0)
            else:
                o_ref, o16_ref = outs.pop(0), outs.pop(0)
                mean_ref = outs.pop(0) if kind == "k_rope_mean" else None
            for t in range(n_t):
                sl = slice(t * LANES, (t + 1) * LANES)
                y = _norm_rope(h[:, sl], g, c, s1, s2, seg)
                if kind == "q_rope_bf16":
                    o_ref[:, sl] = (y * Q_SOFTMAX_SCALE).astype(BF16)
                elif kind == "q_rope_f32":
                    o_ref[:, sl] = y
                else:
                    store_rows(o_ref, layout, t, n_t, y)
                    o16_ref[:, sl] = y.astype(BF16)
                    if mean_ref is not None:
                        for j in range(tm // MOBA_BLOCK):
                            mean_ref[j, :, sl] = jnp.mean(y[j * MOBA_BLOCK:(j + 1) * MOBA_BLOCK], axis=0,
                                                          keepdims=True)
        elif kind == "q_plain_bf16":
            outs.pop(0)[...] = (h * Q_SOFTMAX_SCALE).astype(BF16)
        elif kind == "kv_plain":
            o_ref = outs.pop(0)
            for t in range(n_t):
                store_rows(o_ref, layout, t, n_t, h[:, t * LANES:(t + 1) * LANES])
            outs.pop(0)[...] = h.astype(BF16)
        elif kind == "gate":
            outs.pop(0)[...] = h / (1.0 + jnp.exp(-h))
        else:
            raise ValueError(kind)
    assert not outs


def _rope_tables(pos):
    half = ROT_DIM // 2
    inv = ROPE_THETA ** (-jnp.arange(0, ROT_DIM, 2, dtype=F32) / ROT_DIM)
    ang = pos.astype(F32)[:, None] * inv[None, :]
    cos, sin = jnp.cos(ang), jnp.sin(ang)
    t = pos.shape[0]
    z_half = jnp.zeros((t, half), F32)
    z_rest = jnp.zeros((t, HEAD_DIM - ROT_DIM), F32)
    c = jnp.concatenate([cos, cos, jnp.ones((t, HEAD_DIM - ROT_DIM), F32)], axis=-1)
    s1 = jnp.concatenate([-sin, z_half, z_rest], axis=-1)
    s2 = jnp.concatenate([z_half, sin, z_rest], axis=-1)
    rep = LANES // HEAD_DIM
    return tuple(jnp.tile(a, (1, rep)) for a in (c, s1, s2))


def _project(x, g_norm, w_bf16, gq, gk, tables, plan, tm, seq, name):
    m, d = x.shape
    n_tab = tables[0].shape[0] // tm
    tiles_per_seq = seq // tm
    out_shapes, out_specs = [], []
    for kind, _, width, layout in plan:
        row = pl.BlockSpec((tm, width), lambda i: (i, 0))
        if kind in ("q_rope_bf16", "q_plain_bf16"):
            out_shapes += [jax.ShapeDtypeStruct((m, width), BF16)]
            out_specs += [row]
        elif kind in ("q_rope_f32", "gate"):
            out_shapes += [jax.ShapeDtypeStruct((m, width), F32)]
            out_specs += [row]
        elif kind in ("k_rope", "kv_plain", "k_rope_mean"):
            if layout == "rows":
                out_shapes += [jax.ShapeDtypeStruct((m, width), F32)]
                out_specs += [row]
            elif layout == "cols":
                out_shapes += [jax.ShapeDtypeStruct((m // seq, width, seq), F32)]
                out_specs += [pl.BlockSpec((None, width, tm),
                                           lambda i: (i // tiles_per_seq, 0, i % tiles_per_seq))]
            else:
                n_t = width // LANES
                out_shapes += [jax.ShapeDtypeStruct((m * n_t, LANES), F32)]
                out_specs += [pl.BlockSpec((tm * n_t, LANES), lambda i: (i, 0))]
            out_shapes += [jax.ShapeDtypeStruct((m, width), BF16)]
            out_specs += [row]
            if kind == "k_rope_mean":
                assert tm % MOBA_BLOCK == 0
                out_shapes += [jax.ShapeDtypeStruct((m // MOBA_BLOCK, 1, width), F32)]
                out_specs += [pl.BlockSpec((tm // MOBA_BLOCK, 1, width), lambda i: (i, 0, 0))]
    tab_spec = pl.BlockSpec((tm, LANES), lambda i: (i % n_tab, 0))
    vec_d = pl.BlockSpec((1, d), lambda i: (0, 0))
    vec_l = pl.BlockSpec((1, LANES), lambda i: (0, 0))
    rep = LANES // HEAD_DIM
    return pl.pallas_call(
        functools.partial(_proj_kernel, plan=plan),
        grid=(m // tm,),
        in_specs=[pl.BlockSpec((tm, d), lambda i: (i, 0)), vec_d,
                  pl.BlockSpec(w_bf16.shape, lambda i: (0, 0), pipeline_mode=pl.Buffered(1)),
                  vec_l, vec_l, tab_spec, tab_spec, tab_spec],
        out_specs=out_specs,
        out_shape=out_shapes,
        compiler_params=_params(("parallel",)),
        name=name,
    )(x, g_norm.reshape(1, d), w_bf16, jnp.tile(gq, rep).reshape(1, LANES), jnp.tile(gk, rep).reshape(1, LANES),
      *tables)


def _finish_kernel(x_ref, w_ref, *refs):
    mix_refs, y_ref = refs[:-1], refs[-1]
    y = x_ref[...]
    row0 = 0
    for mix_ref in mix_refs:
        rows = mix_ref.shape[1]
        y = y + _dot(mix_ref[...], w_ref[row0:row0 + rows, :])
        row0 += rows
    y_ref[...] = y


def _finish(x, mix_parts, w_bf16, tm, name):
    m, d = x.shape
    kdim = w_bf16.shape[0]
    assert sum(p.shape[1] for p in mix_parts) == kdim
    return pl.pallas_call(
        _finish_kernel,
        grid=(m // tm,),
        in_specs=[pl.BlockSpec((tm, d), lambda i: (i, 0)), pl.BlockSpec((kdim, d), lambda i: (0, 0))]
        + [pl.BlockSpec((tm, p.shape[1]), lambda i: (i, 0)) for p in mix_parts],
        out_specs=pl.BlockSpec((tm, d), lambda i: (i, 0)),
        out_shape=jax.ShapeDtypeStruct((m, d), F32),
        compiler_params=_params(("parallel",)),
        name=name,
    )(x, w_bf16, *mix_parts)


def _lambda_value(lp, lam_init):
    a = jnp.sum(lp[0:1] * lp[1:2], axis=-1, keepdims=True)
    b = jnp.sum(lp[2:3] * lp[3:4], axis=-1, keepdims=True)
    return jnp.exp(a) - jnp.exp(b) + lam_init


def _softmax_tile(qq, k, v, carry, mask):
    m, l, acc = carry
    s = _dot_nt(qq, k)
    if mask is not None:
        s = jnp.where(mask, s, NEG)
    m_new = jnp.maximum(m, jnp.max(s, axis=-1, keepdims=True))
    alpha = jnp.exp2(m - m_new)
    p = jnp.exp2(s - m_new)
    l = alpha * l + jnp.sum(p, axis=-1, keepdims=True)
    acc = alpha * acc + _dot(p.astype(BF16), v)
    return m_new, l, acc


def _dot_tn(a, b):
    return lax.dot_general(a, b, (((0,), (0,)), ((), ())), preferred_element_type=F32)


def _softmax_tile_t(qq_t, k_ref, v_ref, start, n_keys, carry, mask):
    s, m_s, bias = _scores_t(qq_t, k_ref, start, n_keys, mask)
    return _absorb_t(v_ref, start, n_keys, s, m_s, bias, carry)


def _scores_t(qq_t, k_ref, start, n_keys, mask):
    s = _dot(k_ref[pl.ds(start, n_keys), :], qq_t)
    slabs = [s[c * ATTN_SLAB:(c + 1) * ATTN_SLAB] for c in range(n_keys // ATTN_SLAB)]
    if mask is None:
        return s, jnp.max(s, axis=0, keepdims=True), None
    if mask[0].dtype == jnp.bool_:
        s = jnp.concatenate([jnp.where(mk, x, NEG) for x, mk in zip(slabs, mask)], axis=0)
        return s, jnp.max(s, axis=0, keepdims=True), None
    m_s = None
    for x, row in zip(slabs, mask):
        m_c = jnp.max(x, axis=0, keepdims=True) + row
        m_s = m_c if m_s is None else jnp.maximum(m_s, m_c)
    return s, m_s, jnp.concatenate(mask, axis=0)


def _absorb_t(v_ref, start, n_keys, s, m_s, bias, carry):
    m, l, acc = carry
    m_new = jnp.maximum(m, m_s)
    alpha = jnp.exp2(m - m_new)
    if bias is None:
        p = jnp.exp2(s - m_new)
    else:
        shift = bias - m_new
        p = jnp.concatenate([jnp.exp2(s[c * ATTN_SLAB:(c + 1) * ATTN_SLAB] + shift[c:c + 1])
                             for c in range(n_keys // ATTN_SLAB)], axis=0)
    l = alpha * l + jnp.sum(p, axis=0, keepdims=True)
    v, p16 = v_ref[pl.ds(start, n_keys), :], p.astype(BF16)
    pv = _dot_tn(v, p16) if acc.shape[0] == LANES else _pair_values_t(v, p16)
    return m_new, l, alpha * acc + pv


def _pair_values_t(v, w16):
    half = w16.shape[1] // 2
    return jnp.concatenate([_dot_tn(v[:, :HEAD_DIM], w16[:, :half]), _dot_tn(v[:, HEAD_DIM:], w16[:, half:])],
                           axis=1)


def _merge_pair_t(x, tq):
    return jnp.transpose(jnp.concatenate([x[:, :tq], x[:, tq:]], axis=0))


def _pipelined_pairs(n_pairs, n_keys, qq_t, k_ref, v_ref, mask_of, s_refs, carry):
    s_a, s_b = s_refs

    def scores(t, s_ref):
        s, m_s, bias = _scores_t(qq_t, k_ref, pl.multiple_of(t * n_keys, n_keys), n_keys, mask_of(t))
        s_ref[...] = s
        return m_s, bias

    def absorb(t, s_ref, stats, carry):
        return _absorb_t(v_ref, pl.multiple_of(t * n_keys, n_keys), n_keys, s_ref[...], *stats, carry)

    def trip(u, state):
        m_a, carry = state
        m_b = scores(2 * u + 1, s_b)
        carry = absorb(2 * u, s_a, m_a, carry)
        m_a = scores(2 * u + 2, s_a)
        carry = absorb(2 * u + 1, s_b, m_b, carry)
        return m_a, carry

    def run(carry):
        m_a, carry = lax.fori_loop(0, n_pairs - 1, trip, (scores(n_pairs * 0, s_a), carry))
        last = 2 * (n_pairs - 1)
        m_b = scores(last + 1, s_b)
        carry = absorb(last, s_a, m_a, carry)
        return absorb(last + 1, s_b, m_b, carry)

    return lax.cond(n_pairs > 0, run, lambda c: c, carry)


def _causal_masks(k_start, q_start, tq, n_keys, r, strict):
    q_pos = q_start + _iota((ATTN_SLAB, r), 1) % tq
    masks = []
    for c in range(n_keys // ATTN_SLAB):
        k_pos = k_start + c * ATTN_SLAB + _iota((ATTN_SLAB, r), 0)
        masks.append(k_pos < q_pos if strict else k_pos <= q_pos)
    return masks


def _stack_pair_t(q_t):
    row = _iota(q_t.shape, 0)
    zero = jnp.zeros_like(q_t)
    return jnp.concatenate([jnp.where(row < HEAD_DIM, q_t, zero), jnp.where(row >= HEAD_DIM, q_t, zero)], axis=1)


def _diff_attn_kernel(lam_ref, subg_ref, q_ref, k_ref, v_ref, sg_ref, o_ref, s_a, s_b, *, tq, tk, lam_init):
    i = pl.program_id(2)
    r = 2 * tq
    qq_t = _stack_pair_t(jnp.transpose(q_ref[...].astype(F32))).astype(BF16)
    n_full = (i * tq) // tk
    half = tk // 2

    carry = (jnp.full((1, r), NEG, F32), jnp.zeros((1, r), F32), jnp.zeros((LANES, r), F32))
    start = pl.multiple_of(n_full * tk, tk)
    carry = _softmax_tile_t(qq_t, k_ref, v_ref, start, tk, carry,
                            _causal_masks(start, i * tq, tq, tk, r, strict=False))
    _, l, acc = _pipelined_pairs(n_full, half, qq_t, k_ref, v_ref, lambda t: None, (s_a, s_b), carry)

    a = acc / l
    lam = _lambda_value(lam_ref[...], lam_init)
    o = jnp.transpose(a[:, :tq] - lam * a[:, tq:])
    ms = jnp.mean(o * o, axis=-1, keepdims=True)
    o = o * lax.rsqrt(ms + RMS_EPS) * subg_ref[...] * (1.0 - lam_init)
    o_ref[...] = (o * sg_ref[...]).astype(BF16)


def _softplus2(z):
    return jnp.maximum(z, 0.0) + jnp.log2(1.0 + jnp.exp2(-jnp.abs(z)))


def _later_matrix(tk, keys_axis=1):
    a, b = _iota((tk, tk), 0), _iota((tk, tk), 1)
    return (a > b if keys_axis == 1 else b > a).astype(BF16)


def _sb_kernel(q_ref, k_ref, v_ref, sg_ref, o_ref, *, tq, tk):
    i = pl.program_id(2)
    r = 2 * tq
    qq_t = _stack_pair_t(jnp.transpose(q_ref[...].astype(F32))).astype(BF16)
    tri = _later_matrix(ATTN_SLAB, keys_axis=0)
    n_full = (i * tq) // tk
    n_slabs = tk // ATTN_SLAB

    def tile(st, c, acc, valid):
        parts = []
        for u in range(n_slabs):
            z = _dot(k_ref[pl.ds(st + u * ATTN_SLAB, ATTN_SLAB), :], qq_t)
            sp = _softplus2(z)
            log_keep = -sp if valid is None else jnp.where(valid[u], -sp, 0.0)
            later = _dot(tri, log_keep.astype(BF16))
            parts.append((z - sp + later, later[0:1, :] + log_keep[0:1, :]))
        for u in reversed(range(n_slabs)):
            e, whole = parts[u]
            w = jnp.exp2(e + c)
            if valid is not None:
                w = jnp.where(valid[u], w, 0.0)
            acc = acc + _pair_values_t(v_ref[pl.ds(st + u * ATTN_SLAB, ATTN_SLAB), :], w.astype(BF16))
            c = c + whole
        return c, acc

    start = pl.multiple_of(n_full * tk, tk)
    c, acc = tile(start, jnp.zeros((1, r), F32), jnp.zeros((HEAD_DIM, r), F32),
                  _causal_masks(start, i * tq, tq, tk, r, strict=True))

    def live(state):
        t, c_max, _, _ = state
        return jnp.logical_and(t < n_full, c_max > -SB_DEAD_LOG2)

    def full_tile(state):
        t, _, c, acc = state
        c, acc = tile(pl.multiple_of((n_full - 1 - t) * tk, tk), c, acc, None)
        return t + 1, jnp.max(c), c, acc

    _, _, _, acc = lax.while_loop(live, full_tile, (jnp.int32(0), jnp.max(c), c, acc))
    o_ref[...] = (_merge_pair_t(acc, tq) * sg_ref[...]).astype(BF16)


def _top_blocks(g, n_valid, axis=-1, picks=None):
    axis = axis % g.ndim
    blk = _iota(g.shape, axis)
    nb = g.shape[axis]
    g = jnp.where(blk < n_valid, g, -jnp.inf)
    sel = jnp.zeros(g.shape, jnp.bool_)
    for _ in range(min(MOBA_TOPK, nb)):
        mx = jnp.max(g, axis=axis, keepdims=True)
        idx = jnp.min(jnp.where(g == mx, blk, nb), axis=axis, keepdims=True)
        pick = blk == idx
        sel = jnp.logical_or(sel, pick)
        g = jnp.where(pick, -jnp.inf, g)
        if picks is not None:
            picks.append(idx)
    return jnp.logical_and(sel, blk < n_valid)


def _moba_kernel(q_ref, k_ref, v_ref, km_ref, sg_ref, o_ref, sel_s, s_a, s_b, *, tq, group):
    i = pl.program_id(2)
    r = 2 * tq
    blk = MOBA_BLOCK
    tile_blocks = tq // blk
    first = i * tile_blocks
    qf_t = _stack_pair_t(jnp.transpose(q_ref[...]))
    q_hi, q_lo = _split_bf16(qf_t)
    qq_t = (qf_t * Q_SOFTMAX_SCALE).astype(BF16)

    km_hi, km_lo = _split_bf16(km_ref[...])
    g = _dot(km_hi, q_hi) + _dot(km_hi, q_lo) + _dot(km_lo, q_hi)
    col = _iota((1, r), 1) % tq
    own = col // blk
    sel = _top_blocks(g, first + own, axis=0)
    block_id = _iota(sel.shape, 0)
    sel_s[...] = jnp.where(jnp.logical_and(sel, block_id < first), 0.0, NEG)

    start = pl.multiple_of(first * blk, tq)
    causal = _iota((blk, r), 0) <= col % blk
    masks = []
    for a in range(tile_blocks):
        picked = jnp.max(jnp.where(jnp.logical_and(sel, block_id == first + a), 1.0, 0.0), axis=0,
                         keepdims=True) > 0.0
        masks.append(jnp.logical_or(jnp.logical_and(own == a, causal), jnp.logical_and(own > a, picked)))
    carry = (jnp.full((1, r), NEG, F32), jnp.zeros((1, r), F32), jnp.zeros((HEAD_DIM, r), F32))
    carry = _softmax_tile_t(qq_t, k_ref, v_ref, start, tq, carry, masks)

    per_tile = group // 2

    def chosen(t):
        return [sel_s[pl.ds(t * per_tile + c, 1), :] for c in range(per_tile)]

    _, l, acc = _pipelined_pairs((first + group - 1) // group, per_tile * blk, qq_t, k_ref, v_ref, chosen,
                                 (s_a, s_b), carry)
    o_ref[...] = (_merge_pair_t(acc / l, tq) * sg_ref[...]).astype(BF16)


def _attn_specs(tq, s, group0=0):
    q_spec = pl.BlockSpec((None, tq, LANES), lambda b, h, i: (b, i, h + group0))
    kv_spec = pl.BlockSpec((None, s, LANES), lambda b, h, i: (b, 0, h))
    return q_spec, kv_spec


def _diff_attn_prompt(q16, k16, v16, sgate, lam_p, subln_g, lam_init, tq, tk):
    b, s, w = q16.shape
    q_spec, kv_spec = _attn_specs(tq, s)
    return pl.pallas_call(
        functools.partial(_diff_attn_kernel, tq=tq, tk=tk, lam_init=lam_init),
        grid=(b, w // LANES, s // tq),
        in_specs=[pl.BlockSpec(lam_p.shape, lambda b, h, i: (0, 0)),
                  pl.BlockSpec((1, LANES), lambda b, h, i: (0, 0)),
                  q_spec, kv_spec, kv_spec, q_spec],
        out_specs=q_spec,
        out_shape=jax.ShapeDtypeStruct((b, s, w), BF16),
        scratch_shapes=[pltpu.VMEM((tk // 2, 2 * tq), F32)] * 2,
        compiler_params=_params(("parallel", "parallel", "arbitrary")),
        name="diff_attn_prompt",
    )(lam_p, subln_g.reshape(1, LANES), q16, k16, v16, sgate)


def _sb_prompt(q16, k16, v16, sgate, gate_group0, tq, tk):
    b, s, w = q16.shape
    q_spec, kv_spec = _attn_specs(tq, s)
    sg_spec = pl.BlockSpec((None, tq, LANES), lambda b, h, i: (b, i, h + gate_group0))
    return pl.pallas_call(
        functools.partial(_sb_kernel, tq=tq, tk=tk),
        grid=(b, w // LANES, s // tq),
        in_specs=[q_spec, kv_spec, kv_spec, sg_spec],
        out_specs=q_spec,
        out_shape=jax.ShapeDtypeStruct((b, s, w), BF16),
        compiler_params=_params(("parallel", "parallel", "arbitrary")),
        name="sb_prompt",
    )(q16, k16, v16, sgate)


def _moba_prompt(q, k16, v16, kmean, sgate):
    b, s, w = q.shape
    assert MOBA_BLOCK == ATTN_SLAB
    nb = s // MOBA_BLOCK
    tq = MOBA_BLOCK * math.gcd(nb, MOBA_TILE_BLOCKS)
    q_spec, kv_spec = _attn_specs(tq, s)
    group = math.gcd(nb, MOBA_GROUP)
    return pl.pallas_call(
        functools.partial(_moba_kernel, tq=tq, group=group),
        grid=(b, w // LANES, s // tq),
        in_specs=[q_spec, kv_spec, kv_spec,
                  pl.BlockSpec((None, nb, LANES), lambda b, h, i: (b, 0, h)), q_spec],
        out_specs=q_spec,
        out_shape=jax.ShapeDtypeStruct((b, s, w), BF16),
        scratch_shapes=[pltpu.VMEM((nb, 2 * tq), F32)]
        + [pltpu.VMEM((group // 2 * MOBA_BLOCK, 2 * tq), F32)] * 2,
        compiler_params=_params(("parallel", "parallel", "arbitrary")),
        name="moba_prompt",
    )(q, k16, v16, kmean, sgate)


def _row_heads(x_row, n_rows, lanes_per_row):
    w = x_row.shape[1]
    keep = _iota((n_rows, w), 1) // lanes_per_row == _iota((n_rows, w), 0)
    return jnp.where(keep, jnp.broadcast_to(x_row, (n_rows, w)), 0.0)


def _dup_rows(x, reps):
    n, w = x.shape
    row = _iota((n * reps, w), 0) // reps
    out = jnp.zeros((n * reps, w), x.dtype)
    for h in range(n):
        out = jnp.where(row == h, x[h:h + 1, :], out)
    return out


def _stack_rows(rows):
    n, w = len(rows), rows[0].shape[1]
    row = _iota((n, w), 0)
    out = jnp.zeros((n, w), rows[0].dtype)
    for h, r in enumerate(rows):
        out = jnp.where(row == h, r, out)
    return out


def _head_dots(qcol_ref, kt_ref, n_heads):
    rows = []
    for h in range(n_heads):
        sl = slice(h * HEAD_DIM, (h + 1) * HEAD_DIM)
        rows.append(jnp.sum(qcol_ref[sl, :] * kt_ref[sl, :], axis=0, keepdims=True))
    return _stack_rows(rows)


def _add_weighted(acc_ref, w, vt_refs, n_heads):
    for h in range(n_heads):
        sl = slice(h * HEAD_DIM, (h + 1) * HEAD_DIM)
        a = acc_ref[sl, :]
        for u, vt_ref in enumerate(vt_refs):
            a = a + w[u * n_heads + h:u * n_heads + h + 1, :] * vt_ref[sl, :]
        acc_ref[sl, :] = a


def _sb_weights_pages(zs, tri, c):
    n_h = zs[0].shape[0]
    z = jnp.concatenate(zs, axis=0)
    sp = _softplus2(z)
    log_keep = -sp
    hi, lo = _split_bf16(log_keep)
    rows = z.shape[0]
    both = _dot(jnp.concatenate([hi, lo], axis=0), tri)
    later = both[:rows] + both[rows:]
    whole = later[:, 0:1] + log_keep[:, 0:1]
    offsets = []
    for u in range(len(zs)):
        offsets.append(c)
        c = c + whole[u * n_h:(u + 1) * n_h]
    return jnp.exp2(z - sp + later + jnp.concatenate(offsets, axis=0)), c


def _lane_sums_as_row(x):
    hi, lo = _split_bf16(x)
    ones = jnp.ones((SUBLANES, LANES), BF16)
    return (_dot_nt(ones, hi) + _dot_nt(ones, lo))[0:1, :]


def _decode_even_kernel(pt_ref, lam_ref, subg_ref, qa_ref, kan_ref, van_ref, qb_ref, sg_ref, *refs,
                        pps, lam_init):
    del pt_ref
    page_refs = refs[:4 * pps]
    o_ref = refs[4 * pps]
    qa_s, m_s, l_s, acca_s, c_s, accb_s = refs[4 * pps + 1:]
    step = pl.program_id(1)
    n_ha = qa_ref.shape[0]
    n_a = 2 * n_ha
    n_b = accb_s.shape[0] // HEAD_DIM
    wa = n_ha * LANES

    @pl.when(step == 0)
    def _():
        rows = _dup_rows(qa_ref[...].astype(F32), 2)
        qa = jnp.where(_iota(rows.shape, 1) // HEAD_DIM == _iota(rows.shape, 0) % 2, rows, 0.0)
        qa_s[...] = qa
        m_s[...] = jnp.sum(qa * _dup_rows(kan_ref[...], 2), axis=-1, keepdims=True)
        l_s[...] = jnp.ones_like(l_s)
        acca_s[...] = _dup_rows(van_ref[...], 2)
        c_s[...] = jnp.zeros_like(c_s)
        accb_s[...] = jnp.zeros_like(accb_s)

    rows_a = pps * page_refs[0].shape[0]
    own_head = _iota((n_a, rows_a), 1) % n_ha == _iota((n_a, rows_a), 0) // 2
    ak = jnp.concatenate([page_refs[4 * u][...].astype(BF16) for u in range(pps)], axis=0)
    av = jnp.concatenate([page_refs[4 * u + 1][...].astype(BF16) for u in range(pps)], axis=0)
    m_s[...], l_s[...], acca_s[...] = _softmax_tile(qa_s[...].astype(BF16), ak, av,
                                                    (m_s[...], l_s[...], acca_s[...]), own_head)
    zs = [_head_dots(qb_ref, page_refs[4 * u + 2], n_b) for u in range(pps)]
    w, c_s[...] = _sb_weights_pages(zs, _later_matrix(LANES), c_s[...])
    _add_weighted(accb_s, w, [page_refs[4 * u + 3] for u in range(pps)], n_b)

    @pl.when(step == pl.num_programs(1) - 1)
    def _():
        lam = _lambda_value(lam_ref[...], lam_init)
        a = acca_s[...] / l_s[...]
        for h in range(n_ha):
            sl = slice(h * LANES, (h + 1) * LANES)
            x = a[2 * h:2 * h + 1, :] - lam * a[2 * h + 1:2 * h + 2, :]
            ms = jnp.mean(x * x, axis=-1, keepdims=True)
            x = x * lax.rsqrt(ms + RMS_EPS) * subg_ref[...] * (1.0 - lam_init)
            o_ref[:, sl] = (x * sg_ref[:, sl]).astype(BF16)
        o_ref[:, wa:] = (_lane_sums_as_row(accb_s[...]) * sg_ref[:, wa:]).astype(BF16)


def _page_specs(cache, li, n_pages, pps, reverse):
    r = cache.shape[2]

    def make(u):
        def index(b, s, pt):
            j = s * pps + u
            j = n_pages - 1 - j if reverse else j
            return (li, pt[b, j], 0, 0)
        return pl.BlockSpec((None, None, r, LANES), index)

    return [make(u) for u in range(pps)]


def _col_bcast(x):
    return jnp.broadcast_to(x.astype(F32)[:, :, None], x.shape + (LANES,))


def _decode_even(pt, lam_p, subln_g, lam_init, qa16, ka, va, qb16, sgate, caches, li, pps):
    db, n_pages = pt.shape
    wa, wb = qa16.shape[-1], qb16.shape[-1]
    n_ha = wa // LANES
    whole = lambda shp: pl.BlockSpec((None,) + shp, lambda b, s, pt: (b,) + (0,) * len(shp))
    specs_by_cache = [_page_specs(c, li, n_pages, pps, True) for c in caches]
    page_specs = [specs_by_cache[t][u] for u in range(pps) for t in range(4)]
    page_args = [caches[t] for u in range(pps) for t in range(4)]
    grid_spec = pltpu.PrefetchScalarGridSpec(
        num_scalar_prefetch=1,
        grid=(db, n_pages // pps),
        in_specs=[pl.BlockSpec(lam_p.shape, lambda b, s, pt: (0, 0)),
                  pl.BlockSpec((1, LANES), lambda b, s, pt: (0, 0)),
                  whole((n_ha, LANES)), whole((n_ha, LANES)), whole((n_ha, LANES)), whole((wb, LANES)),
                  whole((1, wa + wb))] + page_specs,
        out_specs=whole((1, wa + wb)),
        scratch_shapes=[pltpu.VMEM((2 * n_ha, LANES), F32),
                        pltpu.VMEM((2 * n_ha, 1), F32), pltpu.VMEM((2 * n_ha, 1), F32),
                        pltpu.VMEM((2 * n_ha, LANES), F32),
                        pltpu.VMEM((wb // HEAD_DIM, 1), F32), pltpu.VMEM((wb, LANES), F32)],
    )
    heads = lambda x: x.reshape(db, n_ha, LANES)
    return pl.pallas_call(
        functools.partial(_decode_even_kernel, pps=pps, lam_init=lam_init),
        grid_spec=grid_spec,
        out_shape=jax.ShapeDtypeStruct((db, 1, wa + wb), BF16),
        compiler_params=_params(("parallel", "arbitrary")),
        name="decode_even",
    )(pt, lam_p, subln_g.reshape(1, LANES), heads(qa16), heads(ka), heads(va), _col_bcast(qb16),
      sgate.reshape(db, 1, wa + wb), *page_args)


def _moba_scores_kernel(pt_ref, qcol_ref, qrow_ref, kn_ref, *refs, pps, n_blocks):
    del pt_ref
    page_refs = refs[:pps]
    p_ref, pnew_ref, ids_ref = refs[pps:pps + 3]
    s_s = refs[pps + 3]
    step = pl.program_id(1)
    n_h = qcol_ref.shape[0] // HEAD_DIM
    ppb = MOBA_BLOCK // LANES

    for u in range(pps):
        s_s[step * pps + u] = _head_dots(qcol_ref, page_refs[u], n_h)

    @pl.when(step == pl.num_programs(1) - 1)
    def _():
        lane = _iota((n_h, LANES), 1)
        g = jnp.zeros((n_h, LANES), F32)
        for n in range(n_blocks):
            tot = sum(s_s[n * ppb + t] for t in range(ppb))
            g = jnp.where(lane == n, jnp.sum(tot, axis=-1, keepdims=True) * (1.0 / MOBA_BLOCK), g)
        picks = []
        sel = _top_blocks(g, n_blocks, picks=picks)
        ids = jnp.zeros((n_h, LANES), jnp.int32)
        for r, idx in enumerate(picks):
            ids = jnp.where(lane == r, idx, ids)
        ids_ref[...] = ids
        chosen = [sel[:, n:n + 1] for n in range(n_blocks)]
        s_new = jnp.sum(_row_heads(qrow_ref[...], n_h, HEAD_DIM) * kn_ref[...], axis=-1, keepdims=True) * QK_SCALE
        m_lanes = jnp.full((n_h, LANES), NEG, F32)
        for j in range(n_blocks * ppb):
            m_lanes = jnp.maximum(m_lanes, jnp.where(chosen[j // ppb], s_s[j] * QK_SCALE, NEG))
        m = jnp.maximum(jnp.max(m_lanes, axis=-1, keepdims=True), s_new)
        l_lanes = jnp.zeros((n_h, LANES), F32)
        for j in range(n_blocks * ppb):
            e = jnp.where(chosen[j // ppb], jnp.exp(s_s[j] * QK_SCALE - m), 0.0)
            s_s[j] = e
            l_lanes = l_lanes + e
        e_new = jnp.exp(s_new - m)
        inv = 1.0 / (jnp.sum(l_lanes, axis=-1, keepdims=True) + e_new)
        for j in range(n_blocks * ppb):
            p_ref[j] = s_s[j] * inv
        pnew_ref[...] = jnp.broadcast_to(e_new * inv, pnew_ref.shape)


def _moba_values_kernel(pt_ref, ids_ref, p_ref, pnew_ref, vn_ref, sg_ref, *refs, n_sel, ppb, heads_per_step):
    del pt_ref
    slices, o_ref = refs[:-1], refs[-1]
    b, pair = pl.program_id(0), pl.program_id(1)
    accs = []
    for c in range(heads_per_step):
        h = pair * heads_per_step + c
        acc = jnp.zeros((HEAD_DIM, LANES), F32)
        for r in range(n_sel):
            blk = ids_ref[b, h * n_sel + r]
            for u in range(ppb):
                w = p_ref[blk * ppb + u, pl.ds(h, 1), :]
                acc = acc + w * slices[(c * n_sel + r) * ppb + u][...]
        accs.append(acc)
    o = _lane_sums_as_row(jnp.concatenate(accs, axis=0)) + pnew_ref[...] * vn_ref[...]
    o_ref[...] = (o * sg_ref[...]).astype(BF16)


def _decode_odd(pt, q, k_new, v_new, sgate, cache_k, cache_v, li, pps):
    db, n_pages = pt.shape
    w = q.shape[-1]
    n_h = w // HEAD_DIM
    page = LANES
    past = n_pages * page
    assert past % MOBA_BLOCK == 0 and past >= MOBA_BLOCK and MOBA_BLOCK % page == 0
    n_blocks = past // MOBA_BLOCK
    row = lambda wd: pl.BlockSpec((None, 1, wd), lambda b, s, pt: (b, 0, 0))
    r3 = lambda x: x.reshape(db, 1, x.shape[-1])
    assert n_blocks >= MOBA_TOPK
    per_head = pl.BlockSpec((None, n_h, LANES), lambda b, s, pt: (b, 0, 0))
    probs, p_new, ids = pl.pallas_call(
        functools.partial(_moba_scores_kernel, pps=pps, n_blocks=n_blocks),
        grid_spec=pltpu.PrefetchScalarGridSpec(
            num_scalar_prefetch=1,
            grid=(db, n_pages // pps),
            in_specs=[pl.BlockSpec((None, w, LANES), lambda b, s, pt: (b, 0, 0)), row(w), row(w)]
            + _page_specs(cache_k, li, n_pages, pps, False),
            out_specs=[pl.BlockSpec((None, n_pages, n_h, page), lambda b, s, pt: (b, 0, 0, 0)),
                       per_head, per_head],
            scratch_shapes=[pltpu.VMEM((n_pages, n_h, page), F32)],
        ),
        out_shape=[jax.ShapeDtypeStruct((db, n_pages, n_h, page), F32),
                   jax.ShapeDtypeStruct((db, n_h, LANES), F32),
                   jax.ShapeDtypeStruct((db, n_h, LANES), jnp.int32)],
        compiler_params=_params(("parallel", "arbitrary")),
        name="decode_moba_scores",
    )(pt, _col_bcast(q), r3(q), r3(k_new), *([cache_k] * pps))
    p_new_row = jnp.repeat(p_new[:, :, 0], HEAD_DIM, axis=-1).reshape(db, 1, w)

    ppb = MOBA_BLOCK // page
    heads_per_step = math.gcd(n_h, VALUE_HEADS_PER_STEP)
    step_lanes = heads_per_step * HEAD_DIM
    assert step_lanes % LANES == 0
    block_ids = ids[:, :, :MOBA_TOPK].reshape(db, n_h * MOBA_TOPK)

    def slice_spec(c, r, u):
        def index(b, pair, pt, bid):
            h = pair * heads_per_step + c
            return (li, pt[b, bid[b, h * MOBA_TOPK + r] * ppb + u], h, 0)
        return pl.BlockSpec((None, None, HEAD_DIM, LANES), index)

    slice_specs = [slice_spec(c, r, u) for c in range(heads_per_step) for r in range(MOBA_TOPK)
                   for u in range(ppb)]
    lanes = lambda: pl.BlockSpec((None, 1, step_lanes), lambda b, pair, pt, bid: (b, 0, pair))
    return pl.pallas_call(
        functools.partial(_moba_values_kernel, n_sel=MOBA_TOPK, ppb=ppb, heads_per_step=heads_per_step),
        grid_spec=pltpu.PrefetchScalarGridSpec(
            num_scalar_prefetch=2,
            grid=(db, n_h // heads_per_step),
            in_specs=[pl.BlockSpec((None, n_pages, n_h, page), lambda b, pair, pt, bid: (b, 0, 0, 0)),
                      lanes(), lanes(), lanes()] + slice_specs,
            out_specs=lanes(),
        ),
        out_shape=jax.ShapeDtypeStruct((db, 1, w), BF16),
        compiler_params=_params(("parallel", "arbitrary")),
        name="decode_moba_values",
    )(pt, block_ids, probs, p_new_row, r3(v_new), r3(sgate), *([cache_v] * len(slice_specs)))


PROMPT_TM = 512
ATTN_TQ = 512
ATTN_TK = 512
SB_TILE = 256
MOBA_GROUP = 4
MOBA_TILE_BLOCKS = 2
PAGES_PER_STEP = 16
VALUE_HEADS_PER_STEP = 4


def _rows_view(c):
    l, n, p, h, hd = c.shape
    assert p == LANES and hd == LANES
    return c.reshape(l, n, p * h, hd)


def _cols_view(c):
    l, n, p, h, hd = c.shape
    assert p == LANES and hd == HEAD_DIM
    return jnp.transpose(c, (0, 1, 3, 4, 2)).reshape(l, n, h * hd, p)


def _heads_from_cols(x, n_heads):
    b, w, s = x.shape
    return jnp.transpose(x.reshape(b, n_heads, w // n_heads, s), (0, 3, 1, 2))


def _even_layer(x_p, x_s, caches, li, pt, g_norm, w_in, qk_g, lam_p, subln_g, w_out, layer):
    lam_init = 0.8 - 0.6 * math.exp(-0.3 * layer)
    b, s, d = x_p.shape
    db, t, _ = x_s.shape
    wa = w_out.shape[0] // 2
    wb = wa
    def plan(wide, narrow):
        return (("q_rope_bf16", 0, wa, None), ("k_rope", wa, wa, wide), ("kv_plain", 2 * wa, wa, wide),
                ("q_plain_bf16", 3 * wa, wb, None), ("kv_plain", 3 * wa + wb, wb, narrow),
                ("kv_plain", 3 * wa + 2 * wb, wb, narrow), ("gate", 3 * wa + 3 * wb, wa + wb, None))

    w_in16 = w_in.astype(BF16)
    w_out16 = w_out.astype(BF16)
    past = pt.shape[1] * LANES
    n_a = wa // (2 * HEAD_DIM)
    n_b = wb // HEAD_DIM

    tabs = _rope_tables(jnp.arange(s))
    qa16, ka, ka16, va, va16, qb16, kb, kb16, vb, vb16, sgate = _project(
        x_p.reshape(b * s, d), g_norm, w_in16, qk_g[0], qk_g[1], tabs, plan("heads", "cols"), PROMPT_TM, s,
        "proj_even_prompt")
    r3 = lambda a: a.reshape(b, s, a.shape[-1])
    sg3 = r3(sgate)
    mix_a = _diff_attn_prompt(r3(qa16), r3(ka16), r3(va16), sg3, lam_p, subln_g, lam_init, ATTN_TQ, ATTN_TK)
    mix_b = _sb_prompt(r3(qb16), r3(kb16), r3(vb16), sg3, wa // LANES, SB_TILE, SB_TILE)
    y_p = _finish(x_p.reshape(b * s, d), [mix_a.reshape(b * s, wa), mix_b.reshape(b * s, wb)], w_out16, PROMPT_TM,
                  "finish_even_prompt").reshape(b, s, d)

    tabs_s = _rope_tables(jnp.tile(past + jnp.arange(t), db))
    qa16_s, ka_s, _, va_s, _, qb16_s, kb_s, _, vb_s, _, sgate_s = _project(
        x_s.reshape(db * t, d), g_norm, w_in16, qk_g[0], qk_g[1], tabs_s, plan("rows", "rows"), db * t, db * t,
        "proj_even_sample")
    assert t == 1
    mix_s = _decode_even(pt, lam_p, subln_g, lam_init, qa16_s, ka_s, va_s, qb16_s, sgate_s,
                         caches, li, math.gcd(pt.shape[1], PAGES_PER_STEP))
    y_s = _finish(x_s.reshape(db * t, d), [mix_s.reshape(db * t, wa + wb)], w_out16, db * t,
                  "finish_even_sample").reshape(db, t, d)

    rows_p = (ka.reshape(b, s, n_a, 2 * HEAD_DIM), va.reshape(b, s, n_a, 2 * HEAD_DIM),
              _heads_from_cols(kb, n_b), _heads_from_cols(vb, n_b))
    rows_s = (ka_s.reshape(db, t, n_a, 2 * HEAD_DIM), va_s.reshape(db, t, n_a, 2 * HEAD_DIM),
              kb_s.reshape(db, t, n_b, HEAD_DIM), vb_s.reshape(db, t, n_b, HEAD_DIM))
    return y_p, y_s, rows_p, rows_s


def _odd_layer(x_p, x_s, cache_k, cache_v, li, pt, g_norm, w_in, qk_g, w_out):
    b, s, d = x_p.shape
    db, t, _ = x_s.shape
    wc = w_out.shape[0]
    w_in16 = w_in.astype(BF16)
    w_out16 = w_out.astype(BF16)
    past = pt.shape[1] * LANES
    assert s % MOBA_BLOCK == 0

    def plan(k_kind, layout):
        return (("q_rope_f32", 0, wc, None), (k_kind, wc, wc, layout), ("kv_plain", 2 * wc, wc, layout),
                ("gate", 3 * wc, wc, None))

    tabs = _rope_tables(jnp.arange(s))
    q, k, k16, kmean, v, v16, sgate = _project(
        x_p.reshape(b * s, d), g_norm, w_in16, qk_g[0], qk_g[1], tabs, plan("k_rope_mean", "cols"),
        MOBA_BLOCK * math.gcd(s // MOBA_BLOCK, PROMPT_TM // MOBA_BLOCK), s, "proj_odd_prompt")
    r3 = lambda a: a.reshape(b, s, a.shape[-1])
    mix = _moba_prompt(r3(q), r3(k16), r3(v16), kmean.reshape(b, s // MOBA_BLOCK, wc), r3(sgate))
    y_p = _finish(x_p.reshape(b * s, d), [mix.reshape(b * s, wc)], w_out16, PROMPT_TM,
                  "finish_odd_prompt").reshape(b, s, d)

    assert t == 1
    tabs_s = _rope_tables(jnp.tile(past + jnp.arange(t), db))
    q_s, k_s, _, v_s, _, sgate_s = _project(
        x_s.reshape(db * t, d), g_norm, w_in16, qk_g[0], qk_g[1], tabs_s, plan("k_rope", "rows"), db * t, db * t,
        "proj_odd_sample")
    mix_s = _decode_odd(pt, q_s, k_s, v_s, sgate_s, cache_k, cache_v, li, math.gcd(pt.shape[1], PAGES_PER_STEP))
    y_s = _finish(x_s.reshape(db * t, d), [mix_s.reshape(db * t, wc)], w_out16, db * t,
                  "finish_odd_sample").reshape(db, t, d)

    n_c = wc // HEAD_DIM
    return (y_p, y_s, (_heads_from_cols(k, n_c), _heads_from_cols(v, n_c)),
            (k_s.reshape(db, t, n_c, HEAD_DIM), v_s.reshape(db, t, n_c, HEAD_DIM)))


def kernel(x_prompt, x_sample, cache_a_k, cache_a_v, cache_b_k, cache_b_v, cache_c_k, cache_c_v, page_table,
           norm_even, w_in_even, qk_norm_a, lambda_a, subln_a, w_out_even, norm_odd, w_in_odd, qk_norm_c,
           w_out_odd):
    depth = norm_even.shape[0] + norm_odd.shape[0]
    even_caches = [_rows_view(cache_a_k), _rows_view(cache_a_v), _cols_view(cache_b_k), _cols_view(cache_b_v)]
    odd_k, odd_v = _cols_view(cache_c_k), _cols_view(cache_c_v)
    y_p, y_s = x_prompt, x_sample
    ev_p, ev_s, od_p, od_s = [], [], [], []
    for layer in range(depth):
        i = layer // 2
        if layer % 2 == 0:
            y_p, y_s, rp, rs = _even_layer(y_p, y_s, even_caches, i, page_table, norm_even[i], w_in_even[i],
                                           qk_norm_a[i], lambda_a[i], subln_a[i], w_out_even[i], layer)
            ev_p.append(rp)
            ev_s.append(rs)
        else:
            y_p, y_s, rp, rs = _odd_layer(y_p, y_s, odd_k, odd_v, i, page_table, norm_odd[i], w_in_odd[i],
                                          qk_norm_c[i], w_out_odd[i])
            od_p.append(rp)
            od_s.append(rs)

    def stack(rows, j):
        return jnp.stack([r[j] for r in rows])

    return (y_p, y_s,
            stack(ev_p, 0), stack(ev_p, 1), stack(ev_p, 2), stack(ev_p, 3), stack(od_p, 0), stack(od_p, 1),
            stack(ev_s, 0), stack(ev_s, 1), stack(ev_s, 2), stack(ev_s, 3), stack(od_s, 0), stack(od_s, 1))
```

```python
import functools
import math

import jax
import jax.numpy as jnp
from jax import lax
from jax.experimental import pallas as pl
from jax.experimental.pallas import tpu as pltpu

HEAD_DIM = 64
ROT_DIM = HEAD_DIM // 4
ROPE_THETA = 500000.0
RMS_EPS = 1e-6
MOBA_BLOCK = 256
MOBA_TOPK = 3
LANES = 128
SUBLANES = 8
NEG = -1e30
SB_DEAD_LOG2 = 150.0
QK_SCALE = HEAD_DIM ** -0.5
Q_SOFTMAX_SCALE = QK_SCALE * math.log2(math.e)
ATTN_SLAB = 256
VMEM_LIMIT = 52 * 1024 * 1024

F32 = jnp.float32
BF16 = jnp.bfloat16


def _params(sem, vmem=VMEM_LIMIT):
    return pltpu.CompilerParams(dimension_semantics=sem, vmem_limit_bytes=vmem)


def _dot(a, b):
    return jnp.dot(a, b, preferred_element_type=F32)


def _dot_nt(a, b):
    return lax.dot_general(a, b, (((1,), (1,)), ((), ())), preferred_element_type=F32)


def _split_bf16(x):
    hi = x.astype(BF16)
    lo = (x - hi.astype(F32)).astype(BF16)
    return hi, lo


def _iota(shape, dim):
    return lax.broadcasted_iota(jnp.int32, shape, dim)


def _head_seg_matrix():
    return (_iota((LANES, LANES), 0) // HEAD_DIM == _iota((LANES, LANES), 1) // HEAD_DIM).astype(BF16)


def _norm_rope(x, g, c, s1, s2, seg):
    hi, lo = _split_bf16(x * x)
    ms = (_dot(hi, seg) + _dot(lo, seg)) * (1.0 / HEAD_DIM)
    xn = x * lax.rsqrt(ms + RMS_EPS) * g
    half = ROT_DIM // 2
    return xn * c + pltpu.roll(xn, LANES - half, 1) * s1 + pltpu.roll(xn, half, 1) * s2


def _proj_kernel(x_ref, gn_ref, w_ref, gq_ref, gk_ref, c_ref, s1_ref, s2_ref, *out_refs, plan):
    x = x_ref[...]
    ms = jnp.mean(x * x, axis=-1, keepdims=True)
    xn = (x * lax.rsqrt(ms + RMS_EPS) * gn_ref[...]).astype(BF16)
    seg = _head_seg_matrix()
    c, s1, s2 = c_ref[...], s1_ref[...], s2_ref[...]
    outs = list(out_refs)
    tm = x.shape[0]

    def store_rows(o_ref, layout, t, n_t, y):
        if layout == "rows":
            o_ref[:, t * LANES:(t + 1) * LANES] = y
        elif layout == "cols":
            o_ref[t * LANES:(t + 1) * LANES, :] = jnp.transpose(y)
        elif layout == "heads":
            o_ref[pl.ds(t, tm, stride=n_t), :] = y
        else:
            raise ValueError(layout)

    for kind, col0, width, layout in plan:
        h = _dot(xn, w_ref[:, col0:col0 + width])
        n_t = width // LANES
        if kind in ("q_rope_bf16", "q_rope_f32", "k_rope", "k_rope_mean"):
            g = gq_ref[...] if kind.startswith("q") else gk_ref[...]
            if kind == "q_rope_bf16":
                o_ref = outs.pop(0)
            elif kind == "q_rope_f32":
                o_ref = outs.pop(0)
            else:
                o_ref, o16_ref = outs.pop(0), outs.pop(0)
                mean_ref = outs.pop(0) if kind == "k_rope_mean" else None
            for t in range(n_t):
                sl = slice(t * LANES, (t + 1) * LANES)
                y = _norm_rope(h[:, sl], g, c, s1, s2, seg)
                if kind == "q_rope_bf16":
                    o_ref[:, sl] = (y * Q_SOFTMAX_SCALE).astype(BF16)
                elif kind == "q_rope_f32":
                    o_ref[:, sl] = y
                else:
                    store_rows(o_ref, layout, t, n_t, y)
                    o16_ref[:, sl] = y.astype(BF16)
                    if mean_ref is not None:
                        for j in range(tm // MOBA_BLOCK):
                            mean_ref[j, :, sl] = jnp.mean(y[j * MOBA_BLOCK:(j + 1) * MOBA_BLOCK], axis=0,
                                                          keepdims=True)
        elif kind == "q_plain_bf16":
            outs.pop(0)[...] = (h * Q_SOFTMAX_SCALE).astype(BF16)
        elif kind == "kv_plain":
            o_ref = outs.pop(0)
            for t in range(n_t):
                store_rows(o_ref, layout, t, n_t, h[:, t * LANES:(t + 1) * LANES])
            outs.pop(0)[...] = h.astype(BF16)
        elif kind == "gate":
            outs.pop(0)[...] = h / (1.0 + jnp.exp(-h))
        else:
            raise ValueError(kind)
    assert not outs


def _rope_tables(pos):
    half = ROT_DIM // 2
    inv = ROPE_THETA ** (-jnp.arange(0, ROT_DIM, 2, dtype=F32) / ROT_DIM)
    ang = pos.astype(F32)[:, None] * inv[None, :]
    cos, sin = jnp.cos(ang), jnp.sin(ang)
    t = pos.shape[0]
    z_half = jnp.zeros((t, half), F32)
    z_rest = jnp.zeros((t, HEAD_DIM - ROT_DIM), F32)
    c = jnp.concatenate([cos, cos, jnp.ones((t, HEAD_DIM - ROT_DIM), F32)], axis=-1)
    s1 = jnp.concatenate([-sin, z_half, z_rest], axis=-1)
    s2 = jnp.concatenate([z_half, sin, z_rest], axis=-1)
    rep = LANES // HEAD_DIM
    return tuple(jnp.tile(a, (1, rep)) for a in (c, s1, s2))


def _project(x, g_norm, w_bf16, gq, gk, tables, plan, tm, seq, name):
    m, d = x.shape
    n_tab = tables[0].shape[0] // tm
    tiles_per_seq = seq // tm
    out_shapes, out_specs = [], []
    for kind, _, width, layout in plan:
        row = pl.BlockSpec((tm, width), lambda i: (i, 0))
        if kind in ("q_rope_bf16", "q_plain_bf16"):
            out_shapes += [jax.ShapeDtypeStruct((m, width), BF16)]
            out_specs += [row]
        elif kind in ("q_rope_f32", "gate"):
            out_shapes += [jax.ShapeDtypeStruct((m, width), F32)]
            out_specs += [row]
        elif kind in ("k_rope", "kv_plain", "k_rope_mean"):
            if layout == "rows":
                out_shapes += [jax.ShapeDtypeStruct((m, width), F32)]
                out_specs += [row]
            elif layout == "cols":
                out_shapes += [jax.ShapeDtypeStruct((m // seq, width, seq), F32)]
                out_specs += [pl.BlockSpec((None, width, tm),
                                           lambda i: (i // tiles_per_seq, 0, i % tiles_per_seq))]
            else:
                n_t = width // LANES
                out_shapes += [jax.ShapeDtypeStruct((m * n_t, LANES), F32)]
                out_specs += [pl.BlockSpec((tm * n_t, LANES), lambda i: (i, 0))]
            out_shapes += [jax.ShapeDtypeStruct((m, width), BF16)]
            out_specs += [row]
            if kind == "k_rope_mean":
                assert tm % MOBA_BLOCK == 0
                out_shapes += [jax.ShapeDtypeStruct((m // MOBA_BLOCK, 1, width), F32)]
                out_specs += [pl.BlockSpec((tm // MOBA_BLOCK, 1, width), lambda i: (i, 0, 0))]
    tab_spec = pl.BlockSpec((tm, LANES), lambda i: (i % n_tab, 0))
    vec_d = pl.BlockSpec((1, d), lambda i: (0, 0))
    vec_l = pl.BlockSpec((1, LANES), lambda i: (0, 0))
    rep = LANES // HEAD_DIM
    return pl.pallas_call(
        functools.partial(_proj_kernel, plan=plan),
        grid=(m // tm,),
        in_specs=[pl.BlockSpec((tm, d), lambda i: (i, 0)), vec_d,
                  pl.BlockSpec(w_bf16.shape, lambda i: (0, 0), pipeline_mode=pl.Buffered(1)),
                  vec_l, vec_l, tab_spec, tab_spec, tab_spec],
        out_specs=out_specs,
        out_shape=out_shapes,
        compiler_params=_params(("parallel",)),
        name=name,
    )(x, g_norm.reshape(1, d), w_bf16, jnp.tile(gq, rep).reshape(1, LANES), jnp.tile(gk, rep).reshape(1, LANES),
      *tables)


def _finish_kernel(x_ref, w_ref, *refs):
    mix_refs, y_ref = refs[:-1], refs[-1]
    y = x_ref[...]
    row0 = 0
    for mix_ref in mix_refs:
        rows = mix_ref.shape[1]
        y = y + _dot(mix_ref[...], w_ref[row0:row0 + rows, :])
        row0 += rows
    y_ref[...] = y


def _finish(x, mix_parts, w_bf16, tm, name):
    m, d = x.shape
    kdim = w_bf16.shape[0]
    assert sum(p.shape[1] for p in mix_parts) == kdim
    return pl.pallas_call(
        _finish_kernel,
        grid=(m // tm,),
        in_specs=[pl.BlockSpec((tm, d), lambda i: (i, 0)), pl.BlockSpec((kdim, d), lambda i: (0, 0))]
        + [pl.BlockSpec((tm, p.shape[1]), lambda i: (i, 0)) for p in mix_parts],
        out_specs=pl.BlockSpec((tm, d), lambda i: (i, 0)),
        out_shape=jax.ShapeDtypeStruct((m, d), F32),
        compiler_params=_params(("parallel",)),
        name=name,
    )(x, w_bf16, *mix_parts)


def _lambda_value(lp, lam_init):
    a = jnp.sum(lp[0:1] * lp[1:2], axis=-1, keepdims=True)
    b = jnp.sum(lp[2:3] * lp[3:4], axis=-1, keepdims=True)
    return jnp.exp(a) - jnp.exp(b) + lam_init


def _softmax_tile(qq, k, v, carry, mask):
    m, l, acc = carry
    s = _dot_nt(qq, k)
    if mask is not None:
        s = jnp.where(mask, s, NEG)
    m_new = jnp.maximum(m, jnp.max(s, axis=-1, keepdims=True))
    alpha = jnp.exp2(m - m_new)
    p = jnp.exp2(s - m_new)
    l = alpha * l + jnp.sum(p, axis=-1, keepdims=True)
    acc = alpha * acc + _dot(p.astype(BF16), v)
    return m_new, l, acc


def _dot_tn(a, b):
    return lax.dot_general(a, b, (((0,), (0,)), ((), ())), preferred_element_type=F32)


def _softmax_tile_t(qq_t, k_ref, v_ref, start, n_keys, carry, mask):
    s, m_s, bias = _scores_t(qq_t, k_ref, start, n_keys, mask)
    return _absorb_t(v_ref, start, n_keys, s, m_s, bias, carry)


def _scores_t(qq_t, k_ref, start, n_keys, mask):
    s = _dot(k_ref[pl.ds(start, n_keys), :], qq_t)
    slabs = [s[c * ATTN_SLAB:(c + 1) * ATTN_SLAB] for c in range(n_keys // ATTN_SLAB)]
    if mask is None:
        return s, jnp.max(s, axis=0, keepdims=True), None
    if mask[0].dtype == jnp.bool_:
        s = jnp.concatenate([jnp.where(mk, x, NEG) for x, mk in zip(slabs, mask)], axis=0)
        return s, jnp.max(s, axis=0, keepdims=True), None
    m_s = None
    for x, row in zip(slabs, mask):
        m_c = jnp.max(x, axis=0, keepdims=True) + row
        m_s = m_c if m_s is None else jnp.maximum(m_s, m_c)
    return s, m_s, jnp.concatenate(mask, axis=0)


def _absorb_t(v_ref, start, n_keys, s, m_s, bias, carry):
    m, l, acc = carry
    m_new = jnp.maximum(m, m_s)
    alpha = jnp.exp2(m - m_new)
    if bias is None:
        p = jnp.exp2(s - m_new)
    else:
        shift = bias - m_new
        p = jnp.concatenate([jnp.exp2(s[c * ATTN_SLAB:(c + 1) * ATTN_SLAB] + shift[c:c + 1])
                             for c in range(n_keys // ATTN_SLAB)], axis=0)
    l = alpha * l + jnp.sum(p, axis=0, keepdims=True)
    v, p16 = v_ref[pl.ds(start, n_keys), :], p.astype(BF16)
    pv = _dot_tn(v, p16) if acc.shape[0] == LANES else _pair_values_t(v, p16)
    return m_new, l, alpha * acc + pv


def _pair_values_t(v, w16):
    half = w16.shape[1] // 2
    return jnp.concatenate([_dot_tn(v[:, :HEAD_DIM], w16[:, :half]), _dot_tn(v[:, HEAD_DIM:], w16[:, half:])],
                           axis=1)


def _merge_pair_t(x, tq):
    return jnp.transpose(jnp.concatenate([x[:, :tq], x[:, tq:]], axis=0))


def _pipelined_pairs(n_pairs, n_keys, qq_t, k_ref, v_ref, mask_of, s_refs, carry, head=None):
    s_a, s_b = s_refs

    def scores(t, s_ref):
        s, m_s, bias = _scores_t(qq_t, k_ref, pl.multiple_of(t * n_keys, n_keys), n_keys, mask_of(t))
        s_ref[...] = s
        return m_s, bias

    def absorb(t, s_ref, stats, carry):
        return _absorb_t(v_ref, pl.multiple_of(t * n_keys, n_keys), n_keys, s_ref[...], *stats, carry)

    def trip(u, state):
        m_a, carry = state
        m_b = scores(2 * u + 1, s_b)
        carry = absorb(2 * u, s_a, m_a, carry)
        m_a = scores(2 * u + 2, s_a)
        carry = absorb(2 * u + 1, s_b, m_b, carry)
        return m_a, carry

    def absorb_head(carry):
        if head is None:
            return carry
        s_h, stats_h, start_h = head
        return _absorb_t(v_ref, start_h, n_keys, s_h, *stats_h, carry)

    def run(carry):
        m_a = scores(n_pairs * 0, s_a)
        carry = absorb_head(carry)
        m_a, carry = lax.fori_loop(0, n_pairs - 1, trip, (m_a, carry))
        last = 2 * (n_pairs - 1)
        m_b = scores(last + 1, s_b)
        carry = absorb(last, s_a, m_a, carry)
        return absorb(last + 1, s_b, m_b, carry)

    return lax.cond(n_pairs > 0, run, absorb_head, carry)


def _causal_masks(k_start, q_start, tq, n_keys, r, strict):
    q_pos = q_start + _iota((ATTN_SLAB, r), 1) % tq
    masks = []
    for c in range(n_keys // ATTN_SLAB):
        k_pos = k_start + c * ATTN_SLAB + _iota((ATTN_SLAB, r), 0)
        masks.append(k_pos < q_pos if strict else k_pos <= q_pos)
    return masks


def _stack_pair_t(q_t):
    row = _iota(q_t.shape, 0)
    zero = jnp.zeros_like(q_t)
    return jnp.concatenate([jnp.where(row < HEAD_DIM, q_t, zero), jnp.where(row >= HEAD_DIM, q_t, zero)], axis=1)


def _diff_attn_kernel(lam_ref, subg_ref, q_ref, k_ref, v_ref, sg_ref, o_ref, s_a, s_b, *, tq, tk, lam_init):
    i = pl.program_id(2)
    r = 2 * tq
    qq_t = _stack_pair_t(jnp.transpose(q_ref[...].astype(F32))).astype(BF16)
    n_full = (i * tq) // tk
    half = tk // 2

    carry = (jnp.full((1, r), NEG, F32), jnp.zeros((1, r), F32), jnp.zeros((LANES, r), F32))
    start = pl.multiple_of(n_full * tk, tk)
    carry = _softmax_tile_t(qq_t, k_ref, v_ref, start, tk, carry,
                            _causal_masks(start, i * tq, tq, tk, r, strict=False))
    _, l, acc = _pipelined_pairs(n_full, half, qq_t, k_ref, v_ref, lambda t: None, (s_a, s_b), carry)

    a = acc / l
    lam = _lambda_value(lam_ref[...], lam_init)
    o = jnp.transpose(a[:, :tq] - lam * a[:, tq:])
    ms = jnp.mean(o * o, axis=-1, keepdims=True)
    o = o * lax.rsqrt(ms + RMS_EPS) * subg_ref[...] * (1.0 - lam_init)
    o_ref[...] = (o * sg_ref[...]).astype(BF16)


def _softplus2(z):
    return jnp.maximum(z, 0.0) + jnp.log2(1.0 + jnp.exp2(-jnp.abs(z)))


def _later_matrix(tk, keys_axis=1):
    a, b = _iota((tk, tk), 0), _iota((tk, tk), 1)
    return (a > b if keys_axis == 1 else b > a).astype(BF16)


def _sb_kernel(q_ref, k_ref, v_ref, sg_ref, o_ref, *, tq, tk):
    i = pl.program_id(2)
    r = 2 * tq
    qq_t = _stack_pair_t(jnp.transpose(q_ref[...].astype(F32))).astype(BF16)
    tri = _later_matrix(ATTN_SLAB, keys_axis=0)
    n_full = (i * tq) // tk
    n_slabs = tk // ATTN_SLAB

    def tile(st, c, acc, valid):
        parts = []
        for u in range(n_slabs):
            z = _dot(k_ref[pl.ds(st + u * ATTN_SLAB, ATTN_SLAB), :], qq_t)
            sp = _softplus2(z)
            log_keep = -sp if valid is None else jnp.where(valid[u], -sp, 0.0)
            later = _dot(tri, log_keep.astype(BF16))
            parts.append((z - sp + later, later[0:1, :] + log_keep[0:1, :]))
        for u in reversed(range(n_slabs)):
            e, whole = parts[u]
            w = jnp.exp2(e + c)
            if valid is not None:
                w = jnp.where(valid[u], w, 0.0)
            acc = acc + _pair_values_t(v_ref[pl.ds(st + u * ATTN_SLAB, ATTN_SLAB), :], w.astype(BF16))
            c = c + whole
        return c, acc

    start = pl.multiple_of(n_full * tk, tk)
    c, acc = tile(start, jnp.zeros((1, r), F32), jnp.zeros((HEAD_DIM, r), F32),
                  _causal_masks(start, i * tq, tq, tk, r, strict=True))

    def live(state):
        t, c_max, _, _ = state
        return jnp.logical_and(t < n_full, c_max > -SB_DEAD_LOG2)

    def full_tile(state):
        t, _, c, acc = state
        c, acc = tile(pl.multiple_of((n_full - 1 - t) * tk, tk), c, acc, None)
        return t + 1, jnp.max(c), c, acc

    _, _, _, acc = lax.while_loop(live, full_tile, (jnp.int32(0), jnp.max(c), c, acc))
    o_ref[...] = (_merge_pair_t(acc, tq) * sg_ref[...]).astype(BF16)


def _top_blocks(g, n_valid, axis=-1, picks=None):
    axis = axis % g.ndim
    blk = _iota(g.shape, axis)
    nb = g.shape[axis]
    g = jnp.where(blk < n_valid, g, -jnp.inf)
    sel = jnp.zeros(g.shape, jnp.bool_)
    for _ in range(min(MOBA_TOPK, nb)):
        mx = jnp.max(g, axis=axis, keepdims=True)
        idx = jnp.min(jnp.where(g == mx, blk, nb), axis=axis, keepdims=True)
        pick = blk == idx
        sel = jnp.logical_or(sel, pick)
        g = jnp.where(pick, -jnp.inf, g)
        if picks is not None:
            picks.append(idx)
    return jnp.logical_and(sel, blk < n_valid)


def _moba_kernel(q_ref, k_ref, v_ref, km_ref, sg_ref, o_ref, sel_s, s_a, s_b, *, tq, group):
    i = pl.program_id(2)
    r = 2 * tq
    blk = MOBA_BLOCK
    tile_blocks = tq // blk
    first = i * tile_blocks
    qf_t = _stack_pair_t(jnp.transpose(q_ref[...]))
    q_hi, q_lo = _split_bf16(qf_t)
    qq_t = (qf_t * Q_SOFTMAX_SCALE).astype(BF16)

    km_hi, km_lo = _split_bf16(km_ref[...])
    g = _dot(km_hi, q_hi) + _dot(km_hi, q_lo) + _dot(km_lo, q_hi)
    col = _iota((1, r), 1) % tq
    own = col // blk
    sel = _top_blocks(g, first + own, axis=0)
    block_id = _iota(sel.shape, 0)
    sel_s[...] = jnp.where(jnp.logical_and(sel, block_id < first), 0.0, NEG)

    start = pl.multiple_of(first * blk, tq)
    causal = _iota((blk, r), 0) <= col % blk
    masks = []
    for a in range(tile_blocks):
        picked = jnp.max(jnp.where(jnp.logical_and(sel, block_id == first + a), 1.0, 0.0), axis=0,
                         keepdims=True) > 0.0
        masks.append(jnp.logical_or(jnp.logical_and(own == a, causal), jnp.logical_and(own > a, picked)))
    carry = (jnp.full((1, r), NEG, F32), jnp.zeros((1, r), F32), jnp.zeros((HEAD_DIM, r), F32))
    s_own, m_own, _ = _scores_t(qq_t, k_ref, start, tq, masks)

    per_tile = group // 2
    assert per_tile * blk == tq

    def chosen(t):
        return [sel_s[pl.ds(t * per_tile + c, 1), :] for c in range(per_tile)]

    _, l, acc = _pipelined_pairs((first + group - 1) // group, tq, qq_t, k_ref, v_ref, chosen, (s_a, s_b), carry,
                                 head=(s_own, (m_own, None), start))
    o_ref[...] = (_merge_pair_t(acc / l, tq) * sg_ref[...]).astype(BF16)


def _attn_specs(tq, s, group0=0):
    q_spec = pl.BlockSpec((None, tq, LANES), lambda b, h, i: (b, i, h + group0))
    kv_spec = pl.BlockSpec((None, s, LANES), lambda b, h, i: (b, 0, h))
    return q_spec, kv_spec


def _diff_attn_prompt(q16, k16, v16, sgate, lam_p, subln_g, lam_init, tq, tk):
    b, s, w = q16.shape
    q_spec, kv_spec = _attn_specs(tq, s)
    return pl.pallas_call(
        functools.partial(_diff_attn_kernel, tq=tq, tk=tk, lam_init=lam_init),
        grid=(b, w // LANES, s // tq),
        in_specs=[pl.BlockSpec(lam_p.shape, lambda b, h, i: (0, 0)),
                  pl.BlockSpec((1, LANES), lambda b, h, i: (0, 0)),
                  q_spec, kv_spec, kv_spec, q_spec],
        out_specs=q_spec,
        out_shape=jax.ShapeDtypeStruct((b, s, w), BF16),
        scratch_shapes=[pltpu.VMEM((tk // 2, 2 * tq), F32)] * 2,
        compiler_params=_params(("parallel", "parallel", "arbitrary")),
        name="diff_attn_prompt",
    )(lam_p, subln_g.reshape(1, LANES), q16, k16, v16, sgate)


def _sb_prompt(q16, k16, v16, sgate, gate_group0, tq, tk):
    b, s, w = q16.shape
    q_spec, kv_spec = _attn_specs(tq, s)
    sg_spec = pl.BlockSpec((None, tq, LANES), lambda b, h, i: (b, i, h + gate_group0))
    return pl.pallas_call(
        functools.partial(_sb_kernel, tq=tq, tk=tk),
        grid=(b, w // LANES, s // tq),
        in_specs=[q_spec, kv_spec, kv_spec, sg_spec],
        out_specs=q_spec,
        out_shape=jax.ShapeDtypeStruct((b, s, w), BF16),
        compiler_params=_params(("parallel", "parallel", "arbitrary")),
        name="sb_prompt",
    )(q16, k16, v16, sgate)


def _moba_prompt(q, k16, v16, kmean, sgate):
    b, s, w = q.shape
    assert MOBA_BLOCK == ATTN_SLAB
    nb = s // MOBA_BLOCK
    tq = MOBA_BLOCK * math.gcd(nb, MOBA_TILE_BLOCKS)
    q_spec, kv_spec = _attn_specs(tq, s)
    group = math.gcd(nb, MOBA_GROUP)
    return pl.pallas_call(
        functools.partial(_moba_kernel, tq=tq, group=group),
        grid=(b, w // LANES, s // tq),
        in_specs=[q_spec, kv_spec, kv_spec,
                  pl.BlockSpec((None, nb, LANES), lambda b, h, i: (b, 0, h)), q_spec],
        out_specs=q_spec,
        out_shape=jax.ShapeDtypeStruct((b, s, w), BF16),
        scratch_shapes=[pltpu.VMEM((nb, 2 * tq), F32)]
        + [pltpu.VMEM((group // 2 * MOBA_BLOCK, 2 * tq), F32)] * 2,
        compiler_params=_params(("parallel", "parallel", "arbitrary")),
        name="moba_prompt",
    )(q, k16, v16, kmean, sgate)


def _row_heads(x_row, n_rows, lanes_per_row):
    w = x_row.shape[1]
    keep = _iota((n_rows, w), 1) // lanes_per_row == _iota((n_rows, w), 0)
    return jnp.where(keep, jnp.broadcast_to(x_row, (n_rows, w)), 0.0)


def _dup_rows(x, reps):
    n, w = x.shape
    row = _iota((n * reps, w), 0) // reps
    out = jnp.zeros((n * reps, w), x.dtype)
    for h in range(n):
        out = jnp.where(row == h, x[h:h + 1, :], out)
    return out


def _stack_rows(rows):
    n, w = len(rows), rows[0].shape[1]
    row = _iota((n, w), 0)
    out = jnp.zeros((n, w), rows[0].dtype)
    for h, r in enumerate(rows):
        out = jnp.where(row == h, r, out)
    return out


def _head_dots(qcol_ref, kt_ref, n_heads):
    rows = []
    for h in range(n_heads):
        sl = slice(h * HEAD_DIM, (h + 1) * HEAD_DIM)
        rows.append(jnp.sum(qcol_ref[sl, :] * kt_ref[sl, :], axis=0, keepdims=True))
    return _stack_rows(rows)


def _add_weighted(acc_ref, w, vt_refs, n_heads):
    for h in range(n_heads):
        sl = slice(h * HEAD_DIM, (h + 1) * HEAD_DIM)
        a = acc_ref[sl, :]
        for u, vt_ref in enumerate(vt_refs):
            a = a + w[u * n_heads + h:u * n_heads + h + 1, :] * vt_ref[sl, :]
        acc_ref[sl, :] = a


def _sb_weights_pages(zs, tri, c):
    n_h = zs[0].shape[0]
    z = jnp.concatenate(zs, axis=0)
    sp = _softplus2(z)
    log_keep = -sp
    hi, lo = _split_bf16(log_keep)
    rows = z.shape[0]
    both = _dot(jnp.concatenate([hi, lo], axis=0), tri)
    later = both[:rows] + both[rows:]
    whole = later[:, 0:1] + log_keep[:, 0:1]
    offsets = []
    for u in range(len(zs)):
        offsets.append(c)
        c = c + whole[u * n_h:(u + 1) * n_h]
    return jnp.exp2(z - sp + later + jnp.concatenate(offsets, axis=0)), c


def _lane_sums_as_row(x):
    hi, lo = _split_bf16(x)
    ones = jnp.ones((SUBLANES, LANES), BF16)
    return (_dot_nt(ones, hi) + _dot_nt(ones, lo))[0:1, :]


def _decode_even_kernel(pt_ref, lam_ref, subg_ref, qa_ref, kan_ref, van_ref, qb_ref, sg_ref, *refs,
                        pps, lam_init):
    del pt_ref
    page_refs = refs[:4 * pps]
    o_ref = refs[4 * pps]
    qa_s, m_s, l_s, acca_s, c_s, accb_s = refs[4 * pps + 1:]
    step = pl.program_id(1)
    n_ha = qa_ref.shape[0]
    n_a = 2 * n_ha
    n_b = accb_s.shape[0] // HEAD_DIM
    wa = n_ha * LANES

    @pl.when(step == 0)
    def _():
        rows = _dup_rows(qa_ref[...].astype(F32), 2)
        qa = jnp.where(_iota(rows.shape, 1) // HEAD_DIM == _iota(rows.shape, 0) % 2, rows, 0.0)
        qa_s[...] = qa
        m_s[...] = jnp.sum(qa * _dup_rows(kan_ref[...], 2), axis=-1, keepdims=True)
        l_s[...] = jnp.ones_like(l_s)
        acca_s[...] = _dup_rows(van_ref[...], 2)
        c_s[...] = jnp.zeros_like(c_s)
        accb_s[...] = jnp.zeros_like(accb_s)

    rows_a = pps * page_refs[0].shape[0]
    own_head = _iota((n_a, rows_a), 1) % n_ha == _iota((n_a, rows_a), 0) // 2
    ak = jnp.concatenate([page_refs[4 * u][...].astype(BF16) for u in range(pps)], axis=0)
    av = jnp.concatenate([page_refs[4 * u + 1][...].astype(BF16) for u in range(pps)], axis=0)
    m_s[...], l_s[...], acca_s[...] = _softmax_tile(qa_s[...].astype(BF16), ak, av,
                                                    (m_s[...], l_s[...], acca_s[...]), own_head)
    zs = [_head_dots(qb_ref, page_refs[4 * u + 2], n_b) for u in range(pps)]
    w, c_s[...] = _sb_weights_pages(zs, _later_matrix(LANES), c_s[...])
    _add_weighted(accb_s, w, [page_refs[4 * u + 3] for u in range(pps)], n_b)

    @pl.when(step == pl.num_programs(1) - 1)
    def _():
        lam = _lambda_value(lam_ref[...], lam_init)
        a = acca_s[...] / l_s[...]
        for h in range(n_ha):
            sl = slice(h * LANES, (h + 1) * LANES)
            x = a[2 * h:2 * h + 1, :] - lam * a[2 * h + 1:2 * h + 2, :]
            ms = jnp.mean(x * x, axis=-1, keepdims=True)
            x = x * lax.rsqrt(ms + RMS_EPS) * subg_ref[...] * (1.0 - lam_init)
            o_ref[:, sl] = (x * sg_ref[:, sl]).astype(BF16)
        o_ref[:, wa:] = (_lane_sums_as_row(accb_s[...]) * sg_ref[:, wa:]).astype(BF16)


def _page_specs(cache, li, n_pages, pps, reverse):
    r = cache.shape[2]

    def make(u):
        def index(b, s, pt):
            j = s * pps + u
            j = n_pages - 1 - j if reverse else j
            return (li, pt[b, j], 0, 0)
        return pl.BlockSpec((None, None, r, LANES), index)

    return [make(u) for u in range(pps)]


def _col_bcast(x):
    return jnp.broadcast_to(x.astype(F32)[:, :, None], x.shape + (LANES,))


def _decode_even(pt, lam_p, subln_g, lam_init, qa16, ka, va, qb16, sgate, caches, li, pps):
    db, n_pages = pt.shape
    wa, wb = qa16.shape[-1], qb16.shape[-1]
    n_ha = wa // LANES
    whole = lambda shp: pl.BlockSpec((None,) + shp, lambda b, s, pt: (b,) + (0,) * len(shp))
    specs_by_cache = [_page_specs(c, li, n_pages, pps, True) for c in caches]
    page_specs = [specs_by_cache[t][u] for u in range(pps) for t in range(4)]
    page_args = [caches[t] for u in range(pps) for t in range(4)]
    grid_spec = pltpu.PrefetchScalarGridSpec(
        num_scalar_prefetch=1,
        grid=(db, n_pages // pps),
        in_specs=[pl.BlockSpec(lam_p.shape, lambda b, s, pt: (0, 0)),
                  pl.BlockSpec((1, LANES), lambda b, s, pt: (0, 0)),
                  whole((n_ha, LANES)), whole((n_ha, LANES)), whole((n_ha, LANES)), whole((wb, LANES)),
                  whole((1, wa + wb))] + page_specs,
        out_specs=whole((1, wa + wb)),
        scratch_shapes=[pltpu.VMEM((2 * n_ha, LANES), F32),
                        pltpu.VMEM((2 * n_ha, 1), F32), pltpu.VMEM((2 * n_ha, 1), F32),
                        pltpu.VMEM((2 * n_ha, LANES), F32),
                        pltpu.VMEM((wb // HEAD_DIM, 1), F32), pltpu.VMEM((wb, LANES), F32)],
    )
    heads = lambda x: x.reshape(db, n_ha, LANES)
    return pl.pallas_call(
        functools.partial(_decode_even_kernel, pps=pps, lam_init=lam_init),
        grid_spec=grid_spec,
        out_shape=jax.ShapeDtypeStruct((db, 1, wa + wb), BF16),
        compiler_params=_params(("parallel", "arbitrary")),
        name="decode_even",
    )(pt, lam_p, subln_g.reshape(1, LANES), heads(qa16), heads(ka), heads(va), _col_bcast(qb16),
      sgate.reshape(db, 1, wa + wb), *page_args)


def _moba_scores_kernel(pt_ref, qcol_ref, qrow_ref, kn_ref, *refs, pps, n_blocks):
    del pt_ref
    page_refs = refs[:pps]
    p_ref, pnew_ref, ids_ref = refs[pps:pps + 3]
    s_s = refs[pps + 3]
    step = pl.program_id(1)
    n_h = qcol_ref.shape[0] // HEAD_DIM
    ppb = MOBA_BLOCK // LANES

    for u in range(pps):
        s_s[step * pps + u] = _head_dots(qcol_ref, page_refs[u], n_h)

    @pl.when(step == pl.num_programs(1) - 1)
    def _():
        lane = _iota((n_h, LANES), 1)
        g = jnp.zeros((n_h, LANES), F32)
        for n in range(n_blocks):
            tot = sum(s_s[n * ppb + t] for t in range(ppb))
            g = jnp.where(lane == n, jnp.sum(tot, axis=-1, keepdims=True) * (1.0 / MOBA_BLOCK), g)
        picks = []
        sel = _top_blocks(g, n_blocks, picks=picks)
        ids = jnp.zeros((n_h, LANES), jnp.int32)
        for r, idx in enumerate(picks):
            ids = jnp.where(lane == r, idx, ids)
        ids_ref[...] = ids
        chosen = [sel[:, n:n + 1] for n in range(n_blocks)]
        s_new = jnp.sum(_row_heads(qrow_ref[...], n_h, HEAD_DIM) * kn_ref[...], axis=-1, keepdims=True) * QK_SCALE
        m_lanes = jnp.full((n_h, LANES), NEG, F32)
        for j in range(n_blocks * ppb):
            m_lanes = jnp.maximum(m_lanes, jnp.where(chosen[j // ppb], s_s[j] * QK_SCALE, NEG))
        m = jnp.maximum(jnp.max(m_lanes, axis=-1, keepdims=True), s_new)
        l_lanes = jnp.zeros((n_h, LANES), F32)
        for j in range(n_blocks * ppb):
            e = jnp.where(chosen[j // ppb], jnp.exp(s_s[j] * QK_SCALE - m), 0.0)
            s_s[j] = e
            l_lanes = l_lanes + e
        e_new = jnp.exp(s_new - m)
        inv = 1.0 / (jnp.sum(l_lanes, axis=-1, keepdims=True) + e_new)
        for j in range(n_blocks * ppb):
            p_ref[j] = s_s[j] * inv
        pnew_ref[...] = jnp.broadcast_to(e_new * inv, pnew_ref.shape)


def _moba_values_kernel(pt_ref, ids_ref, p_ref, pnew_ref, vn_ref, sg_ref, *refs, n_sel, ppb, heads_per_step):
    del pt_ref
    slices, o_ref = refs[:-1], refs[-1]
    b, pair = pl.program_id(0), pl.program_id(1)
    accs = []
    for c in range(heads_per_step):
        h = pair * heads_per_step + c
        acc = jnp.zeros((HEAD_DIM, LANES), F32)
        for r in range(n_sel):
            blk = ids_ref[b, h * n_sel + r]
            for u in range(ppb):
                w = p_ref[blk * ppb + u, pl.ds(h, 1), :]
                acc = acc + w * slices[(c * n_sel + r) * ppb + u][...]
        accs.append(acc)
    o = _lane_sums_as_row(jnp.concatenate(accs, axis=0)) + pnew_ref[...] * vn_ref[...]
    o_ref[...] = (o * sg_ref[...]).astype(BF16)


def _decode_odd(pt, q, k_new, v_new, sgate, cache_k, cache_v, li, pps):
    db, n_pages = pt.shape
    w = q.shape[-1]
    n_h = w // HEAD_DIM
    page = LANES
    past = n_pages * page
    assert past % MOBA_BLOCK == 0 and past >= MOBA_BLOCK and MOBA_BLOCK % page == 0
    n_blocks = past // MOBA_BLOCK
    row = lambda wd: pl.BlockSpec((None, 1, wd), lambda b, s, pt: (b, 0, 0))
    r3 = lambda x: x.reshape(db, 1, x.shape[-1])
    assert n_blocks >= MOBA_TOPK
    per_head = pl.BlockSpec((None, n_h, LANES), lambda b, s, pt: (b, 0, 0))
    probs, p_new, ids = pl.pallas_call(
        functools.partial(_moba_scores_kernel, pps=pps, n_blocks=n_blocks),
        grid_spec=pltpu.PrefetchScalarGridSpec(
            num_scalar_prefetch=1,
            grid=(db, n_pages // pps),
            in_specs=[pl.BlockSpec((None, w, LANES), lambda b, s, pt: (b, 0, 0)), row(w), row(w)]
            + _page_specs(cache_k, li, n_pages, pps, False),
            out_specs=[pl.BlockSpec((None, n_pages, n_h, page), lambda b, s, pt: (b, 0, 0, 0)),
                       per_head, per_head],
            scratch_shapes=[pltpu.VMEM((n_pages, n_h, page), F32)],
        ),
        out_shape=[jax.ShapeDtypeStruct((db, n_pages, n_h, page), F32),
                   jax.ShapeDtypeStruct((db, n_h, LANES), F32),
                   jax.ShapeDtypeStruct((db, n_h, LANES), jnp.int32)],
        compiler_params=_params(("parallel", "arbitrary")),
        name="decode_moba_scores",
    )(pt, _col_bcast(q), r3(q), r3(k_new), *([cache_k] * pps))
    p_new_row = jnp.repeat(p_new[:, :, 0], HEAD_DIM, axis=-1).reshape(db, 1, w)

    ppb = MOBA_BLOCK // page
    heads_per_step = math.gcd(n_h, VALUE_HEADS_PER_STEP)
    step_lanes = heads_per_step * HEAD_DIM
    assert step_lanes % LANES == 0
    block_ids = ids[:, :, :MOBA_TOPK].reshape(db, n_h * MOBA_TOPK)

    def slice_spec(c, r, u):
        def index(b, pair, pt, bid):
            h = pair * heads_per_step + c
            return (li, pt[b, bid[b, h * MOBA_TOPK + r] * ppb + u], h, 0)
        return pl.BlockSpec((None, None, HEAD_DIM, LANES), index)

    slice_specs = [slice_spec(c, r, u) for c in range(heads_per_step) for r in range(MOBA_TOPK)
                   for u in range(ppb)]
    lanes = lambda: pl.BlockSpec((None, 1, step_lanes), lambda b, pair, pt, bid: (b, 0, pair))
    return pl.pallas_call(
        functools.partial(_moba_values_kernel, n_sel=MOBA_TOPK, ppb=ppb, heads_per_step=heads_per_step),
        grid_spec=pltpu.PrefetchScalarGridSpec(
            num_scalar_prefetch=2,
            grid=(db, n_h // heads_per_step),
            in_specs=[pl.BlockSpec((None, n_pages, n_h, page), lambda b, pair, pt, bid: (b, 0, 0, 0)),
                      lanes(), lanes(), lanes()] + slice_specs,
            out_specs=lanes(),
        ),
        out_shape=jax.ShapeDtypeStruct((db, 1, w), BF16),
        compiler_params=_params(("parallel", "arbitrary")),
        name="decode_moba_values",
    )(pt, block_ids, probs, p_new_row, r3(v_new), r3(sgate), *([cache_v] * len(slice_specs)))


PROMPT_TM = 512
ATTN_TQ = 512
ATTN_TK = 512
SB_TILE = 256
MOBA_GROUP = 4
MOBA_TILE_BLOCKS = 2
PAGES_PER_STEP = 16
VALUE_HEADS_PER_STEP = 8


def _rows_view(c):
    l, n, p, h, hd = c.shape
    assert p == LANES and hd == LANES
    return c.reshape(l, n, p * h, hd)


def _cols_view(c):
    l, n, p, h, hd = c.shape
    assert p == LANES and hd == HEAD_DIM
    return jnp.transpose(c, (0, 1, 3, 4, 2)).reshape(l, n, h * hd, p)


def _heads_from_cols(x, n_heads):
    b, w, s = x.shape
    return jnp.transpose(x.reshape(b, n_heads, w // n_heads, s), (0, 3, 1, 2))


def _even_layer(x_p, x_s, caches, li, pt, g_norm, w_in, qk_g, lam_p, subln_g, w_out, layer):
    lam_init = 0.8 - 0.6 * math.exp(-0.3 * layer)
    b, s, d = x_p.shape
    db, t, _ = x_s.shape
    wa = w_out.shape[0] // 2
    wb = wa
    def plan(wide, narrow):
        return (("q_rope_bf16", 0, wa, None), ("k_rope", wa, wa, wide), ("kv_plain", 2 * wa, wa, wide),
                ("q_plain_bf16", 3 * wa, wb, None), ("kv_plain", 3 * wa + wb, wb, narrow),
                ("kv_plain", 3 * wa + 2 * wb, wb, narrow), ("gate", 3 * wa + 3 * wb, wa + wb, None))

    w_in16 = w_in.astype(BF16)
    w_out16 = w_out.astype(BF16)
    past = pt.shape[1] * LANES
    n_a = wa // (2 * HEAD_DIM)
    n_b = wb // HEAD_DIM

    tabs = _rope_tables(jnp.arange(s))
    qa16, ka, ka16, va, va16, qb16, kb, kb16, vb, vb16, sgate = _project(
        x_p.reshape(b * s, d), g_norm, w_in16, qk_g[0], qk_g[1], tabs, plan("heads", "cols"), PROMPT_TM, s,
        "proj_even_prompt")
    r3 = lambda a: a.reshape(b, s, a.shape[-1])
    sg3 = r3(sgate)
    mix_a = _diff_attn_prompt(r3(qa16), r3(ka16), r3(va16), sg3, lam_p, subln_g, lam_init, ATTN_TQ, ATTN_TK)
    mix_b = _sb_prompt(r3(qb16), r3(kb16), r3(vb16), sg3, wa // LANES, SB_TILE, SB_TILE)
    y_p = _finish(x_p.reshape(b * s, d), [mix_a.reshape(b * s, wa), mix_b.reshape(b * s, wb)], w_out16, PROMPT_TM,
                  "finish_even_prompt").reshape(b, s, d)

    tabs_s = _rope_tables(jnp.tile(past + jnp.arange(t), db))
    qa16_s, ka_s, _, va_s, _, qb16_s, kb_s, _, vb_s, _, sgate_s = _project(
        x_s.reshape(db * t, d), g_norm, w_in16, qk_g[0], qk_g[1], tabs_s, plan("rows", "rows"), db * t, db * t,
        "proj_even_sample")
    assert t == 1
    mix_s = _decode_even(pt, lam_p, subln_g, lam_init, qa16_s, ka_s, va_s, qb16_s, sgate_s,
                         caches, li, math.gcd(pt.shape[1], PAGES_PER_STEP))
    y_s = _finish(x_s.reshape(db * t, d), [mix_s.reshape(db * t, wa + wb)], w_out16, db * t,
                  "finish_even_sample").reshape(db, t, d)

    rows_p = (ka.reshape(b, s, n_a, 2 * HEAD_DIM), va.reshape(b, s, n_a, 2 * HEAD_DIM),
              _heads_from_cols(kb, n_b), _heads_from_cols(vb, n_b))
    rows_s = (ka_s.reshape(db, t, n_a, 2 * HEAD_DIM), va_s.reshape(db, t, n_a, 2 * HEAD_DIM),
              kb_s.reshape(db, t, n_b, HEAD_DIM), vb_s.reshape(db, t, n_b, HEAD_DIM))
    return y_p, y_s, rows_p, rows_s


def _odd_layer(x_p, x_s, cache_k, cache_v, li, pt, g_norm, w_in, qk_g, w_out):
    b, s, d = x_p.shape
    db, t, _ = x_s.shape
    wc = w_out.shape[0]
    w_in16 = w_in.astype(BF16)
    w_out16 = w_out.astype(BF16)
    past = pt.shape[1] * LANES
    assert s % MOBA_BLOCK == 0

    def plan(k_kind, layout):
        return (("q_rope_f32", 0, wc, None), (k_kind, wc, wc, layout), ("kv_plain", 2 * wc, wc, layout),
                ("gate", 3 * wc, wc, None))

    tabs = _rope_tables(jnp.arange(s))
    q, k, k16, kmean, v, v16, sgate = _project(
        x_p.reshape(b * s, d), g_norm, w_in16, qk_g[0], qk_g[1], tabs, plan("k_rope_mean", "cols"),
        MOBA_BLOCK * math.gcd(s // MOBA_BLOCK, PROMPT_TM // MOBA_BLOCK), s, "proj_odd_prompt")
    r3 = lambda a: a.reshape(b, s, a.shape[-1])
    mix = _moba_prompt(r3(q), r3(k16), r3(v16), kmean.reshape(b, s // MOBA_BLOCK, wc), r3(sgate))
    y_p = _finish(x_p.reshape(b * s, d), [mix.reshape(b * s, wc)], w_out16, PROMPT_TM,
                  "finish_odd_prompt").reshape(b, s, d)

    assert t == 1
    tabs_s = _rope_tables(jnp.tile(past + jnp.arange(t), db))
    q_s, k_s, _, v_s, _, sgate_s = _project(
        x_s.reshape(db * t, d), g_norm, w_in16, qk_g[0], qk_g[1], tabs_s, plan("k_rope", "rows"), db * t, db * t,
        "proj_odd_sample")
    mix_s = _decode_odd(pt, q_s, k_s, v_s, sgate_s, cache_k, cache_v, li, math.gcd(pt.shape[1], PAGES_PER_STEP))
    y_s = _finish(x_s.reshape(db * t, d), [mix_s.reshape(db * t, wc)], w_out16, db * t,
                  "finish_odd_sample").reshape(db, t, d)

    n_c = wc // HEAD_DIM
    return (y_p, y_s, (_heads_from_cols(k, n_c), _heads_from_cols(v, n_c)),
            (k_s.reshape(db, t, n_c, HEAD_DIM), v_s.reshape(db, t, n_c, HEAD_DIM)))


def kernel(x_prompt, x_sample, cache_a_k, cache_a_v, cache_b_k, cache_b_v, cache_c_k, cache_c_v, page_table,
           norm_even, w_in_even, qk_norm_a, lambda_a, subln_a, w_out_even, norm_odd, w_in_odd, qk_norm_c,
           w_out_odd):
    depth = norm_even.shape[0] + norm_odd.shape[0]
    even_caches = [_rows_view(cache_a_k), _rows_view(cache_a_v), _cols_view(cache_b_k), _cols_view(cache_b_v)]
    odd_k, odd_v = _cols_view(cache_c_k), _cols_view(cache_c_v)
    y_p, y_s = x_prompt, x_sample
    ev_p, ev_s, od_p, od_s = [], [], [], []
    for layer in range(depth):
        i = layer // 2
        if layer % 2 == 0:
            y_p, y_s, rp, rs = _even_layer(y_p, y_s, even_caches, i, page_table, norm_even[i], w_in_even[i],
                                           qk_norm_a[i], lambda_a[i], subln_a[i], w_out_even[i], layer)
            ev_p.append(rp)
            ev_s.append(rs)
        else:
            y_p, y_s, rp, rs = _odd_layer(y_p, y_s, odd_k, odd_v, i, page_table, norm_odd[i], w_in_odd[i],
                                          qk_norm_c[i], w_out_odd[i])
            od_p.append(rp)
            od_s.append(rs)

    def stack(rows, j):
        return jnp.stack([r[j] for r in rows])

    return (y_p, y_s,
            stack(ev_p, 0), stack(ev_p, 1), stack(ev_p, 2), stack(ev_p, 3), stack(od_p, 0), stack(od_p, 1),
            stack(ev_s, 0), stack(ev_s, 1), stack(ev_s, 2), stack(ev_s, 3), stack(od_s, 0), stack(od_s, 1))
```

```python
import functools
import math

import jax
import jax.numpy as jnp
from jax import lax
from jax.experimental import pallas as pl
from jax.experimental.pallas import tpu as pltpu

HEAD_DIM = 64
ROT_DIM = HEAD_DIM // 4
ROPE_THETA = 500000.0
RMS_EPS = 1e-6
MOBA_BLOCK = 256
MOBA_TOPK = 3
LANES = 128
SUBLANES = 8
NEG = -1e30
SB_DEAD_LOG2 = 150.0
QK_SCALE = HEAD_DIM ** -0.5
Q_SOFTMAX_SCALE = QK_SCALE * math.log2(math.e)
ATTN_SLAB = 256
VMEM_LIMIT = 52 * 1024 * 1024

F32 = jnp.float32
BF16 = jnp.bfloat16


def _params(sem, vmem=VMEM_LIMIT):
    return pltpu.CompilerParams(dimension_semantics=sem, vmem_limit_bytes=vmem)


def _dot(a, b):
    return jnp.dot(a, b, preferred_element_type=F32)


def _dot_nt(a, b):
    return lax.dot_general(a, b, (((1,), (1,)), ((), ())), preferred_element_type=F32)


def _split_bf16(x):
    hi = x.astype(BF16)
    lo = (x - hi.astype(F32)).astype(BF16)
    return hi, lo


def _iota(shape, dim):
    return lax.broadcasted_iota(jnp.int32, shape, dim)


def _head_seg_matrix():
    return (_iota((LANES, LANES), 0) // HEAD_DIM == _iota((LANES, LANES), 1) // HEAD_DIM).astype(BF16)


def _norm_rope(x, g, c, s1, s2, seg):
    hi, lo = _split_bf16(x * x)
    ms = (_dot(hi, seg) + _dot(lo, seg)) * (1.0 / HEAD_DIM)
    xn = x * lax.rsqrt(ms + RMS_EPS) * g
    half = ROT_DIM // 2
    return xn * c + pltpu.roll(xn, LANES - half, 1) * s1 + pltpu.roll(xn, half, 1) * s2


def _proj_kernel(x_ref, gn_ref, w_ref, gq_ref, gk_ref, c_ref, s1_ref, s2_ref, *out_refs, plan):
    x = x_ref[...]
    ms = jnp.mean(x * x, axis=-1, keepdims=True)
    xn = (x * lax.rsqrt(ms + RMS_EPS) * gn_ref[...]).astype(BF16)
    seg = _head_seg_matrix()
    c, s1, s2 = c_ref[...], s1_ref[...], s2_ref[...]
    outs = list(out_refs)
    tm = x.shape[0]

    def store_rows(o_ref, layout, t, n_t, y):
        if layout == "rows":
            o_ref[:, t * LANES:(t + 1) * LANES] = y
        elif layout == "cols":
            o_ref[t * LANES:(t + 1) * LANES, :] = jnp.transpose(y)
        elif layout == "heads":
            o_ref[pl.ds(t, tm, stride=n_t), :] = y
        else:
            raise ValueError(layout)

    for kind, col0, width, layout in plan:
        h = _dot(xn, w_ref[:, col0:col0 + width])
        n_t = width // LANES
        if kind in ("q_rope_bf16", "q_rope_f32", "k_rope", "k_rope_mean"):
            g = gq_ref[...] if kind.startswith("q") else gk_ref[...]
            if kind == "q_rope_bf16":
                o_ref = outs.pop(0)
            elif kind == "q_rope_f32":
                o_ref = outs.pop(0)
            else:
                o_ref, o16_ref = outs.pop(0), outs.pop(0)
                mean_ref = outs.pop(0) if kind == "k_rope_mean" else None
            for t in range(n_t):
                sl = slice(t * LANES, (t + 1) * LANES)
                y = _norm_rope(h[:, sl], g, c, s1, s2, seg)
                if kind == "q_rope_bf16":
                    o_ref[:, sl] = (y * Q_SOFTMAX_SCALE).astype(BF16)
                elif kind == "q_rope_f32":
                    o_ref[:, sl] = y
                else:
                    store_rows(o_ref, layout, t, n_t, y)
                    o16_ref[:, sl] = y.astype(BF16)
                    if mean_ref is not None:
                        for j in range(tm // MOBA_BLOCK):
                            mean_ref[j, :, sl] = jnp.mean(y[j * MOBA_BLOCK:(j + 1) * MOBA_BLOCK], axis=0,
                                                          keepdims=True)
        elif kind == "q_plain_bf16":
            outs.pop(0)[...] = (h * Q_SOFTMAX_SCALE).astype(BF16)
        elif kind == "kv_plain":
            o_ref = outs.pop(0)
            for t in range(n_t):
                store_rows(o_ref, layout, t, n_t, h[:, t * LANES:(t + 1) * LANES])
            outs.pop(0)[...] = h.astype(BF16)
        elif kind == "gate":
            outs.pop(0)[...] = h / (1.0 + jnp.exp(-h))
        else:
            raise ValueError(kind)
    assert not outs


def _rope_tables(pos):
    half = ROT_DIM // 2
    inv = ROPE_THETA ** (-jnp.arange(0, ROT_DIM, 2, dtype=F32) / ROT_DIM)
    ang = pos.astype(F32)[:, None] * inv[None, :]
    cos, sin = jnp.cos(ang), jnp.sin(ang)
    t = pos.shape[0]
    z_half = jnp.zeros((t, half), F32)
    z_rest = jnp.zeros((t, HEAD_DIM - ROT_DIM), F32)
    c = jnp.concatenate([cos, cos, jnp.ones((t, HEAD_DIM - ROT_DIM), F32)], axis=-1)
    s1 = jnp.concatenate([-sin, z_half, z_rest], axis=-1)
    s2 = jnp.concatenate([z_half, sin, z_rest], axis=-1)
    rep = LANES // HEAD_DIM
    return tuple(jnp.tile(a, (1, rep)) for a in (c, s1, s2))


def _project(x, g_norm, w_bf16, gq, gk, tables, plan, tm, seq, name):
    m, d = x.shape
    n_tab = tables[0].shape[0] // tm
    tiles_per_seq = seq // tm
    out_shapes, out_specs = [], []
    for kind, _, width, layout in plan:
        row = pl.BlockSpec((tm, width), lambda i: (i, 0))
        if kind in ("q_rope_bf16", "q_plain_bf16"):
            out_shapes += [jax.ShapeDtypeStruct((m, width), BF16)]
            out_specs += [row]
        elif kind in ("q_rope_f32", "gate"):
            out_shapes += [jax.ShapeDtypeStruct((m, width), F32)]
            out_specs += [row]
        elif kind in ("k_rope", "kv_plain", "k_rope_mean"):
            if layout == "rows":
                out_shapes += [jax.ShapeDtypeStruct((m, width), F32)]
                out_specs += [row]
            elif layout == "cols":
                out_shapes += [jax.ShapeDtypeStruct((m // seq, width, seq), F32)]
                out_specs += [pl.BlockSpec((None, width, tm),
                                           lambda i: (i // tiles_per_seq, 0, i % tiles_per_seq))]
            else:
                n_t = width // LANES
                out_shapes += [jax.ShapeDtypeStruct((m * n_t, LANES), F32)]
                out_specs += [pl.BlockSpec((tm * n_t, LANES), lambda i: (i, 0))]
            out_shapes += [jax.ShapeDtypeStruct((m, width), BF16)]
            out_specs += [row]
            if kind == "k_rope_mean":
                assert tm % MOBA_BLOCK == 0
                out_shapes += [jax.ShapeDtypeStruct((m // MOBA_BLOCK, 1, width), F32)]
                out_specs += [pl.BlockSpec((tm // MOBA_BLOCK, 1, width), lambda i: (i, 0, 0))]
    tab_spec = pl.BlockSpec((tm, LANES), lambda i: (i % n_tab, 0))
    vec_d = pl.BlockSpec((1, d), lambda i: (0, 0))
    vec_l = pl.BlockSpec((1, LANES), lambda i: (0, 0))
    rep = LANES // HEAD_DIM
    return pl.pallas_call(
        functools.partial(_proj_kernel, plan=plan),
        grid=(m // tm,),
        in_specs=[pl.BlockSpec((tm, d), lambda i: (i, 0)), vec_d,
                  pl.BlockSpec(w_bf16.shape, lambda i: (0, 0), pipeline_mode=pl.Buffered(1)),
                  vec_l, vec_l, tab_spec, tab_spec, tab_spec],
        out_specs=out_specs,
        out_shape=out_shapes,
        compiler_params=_params(("parallel",)),
        name=name,
    )(x, g_norm.reshape(1, d), w_bf16, jnp.tile(gq, rep).reshape(1, LANES), jnp.tile(gk, rep).reshape(1, LANES),
      *tables)


def _finish_kernel(x_ref, w_ref, *refs):
    mix_refs, y_ref = refs[:-1], refs[-1]
    y = x_ref[...]
    row0 = 0
    for mix_ref in mix_refs:
        rows = mix_ref.shape[1]
        y = y + _dot(mix_ref[...], w_ref[row0:row0 + rows, :])
        row0 += rows
    y_ref[...] = y


def _finish(x, mix_parts, w_bf16, tm, name):
    m, d = x.shape
    kdim = w_bf16.shape[0]
    assert sum(p.shape[1] for p in mix_parts) == kdim
    return pl.pallas_call(
        _finish_kernel,
        grid=(m // tm,),
        in_specs=[pl.BlockSpec((tm, d), lambda i: (i, 0)), pl.BlockSpec((kdim, d), lambda i: (0, 0))]
        + [pl.BlockSpec((tm, p.shape[1]), lambda i: (i, 0)) for p in mix_parts],
        out_specs=pl.BlockSpec((tm, d), lambda i: (i, 0)),
        out_shape=jax.ShapeDtypeStruct((m, d), F32),
        compiler_params=_params(("parallel",)),
        name=name,
    )(x, w_bf16, *mix_parts)


def _lambda_value(lp, lam_init):
    a = jnp.sum(lp[0:1] * lp[1:2], axis=-1, keepdims=True)
    b = jnp.sum(lp[2:3] * lp[3:4], axis=-1, keepdims=True)
    return jnp.exp(a) - jnp.exp(b) + lam_init


def _softmax_tile(qq, k, v, carry, mask):
    m, l, acc = carry
    s = _dot_nt(qq, k)
    if mask is not None:
        s = jnp.where(mask, s, NEG)
    m_new = jnp.maximum(m, jnp.max(s, axis=-1, keepdims=True))
    alpha = jnp.exp2(m - m_new)
    p = jnp.exp2(s - m_new)
    l = alpha * l + jnp.sum(p, axis=-1, keepdims=True)
    acc = alpha * acc + _dot(p.astype(BF16), v)
    return m_new, l, acc


def _dot_tn(a, b):
    return lax.dot_general(a, b, (((0,), (0,)), ((), ())), preferred_element_type=F32)


def _softmax_tile_t(qq_t, k_ref, v_ref, start, n_keys, carry, mask):
    s, m_s, bias = _scores_t(qq_t, k_ref, start, n_keys, mask)
    return _absorb_t(v_ref, start, n_keys, s, m_s, bias, carry)


def _scores_t(qq_t, k_ref, start, n_keys, mask):
    s = _dot(k_ref[pl.ds(start, n_keys), :], qq_t)
    slabs = [s[c * ATTN_SLAB:(c + 1) * ATTN_SLAB] for c in range(n_keys // ATTN_SLAB)]
    if mask is None:
        return s, jnp.max(s, axis=0, keepdims=True), None
    if mask[0].dtype == jnp.bool_:
        s = jnp.concatenate([jnp.where(mk, x, NEG) for x, mk in zip(slabs, mask)], axis=0)
        return s, jnp.max(s, axis=0, keepdims=True), None
    m_s = None
    for x, row in zip(slabs, mask):
        m_c = jnp.max(x, axis=0, keepdims=True) + row
        m_s = m_c if m_s is None else jnp.maximum(m_s, m_c)
    return s, m_s, jnp.concatenate(mask, axis=0)


def _absorb_t(v_ref, start, n_keys, s, m_s, bias, carry):
    m, l, acc = carry
    m_new = jnp.maximum(m, m_s)
    alpha = jnp.exp2(m - m_new)
    if bias is None:
        p = jnp.exp2(s - m_new)
    else:
        shift = bias - m_new
        p = jnp.concatenate([jnp.exp2(s[c * ATTN_SLAB:(c + 1) * ATTN_SLAB] + shift[c:c + 1])
                             for c in range(n_keys // ATTN_SLAB)], axis=0)
    l = alpha * l + jnp.sum(p, axis=0, keepdims=True)
    v, p16 = v_ref[pl.ds(start, n_keys), :], p.astype(BF16)
    pv = _dot_tn(v, p16) if acc.shape[0] == LANES else _pair_values_t(v, p16)
    return m_new, l, alpha * acc + pv


def _pair_values_t(v, w16):
    half = w16.shape[1] // 2
    return jnp.concatenate([_dot_tn(v[:, :HEAD_DIM], w16[:, :half]), _dot_tn(v[:, HEAD_DIM:], w16[:, half:])],
                           axis=1)


def _merge_pair_t(x, tq):
    return jnp.transpose(jnp.concatenate([x[:, :tq], x[:, tq:]], axis=0))


def _pipelined_pairs(n_pairs, n_keys, qq_t, k_ref, v_ref, mask_of, s_refs, carry, head=None):
    s_a, s_b = s_refs

    def scores(t, s_ref):
        s, m_s, bias = _scores_t(qq_t, k_ref, pl.multiple_of(t * n_keys, n_keys), n_keys, mask_of(t))
        s_ref[...] = s
        return m_s, bias

    def absorb(t, s_ref, stats, carry):
        return _absorb_t(v_ref, pl.multiple_of(t * n_keys, n_keys), n_keys, s_ref[...], *stats, carry)

    def trip(u, state):
        m_a, carry = state
        m_b = scores(2 * u + 1, s_b)
        carry = absorb(2 * u, s_a, m_a, carry)
        m_a = scores(2 * u + 2, s_a)
        carry = absorb(2 * u + 1, s_b, m_b, carry)
        return m_a, carry

    def absorb_head(carry):
        if head is None:
            return carry
        s_h, stats_h, start_h = head
        return _absorb_t(v_ref, start_h, n_keys, s_h, *stats_h, carry)

    def run(carry):
        m_a = scores(n_pairs * 0, s_a)
        carry = absorb_head(carry)
        m_a, carry = lax.fori_loop(0, n_pairs - 1, trip, (m_a, carry))
        last = 2 * (n_pairs - 1)
        m_b = scores(last + 1, s_b)
        carry = absorb(last, s_a, m_a, carry)
        return absorb(last + 1, s_b, m_b, carry)

    return lax.cond(n_pairs > 0, run, absorb_head, carry)


def _causal_masks(k_start, q_start, tq, n_keys, r, strict):
    q_pos = q_start + _iota((ATTN_SLAB, r), 1) % tq
    masks = []
    for c in range(n_keys // ATTN_SLAB):
        k_pos = k_start + c * ATTN_SLAB + _iota((ATTN_SLAB, r), 0)
        masks.append(k_pos < q_pos if strict else k_pos <= q_pos)
    return masks


def _stack_pair_t(q_t):
    row = _iota(q_t.shape, 0)
    zero = jnp.zeros_like(q_t)
    return jnp.concatenate([jnp.where(row < HEAD_DIM, q_t, zero), jnp.where(row >= HEAD_DIM, q_t, zero)], axis=1)


def _diff_attn_kernel(lam_ref, subg_ref, q_ref, k_ref, v_ref, sg_ref, o_ref, s_a, s_b, *, tq, tk, lam_init):
    i = pl.program_id(2)
    r = 2 * tq
    qq_t = _stack_pair_t(jnp.transpose(q_ref[...].astype(F32))).astype(BF16)
    n_full = (i * tq) // tk
    half = tk // 2

    carry = (jnp.full((1, r), NEG, F32), jnp.zeros((1, r), F32), jnp.zeros((LANES, r), F32))
    start = pl.multiple_of(n_full * tk, tk)
    carry = _softmax_tile_t(qq_t, k_ref, v_ref, start, tk, carry,
                            _causal_masks(start, i * tq, tq, tk, r, strict=False))
    _, l, acc = _pipelined_pairs(n_full, half, qq_t, k_ref, v_ref, lambda t: None, (s_a, s_b), carry)

    a = acc / l
    lam = _lambda_value(lam_ref[...], lam_init)
    o = jnp.transpose(a[:, :tq] - lam * a[:, tq:])
    ms = jnp.mean(o * o, axis=-1, keepdims=True)
    o = o * lax.rsqrt(ms + RMS_EPS) * subg_ref[...] * (1.0 - lam_init)
    o_ref[...] = (o * sg_ref[...]).astype(BF16)


def _softplus2(z):
    return jnp.maximum(z, 0.0) + jnp.log2(1.0 + jnp.exp2(-jnp.abs(z)))


def _later_matrix(tk, keys_axis=1):
    a, b = _iota((tk, tk), 0), _iota((tk, tk), 1)
    return (a > b if keys_axis == 1 else b > a).astype(BF16)


def _sb_kernel(q_ref, k_ref, v_ref, sg_ref, o_ref, *, tq, tk):
    i = pl.program_id(2)
    r = 2 * tq
    qq_t = _stack_pair_t(jnp.transpose(q_ref[...].astype(F32))).astype(BF16)
    tri = _later_matrix(ATTN_SLAB, keys_axis=0)
    assert tq == tk == ATTN_SLAB

    def tile(st, n_slabs, c, acc, valid):
        parts = []
        for u in range(n_slabs):
            z = _dot(k_ref[pl.ds(st + u * ATTN_SLAB, ATTN_SLAB), :], qq_t)
            sp = _softplus2(z)
            log_keep = -sp if valid is None else jnp.where(valid[u], -sp, 0.0)
            later = _dot(tri, log_keep.astype(BF16))
            parts.append((z - sp + later, later[0:1, :] + log_keep[0:1, :]))
        for u in reversed(range(n_slabs)):
            e, whole = parts[u]
            w = jnp.exp2(e + c)
            if valid is not None:
                w = jnp.where(valid[u], w, 0.0)
            acc = acc + _pair_values_t(v_ref[pl.ds(st + u * ATTN_SLAB, ATTN_SLAB), :], w.astype(BF16))
            c = c + whole
        return c, acc

    n_rest = jnp.maximum(i - 1, 0)
    start = pl.multiple_of(n_rest * tk, tk)
    c, acc = tile(start, 2, jnp.zeros((1, r), F32), jnp.zeros((HEAD_DIM, r), F32),
                  _causal_masks(start, i * tq, tq, 2 * tk, r, strict=True))

    def live(state):
        t, c_max, _, _ = state
        return jnp.logical_and(t < n_rest, c_max > -SB_DEAD_LOG2)

    def full_tile(state):
        t, _, c, acc = state
        c, acc = tile(pl.multiple_of((n_rest - 1 - t) * tk, tk), 1, c, acc, None)
        return t + 1, jnp.max(c), c, acc

    _, _, _, acc = lax.while_loop(live, full_tile, (jnp.int32(0), jnp.max(c), c, acc))
    o_ref[...] = (_merge_pair_t(acc, tq) * sg_ref[...]).astype(BF16)


def _top_blocks(g, n_valid, axis=-1, picks=None):
    axis = axis % g.ndim
    blk = _iota(g.shape, axis)
    nb = g.shape[axis]
    g = jnp.where(blk < n_valid, g, -jnp.inf)
    sel = jnp.zeros(g.shape, jnp.bool_)
    for _ in range(min(MOBA_TOPK, nb)):
        mx = jnp.max(g, axis=axis, keepdims=True)
        idx = jnp.min(jnp.where(g == mx, blk, nb), axis=axis, keepdims=True)
        pick = blk == idx
        sel = jnp.logical_or(sel, pick)
        g = jnp.where(pick, -jnp.inf, g)
        if picks is not None:
            picks.append(idx)
    return jnp.logical_and(sel, blk < n_valid)


def _moba_kernel(q_ref, k_ref, v_ref, km_ref, sg_ref, o_ref, sel_s, s_a, s_b, *, tq, group):
    i = pl.program_id(2)
    r = 2 * tq
    blk = MOBA_BLOCK
    tile_blocks = tq // blk
    first = i * tile_blocks
    qf_t = _stack_pair_t(jnp.transpose(q_ref[...]))
    q_hi, q_lo = _split_bf16(qf_t)
    qq_t = (qf_t * Q_SOFTMAX_SCALE).astype(BF16)

    km_hi, km_lo = _split_bf16(km_ref[...])
    g = _dot(km_hi, q_hi) + _dot(km_hi, q_lo) + _dot(km_lo, q_hi)
    col = _iota((1, r), 1) % tq
    own = col // blk
    sel = _top_blocks(g, first + own, axis=0)
    block_id = _iota(sel.shape, 0)
    sel_s[...] = jnp.where(jnp.logical_and(sel, block_id < first), 0.0, NEG)

    start = pl.multiple_of(first * blk, tq)
    causal = _iota((blk, r), 0) <= col % blk
    masks = []
    for a in range(tile_blocks):
        picked = jnp.max(jnp.where(jnp.logical_and(sel, block_id == first + a), 1.0, 0.0), axis=0,
                         keepdims=True) > 0.0
        masks.append(jnp.logical_or(jnp.logical_and(own == a, causal), jnp.logical_and(own > a, picked)))
    carry = (jnp.full((1, r), NEG, F32), jnp.zeros((1, r), F32), jnp.zeros((HEAD_DIM, r), F32))
    s_own, m_own, _ = _scores_t(qq_t, k_ref, start, tq, masks)

    per_tile = group // 2
    assert per_tile * blk == tq

    def chosen(t):
        return [sel_s[pl.ds(t * per_tile + c, 1), :] for c in range(per_tile)]

    _, l, acc = _pipelined_pairs((first + group - 1) // group, tq, qq_t, k_ref, v_ref, chosen, (s_a, s_b), carry,
                                 head=(s_own, (m_own, None), start))
    o_ref[...] = (_merge_pair_t(acc / l, tq) * sg_ref[...]).astype(BF16)


def _attn_specs(tq, s, group0=0):
    q_spec = pl.BlockSpec((None, tq, LANES), lambda b, h, i: (b, i, h + group0))
    kv_spec = pl.BlockSpec((None, s, LANES), lambda b, h, i: (b, 0, h))
    return q_spec, kv_spec


def _diff_attn_prompt(q16, k16, v16, sgate, lam_p, subln_g, lam_init, tq, tk):
    b, s, w = q16.shape
    q_spec, kv_spec = _attn_specs(tq, s)
    return pl.pallas_call(
        functools.partial(_diff_attn_kernel, tq=tq, tk=tk, lam_init=lam_init),
        grid=(b, w // LANES, s // tq),
        in_specs=[pl.BlockSpec(lam_p.shape, lambda b, h, i: (0, 0)),
                  pl.BlockSpec((1, LANES), lambda b, h, i: (0, 0)),
                  q_spec, kv_spec, kv_spec, q_spec],
        out_specs=q_spec,
        out_shape=jax.ShapeDtypeStruct((b, s, w), BF16),
        scratch_shapes=[pltpu.VMEM((tk // 2, 2 * tq), F32)] * 2,
        compiler_params=_params(("parallel", "parallel", "arbitrary")),
        name="diff_attn_prompt",
    )(lam_p, subln_g.reshape(1, LANES), q16, k16, v16, sgate)


def _sb_prompt(q16, k16, v16, sgate, gate_group0, tq, tk):
    b, s, w = q16.shape
    q_spec, kv_spec = _attn_specs(tq, s)
    sg_spec = pl.BlockSpec((None, tq, LANES), lambda b, h, i: (b, i, h + gate_group0))
    return pl.pallas_call(
        functools.partial(_sb_kernel, tq=tq, tk=tk),
        grid=(b, w // LANES, s // tq),
        in_specs=[q_spec, kv_spec, kv_spec, sg_spec],
        out_specs=q_spec,
        out_shape=jax.ShapeDtypeStruct((b, s, w), BF16),
        compiler_params=_params(("parallel", "parallel", "arbitrary")),
        name="sb_prompt",
    )(q16, k16, v16, sgate)


def _moba_prompt(q, k16, v16, kmean, sgate):
    b, s, w = q.shape
    assert MOBA_BLOCK == ATTN_SLAB
    nb = s // MOBA_BLOCK
    tq = MOBA_BLOCK * math.gcd(nb, MOBA_TILE_BLOCKS)
    q_spec, kv_spec = _attn_specs(tq, s)
    group = math.gcd(nb, MOBA_GROUP)
    return pl.pallas_call(
        functools.partial(_moba_kernel, tq=tq, group=group),
        grid=(b, w // LANES, s // tq),
        in_specs=[q_spec, kv_spec, kv_spec,
                  pl.BlockSpec((None, nb, LANES), lambda b, h, i: (b, 0, h)), q_spec],
        out_specs=q_spec,
        out_shape=jax.ShapeDtypeStruct((b, s, w), BF16),
        scratch_shapes=[pltpu.VMEM((nb, 2 * tq), F32)]
        + [pltpu.VMEM((group // 2 * MOBA_BLOCK, 2 * tq), F32)] * 2,
        compiler_params=_params(("parallel", "parallel", "arbitrary")),
        name="moba_prompt",
    )(q, k16, v16, kmean, sgate)


def _row_heads(x_row, n_rows, lanes_per_row):
    w = x_row.shape[1]
    keep = _iota((n_rows, w), 1) // lanes_per_row == _iota((n_rows, w), 0)
    return jnp.where(keep, jnp.broadcast_to(x_row, (n_rows, w)), 0.0)


def _dup_rows(x, reps):
    n, w = x.shape
    row = _iota((n * reps, w), 0) // reps
    out = jnp.zeros((n * reps, w), x.dtype)
    for h in range(n):
        out = jnp.where(row == h, x[h:h + 1, :], out)
    return out


def _stack_rows(rows):
    n, w = len(rows), rows[0].shape[1]
    row = _iota((n, w), 0)
    out = jnp.zeros((n, w), rows[0].dtype)
    for h, r in enumerate(rows):
        out = jnp.where(row == h, r, out)
    return out


def _head_dots(qcol_ref, kt_ref, n_heads):
    rows = []
    for h in range(n_heads):
        sl = slice(h * HEAD_DIM, (h + 1) * HEAD_DIM)
        rows.append(jnp.sum(qcol_ref[sl, :] * kt_ref[sl, :], axis=0, keepdims=True))
    return _stack_rows(rows)


def _add_weighted(acc_ref, w, vt_refs, n_heads):
    for h in range(n_heads):
        sl = slice(h * HEAD_DIM, (h + 1) * HEAD_DIM)
        a = acc_ref[sl, :]
        for u, vt_ref in enumerate(vt_refs):
            a = a + w[u * n_heads + h:u * n_heads + h + 1, :] * vt_ref[sl, :]
        acc_ref[sl, :] = a


def _sb_weights_pages(zs, tri, c):
    n_h = zs[0].shape[0]
    z = jnp.concatenate(zs, axis=0)
    sp = _softplus2(z)
    log_keep = -sp
    hi, lo = _split_bf16(log_keep)
    rows = z.shape[0]
    both = _dot(jnp.concatenate([hi, lo], axis=0), tri)
    later = both[:rows] + both[rows:]
    whole = later[:, 0:1] + log_keep[:, 0:1]
    offsets = []
    for u in range(len(zs)):
        offsets.append(c)
        c = c + whole[u * n_h:(u + 1) * n_h]
    return jnp.exp2(z - sp + later + jnp.concatenate(offsets, axis=0)), c


def _lane_sums_as_row(x):
    hi, lo = _split_bf16(x)
    ones = jnp.ones((SUBLANES, LANES), BF16)
    return (_dot_nt(ones, hi) + _dot_nt(ones, lo))[0:1, :]


def _decode_even_kernel(pt_ref, lam_ref, subg_ref, qa_ref, kan_ref, van_ref, qb_ref, sg_ref, *refs,
                        pps, lam_init):
    del pt_ref
    page_refs = refs[:4 * pps]
    o_ref = refs[4 * pps]
    qa_s, m_s, l_s, acca_s, c_s, accb_s = refs[4 * pps + 1:]
    step = pl.program_id(1)
    n_ha = qa_ref.shape[0]
    n_a = 2 * n_ha
    n_b = accb_s.shape[0] // HEAD_DIM
    wa = n_ha * LANES

    @pl.when(step == 0)
    def _():
        rows = _dup_rows(qa_ref[...].astype(F32), 2)
        qa = jnp.where(_iota(rows.shape, 1) // HEAD_DIM == _iota(rows.shape, 0) % 2, rows, 0.0)
        qa_s[...] = qa
        m_s[...] = jnp.sum(qa * _dup_rows(kan_ref[...], 2), axis=-1, keepdims=True)
        l_s[...] = jnp.ones_like(l_s)
        acca_s[...] = _dup_rows(van_ref[...], 2)
        c_s[...] = jnp.zeros_like(c_s)
        accb_s[...] = jnp.zeros_like(accb_s)

    rows_a = pps * page_refs[0].shape[0]
    own_head = _iota((n_a, rows_a), 1) % n_ha == _iota((n_a, rows_a), 0) // 2
    ak = jnp.concatenate([page_refs[4 * u][...].astype(BF16) for u in range(pps)], axis=0)
    av = jnp.concatenate([page_refs[4 * u + 1][...].astype(BF16) for u in range(pps)], axis=0)
    m_s[...], l_s[...], acca_s[...] = _softmax_tile(qa_s[...].astype(BF16), ak, av,
                                                    (m_s[...], l_s[...], acca_s[...]), own_head)
    zs = [_head_dots(qb_ref, page_refs[4 * u + 2], n_b) for u in range(pps)]
    w, c_s[...] = _sb_weights_pages(zs, _later_matrix(LANES), c_s[...])
    _add_weighted(accb_s, w, [page_refs[4 * u + 3] for u in range(pps)], n_b)

    @pl.when(step == pl.num_programs(1) - 1)
    def _():
        lam = _lambda_value(lam_ref[...], lam_init)
        a = acca_s[...] / l_s[...]
        for h in range(n_ha):
            sl = slice(h * LANES, (h + 1) * LANES)
            x = a[2 * h:2 * h + 1, :] - lam * a[2 * h + 1:2 * h + 2, :]
            ms = jnp.mean(x * x, axis=-1, keepdims=True)
            x = x * lax.rsqrt(ms + RMS_EPS) * subg_ref[...] * (1.0 - lam_init)
            o_ref[:, sl] = (x * sg_ref[:, sl]).astype(BF16)
        o_ref[:, wa:] = (_lane_sums_as_row(accb_s[...]) * sg_ref[:, wa:]).astype(BF16)


def _page_specs(cache, li, n_pages, pps, reverse):
    r = cache.shape[2]

    def make(u):
        def index(b, s, pt):
            j = s * pps + u
            j = n_pages - 1 - j if reverse else j
            return (li, pt[b, j], 0, 0)
        return pl.BlockSpec((None, None, r, LANES), index)

    return [make(u) for u in range(pps)]


def _col_bcast(x):
    return jnp.broadcast_to(x.astype(F32)[:, :, None], x.shape + (LANES,))


def _decode_even(pt, lam_p, subln_g, lam_init, qa16, ka, va, qb16, sgate, caches, li, pps):
    db, n_pages = pt.shape
    wa, wb = qa16.shape[-1], qb16.shape[-1]
    n_ha = wa // LANES
    whole = lambda shp: pl.BlockSpec((None,) + shp, lambda b, s, pt: (b,) + (0,) * len(shp))
    specs_by_cache = [_page_specs(c, li, n_pages, pps, True) for c in caches]
    page_specs = [specs_by_cache[t][u] for u in range(pps) for t in range(4)]
    page_args = [caches[t] for u in range(pps) for t in range(4)]
    grid_spec = pltpu.PrefetchScalarGridSpec(
        num_scalar_prefetch=1,
        grid=(db, n_pages // pps),
        in_specs=[pl.BlockSpec(lam_p.shape, lambda b, s, pt: (0, 0)),
                  pl.BlockSpec((1, LANES), lambda b, s, pt: (0, 0)),
                  whole((n_ha, LANES)), whole((n_ha, LANES)), whole((n_ha, LANES)), whole((wb, LANES)),
                  whole((1, wa + wb))] + page_specs,
        out_specs=whole((1, wa + wb)),
        scratch_shapes=[pltpu.VMEM((2 * n_ha, LANES), F32),
                        pltpu.VMEM((2 * n_ha, 1), F32), pltpu.VMEM((2 * n_ha, 1), F32),
                        pltpu.VMEM((2 * n_ha, LANES), F32),
                        pltpu.VMEM((wb // HEAD_DIM, 1), F32), pltpu.VMEM((wb, LANES), F32)],
    )
    heads = lambda x: x.reshape(db, n_ha, LANES)
    return pl.pallas_call(
        functools.partial(_decode_even_kernel, pps=pps, lam_init=lam_init),
        grid_spec=grid_spec,
        out_shape=jax.ShapeDtypeStruct((db, 1, wa + wb), BF16),
        compiler_params=_params(("parallel", "arbitrary")),
        name="decode_even",
    )(pt, lam_p, subln_g.reshape(1, LANES), heads(qa16), heads(ka), heads(va), _col_bcast(qb16),
      sgate.reshape(db, 1, wa + wb), *page_args)


def _moba_scores_kernel(pt_ref, qcol_ref, qrow_ref, kn_ref, *refs, pps, n_blocks):
    del pt_ref
    page_refs = refs[:pps]
    p_ref, pnew_ref, ids_ref = refs[pps:pps + 3]
    s_s = refs[pps + 3]
    step = pl.program_id(1)
    n_h = qcol_ref.shape[0] // HEAD_DIM
    ppb = MOBA_BLOCK // LANES

    for u in range(pps):
        s_s[step * pps + u] = _head_dots(qcol_ref, page_refs[u], n_h)

    @pl.when(step == pl.num_programs(1) - 1)
    def _():
        lane = _iota((n_h, LANES), 1)
        g = jnp.zeros((n_h, LANES), F32)
        for n in range(n_blocks):
            tot = sum(s_s[n * ppb + t] for t in range(ppb))
            g = jnp.where(lane == n, jnp.sum(tot, axis=-1, keepdims=True) * (1.0 / MOBA_BLOCK), g)
        picks = []
        sel = _top_blocks(g, n_blocks, picks=picks)
        ids = jnp.zeros((n_h, LANES), jnp.int32)
        for r, idx in enumerate(picks):
            ids = jnp.where(lane == r, idx, ids)
        ids_ref[...] = ids
        chosen = [sel[:, n:n + 1] for n in range(n_blocks)]
        s_new = jnp.sum(_row_heads(qrow_ref[...], n_h, HEAD_DIM) * kn_ref[...], axis=-1, keepdims=True) * QK_SCALE
        m_lanes = jnp.full((n_h, LANES), NEG, F32)
        for j in range(n_blocks * ppb):
            m_lanes = jnp.maximum(m_lanes, jnp.where(chosen[j // ppb], s_s[j] * QK_SCALE, NEG))
        m = jnp.maximum(jnp.max(m_lanes, axis=-1, keepdims=True), s_new)
        l_lanes = jnp.zeros((n_h, LANES), F32)
        for j in range(n_blocks * ppb):
            e = jnp.where(chosen[j // ppb], jnp.exp(s_s[j] * QK_SCALE - m), 0.0)
            s_s[j] = e
            l_lanes = l_lanes + e
        e_new = jnp.exp(s_new - m)
        inv = 1.0 / (jnp.sum(l_lanes, axis=-1, keepdims=True) + e_new)
        for j in range(n_blocks * ppb):
            p_ref[j] = s_s[j] * inv
        pnew_ref[...] = jnp.broadcast_to(e_new * inv, pnew_ref.shape)


def _moba_values_kernel(pt_ref, ids_ref, p_ref, pnew_ref, vn_ref, sg_ref, *refs, n_sel, ppb, heads_per_step):
    del pt_ref
    slices, o_ref = refs[:-1], refs[-1]
    b, pair = pl.program_id(0), pl.program_id(1)
    accs = []
    for c in range(heads_per_step):
        h = pair * heads_per_step + c
        acc = jnp.zeros((HEAD_DIM, LANES), F32)
        for r in range(n_sel):
            blk = ids_ref[b, h * n_sel + r]
            for u in range(ppb):
                w = p_ref[blk * ppb + u, pl.ds(h, 1), :]
                acc = acc + w * slices[(c * n_sel + r) * ppb + u][...]
        accs.append(acc)
    o = _lane_sums_as_row(jnp.concatenate(accs, axis=0)) + pnew_ref[...] * vn_ref[...]
    o_ref[...] = (o * sg_ref[...]).astype(BF16)


def _decode_odd(pt, q, k_new, v_new, sgate, cache_k, cache_v, li, pps):
    db, n_pages = pt.shape
    w = q.shape[-1]
    n_h = w // HEAD_DIM
    page = LANES
    past = n_pages * page
    assert past % MOBA_BLOCK == 0 and past >= MOBA_BLOCK and MOBA_BLOCK % page == 0
    n_blocks = past // MOBA_BLOCK
    row = lambda wd: pl.BlockSpec((None, 1, wd), lambda b, s, pt: (b, 0, 0))
    r3 = lambda x: x.reshape(db, 1, x.shape[-1])
    assert n_blocks >= MOBA_TOPK
    per_head = pl.BlockSpec((None, n_h, LANES), lambda b, s, pt: (b, 0, 0))
    probs, p_new, ids = pl.pallas_call(
        functools.partial(_moba_scores_kernel, pps=pps, n_blocks=n_blocks),
        grid_spec=pltpu.PrefetchScalarGridSpec(
            num_scalar_prefetch=1,
            grid=(db, n_pages // pps),
            in_specs=[pl.BlockSpec((None, w, LANES), lambda b, s, pt: (b, 0, 0)), row(w), row(w)]
            + _page_specs(cache_k, li, n_pages, pps, False),
            out_specs=[pl.BlockSpec((None, n_pages, n_h, page), lambda b, s, pt: (b, 0, 0, 0)),
                       per_head, per_head],
            scratch_shapes=[pltpu.VMEM((n_pages, n_h, page), F32)],
        ),
        out_shape=[jax.ShapeDtypeStruct((db, n_pages, n_h, page), F32),
                   jax.ShapeDtypeStruct((db, n_h, LANES), F32),
                   jax.ShapeDtypeStruct((db, n_h, LANES), jnp.int32)],
        compiler_params=_params(("parallel", "arbitrary")),
        name="decode_moba_scores",
    )(pt, _col_bcast(q), r3(q), r3(k_new), *([cache_k] * pps))
    p_new_row = jnp.repeat(p_new[:, :, 0], HEAD_DIM, axis=-1).reshape(db, 1, w)

    ppb = MOBA_BLOCK // page
    heads_per_step = math.gcd(n_h, VALUE_HEADS_PER_STEP)
    step_lanes = heads_per_step * HEAD_DIM
    assert step_lanes % LANES == 0
    block_ids = ids[:, :, :MOBA_TOPK].reshape(db, n_h * MOBA_TOPK)

    def slice_spec(c, r, u):
        def index(b, pair, pt, bid):
            h = pair * heads_per_step + c
            return (li, pt[b, bid[b, h * MOBA_TOPK + r] * ppb + u], h, 0)
        return pl.BlockSpec((None, None, HEAD_DIM, LANES), index)

    slice_specs = [slice_spec(c, r, u) for c in range(heads_per_step) for r in range(MOBA_TOPK)
                   for u in range(ppb)]
    lanes = lambda: pl.BlockSpec((None, 1, step_lanes), lambda b, pair, pt, bid: (b, 0, pair))
    return pl.pallas_call(
        functools.partial(_moba_values_kernel, n_sel=MOBA_TOPK, ppb=ppb, heads_per_step=heads_per_step),
        grid_spec=pltpu.PrefetchScalarGridSpec(
            num_scalar_prefetch=2,
            grid=(db, n_h // heads_per_step),
            in_specs=[pl.BlockSpec((None, n_pages, n_h, page), lambda b, pair, pt, bid: (b, 0, 0, 0)),
                      lanes(), lanes(), lanes()] + slice_specs,
            out_specs=lanes(),
        ),
        out_shape=jax.ShapeDtypeStruct((db, 1, w), BF16),
        compiler_params=_params(("parallel", "arbitrary")),
        name="decode_moba_values",
    )(pt, block_ids, probs, p_new_row, r3(v_new), r3(sgate), *([cache_v] * len(slice_specs)))


PROMPT_TM = 512
ATTN_TQ = 512
ATTN_TK = 512
SB_TILE = 256
MOBA_GROUP = 4
MOBA_TILE_BLOCKS = 2
PAGES_PER_STEP = 16
VALUE_HEADS_PER_STEP = 8


def _rows_view(c):
    l, n, p, h, hd = c.shape
    assert p == LANES and hd == LANES
    return c.reshape(l, n, p * h, hd)


def _cols_view(c):
    l, n, p, h, hd = c.shape
    assert p == LANES and hd == HEAD_DIM
    return jnp.transpose(c, (0, 1, 3, 4, 2)).reshape(l, n, h * hd, p)


def _heads_from_cols(x, n_heads):
    b, w, s = x.shape
    return jnp.transpose(x.reshape(b, n_heads, w // n_heads, s), (0, 3, 1, 2))


def _even_layer(x_p, x_s, caches, li, pt, g_norm, w_in, qk_g, lam_p, subln_g, w_out, layer):
    lam_init = 0.8 - 0.6 * math.exp(-0.3 * layer)
    b, s, d = x_p.shape
    db, t, _ = x_s.shape
    wa = w_out.shape[0] // 2
    wb = wa
    def plan(wide, narrow):
        return (("q_rope_bf16", 0, wa, None), ("k_rope", wa, wa, wide), ("kv_plain", 2 * wa, wa, wide),
                ("q_plain_bf16", 3 * wa, wb, None), ("kv_plain", 3 * wa + wb, wb, narrow),
                ("kv_plain", 3 * wa + 2 * wb, wb, narrow), ("gate", 3 * wa + 3 * wb, wa + wb, None))

    w_in16 = w_in.astype(BF16)
    w_out16 = w_out.astype(BF16)
    past = pt.shape[1] * LANES
    n_a = wa // (2 * HEAD_DIM)
    n_b = wb // HEAD_DIM

    tabs = _rope_tables(jnp.arange(s))
    qa16, ka, ka16, va, va16, qb16, kb, kb16, vb, vb16, sgate = _project(
        x_p.reshape(b * s, d), g_norm, w_in16, qk_g[0], qk_g[1], tabs, plan("heads", "cols"), PROMPT_TM, s,
        "proj_even_prompt")
    r3 = lambda a: a.reshape(b, s, a.shape[-1])
    sg3 = r3(sgate)
    mix_a = _diff_attn_prompt(r3(qa16), r3(ka16), r3(va16), sg3, lam_p, subln_g, lam_init, ATTN_TQ, ATTN_TK)
    mix_b = _sb_prompt(r3(qb16), r3(kb16), r3(vb16), sg3, wa // LANES, SB_TILE, SB_TILE)
    y_p = _finish(x_p.reshape(b * s, d), [mix_a.reshape(b * s, wa), mix_b.reshape(b * s, wb)], w_out16, PROMPT_TM,
                  "finish_even_prompt").reshape(b, s, d)

    tabs_s = _rope_tables(jnp.tile(past + jnp.arange(t), db))
    qa16_s, ka_s, _, va_s, _, qb16_s, kb_s, _, vb_s, _, sgate_s = _project(
        x_s.reshape(db * t, d), g_norm, w_in16, qk_g[0], qk_g[1], tabs_s, plan("rows", "rows"), db * t, db * t,
        "proj_even_sample")
    assert t == 1
    mix_s = _decode_even(pt, lam_p, subln_g, lam_init, qa16_s, ka_s, va_s, qb16_s, sgate_s,
                         caches, li, math.gcd(pt.shape[1], PAGES_PER_STEP))
    y_s = _finish(x_s.reshape(db * t, d), [mix_s.reshape(db * t, wa + wb)], w_out16, db * t,
                  "finish_even_sample").reshape(db, t, d)

    rows_p = (ka.reshape(b, s, n_a, 2 * HEAD_DIM), va.reshape(b, s, n_a, 2 * HEAD_DIM),
              _heads_from_cols(kb, n_b), _heads_from_cols(vb, n_b))
    rows_s = (ka_s.reshape(db, t, n_a, 2 * HEAD_DIM), va_s.reshape(db, t, n_a, 2 * HEAD_DIM),
              kb_s.reshape(db, t, n_b, HEAD_DIM), vb_s.reshape(db, t, n_b, HEAD_DIM))
    return y_p, y_s, rows_p, rows_s


def _odd_layer(x_p, x_s, cache_k, cache_v, li, pt, g_norm, w_in, qk_g, w_out):
    b, s, d = x_p.shape
    db, t, _ = x_s.shape
    wc = w_out.shape[0]
    w_in16 = w_in.astype(BF16)
    w_out16 = w_out.astype(BF16)
    past = pt.shape[1] * LANES
    assert s % MOBA_BLOCK == 0

    def plan(k_kind, layout):
        return (("q_rope_f32", 0, wc, None), (k_kind, wc, wc, layout), ("kv_plain", 2 * wc, wc, layout),
                ("gate", 3 * wc, wc, None))

    tabs = _rope_tables(jnp.arange(s))
    q, k, k16, kmean, v, v16, sgate = _project(
        x_p.reshape(b * s, d), g_norm, w_in16, qk_g[0], qk_g[1], tabs, plan("k_rope_mean", "cols"),
        MOBA_BLOCK * math.gcd(s // MOBA_BLOCK, PROMPT_TM // MOBA_BLOCK), s, "proj_odd_prompt")
    r3 = lambda a: a.reshape(b, s, a.shape[-1])
    mix = _moba_prompt(r3(q), r3(k16), r3(v16), kmean.reshape(b, s // MOBA_BLOCK, wc), r3(sgate))
    y_p = _finish(x_p.reshape(b * s, d), [mix.reshape(b * s, wc)], w_out16, PROMPT_TM,
                  "finish_odd_prompt").reshape(b, s, d)

    assert t == 1
    tabs_s = _rope_tables(jnp.tile(past + jnp.arange(t), db))
    q_s, k_s, _, v_s, _, sgate_s = _project(
        x_s.reshape(db * t, d), g_norm, w_in16, qk_g[0], qk_g[1], tabs_s, plan("k_rope", "rows"), db * t, db * t,
        "proj_odd_sample")
    mix_s = _decode_odd(pt, q_s, k_s, v_s, sgate_s, cache_k, cache_v, li, math.gcd(pt.shape[1], PAGES_PER_STEP))
    y_s = _finish(x_s.reshape(db * t, d), [mix_s.reshape(db * t, wc)], w_out16, db * t,
                  "finish_odd_sample").reshape(db, t, d)

    n_c = wc // HEAD_DIM
    return (y_p, y_s, (_heads_from_cols(k, n_c), _heads_from_cols(v, n_c)),
            (k_s.reshape(db, t, n_c, HEAD_DIM), v_s.reshape(db, t, n_c, HEAD_DIM)))


def kernel(x_prompt, x_sample, cache_a_k, cache_a_v, cache_b_k, cache_b_v, cache_c_k, cache_c_v, page_table,
           norm_even, w_in_even, qk_norm_a, lambda_a, subln_a, w_out_even, norm_odd, w_in_odd, qk_norm_c,
           w_out_odd):
    depth = norm_even.shape[0] + norm_odd.shape[0]
    even_caches = [_rows_view(cache_a_k), _rows_view(cache_a_v), _cols_view(cache_b_k), _cols_view(cache_b_v)]
    odd_k, odd_v = _cols_view(cache_c_k), _cols_view(cache_c_v)
    y_p, y_s = x_prompt, x_sample
    ev_p, ev_s, od_p, od_s = [], [], [], []
    for layer in range(depth):
        i = layer // 2
        if layer % 2 == 0:
            y_p, y_s, rp, rs = _even_layer(y_p, y_s, even_caches, i, page_table, norm_even[i], w_in_even[i],
                                           qk_norm_a[i], lambda_a[i], subln_a[i], w_out_even[i], layer)
            ev_p.append(rp)
            ev_s.append(rs)
        else:
            y_p, y_s, rp, rs = _odd_layer(y_p, y_s, odd_k, odd_v, i, page_table, norm_odd[i], w_in_odd[i],
                                          qk_norm_c[i], w_out_odd[i])
            od_p.append(rp)
            od_s.append(rs)

    def stack(rows, j):
        return jnp.stack([r[j] for r in rows])

    return (y_p, y_s,
            stack(ev_p, 0), stack(ev_p, 1), stack(ev_p, 2), stack(ev_p, 3), stack(od_p, 0), stack(od_p, 1),
            stack(ev_s, 0), stack(ev_s, 1), stack(ev_s, 2), stack(ev_s, 3), stack(od_s, 0), stack(od_s, 1))
```

```python
import functools
import math

import jax
import jax.numpy as jnp
from jax import lax
from jax.experimental import pallas as pl
from jax.experimental.pallas import tpu as pltpu

HEAD_DIM = 64
ROT_DIM = HEAD_DIM // 4
ROPE_THETA = 500000.0
RMS_EPS = 1e-6
MOBA_BLOCK = 256
MOBA_TOPK = 3
LANES = 128
SUBLANES = 8
NEG = -1e30
SB_DEAD_LOG2 = 150.0
QK_SCALE = HEAD_DIM ** -0.5
Q_SOFTMAX_SCALE = QK_SCALE * math.log2(math.e)
ATTN_SLAB = 256
VMEM_LIMIT = 52 * 1024 * 1024

F32 = jnp.float32
BF16 = jnp.bfloat16


def _params(sem, vmem=VMEM_LIMIT):
    return pltpu.CompilerParams(dimension_semantics=sem, vmem_limit_bytes=vmem)


def _dot(a, b):
    return jnp.dot(a, b, preferred_element_type=F32)


def _dot_nt(a, b):
    return lax.dot_general(a, b, (((1,), (1,)), ((), ())), preferred_element_type=F32)


def _split_bf16(x):
    hi = x.astype(BF16)
    lo = (x - hi.astype(F32)).astype(BF16)
    return hi, lo


def _iota(shape, dim):
    return lax.broadcasted_iota(jnp.int32, shape, dim)


def _head_seg_matrix():
    return (_iota((LANES, LANES), 0) // HEAD_DIM == _iota((LANES, LANES), 1) // HEAD_DIM).astype(BF16)


def _norm_rope(x, g, c, s1, s2, seg):
    hi, lo = _split_bf16(x * x)
    ms = (_dot(hi, seg) + _dot(lo, seg)) * (1.0 / HEAD_DIM)
    xn = x * lax.rsqrt(ms + RMS_EPS) * g
    half = ROT_DIM // 2
    return xn * c + pltpu.roll(xn, LANES - half, 1) * s1 + pltpu.roll(xn, half, 1) * s2


def _proj_kernel(x_ref, gn_ref, w_ref, gq_ref, gk_ref, c_ref, s1_ref, s2_ref, *out_refs, plan):
    x = x_ref[...]
    ms = jnp.mean(x * x, axis=-1, keepdims=True)
    xn = (x * lax.rsqrt(ms + RMS_EPS) * gn_ref[...]).astype(BF16)
    seg = _head_seg_matrix()
    c, s1, s2 = c_ref[...], s1_ref[...], s2_ref[...]
    outs = list(out_refs)
    tm = x.shape[0]

    def store_rows(o_ref, layout, t, n_t, y):
        if layout == "rows":
            o_ref[:, t * LANES:(t + 1) * LANES] = y
        elif layout == "cols":
            o_ref[t * LANES:(t + 1) * LANES, :] = jnp.transpose(y)
        elif layout == "heads":
            o_ref[pl.ds(t, tm, stride=n_t), :] = y
        else:
            raise ValueError(layout)

    for kind, col0, width, layout in plan:
        h = _dot(xn, w_ref[:, col0:col0 + width])
        n_t = width // LANES
        if kind in ("q_rope_bf16", "q_rope_f32", "k_rope", "k_rope_mean"):
            g = gq_ref[...] if kind.startswith("q") else gk_ref[...]
            if kind == "q_rope_bf16":
                o_ref = outs.pop(0)
            elif kind == "q_rope_f32":
                o_ref = outs.pop(0)
            else:
                o_ref, o16_ref = outs.pop(0), outs.pop(0)
                mean_ref = outs.pop(0) if kind == "k_rope_mean" else None
            for t in range(n_t):
                sl = slice(t * LANES, (t + 1) * LANES)
                y = _norm_rope(h[:, sl], g, c, s1, s2, seg)
                if kind == "q_rope_bf16":
                    o_ref[:, sl] = (y * Q_SOFTMAX_SCALE).astype(BF16)
                elif kind == "q_rope_f32":
                    o_ref[:, sl] = y
                else:
                    store_rows(o_ref, layout, t, n_t, y)
                    o16_ref[:, sl] = y.astype(BF16)
                    if mean_ref is not None:
                        for j in range(tm // MOBA_BLOCK):
                            mean_ref[j, :, sl] = jnp.mean(y[j * MOBA_BLOCK:(j + 1) * MOBA_BLOCK], axis=0,
                                                          keepdims=True)
        elif kind == "q_plain_bf16":
            outs.pop(0)[...] = (h * Q_SOFTMAX_SCALE).astype(BF16)
        elif kind == "kv_plain":
            o_ref = outs.pop(0)
            for t in range(n_t):
                store_rows(o_ref, layout, t, n_t, h[:, t * LANES:(t + 1) * LANES])
            outs.pop(0)[...] = h.astype(BF16)
        elif kind == "gate":
            outs.pop(0)[...] = h / (1.0 + jnp.exp(-h))
        else:
            raise ValueError(kind)
    assert not outs


def _rope_tables(pos):
    half = ROT_DIM // 2
    inv = ROPE_THETA ** (-jnp.arange(0, ROT_DIM, 2, dtype=F32) / ROT_DIM)
    ang = pos.astype(F32)[:, None] * inv[None, :]
    cos, sin = jnp.cos(ang), jnp.sin(ang)
    t = pos.shape[0]
    z_half = jnp.zeros((t, half), F32)
    z_rest = jnp.zeros((t, HEAD_DIM - ROT_DIM), F32)
    c = jnp.concatenate([cos, cos, jnp.ones((t, HEAD_DIM - ROT_DIM), F32)], axis=-1)
    s1 = jnp.concatenate([-sin, z_half, z_rest], axis=-1)
    s2 = jnp.concatenate([z_half, sin, z_rest], axis=-1)
    rep = LANES // HEAD_DIM
    return tuple(jnp.tile(a, (1, rep)) for a in (c, s1, s2))


def _project(x, g_norm, w_bf16, gq, gk, tables, plan, tm, seq, name):
    m, d = x.shape
    n_tab = tables[0].shape[0] // tm
    tiles_per_seq = seq // tm
    out_shapes, out_specs = [], []
    for kind, _, width, layout in plan:
        row = pl.BlockSpec((tm, width), lambda i: (i, 0))
        if kind in ("q_rope_bf16", "q_plain_bf16"):
            out_shapes += [jax.ShapeDtypeStruct((m, width), BF16)]
            out_specs += [row]
        elif kind in ("q_rope_f32", "gate"):
            out_shapes += [jax.ShapeDtypeStruct((m, width), F32)]
            out_specs += [row]
        elif kind in ("k_rope", "kv_plain", "k_rope_mean"):
            if layout == "rows":
                out_shapes += [jax.ShapeDtypeStruct((m, width), F32)]
                out_specs += [row]
            elif layout == "cols":
                out_shapes += [jax.ShapeDtypeStruct((m // seq, width, seq), F32)]
                out_specs += [pl.BlockSpec((None, width, tm),
                                           lambda i: (i // tiles_per_seq, 0, i % tiles_per_seq))]
            else:
                n_t = width // LANES
                out_shapes += [jax.ShapeDtypeStruct((m * n_t, LANES), F32)]
                out_specs += [pl.BlockSpec((tm * n_t, LANES), lambda i: (i, 0))]
            out_shapes += [jax.ShapeDtypeStruct((m, width), BF16)]
            out_specs += [row]
            if kind == "k_rope_mean":
                assert tm % MOBA_BLOCK == 0
                out_shapes += [jax.ShapeDtypeStruct((m // MOBA_BLOCK, 1, width), F32)]
                out_specs += [pl.BlockSpec((tm // MOBA_BLOCK, 1, width), lambda i: (i, 0, 0))]
    tab_spec = pl.BlockSpec((tm, LANES), lambda i: (i % n_tab, 0))
    vec_d = pl.BlockSpec((1, d), lambda i: (0, 0))
    vec_l = pl.BlockSpec((1, LANES), lambda i: (0, 0))
    rep = LANES // HEAD_DIM
    return pl.pallas_call(
        functools.partial(_proj_kernel, plan=plan),
        grid=(m // tm,),
        in_specs=[pl.BlockSpec((tm, d), lambda i: (i, 0)), vec_d,
                  pl.BlockSpec(w_bf16.shape, lambda i: (0, 0), pipeline_mode=pl.Buffered(1)),
                  vec_l, vec_l, tab_spec, tab_spec, tab_spec],
        out_specs=out_specs,
        out_shape=out_shapes,
        compiler_params=_params(("parallel",)),
        name=name,
    )(x, g_norm.reshape(1, d), w_bf16, jnp.tile(gq, rep).reshape(1, LANES), jnp.tile(gk, rep).reshape(1, LANES),
      *tables)


def _finish_kernel(x_ref, w_ref, *refs):
    mix_refs, y_ref = refs[:-1], refs[-1]
    y = x_ref[...]
    row0 = 0
    for mix_ref in mix_refs:
        rows = mix_ref.shape[1]
        y = y + _dot(mix_ref[...], w_ref[row0:row0 + rows, :])
        row0 += rows
    y_ref[...] = y


def _finish(x, mix_parts, w_bf16, tm, name):
    m, d = x.shape
    kdim = w_bf16.shape[0]
    assert sum(p.shape[1] for p in mix_parts) == kdim
    return pl.pallas_call(
        _finish_kernel,
        grid=(m // tm,),
        in_specs=[pl.BlockSpec((tm, d), lambda i: (i, 0)), pl.BlockSpec((kdim, d), lambda i: (0, 0))]
        + [pl.BlockSpec((tm, p.shape[1]), lambda i: (i, 0)) for p in mix_parts],
        out_specs=pl.BlockSpec((tm, d), lambda i: (i, 0)),
        out_shape=jax.ShapeDtypeStruct((m, d), F32),
        compiler_params=_params(("parallel",)),
        name=name,
    )(x, w_bf16, *mix_parts)


def _lambda_value(lp, lam_init):
    a = jnp.sum(lp[0:1] * lp[1:2], axis=-1, keepdims=True)
    b = jnp.sum(lp[2:3] * lp[3:4], axis=-1, keepdims=True)
    return jnp.exp(a) - jnp.exp(b) + lam_init


def _softmax_tile(qq, k, v, carry, mask):
    m, l, acc = carry
    s = _dot_nt(qq, k)
    if mask is not None:
        s = jnp.where(mask, s, NEG)
    m_new = jnp.maximum(m, jnp.max(s, axis=-1, keepdims=True))
    alpha = jnp.exp2(m - m_new)
    p = jnp.exp2(s - m_new)
    l = alpha * l + jnp.sum(p, axis=-1, keepdims=True)
    acc = alpha * acc + _dot(p.astype(BF16), v)
    return m_new, l, acc


def _dot_tn(a, b):
    return lax.dot_general(a, b, (((0,), (0,)), ((), ())), preferred_element_type=F32)


def _softmax_tile_t(qq_t, k_ref, v_ref, start, n_keys, carry, mask):
    s, m_s, bias = _scores_t(qq_t, k_ref, start, n_keys, mask)
    return _absorb_t(v_ref, start, n_keys, s, m_s, bias, carry)


def _scores_t(qq_t, k_ref, start, n_keys, mask):
    s = _dot(k_ref[pl.ds(start, n_keys), :], qq_t)
    slabs = [s[c * ATTN_SLAB:(c + 1) * ATTN_SLAB] for c in range(n_keys // ATTN_SLAB)]
    if mask is None:
        return s, jnp.max(s, axis=0, keepdims=True), None
    if mask[0].dtype == jnp.bool_:
        s = jnp.concatenate([jnp.where(mk, x, NEG) for x, mk in zip(slabs, mask)], axis=0)
        return s, jnp.max(s, axis=0, keepdims=True), None
    m_s = None
    for x, row in zip(slabs, mask):
        m_c = jnp.max(x, axis=0, keepdims=True) + row
        m_s = m_c if m_s is None else jnp.maximum(m_s, m_c)
    return s, m_s, jnp.concatenate(mask, axis=0)


def _absorb_t(v_ref, start, n_keys, s, m_s, bias, carry):
    m, l, acc = carry
    m_new = jnp.maximum(m, m_s)
    alpha = jnp.exp2(m - m_new)
    if bias is None:
        p = jnp.exp2(s - m_new)
    else:
        shift = bias - m_new
        p = jnp.concatenate([jnp.exp2(s[c * ATTN_SLAB:(c + 1) * ATTN_SLAB] + shift[c:c + 1])
                             for c in range(n_keys // ATTN_SLAB)], axis=0)
    l = alpha * l + jnp.sum(p, axis=0, keepdims=True)
    v, p16 = v_ref[pl.ds(start, n_keys), :], p.astype(BF16)
    pv = _dot_tn(v, p16) if acc.shape[0] == LANES else _pair_values_t(v, p16)
    return m_new, l, alpha * acc + pv


def _pair_values_t(v, w16):
    half = w16.shape[1] // 2
    return jnp.concatenate([_dot_tn(v[:, :HEAD_DIM], w16[:, :half]), _dot_tn(v[:, HEAD_DIM:], w16[:, half:])],
                           axis=1)


def _merge_pair_t(x, tq):
    return jnp.transpose(jnp.concatenate([x[:, :tq], x[:, tq:]], axis=0))


def _pipelined_pairs(n_pairs, n_keys, qq_t, k_ref, v_ref, mask_of, s_refs, carry, head=None):
    s_a, s_b = s_refs

    def scores(t, s_ref):
        s, m_s, bias = _scores_t(qq_t, k_ref, pl.multiple_of(t * n_keys, n_keys), n_keys, mask_of(t))
        s_ref[...] = s
        return m_s, bias

    def absorb(t, s_ref, stats, carry):
        return _absorb_t(v_ref, pl.multiple_of(t * n_keys, n_keys), n_keys, s_ref[...], *stats, carry)

    def trip(u, state):
        m_a, carry = state
        m_b = scores(2 * u + 1, s_b)
        carry = absorb(2 * u, s_a, m_a, carry)
        m_a = scores(2 * u + 2, s_a)
        carry = absorb(2 * u + 1, s_b, m_b, carry)
        return m_a, carry

    def absorb_head(carry):
        for s_h, stats_h, start_h in head or ():
            carry = _absorb_t(v_ref, start_h, n_keys, s_h, *stats_h, carry)
        return carry

    def run(carry):
        m_a = scores(n_pairs * 0, s_a)
        carry = absorb_head(carry)
        m_a, carry = lax.fori_loop(0, n_pairs - 1, trip, (m_a, carry))
        last = 2 * (n_pairs - 1)
        m_b = scores(last + 1, s_b)
        carry = absorb(last, s_a, m_a, carry)
        return absorb(last + 1, s_b, m_b, carry)

    return lax.cond(n_pairs > 0, run, absorb_head, carry)


def _causal_masks(k_start, q_start, tq, n_keys, r, strict):
    q_pos = q_start + _iota((ATTN_SLAB, r), 1) % tq
    masks = []
    for c in range(n_keys // ATTN_SLAB):
        k_pos = k_start + c * ATTN_SLAB + _iota((ATTN_SLAB, r), 0)
        masks.append(k_pos < q_pos if strict else k_pos <= q_pos)
    return masks


def _stack_pair_t(q_t):
    row = _iota(q_t.shape, 0)
    zero = jnp.zeros_like(q_t)
    return jnp.concatenate([jnp.where(row < HEAD_DIM, q_t, zero), jnp.where(row >= HEAD_DIM, q_t, zero)], axis=1)


def _diff_attn_kernel(lam_ref, subg_ref, q_ref, k_ref, v_ref, sg_ref, o_ref, s_a, s_b, *, tq, tk, lam_init):
    i = pl.program_id(2)
    r = 2 * tq
    qq_t = _stack_pair_t(jnp.transpose(q_ref[...].astype(F32))).astype(BF16)
    n_full = (i * tq) // tk
    half = tk // 2

    carry = (jnp.full((1, r), NEG, F32), jnp.zeros((1, r), F32), jnp.zeros((LANES, r), F32))
    start = pl.multiple_of(n_full * tk, tk)
    diag = []
    for c, mask in enumerate(_causal_masks(start, i * tq, tq, tk, r, strict=False)):
        st = pl.multiple_of(start + c * ATTN_SLAB, ATTN_SLAB)
        s, m_s, _ = _scores_t(qq_t, k_ref, st, ATTN_SLAB, [mask])
        diag.append((s, (m_s, None), st))
    assert half == ATTN_SLAB
    _, l, acc = _pipelined_pairs(n_full, half, qq_t, k_ref, v_ref, lambda t: None, (s_a, s_b), carry, head=diag)

    a = acc / l
    lam = _lambda_value(lam_ref[...], lam_init)
    o = jnp.transpose(a[:, :tq] - lam * a[:, tq:])
    ms = jnp.mean(o * o, axis=-1, keepdims=True)
    o = o * lax.rsqrt(ms + RMS_EPS) * subg_ref[...] * (1.0 - lam_init)
    o_ref[...] = (o * sg_ref[...]).astype(BF16)


def _softplus2(z):
    return jnp.maximum(z, 0.0) + jnp.log2(1.0 + jnp.exp2(-jnp.abs(z)))


def _later_matrix(tk, keys_axis=1):
    a, b = _iota((tk, tk), 0), _iota((tk, tk), 1)
    return (a > b if keys_axis == 1 else b > a).astype(BF16)


def _sb_kernel(q_ref, k_ref, v_ref, sg_ref, o_ref, *, tq, tk):
    i = pl.program_id(2)
    r = 2 * tq
    qq_t = _stack_pair_t(jnp.transpose(q_ref[...].astype(F32))).astype(BF16)
    tri = _later_matrix(ATTN_SLAB, keys_axis=0)
    assert tq == tk == ATTN_SLAB

    def tile(st, n_slabs, c, acc, valid):
        parts = []
        for u in range(n_slabs):
            z = _dot(k_ref[pl.ds(st + u * ATTN_SLAB, ATTN_SLAB), :], qq_t)
            sp = _softplus2(z)
            log_keep = -sp if valid is None else jnp.where(valid[u], -sp, 0.0)
            later = _dot(tri, log_keep.astype(BF16))
            parts.append((z - sp + later, later[0:1, :] + log_keep[0:1, :]))
        for u in reversed(range(n_slabs)):
            e, whole = parts[u]
            w = jnp.exp2(e + c)
            if valid is not None:
                w = jnp.where(valid[u], w, 0.0)
            acc = acc + _pair_values_t(v_ref[pl.ds(st + u * ATTN_SLAB, ATTN_SLAB), :], w.astype(BF16))
            c = c + whole
        return c, acc

    n_rest = jnp.maximum(i - 1, 0)
    start = pl.multiple_of(n_rest * tk, tk)
    c, acc = tile(start, 2, jnp.zeros((1, r), F32), jnp.zeros((HEAD_DIM, r), F32),
                  _causal_masks(start, i * tq, tq, 2 * tk, r, strict=True))

    def live(state):
        t, c_max, _, _ = state
        return jnp.logical_and(t < n_rest, c_max > -SB_DEAD_LOG2)

    def full_tile(state):
        t, _, c, acc = state
        c, acc = tile(pl.multiple_of((n_rest - 1 - t) * tk, tk), 1, c, acc, None)
        return t + 1, jnp.max(c), c, acc

    _, _, _, acc = lax.while_loop(live, full_tile, (jnp.int32(0), jnp.max(c), c, acc))
    o_ref[...] = (_merge_pair_t(acc, tq) * sg_ref[...]).astype(BF16)


def _top_blocks(g, n_valid, axis=-1, picks=None):
    axis = axis % g.ndim
    blk = _iota(g.shape, axis)
    nb = g.shape[axis]
    g = jnp.where(blk < n_valid, g, -jnp.inf)
    sel = jnp.zeros(g.shape, jnp.bool_)
    for _ in range(min(MOBA_TOPK, nb)):
        mx = jnp.max(g, axis=axis, keepdims=True)
        idx = jnp.min(jnp.where(g == mx, blk, nb), axis=axis, keepdims=True)
        pick = blk == idx
        sel = jnp.logical_or(sel, pick)
        g = jnp.where(pick, -jnp.inf, g)
        if picks is not None:
            picks.append(idx)
    return jnp.logical_and(sel, blk < n_valid)


def _moba_kernel(q_ref, k_ref, v_ref, km_ref, sg_ref, o_ref, sel_s, s_a, s_b, *, tq, group):
    i = pl.program_id(2)
    r = 2 * tq
    blk = MOBA_BLOCK
    tile_blocks = tq // blk
    first = i * tile_blocks
    qf_t = _stack_pair_t(jnp.transpose(q_ref[...]))
    q_hi, q_lo = _split_bf16(qf_t)
    qq_t = (qf_t * Q_SOFTMAX_SCALE).astype(BF16)

    km_hi, km_lo = _split_bf16(km_ref[...])
    g = _dot(km_hi, q_hi) + _dot(km_hi, q_lo) + _dot(km_lo, q_hi)
    col = _iota((1, r), 1) % tq
    own = col // blk
    sel = _top_blocks(g, first + own, axis=0)
    block_id = _iota(sel.shape, 0)
    sel_s[...] = jnp.where(jnp.logical_and(sel, block_id < first), 0.0, NEG)

    start = pl.multiple_of(first * blk, tq)
    causal = _iota((blk, r), 0) <= col % blk
    masks = []
    for a in range(tile_blocks):
        picked = jnp.max(jnp.where(jnp.logical_and(sel, block_id == first + a), 1.0, 0.0), axis=0,
                         keepdims=True) > 0.0
        masks.append(jnp.logical_or(jnp.logical_and(own == a, causal), jnp.logical_and(own > a, picked)))
    carry = (jnp.full((1, r), NEG, F32), jnp.zeros((1, r), F32), jnp.zeros((HEAD_DIM, r), F32))
    s_own, m_own, _ = _scores_t(qq_t, k_ref, start, tq, masks)

    per_tile = group // 2
    assert per_tile * blk == tq

    def chosen(t):
        return [sel_s[pl.ds(t * per_tile + c, 1), :] for c in range(per_tile)]

    _, l, acc = _pipelined_pairs((first + group - 1) // group, tq, qq_t, k_ref, v_ref, chosen, (s_a, s_b), carry,
                                 head=[(s_own, (m_own, None), start)])
    o_ref[...] = (_merge_pair_t(acc / l, tq) * sg_ref[...]).astype(BF16)


def _attn_specs(tq, s, group0=0):
    q_spec = pl.BlockSpec((None, tq, LANES), lambda b, h, i: (b, i, h + group0))
    kv_spec = pl.BlockSpec((None, s, LANES), lambda b, h, i: (b, 0, h))
    return q_spec, kv_spec


def _diff_attn_prompt(q16, k16, v16, sgate, lam_p, subln_g, lam_init, tq, tk):
    b, s, w = q16.shape
    q_spec, kv_spec = _attn_specs(tq, s)
    return pl.pallas_call(
        functools.partial(_diff_attn_kernel, tq=tq, tk=tk, lam_init=lam_init),
        grid=(b, w // LANES, s // tq),
        in_specs=[pl.BlockSpec(lam_p.shape, lambda b, h, i: (0, 0)),
                  pl.BlockSpec((1, LANES), lambda b, h, i: (0, 0)),
                  q_spec, kv_spec, kv_spec, q_spec],
        out_specs=q_spec,
        out_shape=jax.ShapeDtypeStruct((b, s, w), BF16),
        scratch_shapes=[pltpu.VMEM((tk // 2, 2 * tq), F32)] * 2,
        compiler_params=_params(("parallel", "parallel", "arbitrary")),
        name="diff_attn_prompt",
    )(lam_p, subln_g.reshape(1, LANES), q16, k16, v16, sgate)


def _sb_prompt(q16, k16, v16, sgate, gate_group0, tq, tk):
    b, s, w = q16.shape
    q_spec, kv_spec = _attn_specs(tq, s)
    sg_spec = pl.BlockSpec((None, tq, LANES), lambda b, h, i: (b, i, h + gate_group0))
    return pl.pallas_call(
        functools.partial(_sb_kernel, tq=tq, tk=tk),
        grid=(b, w // LANES, s // tq),
        in_specs=[q_spec, kv_spec, kv_spec, sg_spec],
        out_specs=q_spec,
        out_shape=jax.ShapeDtypeStruct((b, s, w), BF16),
        compiler_params=_params(("parallel", "parallel", "arbitrary")),
        name="sb_prompt",
    )(q16, k16, v16, sgate)


def _moba_prompt(q, k16, v16, kmean, sgate):
    b, s, w = q.shape
    assert MOBA_BLOCK == ATTN_SLAB
    nb = s // MOBA_BLOCK
    tq = MOBA_BLOCK * math.gcd(nb, MOBA_TILE_BLOCKS)
    q_spec, kv_spec = _attn_specs(tq, s)
    group = math.gcd(nb, MOBA_GROUP)
    return pl.pallas_call(
        functools.partial(_moba_kernel, tq=tq, group=group),
        grid=(b, w // LANES, s // tq),
        in_specs=[q_spec, kv_spec, kv_spec,
                  pl.BlockSpec((None, nb, LANES), lambda b, h, i: (b, 0, h)), q_spec],
        out_specs=q_spec,
        out_shape=jax.ShapeDtypeStruct((b, s, w), BF16),
        scratch_shapes=[pltpu.VMEM((nb, 2 * tq), F32)]
        + [pltpu.VMEM((group // 2 * MOBA_BLOCK, 2 * tq), F32)] * 2,
        compiler_params=_params(("parallel", "parallel", "arbitrary")),
        name="moba_prompt",
    )(q, k16, v16, kmean, sgate)


def _row_heads(x_row, n_rows, lanes_per_row):
    w = x_row.shape[1]
    keep = _iota((n_rows, w), 1) // lanes_per_row == _iota((n_rows, w), 0)
    return jnp.where(keep, jnp.broadcast_to(x_row, (n_rows, w)), 0.0)


def _dup_rows(x, reps):
    n, w = x.shape
    row = _iota((n * reps, w), 0) // reps
    out = jnp.zeros((n * reps, w), x.dtype)
    for h in range(n):
        out = jnp.where(row == h, x[h:h + 1, :], out)
    return out


def _stack_rows(rows):
    n, w = len(rows), rows[0].shape[1]
    row = _iota((n, w), 0)
    out = jnp.zeros((n, w), rows[0].dtype)
    for h, r in enumerate(rows):
        out = jnp.where(row == h, r, out)
    return out


def _head_dots(qcol_ref, kt_ref, n_heads):
    rows = []
    for h in range(n_heads):
        sl = slice(h * HEAD_DIM, (h + 1) * HEAD_DIM)
        rows.append(jnp.sum(qcol_ref[sl, :] * kt_ref[sl, :], axis=0, keepdims=True))
    return _stack_rows(rows)


def _add_weighted(acc_ref, w, vt_refs, n_heads):
    for h in range(n_heads):
        sl = slice(h * HEAD_DIM, (h + 1) * HEAD_DIM)
        a = acc_ref[sl, :]
        for u, vt_ref in enumerate(vt_refs):
            a = a + w[u * n_heads + h:u * n_heads + h + 1, :] * vt_ref[sl, :]
        acc_ref[sl, :] = a


def _sb_weights_pages(zs, tri, c):
    n_h = zs[0].shape[0]
    z = jnp.concatenate(zs, axis=0)
    sp = _softplus2(z)
    log_keep = -sp
    hi, lo = _split_bf16(log_keep)
    rows = z.shape[0]
    both = _dot(jnp.concatenate([hi, lo], axis=0), tri)
    later = both[:rows] + both[rows:]
    whole = later[:, 0:1] + log_keep[:, 0:1]
    offsets = []
    for u in range(len(zs)):
        offsets.append(c)
        c = c + whole[u * n_h:(u + 1) * n_h]
    return jnp.exp2(z - sp + later + jnp.concatenate(offsets, axis=0)), c


def _lane_sums_as_row(x):
    hi, lo = _split_bf16(x)
    ones = jnp.ones((SUBLANES, LANES), BF16)
    return (_dot_nt(ones, hi) + _dot_nt(ones, lo))[0:1, :]


def _decode_even_kernel(pt_ref, lam_ref, subg_ref, qa_ref, kan_ref, van_ref, qb_ref, sg_ref, *refs,
                        pps, lam_init):
    del pt_ref
    page_refs = refs[:4 * pps]
    o_ref = refs[4 * pps]
    qa_s, m_s, l_s, acca_s, c_s, accb_s = refs[4 * pps + 1:]
    step = pl.program_id(1)
    n_ha = qa_ref.shape[0]
    n_a = 2 * n_ha
    n_b = accb_s.shape[0] // HEAD_DIM
    wa = n_ha * LANES

    @pl.when(step == 0)
    def _():
        rows = _dup_rows(qa_ref[...].astype(F32), 2)
        qa = jnp.where(_iota(rows.shape, 1) // HEAD_DIM == _iota(rows.shape, 0) % 2, rows, 0.0)
        qa_s[...] = qa
        m_s[...] = jnp.sum(qa * _dup_rows(kan_ref[...], 2), axis=-1, keepdims=True)
        l_s[...] = jnp.ones_like(l_s)
        acca_s[...] = _dup_rows(van_ref[...], 2)
        c_s[...] = jnp.zeros_like(c_s)
        accb_s[...] = jnp.zeros_like(accb_s)

    rows_a = pps * page_refs[0].shape[0]
    own_head = _iota((n_a, rows_a), 1) % n_ha == _iota((n_a, rows_a), 0) // 2
    ak = jnp.concatenate([page_refs[4 * u][...].astype(BF16) for u in range(pps)], axis=0)
    av = jnp.concatenate([page_refs[4 * u + 1][...].astype(BF16) for u in range(pps)], axis=0)
    m_s[...], l_s[...], acca_s[...] = _softmax_tile(qa_s[...].astype(BF16), ak, av,
                                                    (m_s[...], l_s[...], acca_s[...]), own_head)
    zs = [_head_dots(qb_ref, page_refs[4 * u + 2], n_b) for u in range(pps)]
    w, c_s[...] = _sb_weights_pages(zs, _later_matrix(LANES), c_s[...])
    _add_weighted(accb_s, w, [page_refs[4 * u + 3] for u in range(pps)], n_b)

    @pl.when(step == pl.num_programs(1) - 1)
    def _():
        lam = _lambda_value(lam_ref[...], lam_init)
        a = acca_s[...] / l_s[...]
        for h in range(n_ha):
            sl = slice(h * LANES, (h + 1) * LANES)
            x = a[2 * h:2 * h + 1, :] - lam * a[2 * h + 1:2 * h + 2, :]
            ms = jnp.mean(x * x, axis=-1, keepdims=True)
            x = x * lax.rsqrt(ms + RMS_EPS) * subg_ref[...] * (1.0 - lam_init)
            o_ref[:, sl] = (x * sg_ref[:, sl]).astype(BF16)
        o_ref[:, wa:] = (_lane_sums_as_row(accb_s[...]) * sg_ref[:, wa:]).astype(BF16)


def _page_specs(cache, li, n_pages, pps, reverse):
    r = cache.shape[2]

    def make(u):
        def index(b, s, pt):
            j = s * pps + u
            j = n_pages - 1 - j if reverse else j
            return (li, pt[b, j], 0, 0)
        return pl.BlockSpec((None, None, r, LANES), index)

    return [make(u) for u in range(pps)]


def _col_bcast(x):
    return jnp.broadcast_to(x.astype(F32)[:, :, None], x.shape + (LANES,))


def _decode_even(pt, lam_p, subln_g, lam_init, qa16, ka, va, qb16, sgate, caches, li, pps):
    db, n_pages = pt.shape
    wa, wb = qa16.shape[-1], qb16.shape[-1]
    n_ha = wa // LANES
    whole = lambda shp: pl.BlockSpec((None,) + shp, lambda b, s, pt: (b,) + (0,) * len(shp))
    specs_by_cache = [_page_specs(c, li, n_pages, pps, True) for c in caches]
    page_specs = [specs_by_cache[t][u] for u in range(pps) for t in range(4)]
    page_args = [caches[t] for u in range(pps) for t in range(4)]
    grid_spec = pltpu.PrefetchScalarGridSpec(
        num_scalar_prefetch=1,
        grid=(db, n_pages // pps),
        in_specs=[pl.BlockSpec(lam_p.shape, lambda b, s, pt: (0, 0)),
                  pl.BlockSpec((1, LANES), lambda b, s, pt: (0, 0)),
                  whole((n_ha, LANES)), whole((n_ha, LANES)), whole((n_ha, LANES)), whole((wb, LANES)),
                  whole((1, wa + wb))] + page_specs,
        out_specs=whole((1, wa + wb)),
        scratch_shapes=[pltpu.VMEM((2 * n_ha, LANES), F32),
                        pltpu.VMEM((2 * n_ha, 1), F32), pltpu.VMEM((2 * n_ha, 1), F32),
                        pltpu.VMEM((2 * n_ha, LANES), F32),
                        pltpu.VMEM((wb // HEAD_DIM, 1), F32), pltpu.VMEM((wb, LANES), F32)],
    )
    heads = lambda x: x.reshape(db, n_ha, LANES)
    return pl.pallas_call(
        functools.partial(_decode_even_kernel, pps=pps, lam_init=lam_init),
        grid_spec=grid_spec,
        out_shape=jax.ShapeDtypeStruct((db, 1, wa + wb), BF16),
        compiler_params=_params(("parallel", "arbitrary")),
        name="decode_even",
    )(pt, lam_p, subln_g.reshape(1, LANES), heads(qa16), heads(ka), heads(va), _col_bcast(qb16),
      sgate.reshape(db, 1, wa + wb), *page_args)


def _moba_scores_kernel(pt_ref, qcol_ref, qrow_ref, kn_ref, *refs, pps, n_blocks):
    del pt_ref
    page_refs = refs[:pps]
    p_ref, pnew_ref, ids_ref = refs[pps:pps + 3]
    s_s = refs[pps + 3]
    step = pl.program_id(1)
    n_h = qcol_ref.shape[0] // HEAD_DIM
    ppb = MOBA_BLOCK // LANES

    for u in range(pps):
        s_s[step * pps + u] = _head_dots(qcol_ref, page_refs[u], n_h)

    @pl.when(step == pl.num_programs(1) - 1)
    def _():
        lane = _iota((n_h, LANES), 1)
        g = jnp.zeros((n_h, LANES), F32)
        for n in range(n_blocks):
            tot = sum(s_s[n * ppb + t] for t in range(ppb))
            g = jnp.where(lane == n, jnp.sum(tot, axis=-1, keepdims=True) * (1.0 / MOBA_BLOCK), g)
        picks = []
        sel = _top_blocks(g, n_blocks, picks=picks)
        ids = jnp.zeros((n_h, LANES), jnp.int32)
        for r, idx in enumerate(picks):
            ids = jnp.where(lane == r, idx, ids)
        ids_ref[...] = ids
        chosen = [sel[:, n:n + 1] for n in range(n_blocks)]
        s_new = jnp.sum(_row_heads(qrow_ref[...], n_h, HEAD_DIM) * kn_ref[...], axis=-1, keepdims=True) * QK_SCALE
        m_lanes = jnp.full((n_h, LANES), NEG, F32)
        for j in range(n_blocks * ppb):
            m_lanes = jnp.maximum(m_lanes, jnp.where(chosen[j // ppb], s_s[j] * QK_SCALE, NEG))
        m = jnp.maximum(jnp.max(m_lanes, axis=-1, keepdims=True), s_new)
        l_lanes = jnp.zeros((n_h, LANES), F32)
        for j in range(n_blocks * ppb):
            e = jnp.where(chosen[j // ppb], jnp.exp(s_s[j] * QK_SCALE - m), 0.0)
            s_s[j] = e
            l_lanes = l_lanes + e
        e_new = jnp.exp(s_new - m)
        inv = 1.0 / (jnp.sum(l_lanes, axis=-1, keepdims=True) + e_new)
        for j in range(n_blocks * ppb):
            p_ref[j] = s_s[j] * inv
        pnew_ref[...] = jnp.broadcast_to(e_new * inv, pnew_ref.shape)


def _moba_values_kernel(pt_ref, ids_ref, p_ref, pnew_ref, vn_ref, sg_ref, *refs, n_sel, ppb, heads_per_step):
    del pt_ref
    slices, o_ref = refs[:-1], refs[-1]
    b, pair = pl.program_id(0), pl.program_id(1)
    accs = []
    for c in range(heads_per_step):
        h = pair * heads_per_step + c
        acc = jnp.zeros((HEAD_DIM, LANES), F32)
        for r in range(n_sel):
            blk = ids_ref[b, h * n_sel + r]
            for u in range(ppb):
                w = p_ref[blk * ppb + u, pl.ds(h, 1), :]
                acc = acc + w * slices[(c * n_sel + r) * ppb + u][...]
        accs.append(acc)
    o = _lane_sums_as_row(jnp.concatenate(accs, axis=0)) + pnew_ref[...] * vn_ref[...]
    o_ref[...] = (o * sg_ref[...]).astype(BF16)


def _decode_odd(pt, q, k_new, v_new, sgate, cache_k, cache_v, li, pps):
    db, n_pages = pt.shape
    w = q.shape[-1]
    n_h = w // HEAD_DIM
    page = LANES
    past = n_pages * page
    assert past % MOBA_BLOCK == 0 and past >= MOBA_BLOCK and MOBA_BLOCK % page == 0
    n_blocks = past // MOBA_BLOCK
    row = lambda wd: pl.BlockSpec((None, 1, wd), lambda b, s, pt: (b, 0, 0))
    r3 = lambda x: x.reshape(db, 1, x.shape[-1])
    assert n_blocks >= MOBA_TOPK
    per_head = pl.BlockSpec((None, n_h, LANES), lambda b, s, pt: (b, 0, 0))
    probs, p_new, ids = pl.pallas_call(
        functools.partial(_moba_scores_kernel, pps=pps, n_blocks=n_blocks),
        grid_spec=pltpu.PrefetchScalarGridSpec(
            num_scalar_prefetch=1,
            grid=(db, n_pages // pps),
            in_specs=[pl.BlockSpec((None, w, LANES), lambda b, s, pt: (b, 0, 0)), row(w), row(w)]
            + _page_specs(cache_k, li, n_pages, pps, False),
            out_specs=[pl.BlockSpec((None, n_pages, n_h, page), lambda b, s, pt: (b, 0, 0, 0)),
                       per_head, per_head],
            scratch_shapes=[pltpu.VMEM((n_pages, n_h, page), F32)],
        ),
        out_shape=[jax.ShapeDtypeStruct((db, n_pages, n_h, page), F32),
                   jax.ShapeDtypeStruct((db, n_h, LANES), F32),
                   jax.ShapeDtypeStruct((db, n_h, LANES), jnp.int32)],
        compiler_params=_params(("parallel", "arbitrary")),
        name="decode_moba_scores",
    )(pt, _col_bcast(q), r3(q), r3(k_new), *([cache_k] * pps))
    p_new_row = jnp.repeat(p_new[:, :, 0], HEAD_DIM, axis=-1).reshape(db, 1, w)

    ppb = MOBA_BLOCK // page
    heads_per_step = math.gcd(n_h, VALUE_HEADS_PER_STEP)
    step_lanes = heads_per_step * HEAD_DIM
    assert step_lanes % LANES == 0
    block_ids = ids[:, :, :MOBA_TOPK].reshape(db, n_h * MOBA_TOPK)

    def slice_spec(c, r, u):
        def index(b, pair, pt, bid):
            h = pair * heads_per_step + c
            return (li, pt[b, bid[b, h * MOBA_TOPK + r] * ppb + u], h, 0)
        return pl.BlockSpec((None, None, HEAD_DIM, LANES), index)

    slice_specs = [slice_spec(c, r, u) for c in range(heads_per_step) for r in range(MOBA_TOPK)
                   for u in range(ppb)]
    lanes = lambda: pl.BlockSpec((None, 1, step_lanes), lambda b, pair, pt, bid: (b, 0, pair))
    return pl.pallas_call(
        functools.partial(_moba_values_kernel, n_sel=MOBA_TOPK, ppb=ppb, heads_per_step=heads_per_step),
        grid_spec=pltpu.PrefetchScalarGridSpec(
            num_scalar_prefetch=2,
            grid=(db, n_h // heads_per_step),
            in_specs=[pl.BlockSpec((None, n_pages, n_h, page), lambda b, pair, pt, bid: (b, 0, 0, 0)),
                      lanes(), lanes(), lanes()] + slice_specs,
            out_specs=lanes(),
        ),
        out_shape=jax.ShapeDtypeStruct((db, 1, w), BF16),
        compiler_params=_params(("parallel", "arbitrary")),
        name="decode_moba_values",
    )(pt, block_ids, probs, p_new_row, r3(v_new), r3(sgate), *([cache_v] * len(slice_specs)))


PROMPT_TM = 512
ATTN_TQ = 512
ATTN_TK = 512
SB_TILE = 256
MOBA_GROUP = 4
MOBA_TILE_BLOCKS = 2
PAGES_PER_STEP = 16
MOBA_PAGES_PER_STEP = 32
VALUE_HEADS_PER_STEP = 8


def _rows_view(c):
    l, n, p, h, hd = c.shape
    assert p == LANES and hd == LANES
    return c.reshape(l, n, p * h, hd)


def _cols_view(c):
    l, n, p, h, hd = c.shape
    assert p == LANES and hd == HEAD_DIM
    return jnp.transpose(c, (0, 1, 3, 4, 2)).reshape(l, n, h * hd, p)


def _heads_from_cols(x, n_heads):
    b, w, s = x.shape
    return jnp.transpose(x.reshape(b, n_heads, w // n_heads, s), (0, 3, 1, 2))


def _even_layer(x_p, x_s, caches, li, pt, g_norm, w_in, qk_g, lam_p, subln_g, w_out, layer):
    lam_init = 0.8 - 0.6 * math.exp(-0.3 * layer)
    b, s, d = x_p.shape
    db, t, _ = x_s.shape
    wa = w_out.shape[0] // 2
    wb = wa
    def plan(wide, narrow):
        return (("q_rope_bf16", 0, wa, None), ("k_rope", wa, wa, wide), ("kv_plain", 2 * wa, wa, wide),
                ("q_plain_bf16", 3 * wa, wb, None), ("kv_plain", 3 * wa + wb, wb, narrow),
                ("kv_plain", 3 * wa + 2 * wb, wb, narrow), ("gate", 3 * wa + 3 * wb, wa + wb, None))

    w_in16 = w_in.astype(BF16)
    w_out16 = w_out.astype(BF16)
    past = pt.shape[1] * LANES
    n_a = wa // (2 * HEAD_DIM)
    n_b = wb // HEAD_DIM

    tabs = _rope_tables(jnp.arange(s))
    qa16, ka, ka16, va, va16, qb16, kb, kb16, vb, vb16, sgate = _project(
        x_p.reshape(b * s, d), g_norm, w_in16, qk_g[0], qk_g[1], tabs, plan("heads", "cols"), PROMPT_TM, s,
        "proj_even_prompt")
    r3 = lambda a: a.reshape(b, s, a.shape[-1])
    sg3 = r3(sgate)
    mix_a = _diff_attn_prompt(r3(qa16), r3(ka16), r3(va16), sg3, lam_p, subln_g, lam_init, ATTN_TQ, ATTN_TK)
    mix_b = _sb_prompt(r3(qb16), r3(kb16), r3(vb16), sg3, wa // LANES, SB_TILE, SB_TILE)
    y_p = _finish(x_p.reshape(b * s, d), [mix_a.reshape(b * s, wa), mix_b.reshape(b * s, wb)], w_out16, PROMPT_TM,
                  "finish_even_prompt").reshape(b, s, d)

    tabs_s = _rope_tables(jnp.tile(past + jnp.arange(t), db))
    qa16_s, ka_s, _, va_s, _, qb16_s, kb_s, _, vb_s, _, sgate_s = _project(
        x_s.reshape(db * t, d), g_norm, w_in16, qk_g[0], qk_g[1], tabs_s, plan("rows", "rows"), db * t, db * t,
        "proj_even_sample")
    assert t == 1
    mix_s = _decode_even(pt, lam_p, subln_g, lam_init, qa16_s, ka_s, va_s, qb16_s, sgate_s,
                         caches, li, math.gcd(pt.shape[1], PAGES_PER_STEP))
    y_s = _finish(x_s.reshape(db * t, d), [mix_s.reshape(db * t, wa + wb)], w_out16, db * t,
                  "finish_even_sample").reshape(db, t, d)

    rows_p = (ka.reshape(b, s, n_a, 2 * HEAD_DIM), va.reshape(b, s, n_a, 2 * HEAD_DIM),
              _heads_from_cols(kb, n_b), _heads_from_cols(vb, n_b))
    rows_s = (ka_s.reshape(db, t, n_a, 2 * HEAD_DIM), va_s.reshape(db, t, n_a, 2 * HEAD_DIM),
              kb_s.reshape(db, t, n_b, HEAD_DIM), vb_s.reshape(db, t, n_b, HEAD_DIM))
    return y_p, y_s, rows_p, rows_s


def _odd_layer(x_p, x_s, cache_k, cache_v, li, pt, g_norm, w_in, qk_g, w_out):
    b, s, d = x_p.shape
    db, t, _ = x_s.shape
    wc = w_out.shape[0]
    w_in16 = w_in.astype(BF16)
    w_out16 = w_out.astype(BF16)
    past = pt.shape[1] * LANES
    assert s % MOBA_BLOCK == 0

    def plan(k_kind, layout):
        return (("q_rope_f32", 0, wc, None), (k_kind, wc, wc, layout), ("kv_plain", 2 * wc, wc, layout),
                ("gate", 3 * wc, wc, None))

    tabs = _rope_tables(jnp.arange(s))
    q, k, k16, kmean, v, v16, sgate = _project(
        x_p.reshape(b * s, d), g_norm, w_in16, qk_g[0], qk_g[1], tabs, plan("k_rope_mean", "cols"),
        MOBA_BLOCK * math.gcd(s // MOBA_BLOCK, PROMPT_TM // MOBA_BLOCK), s, "proj_odd_prompt")
    r3 = lambda a: a.reshape(b, s, a.shape[-1])
    mix = _moba_prompt(r3(q), r3(k16), r3(v16), kmean.reshape(b, s // MOBA_BLOCK, wc), r3(sgate))
    y_p = _finish(x_p.reshape(b * s, d), [mix.reshape(b * s, wc)], w_out16, PROMPT_TM,
                  "finish_odd_prompt").reshape(b, s, d)

    assert t == 1
    tabs_s = _rope_tables(jnp.tile(past + jnp.arange(t), db))
    q_s, k_s, _, v_s, _, sgate_s = _project(
        x_s.reshape(db * t, d), g_norm, w_in16, qk_g[0], qk_g[1], tabs_s, plan("k_rope", "rows"), db * t, db * t,
        "proj_odd_sample")
    mix_s = _decode_odd(pt, q_s, k_s, v_s, sgate_s, cache_k, cache_v, li,
                        math.gcd(pt.shape[1], MOBA_PAGES_PER_STEP))
    y_s = _finish(x_s.reshape(db * t, d), [mix_s.reshape(db * t, wc)], w_out16, db * t,
                  "finish_odd_sample").reshape(db, t, d)

    n_c = wc // HEAD_DIM
    return (y_p, y_s, (_heads_from_cols(k, n_c), _heads_from_cols(v, n_c)),
            (k_s.reshape(db, t, n_c, HEAD_DIM), v_s.reshape(db, t, n_c, HEAD_DIM)))


def kernel(x_prompt, x_sample, cache_a_k, cache_a_v, cache_b_k, cache_b_v, cache_c_k, cache_c_v, page_table,
           norm_even, w_in_even, qk_norm_a, lambda_a, subln_a, w_out_even, norm_odd, w_in_odd, qk_norm_c,
           w_out_odd):
    depth = norm_even.shape[0] + norm_odd.shape[0]
    even_caches = [_rows_view(cache_a_k), _rows_view(cache_a_v), _cols_view(cache_b_k), _cols_view(cache_b_v)]
    odd_k, odd_v = _cols_view(cache_c_k), _cols_view(cache_c_v)
    y_p, y_s = x_prompt, x_sample
    ev_p, ev_s, od_p, od_s = [], [], [], []
    for layer in range(depth):
        i = layer // 2
        if layer % 2 == 0:
            y_p, y_s, rp, rs = _even_layer(y_p, y_s, even_caches, i, page_table, norm_even[i], w_in_even[i],
                                           qk_norm_a[i], lambda_a[i], subln_a[i], w_out_even[i], layer)
            ev_p.append(rp)
            ev_s.append(rs)
        else:
            y_p, y_s, rp, rs = _odd_layer(y_p, y_s, odd_k, odd_v, i, page_table, norm_odd[i], w_in_odd[i],
                                          qk_norm_c[i], w_out_odd[i])
            od_p.append(rp)
            od_s.append(rs)

    def stack(rows, j):
        return jnp.stack([r[j] for r in rows])

    return (y_p, y_s,
            stack(ev_p, 0), stack(ev_p, 1), stack(ev_p, 2), stack(ev_p, 3), stack(od_p, 0), stack(od_p, 1),
            stack(ev_s, 0), stack(ev_s, 1), stack(ev_s, 2), stack(ev_s, 3), stack(od_s, 0), stack(od_s, 1))
```

```python
import functools
import math

import jax
import jax.numpy as jnp
from jax import lax
from jax.experimental import pallas as pl
from jax.experimental.pallas import tpu as pltpu

HEAD_DIM = 64
ROT_DIM = HEAD_DIM // 4
ROPE_THETA = 500000.0
RMS_EPS = 1e-6
MOBA_BLOCK = 256
MOBA_TOPK = 3
LANES = 128
SUBLANES = 8
NEG = -1e30
SB_DEAD_LOG2 = 150.0
QK_SCALE = HEAD_DIM ** -0.5
Q_SOFTMAX_SCALE = QK_SCALE * math.log2(math.e)
ATTN_SLAB = 256
VMEM_LIMIT = 52 * 1024 * 1024

F32 = jnp.float32
BF16 = jnp.bfloat16


def _params(sem, vmem=VMEM_LIMIT):
    return pltpu.CompilerParams(dimension_semantics=sem, vmem_limit_bytes=vmem)


def _dot(a, b):
    return jnp.dot(a, b, preferred_element_type=F32)


def _dot_nt(a, b):
    return lax.dot_general(a, b, (((1,), (1,)), ((), ())), preferred_element_type=F32)


def _split_bf16(x):
    hi = x.astype(BF16)
    lo = (x - hi.astype(F32)).astype(BF16)
    return hi, lo


def _iota(shape, dim):
    return lax.broadcasted_iota(jnp.int32, shape, dim)


def _head_seg_matrix():
    return (_iota((LANES, LANES), 0) // HEAD_DIM == _iota((LANES, LANES), 1) // HEAD_DIM).astype(BF16)


def _norm_rope(x, g, c, s1, s2, seg):
    hi, lo = _split_bf16(x * x)
    ms = (_dot(hi, seg) + _dot(lo, seg)) * (1.0 / HEAD_DIM)
    xn = x * lax.rsqrt(ms + RMS_EPS) * g
    half = ROT_DIM // 2
    return xn * c + pltpu.roll(xn, LANES - half, 1) * s1 + pltpu.roll(xn, half, 1) * s2


def _proj_kernel(x_ref, gn_ref, w_ref, gq_ref, gk_ref, c_ref, s1_ref, s2_ref, *out_refs, plan):
    x = x_ref[...]
    ms = jnp.mean(x * x, axis=-1, keepdims=True)
    xn = (x * lax.rsqrt(ms + RMS_EPS) * gn_ref[...]).astype(BF16)
    seg = _head_seg_matrix()
    c, s1, s2 = c_ref[...], s1_ref[...], s2_ref[...]
    outs = list(out_refs)
    tm = x.shape[0]

    def store_rows(o_ref, layout, t, n_t, y):
        if layout == "rows":
            o_ref[:, t * LANES:(t + 1) * LANES] = y
        elif layout == "cols":
            o_ref[t * LANES:(t + 1) * LANES, :] = jnp.transpose(y)
        elif layout == "heads":
            o_ref[pl.ds(t, tm, stride=n_t), :] = y
        else:
            raise ValueError(layout)

    for kind, col0, width, layout in plan:
        h = _dot(xn, w_ref[:, col0:col0 + width])
        n_t = width // LANES
        if kind in ("q_rope_bf16", "q_rope_f32", "k_rope", "k_rope_mean"):
            g = gq_ref[...] if kind.startswith("q") else gk_ref[...]
            if kind == "q_rope_bf16":
                o_ref = outs.pop(0)
            elif kind == "q_rope_f32":
                o_ref = outs.pop(0)
            else:
                o_ref, o16_ref = outs.pop(0), outs.pop(0)
                mean_ref = outs.pop(0) if kind == "k_rope_mean" else None
            for t in range(n_t):
                sl = slice(t * LANES, (t + 1) * LANES)
                y = _norm_rope(h[:, sl], g, c, s1, s2, seg)
                if kind == "q_rope_bf16":
                    o_ref[:, sl] = (y * Q_SOFTMAX_SCALE).astype(BF16)
                elif kind == "q_rope_f32":
                    o_ref[:, sl] = y
                else:
                    store_rows(o_ref, layout, t, n_t, y)
                    o16_ref[:, sl] = y.astype(BF16)
                    if mean_ref is not None:
                        for j in range(tm // MOBA_BLOCK):
                            mean_ref[j, :, sl] = jnp.mean(y[j * MOBA_BLOCK:(j + 1) * MOBA_BLOCK], axis=0,
                                                          keepdims=True)
        elif kind == "q_plain_bf16":
            outs.pop(0)[...] = (h * Q_SOFTMAX_SCALE).astype(BF16)
        elif kind == "kv_plain":
            o_ref = outs.pop(0)
            for t in range(n_t):
                store_rows(o_ref, layout, t, n_t, h[:, t * LANES:(t + 1) * LANES])
            outs.pop(0)[...] = h.astype(BF16)
        elif kind == "gate":
            outs.pop(0)[...] = (h / (1.0 + jnp.exp(-h))).astype(BF16)
        else:
            raise ValueError(kind)
    assert not outs


def _rope_tables(pos):
    half = ROT_DIM // 2
    inv = ROPE_THETA ** (-jnp.arange(0, ROT_DIM, 2, dtype=F32) / ROT_DIM)
    ang = pos.astype(F32)[:, None] * inv[None, :]
    cos, sin = jnp.cos(ang), jnp.sin(ang)
    t = pos.shape[0]
    z_half = jnp.zeros((t, half), F32)
    z_rest = jnp.zeros((t, HEAD_DIM - ROT_DIM), F32)
    c = jnp.concatenate([cos, cos, jnp.ones((t, HEAD_DIM - ROT_DIM), F32)], axis=-1)
    s1 = jnp.concatenate([-sin, z_half, z_rest], axis=-1)
    s2 = jnp.concatenate([z_half, sin, z_rest], axis=-1)
    rep = LANES // HEAD_DIM
    return tuple(jnp.tile(a, (1, rep)) for a in (c, s1, s2))


def _project(x, g_norm, w_bf16, gq, gk, tables, plan, tm, seq, name):
    m, d = x.shape
    n_tab = tables[0].shape[0] // tm
    tiles_per_seq = seq // tm
    out_shapes, out_specs = [], []
    for kind, _, width, layout in plan:
        row = pl.BlockSpec((tm, width), lambda i: (i, 0))
        if kind in ("q_rope_bf16", "q_plain_bf16", "gate"):
            out_shapes += [jax.ShapeDtypeStruct((m, width), BF16)]
            out_specs += [row]
        elif kind == "q_rope_f32":
            out_shapes += [jax.ShapeDtypeStruct((m, width), F32)]
            out_specs += [row]
        elif kind in ("k_rope", "kv_plain", "k_rope_mean"):
            if layout == "rows":
                out_shapes += [jax.ShapeDtypeStruct((m, width), F32)]
                out_specs += [row]
            elif layout == "cols":
                out_shapes += [jax.ShapeDtypeStruct((m // seq, width, seq), F32)]
                out_specs += [pl.BlockSpec((None, width, tm),
                                           lambda i: (i // tiles_per_seq, 0, i % tiles_per_seq))]
            else:
                n_t = width // LANES
                out_shapes += [jax.ShapeDtypeStruct((m * n_t, LANES), F32)]
                out_specs += [pl.BlockSpec((tm * n_t, LANES), lambda i: (i, 0))]
            out_shapes += [jax.ShapeDtypeStruct((m, width), BF16)]
            out_specs += [row]
            if kind == "k_rope_mean":
                assert tm % MOBA_BLOCK == 0
                out_shapes += [jax.ShapeDtypeStruct((m // MOBA_BLOCK, 1, width), F32)]
                out_specs += [pl.BlockSpec((tm // MOBA_BLOCK, 1, width), lambda i: (i, 0, 0))]
    tab_spec = pl.BlockSpec((tm, LANES), lambda i: (i % n_tab, 0))
    vec_d = pl.BlockSpec((1, d), lambda i: (0, 0))
    vec_l = pl.BlockSpec((1, LANES), lambda i: (0, 0))
    rep = LANES // HEAD_DIM
    return pl.pallas_call(
        functools.partial(_proj_kernel, plan=plan),
        grid=(m // tm,),
        in_specs=[pl.BlockSpec((tm, d), lambda i: (i, 0)), vec_d,
                  pl.BlockSpec(w_bf16.shape, lambda i: (0, 0), pipeline_mode=pl.Buffered(1)),
                  vec_l, vec_l, tab_spec, tab_spec, tab_spec],
        out_specs=out_specs,
        out_shape=out_shapes,
        compiler_params=_params(("parallel",)),
        name=name,
    )(x, g_norm.reshape(1, d), w_bf16, jnp.tile(gq, rep).reshape(1, LANES), jnp.tile(gk, rep).reshape(1, LANES),
      *tables)


def _finish_kernel(x_ref, w_ref, *refs):
    mix_refs, y_ref = refs[:-1], refs[-1]
    y = x_ref[...]
    row0 = 0
    for mix_ref in mix_refs:
        rows = mix_ref.shape[1]
        y = y + _dot(mix_ref[...], w_ref[row0:row0 + rows, :])
        row0 += rows
    y_ref[...] = y


def _finish(x, mix_parts, w_bf16, tm, name):
    m, d = x.shape
    kdim = w_bf16.shape[0]
    assert sum(p.shape[1] for p in mix_parts) == kdim
    return pl.pallas_call(
        _finish_kernel,
        grid=(m // tm,),
        in_specs=[pl.BlockSpec((tm, d), lambda i: (i, 0)), pl.BlockSpec((kdim, d), lambda i: (0, 0))]
        + [pl.BlockSpec((tm, p.shape[1]), lambda i: (i, 0)) for p in mix_parts],
        out_specs=pl.BlockSpec((tm, d), lambda i: (i, 0)),
        out_shape=jax.ShapeDtypeStruct((m, d), F32),
        compiler_params=_params(("parallel",)),
        name=name,
    )(x, w_bf16, *mix_parts)


def _lambda_value(lp, lam_init):
    a = jnp.sum(lp[0:1] * lp[1:2], axis=-1, keepdims=True)
    b = jnp.sum(lp[2:3] * lp[3:4], axis=-1, keepdims=True)
    return jnp.exp(a) - jnp.exp(b) + lam_init


def _softmax_tile(qq, k, v, carry, mask):
    m, l, acc = carry
    s = _dot_nt(qq, k)
    if mask is not None:
        s = jnp.where(mask, s, NEG)
    m_new = jnp.maximum(m, jnp.max(s, axis=-1, keepdims=True))
    alpha = jnp.exp2(m - m_new)
    p = jnp.exp2(s - m_new)
    l = alpha * l + jnp.sum(p, axis=-1, keepdims=True)
    acc = alpha * acc + _dot(p.astype(BF16), v)
    return m_new, l, acc


def _dot_tn(a, b):
    return lax.dot_general(a, b, (((0,), (0,)), ((), ())), preferred_element_type=F32)


def _softmax_tile_t(qq_t, k_ref, v_ref, start, n_keys, carry, mask):
    s, m_s, bias = _scores_t(qq_t, k_ref, start, n_keys, mask)
    return _absorb_t(v_ref, start, n_keys, s, m_s, bias, carry)


def _scores_t(qq_t, k_ref, start, n_keys, mask):
    s = _dot(k_ref[pl.ds(start, n_keys), :], qq_t)
    slabs = [s[c * ATTN_SLAB:(c + 1) * ATTN_SLAB] for c in range(n_keys // ATTN_SLAB)]
    if mask is None:
        return s, jnp.max(s, axis=0, keepdims=True), None
    if mask[0].dtype == jnp.bool_:
        s = jnp.concatenate([jnp.where(mk, x, NEG) for x, mk in zip(slabs, mask)], axis=0)
        return s, jnp.max(s, axis=0, keepdims=True), None
    m_s = None
    for x, row in zip(slabs, mask):
        m_c = jnp.max(x, axis=0, keepdims=True) + row
        m_s = m_c if m_s is None else jnp.maximum(m_s, m_c)
    return s, m_s, jnp.concatenate(mask, axis=0)


def _absorb_t(v_ref, start, n_keys, s, m_s, bias, carry):
    m, l, acc = carry
    m_new = jnp.maximum(m, m_s)
    alpha = jnp.exp2(m - m_new)
    if bias is None:
        p = jnp.exp2(s - m_new)
    else:
        shift = bias - m_new
        p = jnp.concatenate([jnp.exp2(s[c * ATTN_SLAB:(c + 1) * ATTN_SLAB] + shift[c:c + 1])
                             for c in range(n_keys // ATTN_SLAB)], axis=0)
    l = alpha * l + jnp.sum(p, axis=0, keepdims=True)
    v, p16 = v_ref[pl.ds(start, n_keys), :], p.astype(BF16)
    pv = _dot_tn(v, p16) if acc.shape[0] == LANES else _pair_values_t(v, p16)
    return m_new, l, alpha * acc + pv


def _pair_values_t(v, w16):
    half = w16.shape[1] // 2
    return jnp.concatenate([_dot_tn(v[:, :HEAD_DIM], w16[:, :half]), _dot_tn(v[:, HEAD_DIM:], w16[:, half:])],
                           axis=1)


def _merge_pair_t(x, tq):
    return jnp.transpose(jnp.concatenate([x[:, :tq], x[:, tq:]], axis=0))


def _pipelined_pairs(n_pairs, n_keys, qq_t, k_ref, v_ref, mask_of, s_refs, carry, head=None):
    s_a, s_b = s_refs

    def scores(t, s_ref):
        s, m_s, bias = _scores_t(qq_t, k_ref, pl.multiple_of(t * n_keys, n_keys), n_keys, mask_of(t))
        s_ref[...] = s
        return m_s, bias

    def absorb(t, s_ref, stats, carry):
        return _absorb_t(v_ref, pl.multiple_of(t * n_keys, n_keys), n_keys, s_ref[...], *stats, carry)

    def trip(u, state):
        m_a, carry = state
        m_b = scores(2 * u + 1, s_b)
        carry = absorb(2 * u, s_a, m_a, carry)
        m_a = scores(2 * u + 2, s_a)
        carry = absorb(2 * u + 1, s_b, m_b, carry)
        return m_a, carry

    def absorb_head(carry):
        for s_h, stats_h, start_h in head or ():
            carry = _absorb_t(v_ref, start_h, n_keys, s_h, *stats_h, carry)
        return carry

    def run(carry):
        m_a = scores(n_pairs * 0, s_a)
        carry = absorb_head(carry)
        m_a, carry = lax.fori_loop(0, n_pairs - 1, trip, (m_a, carry))
        last = 2 * (n_pairs - 1)
        m_b = scores(last + 1, s_b)
        carry = absorb(last, s_a, m_a, carry)
        return absorb(last + 1, s_b, m_b, carry)

    return lax.cond(n_pairs > 0, run, absorb_head, carry)


def _causal_masks(k_start, q_start, tq, n_keys, r, strict):
    q_pos = q_start + _iota((ATTN_SLAB, r), 1) % tq
    masks = []
    for c in range(n_keys // ATTN_SLAB):
        k_pos = k_start + c * ATTN_SLAB + _iota((ATTN_SLAB, r), 0)
        masks.append(k_pos < q_pos if strict else k_pos <= q_pos)
    return masks


def _stack_pair_t(q_t):
    row = _iota(q_t.shape, 0)
    zero = jnp.zeros_like(q_t)
    return jnp.concatenate([jnp.where(row < HEAD_DIM, q_t, zero), jnp.where(row >= HEAD_DIM, q_t, zero)], axis=1)


def _diff_attn_kernel(lam_ref, subg_ref, q_ref, k_ref, v_ref, sg_ref, o_ref, s_a, s_b, *, tq, tk, lam_init):
    i = pl.program_id(2)
    r = 2 * tq
    qq_t = _stack_pair_t(jnp.transpose(q_ref[...].astype(F32))).astype(BF16)
    n_full = (i * tq) // tk
    half = tk // 2

    carry = (jnp.full((1, r), NEG, F32), jnp.zeros((1, r), F32), jnp.zeros((LANES, r), F32))
    start = pl.multiple_of(n_full * tk, tk)
    diag = []
    for c, mask in enumerate(_causal_masks(start, i * tq, tq, tk, r, strict=False)):
        st = pl.multiple_of(start + c * ATTN_SLAB, ATTN_SLAB)
        s, m_s, _ = _scores_t(qq_t, k_ref, st, ATTN_SLAB, [mask])
        diag.append((s, (m_s, None), st))
    assert half == ATTN_SLAB
    _, l, acc = _pipelined_pairs(n_full, half, qq_t, k_ref, v_ref, lambda t: None, (s_a, s_b), carry, head=diag)

    a = acc / l
    lam = _lambda_value(lam_ref[...], lam_init)
    o = jnp.transpose(a[:, :tq] - lam * a[:, tq:])
    ms = jnp.mean(o * o, axis=-1, keepdims=True)
    o = o * lax.rsqrt(ms + RMS_EPS) * subg_ref[...] * (1.0 - lam_init)
    o_ref[...] = (o * sg_ref[...]).astype(BF16)


def _softplus2(z):
    return jnp.maximum(z, 0.0) + jnp.log2(1.0 + jnp.exp2(-jnp.abs(z)))


def _later_matrix(tk, keys_axis=1):
    a, b = _iota((tk, tk), 0), _iota((tk, tk), 1)
    return (a > b if keys_axis == 1 else b > a).astype(BF16)


def _sb_kernel(q_ref, k_ref, v_ref, sg_ref, o_ref, *, tq, tk):
    i = pl.program_id(2)
    r = 2 * tq
    qq_t = _stack_pair_t(jnp.transpose(q_ref[...].astype(F32))).astype(BF16)
    tri = _later_matrix(ATTN_SLAB, keys_axis=0)
    assert tq == tk == ATTN_SLAB

    def tile(st, n_slabs, c, acc, valid):
        parts = []
        for u in range(n_slabs):
            z = _dot(k_ref[pl.ds(st + u * ATTN_SLAB, ATTN_SLAB), :], qq_t)
            sp = _softplus2(z)
            log_keep = -sp if valid is None else jnp.where(valid[u], -sp, 0.0)
            later = _dot(tri, log_keep.astype(BF16))
            parts.append((z - sp + later, later[0:1, :] + log_keep[0:1, :]))
        for u in reversed(range(n_slabs)):
            e, whole = parts[u]
            w = jnp.exp2(e + c)
            if valid is not None:
                w = jnp.where(valid[u], w, 0.0)
            acc = acc + _pair_values_t(v_ref[pl.ds(st + u * ATTN_SLAB, ATTN_SLAB), :], w.astype(BF16))
            c = c + whole
        return c, acc

    n_rest = jnp.maximum(i - 1, 0)
    start = pl.multiple_of(n_rest * tk, tk)
    c, acc = tile(start, 2, jnp.zeros((1, r), F32), jnp.zeros((HEAD_DIM, r), F32),
                  _causal_masks(start, i * tq, tq, 2 * tk, r, strict=True))

    def live(state):
        t, c_max, _, _ = state
        return jnp.logical_and(t < n_rest, c_max > -SB_DEAD_LOG2)

    def full_tile(state):
        t, _, c, acc = state
        c, acc = tile(pl.multiple_of((n_rest - 1 - t) * tk, tk), 1, c, acc, None)
        return t + 1, jnp.max(c), c, acc

    _, _, _, acc = lax.while_loop(live, full_tile, (jnp.int32(0), jnp.max(c), c, acc))
    o_ref[...] = (_merge_pair_t(acc, tq) * sg_ref[...]).astype(BF16)


def _top_blocks(g, n_valid, axis=-1, picks=None):
    axis = axis % g.ndim
    blk = _iota(g.shape, axis)
    nb = g.shape[axis]
    g = jnp.where(blk < n_valid, g, -jnp.inf)
    sel = jnp.zeros(g.shape, jnp.bool_)
    for _ in range(min(MOBA_TOPK, nb)):
        mx = jnp.max(g, axis=axis, keepdims=True)
        idx = jnp.min(jnp.where(g == mx, blk, nb), axis=axis, keepdims=True)
        pick = blk == idx
        sel = jnp.logical_or(sel, pick)
        g = jnp.where(pick, -jnp.inf, g)
        if picks is not None:
            picks.append(idx)
    return jnp.logical_and(sel, blk < n_valid)


def _moba_kernel(q_ref, k_ref, v_ref, km_ref, sg_ref, o_ref, sel_s, s_a, s_b, *, tq, group):
    i = pl.program_id(2)
    r = 2 * tq
    blk = MOBA_BLOCK
    tile_blocks = tq // blk
    first = i * tile_blocks
    qf_t = _stack_pair_t(jnp.transpose(q_ref[...]))
    q_hi, q_lo = _split_bf16(qf_t)
    qq_t = (qf_t * Q_SOFTMAX_SCALE).astype(BF16)

    km_hi, km_lo = _split_bf16(km_ref[...])
    g = _dot(km_hi, q_hi) + _dot(km_hi, q_lo) + _dot(km_lo, q_hi)
    col = _iota((1, r), 1) % tq
    own = col // blk
    sel = _top_blocks(g, first + own, axis=0)
    block_id = _iota(sel.shape, 0)
    sel_s[...] = jnp.where(jnp.logical_and(sel, block_id < first), 0.0, NEG)

    start = pl.multiple_of(first * blk, tq)
    causal = _iota((blk, r), 0) <= col % blk
    masks = []
    for a in range(tile_blocks):
        picked = jnp.max(jnp.where(jnp.logical_and(sel, block_id == first + a), 1.0, 0.0), axis=0,
                         keepdims=True) > 0.0
        masks.append(jnp.logical_or(jnp.logical_and(own == a, causal), jnp.logical_and(own > a, picked)))
    carry = (jnp.full((1, r), NEG, F32), jnp.zeros((1, r), F32), jnp.zeros((HEAD_DIM, r), F32))
    s_own, m_own, _ = _scores_t(qq_t, k_ref, start, tq, masks)

    per_tile = group // 2
    assert per_tile * blk == tq

    def chosen(t):
        return [sel_s[pl.ds(t * per_tile + c, 1), :] for c in range(per_tile)]

    _, l, acc = _pipelined_pairs((first + group - 1) // group, tq, qq_t, k_ref, v_ref, chosen, (s_a, s_b), carry,
                                 head=[(s_own, (m_own, None), start)])
    o_ref[...] = (_merge_pair_t(acc / l, tq) * sg_ref[...]).astype(BF16)


def _attn_specs(tq, s, group0=0):
    q_spec = pl.BlockSpec((None, tq, LANES), lambda b, h, i: (b, i, h + group0))
    kv_spec = pl.BlockSpec((None, s, LANES), lambda b, h, i: (b, 0, h))
    return q_spec, kv_spec


def _diff_attn_prompt(q16, k16, v16, sgate, lam_p, subln_g, lam_init, tq, tk):
    b, s, w = q16.shape
    q_spec, kv_spec = _attn_specs(tq, s)
    return pl.pallas_call(
        functools.partial(_diff_attn_kernel, tq=tq, tk=tk, lam_init=lam_init),
        grid=(b, w // LANES, s // tq),
        in_specs=[pl.BlockSpec(lam_p.shape, lambda b, h, i: (0, 0)),
                  pl.BlockSpec((1, LANES), lambda b, h, i: (0, 0)),
                  q_spec, kv_spec, kv_spec, q_spec],
        out_specs=q_spec,
        out_shape=jax.ShapeDtypeStruct((b, s, w), BF16),
        scratch_shapes=[pltpu.VMEM((tk // 2, 2 * tq), F32)] * 2,
        compiler_params=_params(("parallel", "parallel", "arbitrary")),
        name="diff_attn_prompt",
    )(lam_p, subln_g.reshape(1, LANES), q16, k16, v16, sgate)


def _sb_prompt(q16, k16, v16, sgate, gate_group0, tq, tk):
    b, s, w = q16.shape
    q_spec, kv_spec = _attn_specs(tq, s)
    sg_spec = pl.BlockSpec((None, tq, LANES), lambda b, h, i: (b, i, h + gate_group0))
    return pl.pallas_call(
        functools.partial(_sb_kernel, tq=tq, tk=tk),
        grid=(b, w // LANES, s // tq),
        in_specs=[q_spec, kv_spec, kv_spec, sg_spec],
        out_specs=q_spec,
        out_shape=jax.ShapeDtypeStruct((b, s, w), BF16),
        compiler_params=_params(("parallel", "parallel", "arbitrary")),
        name="sb_prompt",
    )(q16, k16, v16, sgate)


def _moba_prompt(q, k16, v16, kmean, sgate):
    b, s, w = q.shape
    assert MOBA_BLOCK == ATTN_SLAB
    nb = s // MOBA_BLOCK
    tq = MOBA_BLOCK * math.gcd(nb, MOBA_TILE_BLOCKS)
    q_spec, kv_spec = _attn_specs(tq, s)
    group = math.gcd(nb, MOBA_GROUP)
    return pl.pallas_call(
        functools.partial(_moba_kernel, tq=tq, group=group),
        grid=(b, w // LANES, s // tq),
        in_specs=[q_spec, kv_spec, kv_spec,
                  pl.BlockSpec((None, nb, LANES), lambda b, h, i: (b, 0, h)), q_spec],
        out_specs=q_spec,
        out_shape=jax.ShapeDtypeStruct((b, s, w), BF16),
        scratch_shapes=[pltpu.VMEM((nb, 2 * tq), F32)]
        + [pltpu.VMEM((group // 2 * MOBA_BLOCK, 2 * tq), F32)] * 2,
        compiler_params=_params(("parallel", "parallel", "arbitrary")),
        name="moba_prompt",
    )(q, k16, v16, kmean, sgate)


def _row_heads(x_row, n_rows, lanes_per_row):
    w = x_row.shape[1]
    keep = _iota((n_rows, w), 1) // lanes_per_row == _iota((n_rows, w), 0)
    return jnp.where(keep, jnp.broadcast_to(x_row, (n_rows, w)), 0.0)


def _dup_rows(x, reps):
    n, w = x.shape
    row = _iota((n * reps, w), 0) // reps
    out = jnp.zeros((n * reps, w), x.dtype)
    for h in range(n):
        out = jnp.where(row == h, x[h:h + 1, :], out)
    return out


def _stack_rows(rows):
    n, w = len(rows), rows[0].shape[1]
    row = _iota((n, w), 0)
    out = jnp.zeros((n, w), rows[0].dtype)
    for h, r in enumerate(rows):
        out = jnp.where(row == h, r, out)
    return out


def _head_dots(qcol_ref, kt_ref, n_heads):
    rows = []
    for h in range(n_heads):
        sl = slice(h * HEAD_DIM, (h + 1) * HEAD_DIM)
        rows.append(jnp.sum(qcol_ref[sl, :] * kt_ref[sl, :], axis=0, keepdims=True))
    return _stack_rows(rows)


def _add_weighted(acc_ref, w, vt_refs, n_heads):
    for h in range(n_heads):
        sl = slice(h * HEAD_DIM, (h + 1) * HEAD_DIM)
        a = acc_ref[sl, :]
        for u, vt_ref in enumerate(vt_refs):
            a = a + w[u * n_heads + h:u * n_heads + h + 1, :] * vt_ref[sl, :]
        acc_ref[sl, :] = a


def _sb_weights_pages(zs, tri, c):
    n_h = zs[0].shape[0]
    z = jnp.concatenate(zs, axis=0)
    sp = _softplus2(z)
    log_keep = -sp
    hi, lo = _split_bf16(log_keep)
    rows = z.shape[0]
    both = _dot(jnp.concatenate([hi, lo], axis=0), tri)
    later = both[:rows] + both[rows:]
    whole = later[:, 0:1] + log_keep[:, 0:1]
    offsets = []
    for u in range(len(zs)):
        offsets.append(c)
        c = c + whole[u * n_h:(u + 1) * n_h]
    return jnp.exp2(z - sp + later + jnp.concatenate(offsets, axis=0)), c


def _lane_sums_as_row(x):
    hi, lo = _split_bf16(x)
    ones = jnp.ones((SUBLANES, LANES), BF16)
    return (_dot_nt(ones, hi) + _dot_nt(ones, lo))[0:1, :]


def _decode_even_kernel(pt_ref, lam_ref, subg_ref, qa_ref, kan_ref, van_ref, qb_ref, sg_ref, *refs,
                        pps, lam_init):
    del pt_ref
    page_refs = refs[:4 * pps]
    o_ref = refs[4 * pps]
    qa_s, m_s, l_s, acca_s, c_s, accb_s = refs[4 * pps + 1:]
    step = pl.program_id(1)
    n_ha = qa_ref.shape[0]
    n_a = 2 * n_ha
    n_b = accb_s.shape[0] // HEAD_DIM
    wa = n_ha * LANES

    @pl.when(step == 0)
    def _():
        rows = _dup_rows(qa_ref[...].astype(F32), 2)
        qa = jnp.where(_iota(rows.shape, 1) // HEAD_DIM == _iota(rows.shape, 0) % 2, rows, 0.0)
        qa_s[...] = qa
        m_s[...] = jnp.sum(qa * _dup_rows(kan_ref[...], 2), axis=-1, keepdims=True)
        l_s[...] = jnp.ones_like(l_s)
        acca_s[...] = _dup_rows(van_ref[...], 2)
        c_s[...] = jnp.zeros_like(c_s)
        accb_s[...] = jnp.zeros_like(accb_s)

    rows_a = pps * page_refs[0].shape[0]
    own_head = _iota((n_a, rows_a), 1) % n_ha == _iota((n_a, rows_a), 0) // 2
    ak = jnp.concatenate([page_refs[4 * u][...].astype(BF16) for u in range(pps)], axis=0)
    av = jnp.concatenate([page_refs[4 * u + 1][...].astype(BF16) for u in range(pps)], axis=0)
    m_s[...], l_s[...], acca_s[...] = _softmax_tile(qa_s[...].astype(BF16), ak, av,
                                                    (m_s[...], l_s[...], acca_s[...]), own_head)
    zs = [_head_dots(qb_ref, page_refs[4 * u + 2], n_b) for u in range(pps)]
    w, c_s[...] = _sb_weights_pages(zs, _later_matrix(LANES), c_s[...])
    _add_weighted(accb_s, w, [page_refs[4 * u + 3] for u in range(pps)], n_b)

    @pl.when(step == pl.num_programs(1) - 1)
    def _():
        lam = _lambda_value(lam_ref[...], lam_init)
        a = acca_s[...] / l_s[...]
        for h in range(n_ha):
            sl = slice(h * LANES, (h + 1) * LANES)
            x = a[2 * h:2 * h + 1, :] - lam * a[2 * h + 1:2 * h + 2, :]
            ms = jnp.mean(x * x, axis=-1, keepdims=True)
            x = x * lax.rsqrt(ms + RMS_EPS) * subg_ref[...] * (1.0 - lam_init)
            o_ref[:, sl] = (x * sg_ref[:, sl]).astype(BF16)
        o_ref[:, wa:] = (_lane_sums_as_row(accb_s[...]) * sg_ref[:, wa:]).astype(BF16)


def _page_specs(cache, li, n_pages, pps, reverse):
    r = cache.shape[2]

    def make(u):
        def index(b, s, pt):
            j = s * pps + u
            j = n_pages - 1 - j if reverse else j
            return (li, pt[b, j], 0, 0)
        return pl.BlockSpec((None, None, r, LANES), index)

    return [make(u) for u in range(pps)]


def _col_bcast(x):
    return jnp.broadcast_to(x.astype(F32)[:, :, None], x.shape + (LANES,))


def _decode_even(pt, lam_p, subln_g, lam_init, qa16, ka, va, qb16, sgate, caches, li, pps):
    db, n_pages = pt.shape
    wa, wb = qa16.shape[-1], qb16.shape[-1]
    n_ha = wa // LANES
    whole = lambda shp: pl.BlockSpec((None,) + shp, lambda b, s, pt: (b,) + (0,) * len(shp))
    specs_by_cache = [_page_specs(c, li, n_pages, pps, True) for c in caches]
    page_specs = [specs_by_cache[t][u] for u in range(pps) for t in range(4)]
    page_args = [caches[t] for u in range(pps) for t in range(4)]
    grid_spec = pltpu.PrefetchScalarGridSpec(
        num_scalar_prefetch=1,
        grid=(db, n_pages // pps),
        in_specs=[pl.BlockSpec(lam_p.shape, lambda b, s, pt: (0, 0)),
                  pl.BlockSpec((1, LANES), lambda b, s, pt: (0, 0)),
                  whole((n_ha, LANES)), whole((n_ha, LANES)), whole((n_ha, LANES)), whole((wb, LANES)),
                  whole((1, wa + wb))] + page_specs,
        out_specs=whole((1, wa + wb)),
        scratch_shapes=[pltpu.VMEM((2 * n_ha, LANES), F32),
                        pltpu.VMEM((2 * n_ha, 1), F32), pltpu.VMEM((2 * n_ha, 1), F32),
                        pltpu.VMEM((2 * n_ha, LANES), F32),
                        pltpu.VMEM((wb // HEAD_DIM, 1), F32), pltpu.VMEM((wb, LANES), F32)],
    )
    heads = lambda x: x.reshape(db, n_ha, LANES)
    return pl.pallas_call(
        functools.partial(_decode_even_kernel, pps=pps, lam_init=lam_init),
        grid_spec=grid_spec,
        out_shape=jax.ShapeDtypeStruct((db, 1, wa + wb), BF16),
        compiler_params=_params(("parallel", "arbitrary")),
        name="decode_even",
    )(pt, lam_p, subln_g.reshape(1, LANES), heads(qa16), heads(ka), heads(va), _col_bcast(qb16),
      sgate.reshape(db, 1, wa + wb), *page_args)


def _moba_scores_kernel(pt_ref, qcol_ref, qrow_ref, kn_ref, *refs, pps, n_blocks):
    del pt_ref
    page_refs = refs[:pps]
    p_ref, pnew_ref, ids_ref = refs[pps:pps + 3]
    s_s = refs[pps + 3]
    step = pl.program_id(1)
    n_h = qcol_ref.shape[0] // HEAD_DIM
    ppb = MOBA_BLOCK // LANES

    for u in range(pps):
        s_s[step * pps + u] = _head_dots(qcol_ref, page_refs[u], n_h)

    @pl.when(step == pl.num_programs(1) - 1)
    def _():
        lane = _iota((n_h, LANES), 1)
        g = jnp.zeros((n_h, LANES), F32)
        for n in range(n_blocks):
            tot = sum(s_s[n * ppb + t] for t in range(ppb))
            g = jnp.where(lane == n, jnp.sum(tot, axis=-1, keepdims=True) * (1.0 / MOBA_BLOCK), g)
        picks = []
        sel = _top_blocks(g, n_blocks, picks=picks)
        ids = jnp.zeros((n_h, LANES), jnp.int32)
        for r, idx in enumerate(picks):
            ids = jnp.where(lane == r, idx, ids)
        ids_ref[...] = ids
        chosen = [sel[:, n:n + 1] for n in range(n_blocks)]
        s_new = jnp.sum(_row_heads(qrow_ref[...], n_h, HEAD_DIM) * kn_ref[...], axis=-1, keepdims=True) * QK_SCALE
        m_lanes = jnp.full((n_h, LANES), NEG, F32)
        for j in range(n_blocks * ppb):
            m_lanes = jnp.maximum(m_lanes, jnp.where(chosen[j // ppb], s_s[j] * QK_SCALE, NEG))
        m = jnp.maximum(jnp.max(m_lanes, axis=-1, keepdims=True), s_new)
        l_lanes = jnp.zeros((n_h, LANES), F32)
        for j in range(n_blocks * ppb):
            e = jnp.where(chosen[j // ppb], jnp.exp(s_s[j] * QK_SCALE - m), 0.0)
            s_s[j] = e
            l_lanes = l_lanes + e
        e_new = jnp.exp(s_new - m)
        inv = 1.0 / (jnp.sum(l_lanes, axis=-1, keepdims=True) + e_new)
        for j in range(n_blocks * ppb):
            p_ref[j] = s_s[j] * inv
        pnew_ref[...] = jnp.broadcast_to(e_new * inv, pnew_ref.shape)


def _moba_values_kernel(pt_ref, ids_ref, p_ref, pnew_ref, vn_ref, sg_ref, *refs, n_sel, ppb, heads_per_step):
    del pt_ref
    slices, o_ref = refs[:-1], refs[-1]
    b, pair = pl.program_id(0), pl.program_id(1)
    accs = []
    for c in range(heads_per_step):
        h = pair * heads_per_step + c
        acc = jnp.zeros((HEAD_DIM, LANES), F32)
        for r in range(n_sel):
            blk = ids_ref[b, h * n_sel + r]
            for u in range(ppb):
                w = p_ref[blk * ppb + u, pl.ds(h, 1), :]
                acc = acc + w * slices[(c * n_sel + r) * ppb + u][...]
        accs.append(acc)
    o = _lane_sums_as_row(jnp.concatenate(accs, axis=0)) + pnew_ref[...] * vn_ref[...]
    o_ref[...] = (o * sg_ref[...]).astype(BF16)


def _decode_odd(pt, q, k_new, v_new, sgate, cache_k, cache_v, li, pps):
    db, n_pages = pt.shape
    w = q.shape[-1]
    n_h = w // HEAD_DIM
    page = LANES
    past = n_pages * page
    assert past % MOBA_BLOCK == 0 and past >= MOBA_BLOCK and MOBA_BLOCK % page == 0
    n_blocks = past // MOBA_BLOCK
    row = lambda wd: pl.BlockSpec((None, 1, wd), lambda b, s, pt: (b, 0, 0))
    r3 = lambda x: x.reshape(db, 1, x.shape[-1])
    assert n_blocks >= MOBA_TOPK
    per_head = pl.BlockSpec((None, n_h, LANES), lambda b, s, pt: (b, 0, 0))
    probs, p_new, ids = pl.pallas_call(
        functools.partial(_moba_scores_kernel, pps=pps, n_blocks=n_blocks),
        grid_spec=pltpu.PrefetchScalarGridSpec(
            num_scalar_prefetch=1,
            grid=(db, n_pages // pps),
            in_specs=[pl.BlockSpec((None, w, LANES), lambda b, s, pt: (b, 0, 0)), row(w), row(w)]
            + _page_specs(cache_k, li, n_pages, pps, False),
            out_specs=[pl.BlockSpec((None, n_pages, n_h, page), lambda b, s, pt: (b, 0, 0, 0)),
                       per_head, per_head],
            scratch_shapes=[pltpu.VMEM((n_pages, n_h, page), F32)],
        ),
        out_shape=[jax.ShapeDtypeStruct((db, n_pages, n_h, page), F32),
                   jax.ShapeDtypeStruct((db, n_h, LANES), F32),
                   jax.ShapeDtypeStruct((db, n_h, LANES), jnp.int32)],
        compiler_params=_params(("parallel", "arbitrary")),
        name="decode_moba_scores",
    )(pt, _col_bcast(q), r3(q), r3(k_new), *([cache_k] * pps))
    p_new_row = jnp.repeat(p_new[:, :, 0], HEAD_DIM, axis=-1).reshape(db, 1, w)

    ppb = MOBA_BLOCK // page
    heads_per_step = math.gcd(n_h, VALUE_HEADS_PER_STEP)
    step_lanes = heads_per_step * HEAD_DIM
    assert step_lanes % LANES == 0
    block_ids = ids[:, :, :MOBA_TOPK].reshape(db, n_h * MOBA_TOPK)

    def slice_spec(c, r, u):
        def index(b, pair, pt, bid):
            h = pair * heads_per_step + c
            return (li, pt[b, bid[b, h * MOBA_TOPK + r] * ppb + u], h, 0)
        return pl.BlockSpec((None, None, HEAD_DIM, LANES), index)

    slice_specs = [slice_spec(c, r, u) for c in range(heads_per_step) for r in range(MOBA_TOPK)
                   for u in range(ppb)]
    lanes = lambda: pl.BlockSpec((None, 1, step_lanes), lambda b, pair, pt, bid: (b, 0, pair))
    return pl.pallas_call(
        functools.partial(_moba_values_kernel, n_sel=MOBA_TOPK, ppb=ppb, heads_per_step=heads_per_step),
        grid_spec=pltpu.PrefetchScalarGridSpec(
            num_scalar_prefetch=2,
            grid=(db, n_h // heads_per_step),
            in_specs=[pl.BlockSpec((None, n_pages, n_h, page), lambda b, pair, pt, bid: (b, 0, 0, 0)),
                      lanes(), lanes(), lanes()] + slice_specs,
            out_specs=lanes(),
        ),
        out_shape=jax.ShapeDtypeStruct((db, 1, w), BF16),
        compiler_params=_params(("parallel", "arbitrary")),
        name="decode_moba_values",
    )(pt, block_ids, probs, p_new_row, r3(v_new), r3(sgate), *([cache_v] * len(slice_specs)))


PROMPT_TM = 512
ATTN_TQ = 512
ATTN_TK = 512
SB_TILE = 256
MOBA_GROUP = 4
MOBA_TILE_BLOCKS = 2
PAGES_PER_STEP = 16
MOBA_PAGES_PER_STEP = 32
VALUE_HEADS_PER_STEP = 8


def _rows_view(c):
    l, n, p, h, hd = c.shape
    assert p == LANES and hd == LANES
    return c.reshape(l, n, p * h, hd)


def _cols_view(c):
    l, n, p, h, hd = c.shape
    assert p == LANES and hd == HEAD_DIM
    return jnp.transpose(c, (0, 1, 3, 4, 2)).reshape(l, n, h * hd, p)


def _heads_from_cols(x, n_heads):
    b, w, s = x.shape
    return jnp.transpose(x.reshape(b, n_heads, w // n_heads, s), (0, 3, 1, 2))


def _even_layer(x_p, x_s, caches, li, pt, g_norm, w_in, qk_g, lam_p, subln_g, w_out, layer):
    lam_init = 0.8 - 0.6 * math.exp(-0.3 * layer)
    b, s, d = x_p.shape
    db, t, _ = x_s.shape
    wa = w_out.shape[0] // 2
    wb = wa
    def plan(wide, narrow):
        return (("q_rope_bf16", 0, wa, None), ("k_rope", wa, wa, wide), ("kv_plain", 2 * wa, wa, wide),
                ("q_plain_bf16", 3 * wa, wb, None), ("kv_plain", 3 * wa + wb, wb, narrow),
                ("kv_plain", 3 * wa + 2 * wb, wb, narrow), ("gate", 3 * wa + 3 * wb, wa + wb, None))

    w_in16 = w_in.astype(BF16)
    w_out16 = w_out.astype(BF16)
    past = pt.shape[1] * LANES
    n_a = wa // (2 * HEAD_DIM)
    n_b = wb // HEAD_DIM

    tabs = _rope_tables(jnp.arange(s))
    qa16, ka, ka16, va, va16, qb16, kb, kb16, vb, vb16, sgate = _project(
        x_p.reshape(b * s, d), g_norm, w_in16, qk_g[0], qk_g[1], tabs, plan("heads", "cols"), PROMPT_TM, s,
        "proj_even_prompt")
    r3 = lambda a: a.reshape(b, s, a.shape[-1])
    sg3 = r3(sgate)
    mix_a = _diff_attn_prompt(r3(qa16), r3(ka16), r3(va16), sg3, lam_p, subln_g, lam_init, ATTN_TQ, ATTN_TK)
    mix_b = _sb_prompt(r3(qb16), r3(kb16), r3(vb16), sg3, wa // LANES, SB_TILE, SB_TILE)
    y_p = _finish(x_p.reshape(b * s, d), [mix_a.reshape(b * s, wa), mix_b.reshape(b * s, wb)], w_out16, PROMPT_TM,
                  "finish_even_prompt").reshape(b, s, d)

    tabs_s = _rope_tables(jnp.tile(past + jnp.arange(t), db))
    qa16_s, ka_s, _, va_s, _, qb16_s, kb_s, _, vb_s, _, sgate_s = _project(
        x_s.reshape(db * t, d), g_norm, w_in16, qk_g[0], qk_g[1], tabs_s, plan("rows", "rows"), db * t, db * t,
        "proj_even_sample")
    assert t == 1
    mix_s = _decode_even(pt, lam_p, subln_g, lam_init, qa16_s, ka_s, va_s, qb16_s, sgate_s,
                         caches, li, math.gcd(pt.shape[1], PAGES_PER_STEP))
    y_s = _finish(x_s.reshape(db * t, d), [mix_s.reshape(db * t, wa + wb)], w_out16, db * t,
                  "finish_even_sample").reshape(db, t, d)

    rows_p = (ka.reshape(b, s, n_a, 2 * HEAD_DIM), va.reshape(b, s, n_a, 2 * HEAD_DIM),
              _heads_from_cols(kb, n_b), _heads_from_cols(vb, n_b))
    rows_s = (ka_s.reshape(db, t, n_a, 2 * HEAD_DIM), va_s.reshape(db, t, n_a, 2 * HEAD_DIM),
              kb_s.reshape(db, t, n_b, HEAD_DIM), vb_s.reshape(db, t, n_b, HEAD_DIM))
    return y_p, y_s, rows_p, rows_s


def _odd_layer(x_p, x_s, cache_k, cache_v, li, pt, g_norm, w_in, qk_g, w_out):
    b, s, d = x_p.shape
    db, t, _ = x_s.shape
    wc = w_out.shape[0]
    w_in16 = w_in.astype(BF16)
    w_out16 = w_out.astype(BF16)
    past = pt.shape[1] * LANES
    assert s % MOBA_BLOCK == 0

    def plan(k_kind, layout):
        return (("q_rope_f32", 0, wc, None), (k_kind, wc, wc, layout), ("kv_plain", 2 * wc, wc, layout),
                ("gate", 3 * wc, wc, None))

    tabs = _rope_tables(jnp.arange(s))
    q, k, k16, kmean, v, v16, sgate = _project(
        x_p.reshape(b * s, d), g_norm, w_in16, qk_g[0], qk_g[1], tabs, plan("k_rope_mean", "cols"),
        MOBA_BLOCK * math.gcd(s // MOBA_BLOCK, PROMPT_TM // MOBA_BLOCK), s, "proj_odd_prompt")
    r3 = lambda a: a.reshape(b, s, a.shape[-1])
    mix = _moba_prompt(r3(q), r3(k16), r3(v16), kmean.reshape(b, s // MOBA_BLOCK, wc), r3(sgate))
    y_p = _finish(x_p.reshape(b * s, d), [mix.reshape(b * s, wc)], w_out16, PROMPT_TM,
                  "finish_odd_prompt").reshape(b, s, d)

    assert t == 1
    tabs_s = _rope_tables(jnp.tile(past + jnp.arange(t), db))
    q_s, k_s, _, v_s, _, sgate_s = _project(
        x_s.reshape(db * t, d), g_norm, w_in16, qk_g[0], qk_g[1], tabs_s, plan("k_rope", "rows"), db * t, db * t,
        "proj_odd_sample")
    mix_s = _decode_odd(pt, q_s, k_s, v_s, sgate_s, cache_k, cache_v, li,
                        math.gcd(pt.shape[1], MOBA_PAGES_PER_STEP))
    y_s = _finish(x_s.reshape(db * t, d), [mix_s.reshape(db * t, wc)], w_out16, db * t,
                  "finish_odd_sample").reshape(db, t, d)

    n_c = wc // HEAD_DIM
    return (y_p, y_s, (_heads_from_cols(k, n_c), _heads_from_cols(v, n_c)),
            (k_s.reshape(db, t, n_c, HEAD_DIM), v_s.reshape(db, t, n_c, HEAD_DIM)))


def kernel(x_prompt, x_sample, cache_a_k, cache_a_v, cache_b_k, cache_b_v, cache_c_k, cache_c_v, page_table,
           norm_even, w_in_even, qk_norm_a, lambda_a, subln_a, w_out_even, norm_odd, w_in_odd, qk_norm_c,
           w_out_odd):
    depth = norm_even.shape[0] + norm_odd.shape[0]
    even_caches = [_rows_view(cache_a_k), _rows_view(cache_a_v), _cols_view(cache_b_k), _cols_view(cache_b_v)]
    odd_k, odd_v = _cols_view(cache_c_k), _cols_view(cache_c_v)
    y_p, y_s = x_prompt, x_sample
    ev_p, ev_s, od_p, od_s = [], [], [], []
    for layer in range(depth):
        i = layer // 2
        if layer % 2 == 0:
            y_p, y_s, rp, rs = _even_layer(y_p, y_s, even_caches, i, page_table, norm_even[i], w_in_even[i],
                                           qk_norm_a[i], lambda_a[i], subln_a[i], w_out_even[i], layer)
            ev_p.append(rp)
            ev_s.append(rs)
        else:
            y_p, y_s, rp, rs = _odd_layer(y_p, y_s, odd_k, odd_v, i, page_table, norm_odd[i], w_in_odd[i],
                                          qk_norm_c[i], w_out_odd[i])
            od_p.append(rp)
            od_s.append(rs)

    def stack(rows, j):
        return jnp.stack([r[j] for r in rows])

    return (y_p, y_s,
            stack(ev_p, 0), stack(ev_p, 1), stack(ev_p, 2), stack(ev_p, 3), stack(od_p, 0), stack(od_p, 1),
            stack(ev_s, 0), stack(ev_s, 1), stack(ev_s, 2), stack(ev_s, 3), stack(od_s, 0), stack(od_s, 1))
```
